```python
import math, functools
import jax, jax.numpy as jnp
from jax import lax
import numpy as np

D_MODEL = 1024
BATCH = 2
SEQ = 8192
DEPTH = 2

HEAD_DIM = 64
N_MIX_HEADS = D_MODEL // HEAD_DIM
HEADS_PER_MIXER = N_MIX_HEADS // 4
GROUP_WIDTH = HEADS_PER_MIXER * HEAD_DIM
Q_BLOCK = 128
MOBA_BLOCK = 256
MOBA_TOPK = 3
MLA_Q_RANK = D_MODEL // 4
MLA_KV_RANK = D_MODEL // 8
MLA_NOPE = HEAD_DIM
MLA_ROPE = HEAD_DIM // 2
MLA_V = HEAD_DIM
ROPE_THETA = 10000.0
DILATED_BRANCHES = ((128, 1), (512, 4), (2048, 16))
SWA_WINDOW = 128
SWA_KV_HEADS = 2
SWA_KV_WIDTH = SWA_KV_HEADS * HEAD_DIM
D_FF = 4 * D_MODEL
EPS = 1e-6
NEG = -1e30
COL_WIDTHS = (GROUP_WIDTH, GROUP_WIDTH, GROUP_WIDTH,
              MLA_Q_RANK, MLA_KV_RANK, MLA_ROPE,
              GROUP_WIDTH, GROUP_WIDTH, GROUP_WIDTH,
              GROUP_WIDTH, SWA_KV_WIDTH, SWA_KV_WIDTH)
IN_COLS = sum(COL_WIDTHS)

kernel_name = "hybrid_parallel_moba_mla_dilated_swa"


def rms_norm(x, g):
    xf = x.astype(jnp.float32)
    y = xf * lax.rsqrt(jnp.mean(xf * xf, axis=-1, keepdims=True) + EPS)
    return (y * g.astype(jnp.float32)).astype(x.dtype)


def split_columns(h):
    parts, off = [], 0
    for wdt in COL_WIDTHS:
        parts.append(h[..., off:off + wdt])
        off += wdt
    return parts


def alibi_slopes():
    n = 3 * HEADS_PER_MIXER
    idx = np.arange(1, n + 1, dtype=np.float32).reshape(HEADS_PER_MIXER, 3)
    s = jnp.asarray(np.exp2(-8.0 * idx / n), dtype=jnp.float32)
    return s[:, 0], s[:, 1], s[:, 2]


def rope_tables(S):
    inv = 1.0 / (ROPE_THETA ** (jnp.arange(0, MLA_ROPE, 2, dtype=jnp.float32) / MLA_ROPE))
    ang = jnp.arange(S, dtype=jnp.float32)[:, None] * inv[None, :]
    return jnp.cos(ang), jnp.sin(ang)


def apply_rope(x, cos, sin):
    x1, x2 = jnp.split(x.astype(jnp.float32), 2, axis=-1)
    c, s = cos[None, :, None, :], sin[None, :, None, :]
    return jnp.concatenate([x1 * c - x2 * s, x1 * s + x2 * c], axis=-1).astype(x.dtype)


def moba_attention(q, k, v, slopes):
    B, H, S, dh = q.shape
    nb = -(-S // MOBA_BLOCK)
    Sp = nb * MOBA_BLOCK
    topk = min(MOBA_TOPK, nb)
    pad = ((0, 0), (0, 0), (0, Sp - S), (0, 0))
    qp, kp, vp = jnp.pad(q, pad), jnp.pad(k, pad), jnp.pad(v, pad)
    kb = kp.reshape(B, H, nb, MOBA_BLOCK, dh)
    vb = vp.reshape(B, H, nb, MOBA_BLOCK, dh)
    kmean = jnp.mean(kb.astype(jnp.float32), axis=3)
    scale = dh ** -0.5
    bi = jnp.arange(B)[:, None, None, None]
    hi = jnp.arange(H)[None, :, None, None]
    slope_b = slopes.astype(jnp.float32)

    def chunk(c):
        t0 = c * Q_BLOCK
        qc = lax.dynamic_slice_in_dim(qp, t0, Q_BLOCK, axis=2)
        tpos = t0 + jnp.arange(Q_BLOCK)
        qblk = tpos // MOBA_BLOCK
        own = t0 // MOBA_BLOCK
        gate = jnp.einsum('bhqd,bhnd->bhqn', qc.astype(jnp.float32), kmean)
        past = jnp.arange(nb)[None, :] < qblk[:, None]
        gate = jnp.where(past, gate, NEG)
        _, sel = lax.top_k(gate, topk)
        sel_ok = sel < qblk[:, None]
        ksel = kb[bi, hi, sel]
        vsel = vb[bi, hi, sel]
        kpos_sel = sel[..., None] * MOBA_BLOCK + jnp.arange(MOBA_BLOCK)
        dist_sel = (tpos[:, None, None] - kpos_sel).astype(jnp.float32)
        s_sel = (jnp.einsum('bhqd,bhqjkd->bhqjk', qc, ksel).astype(jnp.float32) * scale
                 - slope_b[None, :, None, None, None] * dist_sel)
        s_sel = jnp.where(sel_ok[..., None], s_sel, NEG).reshape(B, H, Q_BLOCK, topk * MOBA_BLOCK)
        kown = lax.dynamic_slice_in_dim(kb, own, 1, axis=2)[:, :, 0]
        vown = lax.dynamic_slice_in_dim(vb, own, 1, axis=2)[:, :, 0]
        dist_own = tpos[:, None] - (own * MOBA_BLOCK + jnp.arange(MOBA_BLOCK))[None, :]
        s_own = (jnp.einsum('bhqd,bhkd->bhqk', qc, kown).astype(jnp.float32) * scale
                 - slope_b[None, :, None, None] * dist_own.astype(jnp.float32))
        s_own = jnp.where(dist_own >= 0, s_own, NEG)
        p = jax.nn.softmax(jnp.concatenate([s_sel, s_own], axis=-1), axis=-1)
        p_sel = p[..., :topk * MOBA_BLOCK].reshape(B, H, Q_BLOCK, topk, MOBA_BLOCK)
        p_own = p[..., topk * MOBA_BLOCK:]
        return (jnp.einsum('bhqjk,bhqjkd->bhqd', p_sel.astype(v.dtype), vsel)
                + jnp.einsum('bhqk,bhkd->bhqd', p_own.astype(v.dtype), vown))

    out = lax.map(chunk, jnp.arange(Sp // Q_BLOCK))
    return out.transpose(1, 2, 0, 3, 4).reshape(B, H, Sp, dh)[:, :, :S]


def causal_dense_attention(q, k, v, scale):
    B, H, S, _ = q.shape
    kpos = jnp.arange(S)

    def blk(c):
        qc = lax.dynamic_slice_in_dim(q, c * Q_BLOCK, Q_BLOCK, axis=2)
        s = jnp.einsum('bhqd,bhkd->bhqk', qc, k).astype(jnp.float32) * scale
        tpos = c * Q_BLOCK + jnp.arange(Q_BLOCK)
        s = jnp.where(kpos[None, :] <= tpos[:, None], s, NEG)
        p = jax.nn.softmax(s, axis=-1)
        return jnp.einsum('bhqk,bhkd->bhqd', p.astype(v.dtype), v)

    out = lax.map(blk, jnp.arange(S // Q_BLOCK))
    return out.transpose(1, 2, 0, 3, 4).reshape(B, H, S, -1)


def mla_attention(q_lat, kv_lat, k_rope, g_qlat, g_kvlat, w_uq, w_ukv, g_q, g_k):
    B, S, _ = q_lat.shape
    H = HEADS_PER_MIXER
    qk_dim = MLA_NOPE + MLA_ROPE
    q = (rms_norm(q_lat, g_qlat) @ w_uq).reshape(B, S, H, qk_dim)
    kv = (rms_norm(kv_lat, g_kvlat) @ w_ukv).reshape(B, S, H, MLA_NOPE + MLA_V)
    k_nope, v = kv[..., :MLA_NOPE], kv[..., MLA_NOPE:]
    k = jnp.concatenate([k_nope, jnp.broadcast_to(k_rope[:, :, None, :], (B, S, H, MLA_ROPE))], axis=-1)
    q, k = rms_norm(q, g_q), rms_norm(k, g_k)
    cos, sin = rope_tables(S)
    q = jnp.concatenate([q[..., :MLA_NOPE], apply_rope(q[..., MLA_NOPE:], cos, sin)], axis=-1)
    k = jnp.concatenate([k[..., :MLA_NOPE], apply_rope(k[..., MLA_NOPE:], cos, sin)], axis=-1)
    t = lambda a: a.transpose(0, 2, 1, 3)
    return causal_dense_attention(t(q), t(k), t(v), qk_dim ** -0.5)


def dilated_branch(q, k, v, slopes, window, dilation):
    B, H, S, dh = q.shape
    span = dilation * Q_BLOCK
    Sp = -(-S // span) * span
    L = Sp // dilation
    nb = L // Q_BLOCK
    steps = window // dilation

    def to_sub(t):
        t = jnp.pad(t, ((0, 0), (0, 0), (0, Sp - S), (0, 0)))
        return t.reshape(B, H, L, dilation, dh).transpose(0, 1, 3, 2, 4).reshape(B, H, dilation, nb, Q_BLOCK, dh)

    def with_prev(t):
        prev = jnp.pad(t, ((0, 0), (0, 0), (0, 0), (1, 0), (0, 0), (0, 0)))[:, :, :, :-1]
        return jnp.concatenate([prev, t], axis=4)

    qs = to_sub(q)
    kk, vv = with_prev(to_sub(k)), with_prev(to_sub(v))
    qi = jnp.arange(Q_BLOCK)[:, None]
    kidx = jnp.arange(2 * Q_BLOCK)[None, :]
    rel = qi + Q_BLOCK - kidx
    blk = jnp.arange(nb)[:, None, None]
    valid = (rel >= 0) & (rel <= steps) & (blk * Q_BLOCK + kidx[None] - Q_BLOCK >= 0)
    s = (jnp.einsum('bhrnqd,bhrnkd->bhrnqk', qs, kk).astype(jnp.float32) * (dh ** -0.5)
         - slopes.astype(jnp.float32)[None, :, None, None, None, None] * (dilation * rel).astype(jnp.float32))
    s = jnp.where(valid, s, NEG)
    m = jnp.max(s, axis=-1, keepdims=True)
    e = jnp.exp(s - m)
    l = jnp.sum(e, axis=-1, keepdims=True)
    o = jnp.einsum('bhrnqk,bhrnkd->bhrnqd', e, vv.astype(jnp.float32))
    back = lambda t: t.reshape(B, H, dilation, L, -1).transpose(0, 1, 3, 2, 4).reshape(B, H, Sp, -1)[:, :, :S]
    return back(o), back(m), back(l)


def dilated_mixture(q, k, v, slopes):
    branches = [dilated_branch(q, k, v, slopes, w, d) for (w, d) in DILATED_BRANCHES]
    M = functools.reduce(jnp.maximum, [b[1] for b in branches])
    num = jnp.zeros(q.shape, jnp.float32)
    den = jnp.zeros(q.shape[:-1] + (1,), jnp.float32)
    for o, m, l in branches:
        w = jnp.exp(m - M)
        num = num + w * o
        den = den + w * l
    return (num / den).astype(q.dtype)


def swa_sink_attention(q, k, v, sinks, slopes):
    B, H, S, dh = q.shape
    G = H // SWA_KV_HEADS
    nb = S // Q_BLOCK
    qb = q.reshape(B, SWA_KV_HEADS, G, nb, Q_BLOCK, dh)

    def with_prev(t):
        t = t.reshape(B, SWA_KV_HEADS, nb, Q_BLOCK, dh)
        prev = jnp.pad(t, ((0, 0), (0, 0), (1, 0), (0, 0), (0, 0)))[:, :, :-1]
        return jnp.concatenate([prev, t], axis=3)

    kk, vv = with_prev(k), with_prev(v)
    qi = jnp.arange(Q_BLOCK)[:, None]
    kidx = jnp.arange(2 * Q_BLOCK)[None, :]
    rel = qi + Q_BLOCK - kidx
    blk = jnp.arange(nb)[:, None, None]
    valid = (rel >= 0) & (rel < SWA_WINDOW) & (blk * Q_BLOCK + kidx[None] - Q_BLOCK >= 0)
    sl = slopes.astype(jnp.float32).reshape(SWA_KV_HEADS, G)[None, :, :, None, None, None]
    s = jnp.einsum('bkgnqd,bkncd->bkgnqc', qb, kk).astype(jnp.float32) * (dh ** -0.5) - sl * rel.astype(jnp.float32)
    s = jnp.where(valid, s, NEG)
    sink = jnp.broadcast_to(sinks.astype(jnp.float32).reshape(SWA_KV_HEADS, G)[None, :, :, None, None, None],
                            s.shape[:-1] + (1,))
    p = jax.nn.softmax(jnp.concatenate([s, sink], axis=-1), axis=-1)[..., :-1]
    o = jnp.einsum('bkgnqc,bkncd->bkgnqd', p.astype(v.dtype), vv)
    return o.reshape(B, H, S, dh)


def hybrid_layer(x, attn_norm_g, w_in, moba_q_g, moba_k_g, mla_qlat_g, mla_kvlat_g,
                 mla_w_uq, mla_w_ukv, mla_q_g, mla_k_g, dil_q_g, dil_k_g, swa_q_g, swa_k_g,
                 swa_sinks, group_out_g, w_o, mlp_norm_g, w_up, w_down):
    B, S, _ = x.shape
    H = HEADS_PER_MIXER
    slope_a, slope_c, slope_d = alibi_slopes()
    h = rms_norm(x, attn_norm_g) @ w_in
    a_q, a_k, a_v, b_ql, b_kvl, b_kr, c_q, c_k, c_v, d_q, d_k, d_v = split_columns(h)
    heads = lambda t, n: t.reshape(B, S, n, HEAD_DIM)
    tr = lambda t: t.transpose(0, 2, 1, 3)
    oa = moba_attention(tr(rms_norm(heads(a_q, H), moba_q_g)), tr(rms_norm(heads(a_k, H), moba_k_g)),
                        tr(heads(a_v, H)), slope_a)
    ob = mla_attention(b_ql, b_kvl, b_kr, mla_qlat_g, mla_kvlat_g, mla_w_uq, mla_w_ukv, mla_q_g, mla_k_g)
    oc = dilated_mixture(tr(rms_norm(heads(c_q, H), dil_q_g)), tr(rms_norm(heads(c_k, H), dil_k_g)),
                         tr(heads(c_v, H)), slope_c)
    od = swa_sink_attention(tr(rms_norm(heads(d_q, H), swa_q_g)),
                            tr(rms_norm(heads(d_k, SWA_KV_HEADS), swa_k_g)),
                            tr(heads(d_v, SWA_KV_HEADS)), swa_sinks, slope_d)
    groups = [tr(o).reshape(B, S, GROUP_WIDTH) for o in (oa, ob, oc, od)]
    mixed = jnp.concatenate([rms_norm(gr, group_out_g[i]) for i, gr in enumerate(groups)], axis=-1)
    x = x + mixed @ w_o
    u = jax.nn.relu(rms_norm(x, mlp_norm_g) @ w_up)
    return x + (u * u) @ w_down


def setup_inputs(seed: int = 0) -> dict:
    key = jax.random.key(seed)
    ks = jax.random.split(key, 24)
    f = jnp.float32
    H = HEADS_PER_MIXER
    w = lambda k, shape, fan_in: jax.random.normal(k, shape, f) * (fan_in ** -0.5)
    g = lambda k, shape: 1.0 + 0.02 * jax.random.normal(k, shape, f)
    return {
        "x": jax.random.normal(ks[0], (BATCH, SEQ, D_MODEL), f),
        "attn_norm_g": g(ks[1], (DEPTH, D_MODEL)),
        "w_in": w(ks[2], (DEPTH, D_MODEL, IN_COLS), D_MODEL),
        "moba_q_g": g(ks[3], (DEPTH, HEAD_DIM)),
        "moba_k_g": g(ks[4], (DEPTH, HEAD_DIM)),
        "mla_qlat_g": g(ks[5], (DEPTH, MLA_Q_RANK)),
        "mla_kvlat_g": g(ks[6], (DEPTH, MLA_KV_RANK)),
        "mla_w_uq": w(ks[7], (DEPTH, MLA_Q_RANK, H * (MLA_NOPE + MLA_ROPE)), MLA_Q_RANK),
        "mla_w_ukv": w(ks[8], (DEPTH, MLA_KV_RANK, H * (MLA_NOPE + MLA_V)), MLA_KV_RANK),
        "mla_q_g": g(ks[9], (DEPTH, MLA_NOPE + MLA_ROPE)),
        "mla_k_g": g(ks[10], (DEPTH, MLA_NOPE + MLA_ROPE)),
        "dil_q_g": g(ks[11], (DEPTH, HEAD_DIM)),
        "dil_k_g": g(ks[12], (DEPTH, HEAD_DIM)),
        "swa_q_g": g(ks[13], (DEPTH, HEAD_DIM)),
        "swa_k_g": g(ks[14], (DEPTH, HEAD_DIM)),
        "swa_sinks": 0.5 * jax.random.normal(ks[15], (DEPTH, H), f),
        "group_out_g": g(ks[16], (DEPTH, 4, GROUP_WIDTH)),
        "w_o": w(ks[17], (DEPTH, D_MODEL, D_MODEL), D_MODEL),
        "mlp_norm_g": g(ks[18], (DEPTH, D_MODEL)),
        "w_up": w(ks[19], (DEPTH, D_MODEL, D_FF), D_MODEL),
        "w_down": w(ks[20], (DEPTH, D_FF, D_MODEL), D_FF),
    }


def reference(x, attn_norm_g, w_in, moba_q_g, moba_k_g, mla_qlat_g, mla_kvlat_g, mla_w_uq,
              mla_w_ukv, mla_q_g, mla_k_g, dil_q_g, dil_k_g, swa_q_g, swa_k_g, swa_sinks,
              group_out_g, w_o, mlp_norm_g, w_up, w_down):
    for l in range(DEPTH):
        x = hybrid_layer(x, attn_norm_g[l], w_in[l], moba_q_g[l], moba_k_g[l], mla_qlat_g[l],
                         mla_kvlat_g[l], mla_w_uq[l], mla_w_ukv[l], mla_q_g[l], mla_k_g[l],
                         dil_q_g[l], dil_k_g[l], swa_q_g[l], swa_k_g[l], swa_sinks[l],
                         group_out_g[l], w_o[l], mlp_norm_g[l], w_up[l], w_down[l])
    return x
```

```python
import functools

import numpy as np
import jax
import jax.numpy as jnp
from jax import lax
from jax.experimental import pallas as pl
from jax.experimental.pallas import tpu as pltpu

F32 = jnp.float32
BF16 = jnp.bfloat16

D_MODEL = 1024
HEAD_DIM = 64
N_HEADS = 4
GROUP = N_HEADS * HEAD_DIM
MOBA_BLOCK = 256
MOBA_TOPK = 3
MLA_Q_RANK = 256
MLA_KV_RANK = 128
MLA_NOPE = 64
MLA_ROPE = 32
MLA_QK = MLA_NOPE + MLA_ROPE
MLA_QK_PAD = 128
ROPE_THETA = 10000.0
DILATED_BRANCHES = ((128, 1), (512, 4), (2048, 16))
Q_BLOCK = 128
SWA_WINDOW = 128
SWA_KV_HEADS = 2
SWA_KV_WIDTH = SWA_KV_HEADS * HEAD_DIM
D_FF = 4 * D_MODEL
EPS = 1e-6
NEG = -1e30

_WIDTHS = (GROUP, GROUP, GROUP, MLA_Q_RANK, MLA_KV_RANK, MLA_ROPE,
           GROUP, GROUP, GROUP, GROUP, SWA_KV_WIDTH, SWA_KV_WIDTH)
_OFFS = tuple(int(v) for v in np.cumsum((0,) + _WIDTHS))
IN_COLS = _OFFS[-1]
(_A_Q, _A_K, _A_V, _B_QL, _B_KVL, _B_KR, _C_Q, _C_K, _C_V, _D_Q, _D_K, _D_V) = _OFFS[:-1]

VMEM_LIMIT = 56 * 1024 * 1024

INPROJ_TM = 512
TAIL_TM = 256
ATT_TQ = 256
ATT_TK = 256

_NT = (((1,), (1,)), ((), ()))


def _alibi_slopes():
    n = 3 * N_HEADS
    idx = np.arange(1, n + 1, dtype=np.float32).reshape(N_HEADS, 3)
    s = np.exp2(-8.0 * idx / n).astype(np.float32)
    return s[:, 0], s[:, 1], s[:, 2]


SLOPE_A, SLOPE_C, SLOPE_D = _alibi_slopes()


def _params(n_axes):
    return pltpu.CompilerParams(dimension_semantics=("arbitrary",) * n_axes,
                                vmem_limit_bytes=VMEM_LIMIT)


def _head_norm_t(sec, g_col, n_heads, width):
    outs = []
    for h in range(n_heads):
        s = sec[h * width:(h + 1) * width, :]
        ms = jnp.sum(s * s, axis=0, keepdims=True) * (1.0 / width)
        outs.append(s * lax.rsqrt(ms + EPS) * g_col)
    return outs


def _inproj_kernel(x_ref, gx_ref, w1t_ref, gaq_ref, gak_ref, gql_ref, gkvl_ref, wuqt_ref, wukvt_ref,
                   gbq_ref, gbk_ref, gcq_ref, gck_ref, gdq_ref, gdk_ref, cos_ref, sin_ref,
                   aqt_ref, aq32t_ref, ak_ref, ak32_ref, avt_ref,
                   bqt_ref, bk_ref, bvt_ref,
                   cq_ref, ck_ref, cv_ref, dq_ref, dk_ref, dv_ref,
                   h_scr):
    tm = x_ref.shape[1]
    x = x_ref[0]
    ms = jnp.mean(x * x, axis=-1, keepdims=True)
    xn = (x * lax.rsqrt(ms + EPS) * gx_ref[...]).astype(BF16)
    h_scr[...] = lax.dot_general(w1t_ref[...], xn, _NT, preferred_element_type=F32)

    scale = HEAD_DIM ** -0.5

    qa = _head_norm_t(h_scr[_A_Q:_A_Q + GROUP, :], gaq_ref[...], N_HEADS, HEAD_DIM)
    for h in range(N_HEADS):
        aq32t_ref[0, h] = qa[h]
        aqt_ref[0, h] = (qa[h] * scale).astype(BF16)
    ka = jnp.concatenate(_head_norm_t(h_scr[_A_K:_A_K + GROUP, :], gak_ref[...], N_HEADS, HEAD_DIM), axis=0)
    ka_nat = ka.T
    ak32_ref[0] = ka_nat
    ka_bf = ka_nat.astype(BF16)
    for h in range(N_HEADS):
        ak_ref[0, h] = ka_bf[:, h * HEAD_DIM:(h + 1) * HEAD_DIM]
    for h in range(N_HEADS):
        for c in range(tm // ATT_TK):
            avt_ref[0, h, c] = h_scr[_A_V + h * HEAD_DIM:_A_V + (h + 1) * HEAD_DIM,
                                     c * ATT_TK:(c + 1) * ATT_TK].astype(BF16)

    cos = cos_ref[...]
    sin = sin_ref[...]
    half = MLA_ROPE // 2

    def rope_pad(t, sc):
        x1 = t[MLA_NOPE:MLA_NOPE + half, :]
        x2 = t[MLA_NOPE + half:MLA_QK, :]
        return jnp.concatenate([t[:MLA_NOPE, :] * sc, (x1 * cos - x2 * sin) * sc, (x1 * sin + x2 * cos) * sc,
                                jnp.zeros((MLA_QK_PAD - MLA_QK, tm), F32)], axis=0)

    ql = h_scr[_B_QL:_B_QL + MLA_Q_RANK, :]
    ql = ql * lax.rsqrt(jnp.sum(ql * ql, axis=0, keepdims=True) * (1.0 / MLA_Q_RANK) + EPS) * gql_ref[...]
    qb = jnp.dot(wuqt_ref[...], ql.astype(BF16), preferred_element_type=F32)
    qb = _head_norm_t(qb, gbq_ref[...], N_HEADS, MLA_QK)
    kvl = h_scr[_B_KVL:_B_KVL + MLA_KV_RANK, :]
    kvl = kvl * lax.rsqrt(jnp.sum(kvl * kvl, axis=0, keepdims=True) * (1.0 / MLA_KV_RANK) + EPS) * gkvl_ref[...]
    kvb = jnp.dot(wukvt_ref[...], kvl.astype(BF16), preferred_element_type=F32)
    kr = h_scr[_B_KR:_B_KR + MLA_ROPE, :]
    gbk = gbk_ref[...]
    for h in range(N_HEADS):
        bqt_ref[0, h] = rope_pad(qb[h], MLA_QK ** -0.5).astype(BF16)
        kh = jnp.concatenate([kvb[h * 2 * HEAD_DIM:h * 2 * HEAD_DIM + MLA_NOPE, :], kr], axis=0)
        kh = kh * lax.rsqrt(jnp.sum(kh * kh, axis=0, keepdims=True) * (1.0 / MLA_QK) + EPS) * gbk
        bk_ref[0, h] = rope_pad(kh, 1.0).T.astype(BF16)
        vh = kvb[h * 2 * HEAD_DIM + MLA_NOPE:(h + 1) * 2 * HEAD_DIM, :]
        for c in range(tm // ATT_TK):
            bvt_ref[0, h, c] = vh[:, c * ATT_TK:(c + 1) * ATT_TK].astype(BF16)

    qc = jnp.concatenate(_head_norm_t(h_scr[_C_Q:_C_Q + GROUP, :], gcq_ref[...], N_HEADS, HEAD_DIM), axis=0)
    cq_ref[0] = (qc * scale).T.astype(BF16)
    kc = jnp.concatenate(_head_norm_t(h_scr[_C_K:_C_K + GROUP, :], gck_ref[...], N_HEADS, HEAD_DIM), axis=0)
    ck_ref[0] = kc.T.astype(BF16)
    cv_ref[0] = h_scr[_C_V:_C_V + GROUP, :].T.astype(BF16)

    qd = jnp.concatenate(_head_norm_t(h_scr[_D_Q:_D_Q + GROUP, :], gdq_ref[...], N_HEADS, HEAD_DIM), axis=0)
    dq_ref[0] = (qd * scale).T.astype(BF16)
    kd = jnp.concatenate(_head_norm_t(h_scr[_D_K:_D_K + SWA_KV_WIDTH, :], gdk_ref[...], SWA_KV_HEADS, HEAD_DIM),
                         axis=0)
    dk_ref[0] = kd.T.astype(BF16)
    dv_ref[0] = h_scr[_D_V:_D_V + SWA_KV_WIDTH, :].T.astype(BF16)


def _inproj(x, gx, w1t, gaq, gak, gql, gkvl, wuqt, wukvt, gbq, gbk, gcq, gck, gdq, gdk, cos_t, sin_t):
    B, S, _ = x.shape
    tm = INPROJ_TM
    nb = S // ATT_TK
    cpt = tm // ATT_TK
    H = N_HEADS
    full = lambda a: pl.BlockSpec(a.shape, lambda b, t: (0,) * a.ndim)
    in_specs = [pl.BlockSpec((1, tm, D_MODEL), lambda b, t: (b, t, 0)), full(gx), full(w1t), full(gaq), full(gak),
                full(gql), full(gkvl), full(wuqt), full(wukvt), full(gbq), full(gbk), full(gcq), full(gck),
                full(gdq), full(gdk),
                pl.BlockSpec((MLA_ROPE // 2, tm), lambda b, t: (0, t)),
                pl.BlockSpec((MLA_ROPE // 2, tm), lambda b, t: (0, t))]
    head_t = lambda w: pl.BlockSpec((1, H, w, tm), lambda b, t: (b, 0, 0, t))
    vt_spec = pl.BlockSpec((1, H, cpt, HEAD_DIM, ATT_TK), lambda b, t: (b, 0, t, 0, 0))
    nat = lambda w: pl.BlockSpec((1, tm, w), lambda b, t: (b, t, 0))
    out_shape = [
        jax.ShapeDtypeStruct((B, H, HEAD_DIM, S), BF16),
        jax.ShapeDtypeStruct((B, H, HEAD_DIM, S), F32),
        jax.ShapeDtypeStruct((B, H, S, HEAD_DIM), BF16),
        jax.ShapeDtypeStruct((B, S, GROUP), F32),
        jax.ShapeDtypeStruct((B, H, nb, HEAD_DIM, ATT_TK), BF16),
        jax.ShapeDtypeStruct((B, H, MLA_QK_PAD, S), BF16),
        jax.ShapeDtypeStruct((B, H, S, MLA_QK_PAD), BF16),
        jax.ShapeDtypeStruct((B, H, nb, HEAD_DIM, ATT_TK), BF16),
        jax.ShapeDtypeStruct((B, S, GROUP), BF16),
        jax.ShapeDtypeStruct((B, S, GROUP), BF16),
        jax.ShapeDtypeStruct((B, S, GROUP), BF16),
        jax.ShapeDtypeStruct((B, S, GROUP), BF16),
        jax.ShapeDtypeStruct((B, S, SWA_KV_WIDTH), BF16),
        jax.ShapeDtypeStruct((B, S, SWA_KV_WIDTH), BF16),
    ]
    out_specs = [head_t(HEAD_DIM), head_t(HEAD_DIM),
                 pl.BlockSpec((1, H, tm, HEAD_DIM), lambda b, t: (b, 0, t, 0)),
                 nat(GROUP), vt_spec,
                 head_t(MLA_QK_PAD),
                 pl.BlockSpec((1, H, tm, MLA_QK_PAD), lambda b, t: (b, 0, t, 0)),
                 vt_spec,
                 nat(GROUP), nat(GROUP), nat(GROUP), nat(GROUP), nat(SWA_KV_WIDTH), nat(SWA_KV_WIDTH)]
    return pl.pallas_call(
        _inproj_kernel,
        grid=(B, S // tm),
        in_specs=in_specs,
        out_specs=out_specs,
        out_shape=out_shape,
        scratch_shapes=[pltpu.VMEM((IN_COLS, tm), F32)],
        compiler_params=_params(2),
        name="inproj",
    )(x, gx, w1t, gaq, gak, gql, gkvl, wuqt, wukvt, gbq, gbk, gcq, gck, gdq, gdk, cos_t, sin_t)


def _kmean_kernel(k_ref, o_ref):
    S = k_ref.shape[1]
    k = k_ref[0].reshape(S // MOBA_BLOCK, MOBA_BLOCK, GROUP)
    o_ref[0] = jnp.sum(k, axis=1) * (1.0 / MOBA_BLOCK)


def _kmean(ak32):
    B, S, _ = ak32.shape
    nb = S // MOBA_BLOCK
    return pl.pallas_call(
        _kmean_kernel,
        grid=(B,),
        in_specs=[pl.BlockSpec((1, S, GROUP), lambda b: (b, 0, 0))],
        out_specs=pl.BlockSpec((1, nb, GROUP), lambda b: (b, 0, 0)),
        out_shape=jax.ShapeDtypeStruct((B, nb, GROUP), F32),
        compiler_params=_params(1),
        name="kmean",
    )(ak32)


def _softmax_step(st, vt, carry):
    m, l, acc = carry
    m_new = jnp.maximum(m, jnp.max(st, axis=0, keepdims=True))
    alpha = jnp.exp(m - m_new)
    p = jnp.exp(st - m_new)
    l = alpha * l + jnp.sum(p, axis=0, keepdims=True)
    acc = alpha * acc + jnp.dot(vt, p.astype(BF16), preferred_element_type=F32)
    return m_new, l, acc


def _softmax_first(st, vt):
    m = jnp.max(st, axis=0, keepdims=True)
    p = jnp.exp(st - m)
    l = jnp.sum(p, axis=0, keepdims=True)
    acc = jnp.dot(vt, p.astype(BF16), preferred_element_type=F32)
    return m, l, acc


def _moba_kernel(slope_ref, qt_ref, q32t_ref, k_ref, vt_ref, kmean_ref, o_ref, selb_scr):
    h = pl.program_id(1)
    i = pl.program_id(2)
    nb = kmean_ref.shape[2]
    tq = qt_ref.shape[3]
    slope = slope_ref[h]
    qt = qt_ref[0, 0]

    gate = jnp.dot(kmean_ref[0, 0], q32t_ref[0, 0], preferred_element_type=F32,
                   precision=lax.Precision.HIGHEST)
    blk = lax.broadcasted_iota(jnp.int32, (nb, tq), 0)
    gate = jnp.where(blk < i, gate, NEG)
    sel = jnp.zeros((nb, tq), jnp.bool_)
    for _ in range(MOBA_TOPK):
        best = jnp.max(gate, axis=0, keepdims=True)
        first = jnp.min(jnp.where(gate == best, blk, nb), axis=0, keepdims=True)
        pick = blk == first
        sel = jnp.logical_or(sel, pick)
        gate = jnp.where(pick, -jnp.inf, gate)
    sel = jnp.logical_and(sel, blk < i)
    blk_dist = ((i - blk) * ATT_TK).astype(F32)
    selb_scr[...] = jnp.where(sel, -slope * blk_dist, NEG)

    kj = lax.broadcasted_iota(jnp.int32, (ATT_TK, tq), 0)
    ti = lax.broadcasted_iota(jnp.int32, (ATT_TK, tq), 1)
    local = (ti - kj).astype(F32) * (-slope)

    def scores(j):
        kb = k_ref[0, 0, pl.ds(pl.multiple_of(j * ATT_TK, ATT_TK), ATT_TK), :]
        return jnp.dot(kb, qt, preferred_element_type=F32)

    carry = _softmax_first(jnp.where(ti >= kj, scores(i) + local, NEG), vt_ref[0, 0, i])

    def body(j, carry):
        st = scores(j) + local + selb_scr[pl.ds(j, 1), :]
        return _softmax_step(st, vt_ref[0, 0, j], carry)

    m, l, acc = lax.fori_loop(0, i, body, carry)
    o_ref[0, 0] = acc / l


def _moba(aqt, aq32t, ak, avt, kmean_h):
    B, H, _, S = aqt.shape
    tq = ATT_TQ
    nb = S // ATT_TK
    return pl.pallas_call(
        _moba_kernel,
        grid=(B, H, S // tq),
        in_specs=[pl.BlockSpec(memory_space=pltpu.SMEM),
                  pl.BlockSpec((1, 1, HEAD_DIM, tq), lambda b, h, i: (b, h, 0, i)),
                  pl.BlockSpec((1, 1, HEAD_DIM, tq), lambda b, h, i: (b, h, 0, i)),
                  pl.BlockSpec((1, 1, S, HEAD_DIM), lambda b, h, i: (b, h, 0, 0)),
                  pl.BlockSpec((1, 1, nb, HEAD_DIM, ATT_TK), lambda b, h, i: (b, h, 0, 0, 0)),
                  pl.BlockSpec((1, 1, nb, HEAD_DIM), lambda b, h, i: (b, h, 0, 0))],
        out_specs=pl.BlockSpec((1, 1, HEAD_DIM, tq), lambda b, h, i: (b, h, 0, i)),
        out_shape=jax.ShapeDtypeStruct((B, H, HEAD_DIM, S), F32),
        scratch_shapes=[pltpu.VMEM((nb, tq), F32)],
        compiler_params=_params(3),
        name="moba",
    )(jnp.asarray(SLOPE_A), aqt, aq32t, ak, avt, kmean_h)


def _mla_kernel(qt_ref, k_ref, vt_ref, o_ref):
    i = pl.program_id(2)
    tq = qt_ref.shape[3]
    qt = qt_ref[0, 0]

    def scores(j):
        kb = k_ref[0, 0, pl.ds(pl.multiple_of(j * ATT_TK, ATT_TK), ATT_TK), :]
        return jnp.dot(kb, qt, preferred_element_type=F32)

    kj = lax.broadcasted_iota(jnp.int32, (ATT_TK, tq), 0)
    ti = lax.broadcasted_iota(jnp.int32, (ATT_TK, tq), 1)
    carry = _softmax_first(jnp.where(ti >= kj, scores(i), NEG), vt_ref[0, 0, i])

    def body(j, carry):
        return _softmax_step(scores(j), vt_ref[0, 0, j], carry)

    m, l, acc = lax.fori_loop(0, i, body, carry)
    o_ref[0, 0] = acc / l


def _mla(bqt, bk, bvt):
    B, H, _, S = bqt.shape
    tq = ATT_TQ
    nb = S // ATT_TK
    return pl.pallas_call(
        _mla_kernel,
        grid=(B, H, S // tq),
        in_specs=[pl.BlockSpec((1, 1, MLA_QK_PAD, tq), lambda b, h, i: (b, h, 0, i)),
                  pl.BlockSpec((1, 1, S, MLA_QK_PAD), lambda b, h, i: (b, h, 0, 0)),
                  pl.BlockSpec((1, 1, nb, HEAD_DIM, ATT_TK), lambda b, h, i: (b, h, 0, 0, 0))],
        out_specs=pl.BlockSpec((1, 1, HEAD_DIM, tq), lambda b, h, i: (b, h, 0, i)),
        out_shape=jax.ShapeDtypeStruct((B, H, HEAD_DIM, S), F32),
        compiler_params=_params(3),
        name="mla",
    )(bqt, bk, bvt)


def _dilated_kernel(q_ref, kp_ref, kc_ref, vp_ref, vc_ref, o_ref, m_ref, l_ref, *, dilation, steps):
    n = pl.program_id(2)
    q = q_ref[0]
    k2 = jnp.concatenate([kp_ref[0], kc_ref[0]], axis=0)
    v2 = jnp.concatenate([vp_ref[0], vc_ref[0]], axis=0)
    qi = lax.broadcasted_iota(jnp.int32, (Q_BLOCK, 2 * Q_BLOCK), 0)
    kidx = lax.broadcasted_iota(jnp.int32, (Q_BLOCK, 2 * Q_BLOCK), 1)
    rel = qi + Q_BLOCK - kidx
    valid = (rel >= 0) & (rel <= steps) & ((n > 0) | (kidx >= Q_BLOCK))
    dist = (dilation * rel).astype(F32)
    os, ms, ls = [], [], []
    for h in range(N_HEADS):
        sl = slice(h * HEAD_DIM, (h + 1) * HEAD_DIM)
        s = lax.dot_general(q[:, sl], k2[:, sl], _NT, preferred_element_type=F32)
        s = jnp.where(valid, s - float(SLOPE_C[h]) * dist, NEG)
        m = jnp.max(s, axis=-1, keepdims=True)
        e = jnp.exp(s - m)
        l = jnp.sum(e, axis=-1, keepdims=True)
        os.append(jnp.dot(e.astype(BF16), v2[:, sl], preferred_element_type=F32))
        ms.append(jnp.broadcast_to(m, (Q_BLOCK, HEAD_DIM)))
        ls.append(jnp.broadcast_to(l, (Q_BLOCK, HEAD_DIM)))
    o_ref[0] = jnp.concatenate(os, axis=-1)
    m_ref[0] = jnp.concatenate(ms, axis=-1)
    l_ref[0] = jnp.concatenate(ls, axis=-1)


def _dilated_branch(cq, ck, cv, window, dilation):
    B, S, _ = cq.shape
    L = S // dilation
    view = lambda a: a.reshape(B, L, dilation * GROUP)
    cur = pl.BlockSpec((1, Q_BLOCK, GROUP), lambda b, r, n: (b, n, r))
    prev = pl.BlockSpec((1, Q_BLOCK, GROUP), lambda b, r, n: (b, jnp.maximum(n - 1, 0), r))
    shp = jax.ShapeDtypeStruct((B, L, dilation * GROUP), F32)
    o, m, l = pl.pallas_call(
        functools.partial(_dilated_kernel, dilation=dilation, steps=window // dilation),
        grid=(B, dilation, L // Q_BLOCK),
        in_specs=[cur, prev, cur, prev, cur],
        out_specs=[cur, cur, cur],
        out_shape=[shp, shp, shp],
        compiler_params=_params(3),
        name=f"dilated{dilation}",
    )(view(cq), view(ck), view(ck), view(cv), view(cv))
    back = lambda a: a.reshape(B, S, GROUP)
    return back(o), back(m), back(l)


def _swa_kernel(sink_ref, q_ref, k_ref, v_ref, o_ref):
    i = pl.program_id(1)
    G = N_HEADS // SWA_KV_HEADS
    start = pl.multiple_of(jnp.maximum(i - 1, 0) * Q_BLOCK, Q_BLOCK)
    q = q_ref[0]
    kw = k_ref[0, pl.ds(start, 2 * Q_BLOCK), :]
    vw = v_ref[0, pl.ds(start, 2 * Q_BLOCK), :]
    tpos = i * Q_BLOCK + lax.broadcasted_iota(jnp.int32, (Q_BLOCK, 2 * Q_BLOCK), 0)
    kpos = start + lax.broadcasted_iota(jnp.int32, (Q_BLOCK, 2 * Q_BLOCK), 1)
    rel = tpos - kpos
    valid = (rel >= 0) & (rel < SWA_WINDOW)
    dist = rel.astype(F32)
    outs = []
    for h in range(N_HEADS):
        g = h // G
        ksl = slice(g * HEAD_DIM, (g + 1) * HEAD_DIM)
        s = lax.dot_general(q[:, h * HEAD_DIM:(h + 1) * HEAD_DIM], kw[:, ksl], _NT, preferred_element_type=F32)
        s = jnp.where(valid, s - float(SLOPE_D[h]) * dist, NEG)
        sink = sink_ref[h]
        m = jnp.maximum(jnp.max(s, axis=-1, keepdims=True), sink)
        p = jnp.exp(s - m)
        l = jnp.sum(p, axis=-1, keepdims=True) + jnp.exp(sink - m)
        outs.append(jnp.dot(p.astype(BF16), vw[:, ksl], preferred_element_type=F32) / l)
    o_ref[0] = jnp.concatenate(outs, axis=-1)


def _swa(dq, dk, dv, sinks):
    B, S, _ = dq.shape
    return pl.pallas_call(
        _swa_kernel,
        grid=(B, S // Q_BLOCK),
        in_specs=[pl.BlockSpec(memory_space=pltpu.SMEM),
                  pl.BlockSpec((1, Q_BLOCK, GROUP), lambda b, i: (b, i, 0)),
                  pl.BlockSpec((1, S, SWA_KV_WIDTH), lambda b, i: (b, 0, 0)),
                  pl.BlockSpec((1, S, SWA_KV_WIDTH), lambda b, i: (b, 0, 0))],
        out_specs=pl.BlockSpec((1, Q_BLOCK, GROUP), lambda b, i: (b, i, 0)),
        out_shape=jax.ShapeDtypeStruct((B, S, GROUP), F32),
        compiler_params=_params(2),
        name="swa",
    )(sinks, dq, dk, dv)


def _row_norm(y, g_row):
    return y * lax.rsqrt(jnp.mean(y * y, axis=-1, keepdims=True) + EPS) * g_row


def _tail_kernel(x_ref, oat_ref, obt_ref, o1_ref, m1_ref, l1_ref, o2_ref, m2_ref, l2_ref, o3_ref, m3_ref, l3_ref,
                 od_ref, gg_ref, wo_ref, gm_ref, wup_ref, wdn_ref, out_ref):
    gg = gg_ref[...]

    def col_norm_t(yt, g_row):
        y = (yt * lax.rsqrt(jnp.mean(yt * yt, axis=0, keepdims=True) + EPS)).T
        return y * g_row

    ga = col_norm_t(oat_ref[0], gg[0:1, :])
    gb = col_norm_t(obt_ref[0], gg[1:2, :])
    m1, m2, m3 = m1_ref[0], m2_ref[0], m3_ref[0]
    mm = jnp.maximum(jnp.maximum(m1, m2), m3)
    w1, w2, w3 = jnp.exp(m1 - mm), jnp.exp(m2 - mm), jnp.exp(m3 - mm)
    num = w1 * o1_ref[0] + w2 * o2_ref[0] + w3 * o3_ref[0]
    den = w1 * l1_ref[0] + w2 * l2_ref[0] + w3 * l3_ref[0]
    gc = _row_norm(num / den, gg[2:3, :])
    gd = _row_norm(od_ref[0], gg[3:4, :])
    mixed = jnp.concatenate([ga, gb, gc, gd], axis=-1).astype(BF16)
    x1 = x_ref[0] + jnp.dot(mixed, wo_ref[...], preferred_element_type=F32)
    xn = _row_norm(x1, gm_ref[...]).astype(BF16)
    u = jnp.maximum(jnp.dot(xn, wup_ref[...], preferred_element_type=F32), 0.0)
    out_ref[0] = x1 + jnp.dot((u * u).astype(BF16), wdn_ref[...], preferred_element_type=F32)


def _tail(x, oat, obt, dil, od, gg, wo, gm, wup, wdn):
    B, S, _ = x.shape
    tm = TAIL_TM
    nat = lambda w: pl.BlockSpec((1, tm, w), lambda b, t: (b, t, 0))
    ft = pl.BlockSpec((1, GROUP, tm), lambda b, t: (b, 0, t))
    const = lambda a: pl.BlockSpec(a.shape, lambda b, t: (0,) * a.ndim, pipeline_mode=pl.Buffered(1))
    dil_flat = [a for br in dil for a in br]
    return pl.pallas_call(
        _tail_kernel,
        grid=(B, S // tm),
        in_specs=[nat(D_MODEL), ft, ft] + [nat(GROUP)] * 9 + [nat(GROUP), const(gg), const(wo), const(gm),
                                                             const(wup), const(wdn)],
        out_specs=nat(D_MODEL),
        out_shape=jax.ShapeDtypeStruct((B, S, D_MODEL), F32),
        compiler_params=_params(2),
        name="tail",
    )(x, oat, obt, *dil_flat, od, gg, wo, gm, wup, wdn)


def _rope_tables_t(S):
    inv = 1.0 / (ROPE_THETA ** (jnp.arange(0, MLA_ROPE, 2, dtype=F32) / MLA_ROPE))
    ang = inv[:, None] * jnp.arange(S, dtype=F32)[None, :]
    return jnp.cos(ang), jnp.sin(ang)


def _layer(x, cos_t, sin_t, attn_norm_g, w_in, moba_q_g, moba_k_g, mla_qlat_g, mla_kvlat_g, mla_w_uq, mla_w_ukv,
           mla_q_g, mla_k_g, dil_q_g, dil_k_g, swa_q_g, swa_k_g, swa_sinks, group_out_g, w_o, mlp_norm_g,
           w_up, w_down):
    B, S, _ = x.shape
    col = lambda g: g.reshape(-1, 1)
    (aqt, aq32t, ak, ak32, avt, bqt, bk, bvt, cq, ck, cv, dq, dk, dv) = _inproj(
        x, attn_norm_g.reshape(1, -1), w_in.T.astype(BF16), col(moba_q_g), col(moba_k_g), col(mla_qlat_g),
        col(mla_kvlat_g), mla_w_uq.T.astype(BF16), mla_w_ukv.T.astype(BF16), col(mla_q_g), col(mla_k_g),
        col(dil_q_g), col(dil_k_g), col(swa_q_g), col(swa_k_g), cos_t, sin_t)
    kmean = _kmean(ak32)
    nb = S // MOBA_BLOCK
    kmean_h = kmean.reshape(B, nb, N_HEADS, HEAD_DIM).transpose(0, 2, 1, 3)
    oat = _moba(aqt, aq32t, ak, avt, kmean_h).reshape(B, GROUP, S)
    obt = _mla(bqt, bk, bvt).reshape(B, GROUP, S)
    dil = [_dilated_branch(cq, ck, cv, w, d) for (w, d) in DILATED_BRANCHES]
    od = _swa(dq, dk, dv, swa_sinks)
    return _tail(x, oat, obt, dil, od, group_out_g, w_o.astype(BF16), mlp_norm_g.reshape(1, -1),
                 w_up.astype(BF16), w_down.astype(BF16))


def kernel(x, attn_norm_g, w_in, moba_q_g, moba_k_g, mla_qlat_g, mla_kvlat_g, mla_w_uq, mla_w_ukv, mla_q_g,
           mla_k_g, dil_q_g, dil_k_g, swa_q_g, swa_k_g, swa_sinks, group_out_g, w_o, mlp_norm_g, w_up, w_down):
    S = x.shape[1]
    assert S % max(d * Q_BLOCK for _, d in DILATED_BRANCHES) == 0 and S % INPROJ_TM == 0
    cos_t, sin_t = _rope_tables_t(S)
    params = (attn_norm_g, w_in, moba_q_g, moba_k_g, mla_qlat_g, mla_kvlat_g, mla_w_uq, mla_w_ukv, mla_q_g,
              mla_k_g, dil_q_g, dil_k_g, swa_q_g, swa_k_g, swa_sinks, group_out_g, w_o, mlp_norm_g, w_up, w_down)
    for l in range(attn_norm_g.shape[0]):
        x = _layer(x, cos_t, sin_t, *[p[l] for p in params])
    return x
```

```python
import functools

import numpy as np
import jax
import jax.numpy as jnp
from jax import lax
from jax.experimental import pallas as pl
from jax.experimental.pallas import tpu as pltpu

F32 = jnp.float32
BF16 = jnp.bfloat16

D_MODEL = 1024
HEAD_DIM = 64
N_HEADS = 4
GROUP = N_HEADS * HEAD_DIM
MOBA_BLOCK = 256
MOBA_TOPK = 3
MLA_Q_RANK = 256
MLA_KV_RANK = 128
MLA_NOPE = 64
MLA_ROPE = 32
MLA_QK = MLA_NOPE + MLA_ROPE
MLA_QK_PAD = 128
ROPE_THETA = 10000.0
DILATED_BRANCHES = ((128, 1), (512, 4), (2048, 16))
Q_BLOCK = 128
SWA_WINDOW = 128
SWA_KV_HEADS = 2
SWA_KV_WIDTH = SWA_KV_HEADS * HEAD_DIM
D_FF = 4 * D_MODEL
EPS = 1e-6
NEG = -1e30

_WIDTHS = (GROUP, GROUP, GROUP, MLA_Q_RANK, MLA_KV_RANK, MLA_ROPE,
           GROUP, GROUP, GROUP, GROUP, SWA_KV_WIDTH, SWA_KV_WIDTH)
_OFFS = tuple(int(v) for v in np.cumsum((0,) + _WIDTHS))
IN_COLS = _OFFS[-1]
(_A_Q, _A_K, _A_V, _B_QL, _B_KVL, _B_KR, _C_Q, _C_K, _C_V, _D_Q, _D_K, _D_V) = _OFFS[:-1]

VMEM_LIMIT = 56 * 1024 * 1024

INPROJ_TM = 512
TAIL_TM = 256
ATT_TQ = 256
ATT_TK = 256

_NT = (((1,), (1,)), ((), ()))


def _alibi_slopes():
    n = 3 * N_HEADS
    idx = np.arange(1, n + 1, dtype=np.float32).reshape(N_HEADS, 3)
    s = np.exp2(-8.0 * idx / n).astype(np.float32)
    return s[:, 0], s[:, 1], s[:, 2]


SLOPE_A, SLOPE_C, SLOPE_D = _alibi_slopes()


def _params(n_axes):
    return pltpu.CompilerParams(dimension_semantics=("arbitrary",) * n_axes,
                                vmem_limit_bytes=VMEM_LIMIT)


def _head_norm_t(sec, g_col, n_heads, width):
    outs = []
    for h in range(n_heads):
        s = sec[h * width:(h + 1) * width, :]
        ms = jnp.sum(s * s, axis=0, keepdims=True) * (1.0 / width)
        outs.append(s * lax.rsqrt(ms + EPS) * g_col)
    return outs


def _inproj_kernel(x_ref, gx_ref, w1t_ref, gaq_ref, gak_ref, gql_ref, gkvl_ref, wuqt_ref, wukvt_ref,
                   gbq_ref, gbk_ref, gcq_ref, gck_ref, gdq_ref, gdk_ref, cos_ref, sin_ref,
                   aqt_ref, aq32t_ref, ak_ref, ak32_ref, avt_ref,
                   bqt_ref, bk_ref, bvt_ref,
                   cq_ref, ck_ref, cv_ref, dq_ref, dk_ref, dv_ref,
                   h_scr):
    tm = x_ref.shape[1]
    x = x_ref[0]
    ms = jnp.mean(x * x, axis=-1, keepdims=True)
    xn = (x * lax.rsqrt(ms + EPS) * gx_ref[...]).astype(BF16)
    h_scr[...] = lax.dot_general(w1t_ref[...], xn, _NT, preferred_element_type=F32)

    scale = HEAD_DIM ** -0.5

    qa = _head_norm_t(h_scr[_A_Q:_A_Q + GROUP, :], gaq_ref[...], N_HEADS, HEAD_DIM)
    for h in range(N_HEADS):
        aq32t_ref[0, h] = qa[h]
        aqt_ref[0, h] = (qa[h] * scale).astype(BF16)
    ka = jnp.concatenate(_head_norm_t(h_scr[_A_K:_A_K + GROUP, :], gak_ref[...], N_HEADS, HEAD_DIM), axis=0)
    ka_nat = ka.T
    ak32_ref[0] = ka_nat
    ka_bf = ka_nat.astype(BF16)
    for h in range(N_HEADS):
        ak_ref[0, h] = ka_bf[:, h * HEAD_DIM:(h + 1) * HEAD_DIM]
    for h in range(N_HEADS):
        for c in range(tm // ATT_TK):
            avt_ref[0, h, c] = h_scr[_A_V + h * HEAD_DIM:_A_V + (h + 1) * HEAD_DIM,
                                     c * ATT_TK:(c + 1) * ATT_TK].astype(BF16)

    cos = cos_ref[...]
    sin = sin_ref[...]
    half = MLA_ROPE // 2

    def rope_pad(t, sc):
        x1 = t[MLA_NOPE:MLA_NOPE + half, :]
        x2 = t[MLA_NOPE + half:MLA_QK, :]
        return jnp.concatenate([t[:MLA_NOPE, :] * sc, (x1 * cos - x2 * sin) * sc, (x1 * sin + x2 * cos) * sc,
                                jnp.zeros((MLA_QK_PAD - MLA_QK, tm), F32)], axis=0)

    ql = h_scr[_B_QL:_B_QL + MLA_Q_RANK, :]
    ql = ql * lax.rsqrt(jnp.sum(ql * ql, axis=0, keepdims=True) * (1.0 / MLA_Q_RANK) + EPS) * gql_ref[...]
    qb = jnp.dot(wuqt_ref[...], ql.astype(BF16), preferred_element_type=F32)
    qb = _head_norm_t(qb, gbq_ref[...], N_HEADS, MLA_QK)
    kvl = h_scr[_B_KVL:_B_KVL + MLA_KV_RANK, :]
    kvl = kvl * lax.rsqrt(jnp.sum(kvl * kvl, axis=0, keepdims=True) * (1.0 / MLA_KV_RANK) + EPS) * gkvl_ref[...]
    kvb = jnp.dot(wukvt_ref[...], kvl.astype(BF16), preferred_element_type=F32)
    kr = h_scr[_B_KR:_B_KR + MLA_ROPE, :]
    gbk = gbk_ref[...]
    for h in range(N_HEADS):
        bqt_ref[0, h] = rope_pad(qb[h], MLA_QK ** -0.5).astype(BF16)
        kh = jnp.concatenate([kvb[h * 2 * HEAD_DIM:h * 2 * HEAD_DIM + MLA_NOPE, :], kr], axis=0)
        kh = kh * lax.rsqrt(jnp.sum(kh * kh, axis=0, keepdims=True) * (1.0 / MLA_QK) + EPS) * gbk
        bk_ref[0, h] = rope_pad(kh, 1.0).T.astype(BF16)
        vh = kvb[h * 2 * HEAD_DIM + MLA_NOPE:(h + 1) * 2 * HEAD_DIM, :]
        for c in range(tm // ATT_TK):
            bvt_ref[0, h, c] = vh[:, c * ATT_TK:(c + 1) * ATT_TK].astype(BF16)

    qc = jnp.concatenate(_head_norm_t(h_scr[_C_Q:_C_Q + GROUP, :], gcq_ref[...], N_HEADS, HEAD_DIM), axis=0)
    cq_ref[0] = (qc * scale).T.astype(BF16)
    kc = jnp.concatenate(_head_norm_t(h_scr[_C_K:_C_K + GROUP, :], gck_ref[...], N_HEADS, HEAD_DIM), axis=0)
    ck_ref[0] = kc.T.astype(BF16)
    cv_ref[0] = h_scr[_C_V:_C_V + GROUP, :].T.astype(BF16)

    qd = jnp.concatenate(_head_norm_t(h_scr[_D_Q:_D_Q + GROUP, :], gdq_ref[...], N_HEADS, HEAD_DIM), axis=0)
    dq_ref[0] = (qd * scale).T.astype(BF16)
    kd = jnp.concatenate(_head_norm_t(h_scr[_D_K:_D_K + SWA_KV_WIDTH, :], gdk_ref[...], SWA_KV_HEADS, HEAD_DIM),
                         axis=0)
    dk_ref[0] = kd.T.astype(BF16)
    dv_ref[0] = h_scr[_D_V:_D_V + SWA_KV_WIDTH, :].T.astype(BF16)


def _inproj(x, gx, w1t, gaq, gak, gql, gkvl, wuqt, wukvt, gbq, gbk, gcq, gck, gdq, gdk, cos_t, sin_t):
    B, S, _ = x.shape
    tm = INPROJ_TM
    nb = S // ATT_TK
    cpt = tm // ATT_TK
    H = N_HEADS
    full = lambda a: pl.BlockSpec(a.shape, lambda b, t: (0,) * a.ndim)
    in_specs = [pl.BlockSpec((1, tm, D_MODEL), lambda b, t: (b, t, 0)), full(gx), full(w1t), full(gaq), full(gak),
                full(gql), full(gkvl), full(wuqt), full(wukvt), full(gbq), full(gbk), full(gcq), full(gck),
                full(gdq), full(gdk),
                pl.BlockSpec((MLA_ROPE // 2, tm), lambda b, t: (0, t)),
                pl.BlockSpec((MLA_ROPE // 2, tm), lambda b, t: (0, t))]
    head_t = lambda w: pl.BlockSpec((1, H, w, tm), lambda b, t: (b, 0, 0, t))
    vt_spec = pl.BlockSpec((1, H, cpt, HEAD_DIM, ATT_TK), lambda b, t: (b, 0, t, 0, 0))
    nat = lambda w: pl.BlockSpec((1, tm, w), lambda b, t: (b, t, 0))
    out_shape = [
        jax.ShapeDtypeStruct((B, H, HEAD_DIM, S), BF16),
        jax.ShapeDtypeStruct((B, H, HEAD_DIM, S), F32),
        jax.ShapeDtypeStruct((B, H, S, HEAD_DIM), BF16),
        jax.ShapeDtypeStruct((B, S, GROUP), F32),
        jax.ShapeDtypeStruct((B, H, nb, HEAD_DIM, ATT_TK), BF16),
        jax.ShapeDtypeStruct((B, H, MLA_QK_PAD, S), BF16),
        jax.ShapeDtypeStruct((B, H, S, MLA_QK_PAD), BF16),
        jax.ShapeDtypeStruct((B, H, nb, HEAD_DIM, ATT_TK), BF16),
        jax.ShapeDtypeStruct((B, S, GROUP), BF16),
        jax.ShapeDtypeStruct((B, S, GROUP), BF16),
        jax.ShapeDtypeStruct((B, S, GROUP), BF16),
        jax.ShapeDtypeStruct((B, S, GROUP), BF16),
        jax.ShapeDtypeStruct((B, S, SWA_KV_WIDTH), BF16),
        jax.ShapeDtypeStruct((B, S, SWA_KV_WIDTH), BF16),
    ]
    out_specs = [head_t(HEAD_DIM), head_t(HEAD_DIM),
                 pl.BlockSpec((1, H, tm, HEAD_DIM), lambda b, t: (b, 0, t, 0)),
                 nat(GROUP), vt_spec,
                 head_t(MLA_QK_PAD),
                 pl.BlockSpec((1, H, tm, MLA_QK_PAD), lambda b, t: (b, 0, t, 0)),
                 vt_spec,
                 nat(GROUP), nat(GROUP), nat(GROUP), nat(GROUP), nat(SWA_KV_WIDTH), nat(SWA_KV_WIDTH)]
    return pl.pallas_call(
        _inproj_kernel,
        grid=(B, S // tm),
        in_specs=in_specs,
        out_specs=out_specs,
        out_shape=out_shape,
        scratch_shapes=[pltpu.VMEM((IN_COLS, tm), F32)],
        compiler_params=_params(2),
        name="inproj",
    )(x, gx, w1t, gaq, gak, gql, gkvl, wuqt, wukvt, gbq, gbk, gcq, gck, gdq, gdk, cos_t, sin_t)


def _kmean_kernel(k_ref, o_ref):
    S = k_ref.shape[1]
    k = k_ref[0].reshape(S // MOBA_BLOCK, MOBA_BLOCK, GROUP)
    o_ref[0] = jnp.sum(k, axis=1) * (1.0 / MOBA_BLOCK)


def _kmean(ak32):
    B, S, _ = ak32.shape
    nb = S // MOBA_BLOCK
    return pl.pallas_call(
        _kmean_kernel,
        grid=(B,),
        in_specs=[pl.BlockSpec((1, S, GROUP), lambda b: (b, 0, 0))],
        out_specs=pl.BlockSpec((1, nb, GROUP), lambda b: (b, 0, 0)),
        out_shape=jax.ShapeDtypeStruct((B, nb, GROUP), F32),
        compiler_params=_params(1),
        name="kmean",
    )(ak32)


def _softmax_step(st, vt, carry):
    m, l, acc = carry
    m_new = jnp.maximum(m, jnp.max(st, axis=0, keepdims=True))
    alpha = jnp.exp(m - m_new)
    p = jnp.exp(st - m_new)
    l = alpha * l + jnp.sum(p, axis=0, keepdims=True)
    acc = alpha * acc + jnp.dot(vt, p.astype(BF16), preferred_element_type=F32)
    return m_new, l, acc


def _softmax_first(st, vt):
    m = jnp.max(st, axis=0, keepdims=True)
    p = jnp.exp(st - m)
    l = jnp.sum(p, axis=0, keepdims=True)
    acc = jnp.dot(vt, p.astype(BF16), preferred_element_type=F32)
    return m, l, acc


def _flash_scratch(tq):
    return [pltpu.VMEM((2, N_HEADS, ATT_TK, tq), F32), pltpu.VMEM((N_HEADS, HEAD_DIM, tq), F32)]


def _flash_heads(raw_scores, past_bias, diag_bias, vt_ref, o_ref, s_scr, acc_scr, i):
    tq = o_ref.shape[3]
    for h in range(N_HEADS):
        s_scr[0, h] = raw_scores(h, 0)
        acc_scr[h] = jnp.zeros((HEAD_DIM, tq), F32)

    def step(st, m, l):
        m_new = jnp.maximum(m, jnp.max(st, axis=0, keepdims=True))
        alpha = jnp.exp(m - m_new)
        p = jnp.exp(st - m_new)
        l_new = alpha * l + jnp.sum(p, axis=0, keepdims=True)
        return m_new, l_new, alpha, p.astype(BF16)

    def body(j, carry):
        cur = j % 2
        out = []
        for h in range(N_HEADS):
            m, l = carry[h]
            m, l, alpha, p = step(past_bias(h, j, s_scr[cur, h]), m, l)
            s_scr[1 - cur, h] = raw_scores(h, j + 1)
            acc_scr[h] = alpha * acc_scr[h] + jnp.dot(vt_ref[0, h, j], p, preferred_element_type=F32)
            out.append((m, l))
        return tuple(out)

    init = tuple((jnp.full((1, tq), NEG, F32), jnp.zeros((1, tq), F32)) for _ in range(N_HEADS))
    carry = lax.fori_loop(0, i, body, init)
    for h in range(N_HEADS):
        m, l = carry[h]
        m, l, alpha, p = step(diag_bias(h, s_scr[i % 2, h]), m, l)
        acc = alpha * acc_scr[h] + jnp.dot(vt_ref[0, h, i], p, preferred_element_type=F32)
        o_ref[0, h] = acc / l


def _moba_kernel(qt_ref, q32t_ref, k_ref, vt_ref, kmean_ref, o_ref, selb_scr, s_scr, acc_scr):
    i = pl.program_id(1)
    nb = kmean_ref.shape[2]
    tq = qt_ref.shape[3]
    blk = lax.broadcasted_iota(jnp.int32, (nb, tq), 0)
    blk_dist = ((i - blk) * ATT_TK).astype(F32)
    for h in range(N_HEADS):
        gate = jnp.dot(kmean_ref[0, h], q32t_ref[0, h], preferred_element_type=F32,
                       precision=lax.Precision.HIGHEST)
        gate = jnp.where(blk < i, gate, NEG)
        sel = jnp.zeros((nb, tq), jnp.bool_)
        for _ in range(MOBA_TOPK):
            best = jnp.max(gate, axis=0, keepdims=True)
            first = jnp.min(jnp.where(gate == best, blk, nb), axis=0, keepdims=True)
            pick = blk == first
            sel = jnp.logical_or(sel, pick)
            gate = jnp.where(pick, -jnp.inf, gate)
        sel = jnp.logical_and(sel, blk < i)
        selb_scr[h] = jnp.where(sel, -float(SLOPE_A[h]) * blk_dist, 2 * NEG)

    kj = lax.broadcasted_iota(jnp.int32, (ATT_TK, tq), 0)
    ti = lax.broadcasted_iota(jnp.int32, (ATT_TK, tq), 1)
    causal = ti >= kj
    local = (ti - kj).astype(F32)

    def raw_scores(h, j):
        kb = k_ref[0, h, pl.ds(pl.multiple_of(j * ATT_TK, ATT_TK), ATT_TK), :]
        return jnp.dot(kb, qt_ref[0, h], preferred_element_type=F32)

    def past_bias(h, j, s):
        return s - float(SLOPE_A[h]) * local + selb_scr[h, pl.ds(j, 1), :]

    def diag_bias(h, s):
        return jnp.where(causal, s - float(SLOPE_A[h]) * local, NEG)

    _flash_heads(raw_scores, past_bias, diag_bias, vt_ref, o_ref, s_scr, acc_scr, i)


def _moba(aqt, aq32t, ak, avt, kmean_h):
    B, H, _, S = aqt.shape
    tq = ATT_TQ
    nb = S // ATT_TK
    return pl.pallas_call(
        _moba_kernel,
        grid=(B, S // tq),
        in_specs=[pl.BlockSpec((1, H, HEAD_DIM, tq), lambda b, i: (b, 0, 0, i)),
                  pl.BlockSpec((1, H, HEAD_DIM, tq), lambda b, i: (b, 0, 0, i)),
                  pl.BlockSpec((1, H, S, HEAD_DIM), lambda b, i: (b, 0, 0, 0)),
                  pl.BlockSpec((1, H, nb, HEAD_DIM, ATT_TK), lambda b, i: (b, 0, 0, 0, 0)),
                  pl.BlockSpec((1, H, nb, HEAD_DIM), lambda b, i: (b, 0, 0, 0))],
        out_specs=pl.BlockSpec((1, H, HEAD_DIM, tq), lambda b, i: (b, 0, 0, i)),
        out_shape=jax.ShapeDtypeStruct((B, H, HEAD_DIM, S), F32),
        scratch_shapes=[pltpu.VMEM((H, nb, tq), F32)] + _flash_scratch(tq),
        compiler_params=_params(2),
        name="moba",
    )(aqt, aq32t, ak, avt, kmean_h)


def _mla_kernel(qt_ref, k_ref, vt_ref, o_ref, s_scr, acc_scr):
    i = pl.program_id(1)
    tq = qt_ref.shape[3]

    def raw_scores(h, j):
        kb = k_ref[0, h, pl.ds(pl.multiple_of(j * ATT_TK, ATT_TK), ATT_TK), :]
        return jnp.dot(kb, qt_ref[0, h], preferred_element_type=F32)

    kj = lax.broadcasted_iota(jnp.int32, (ATT_TK, tq), 0)
    ti = lax.broadcasted_iota(jnp.int32, (ATT_TK, tq), 1)
    causal = ti >= kj
    _flash_heads(raw_scores, lambda h, j, s: s, lambda h, s: jnp.where(causal, s, NEG),
                 vt_ref, o_ref, s_scr, acc_scr, i)


def _mla(bqt, bk, bvt):
    B, H, _, S = bqt.shape
    tq = ATT_TQ
    nb = S // ATT_TK
    return pl.pallas_call(
        _mla_kernel,
        grid=(B, S // tq),
        in_specs=[pl.BlockSpec((1, H, MLA_QK_PAD, tq), lambda b, i: (b, 0, 0, i)),
                  pl.BlockSpec((1, H, S, MLA_QK_PAD), lambda b, i: (b, 0, 0, 0)),
                  pl.BlockSpec((1, H, nb, HEAD_DIM, ATT_TK), lambda b, i: (b, 0, 0, 0, 0))],
        out_specs=pl.BlockSpec((1, H, HEAD_DIM, tq), lambda b, i: (b, 0, 0, i)),
        out_shape=jax.ShapeDtypeStruct((B, H, HEAD_DIM, S), F32),
        scratch_shapes=_flash_scratch(tq),
        compiler_params=_params(2),
        name="mla",
    )(bqt, bk, bvt)


def _dilated_kernel(q_ref, kp_ref, kc_ref, vp_ref, vc_ref, o_ref, m_ref, l_ref, *, dilation, steps):
    n = pl.program_id(2)
    q = q_ref[0]
    k2 = jnp.concatenate([kp_ref[0], kc_ref[0]], axis=0)
    v2 = jnp.concatenate([vp_ref[0], vc_ref[0]], axis=0)
    qi = lax.broadcasted_iota(jnp.int32, (Q_BLOCK, 2 * Q_BLOCK), 0)
    kidx = lax.broadcasted_iota(jnp.int32, (Q_BLOCK, 2 * Q_BLOCK), 1)
    rel = qi + Q_BLOCK - kidx
    valid = (rel >= 0) & (rel <= steps) & ((n > 0) | (kidx >= Q_BLOCK))
    dist = (dilation * rel).astype(F32)
    os, ms, ls = [], [], []
    for h in range(N_HEADS):
        sl = slice(h * HEAD_DIM, (h + 1) * HEAD_DIM)
        s = lax.dot_general(q[:, sl], k2[:, sl], _NT, preferred_element_type=F32)
        s = jnp.where(valid, s - float(SLOPE_C[h]) * dist, NEG)
        m = jnp.max(s, axis=-1, keepdims=True)
        e = jnp.exp(s - m)
        l = jnp.sum(e, axis=-1, keepdims=True)
        os.append(jnp.dot(e.astype(BF16), v2[:, sl], preferred_element_type=F32))
        ms.append(jnp.broadcast_to(m, (Q_BLOCK, HEAD_DIM)))
        ls.append(jnp.broadcast_to(l, (Q_BLOCK, HEAD_DIM)))
    o_ref[0] = jnp.concatenate(os, axis=-1)
    m_ref[0] = jnp.concatenate(ms, axis=-1)
    l_ref[0] = jnp.concatenate(ls, axis=-1)


def _dilated_branch(cq, ck, cv, window, dilation):
    B, S, _ = cq.shape
    L = S // dilation
    view = lambda a: a.reshape(B, L, dilation * GROUP)
    cur = pl.BlockSpec((1, Q_BLOCK, GROUP), lambda b, r, n: (b, n, r))
    prev = pl.BlockSpec((1, Q_BLOCK, GROUP), lambda b, r, n: (b, jnp.maximum(n - 1, 0), r))
    shp = jax.ShapeDtypeStruct((B, L, dilation * GROUP), F32)
    o, m, l = pl.pallas_call(
        functools.partial(_dilated_kernel, dilation=dilation, steps=window // dilation),
        grid=(B, dilation, L // Q_BLOCK),
        in_specs=[cur, prev, cur, prev, cur],
        out_specs=[cur, cur, cur],
        out_shape=[shp, shp, shp],
        compiler_params=_params(3),
        name=f"dilated{dilation}",
    )(view(cq), view(ck), view(ck), view(cv), view(cv))
    back = lambda a: a.reshape(B, S, GROUP)
    return back(o), back(m), back(l)


def _swa_kernel(sink_ref, q_ref, k_ref, v_ref, o_ref):
    i = pl.program_id(1)
    G = N_HEADS // SWA_KV_HEADS
    start = pl.multiple_of(jnp.maximum(i - 1, 0) * Q_BLOCK, Q_BLOCK)
    q = q_ref[0]
    kw = k_ref[0, pl.ds(start, 2 * Q_BLOCK), :]
    vw = v_ref[0, pl.ds(start, 2 * Q_BLOCK), :]
    tpos = i * Q_BLOCK + lax.broadcasted_iota(jnp.int32, (Q_BLOCK, 2 * Q_BLOCK), 0)
    kpos = start + lax.broadcasted_iota(jnp.int32, (Q_BLOCK, 2 * Q_BLOCK), 1)
    rel = tpos - kpos
    valid = (rel >= 0) & (rel < SWA_WINDOW)
    dist = rel.astype(F32)
    outs = []
    for h in range(N_HEADS):
        g = h // G
        ksl = slice(g * HEAD_DIM, (g + 1) * HEAD_DIM)
        s = lax.dot_general(q[:, h * HEAD_DIM:(h + 1) * HEAD_DIM], kw[:, ksl], _NT, preferred_element_type=F32)
        s = jnp.where(valid, s - float(SLOPE_D[h]) * dist, NEG)
        sink = sink_ref[h]
        m = jnp.maximum(jnp.max(s, axis=-1, keepdims=True), sink)
        p = jnp.exp(s - m)
        l = jnp.sum(p, axis=-1, keepdims=True) + jnp.exp(sink - m)
        outs.append(jnp.dot(p.astype(BF16), vw[:, ksl], preferred_element_type=F32) / l)
    o_ref[0] = jnp.concatenate(outs, axis=-1)


def _swa(dq, dk, dv, sinks):
    B, S, _ = dq.shape
    return pl.pallas_call(
        _swa_kernel,
        grid=(B, S // Q_BLOCK),
        in_specs=[pl.BlockSpec(memory_space=pltpu.SMEM),
                  pl.BlockSpec((1, Q_BLOCK, GROUP), lambda b, i: (b, i, 0)),
                  pl.BlockSpec((1, S, SWA_KV_WIDTH), lambda b, i: (b, 0, 0)),
                  pl.BlockSpec((1, S, SWA_KV_WIDTH), lambda b, i: (b, 0, 0))],
        out_specs=pl.BlockSpec((1, Q_BLOCK, GROUP), lambda b, i: (b, i, 0)),
        out_shape=jax.ShapeDtypeStruct((B, S, GROUP), F32),
        compiler_params=_params(2),
        name="swa",
    )(sinks, dq, dk, dv)


def _row_norm(y, g_row):
    return y * lax.rsqrt(jnp.mean(y * y, axis=-1, keepdims=True) + EPS) * g_row


def _tail_kernel(x_ref, oat_ref, obt_ref, o1_ref, m1_ref, l1_ref, o2_ref, m2_ref, l2_ref, o3_ref, m3_ref, l3_ref,
                 od_ref, gg_ref, wo_ref, gm_ref, wup_ref, wdn_ref, out_ref):
    gg = gg_ref[...]

    def col_norm_t(yt, g_row):
        y = (yt * lax.rsqrt(jnp.mean(yt * yt, axis=0, keepdims=True) + EPS)).T
        return y * g_row

    ga = col_norm_t(oat_ref[0], gg[0:1, :])
    gb = col_norm_t(obt_ref[0], gg[1:2, :])
    m1, m2, m3 = m1_ref[0], m2_ref[0], m3_ref[0]
    mm = jnp.maximum(jnp.maximum(m1, m2), m3)
    w1, w2, w3 = jnp.exp(m1 - mm), jnp.exp(m2 - mm), jnp.exp(m3 - mm)
    num = w1 * o1_ref[0] + w2 * o2_ref[0] + w3 * o3_ref[0]
    den = w1 * l1_ref[0] + w2 * l2_ref[0] + w3 * l3_ref[0]
    gc = _row_norm(num / den, gg[2:3, :])
    gd = _row_norm(od_ref[0], gg[3:4, :])
    mixed = jnp.concatenate([ga, gb, gc, gd], axis=-1).astype(BF16)
    x1 = x_ref[0] + jnp.dot(mixed, wo_ref[...], preferred_element_type=F32)
    xn = _row_norm(x1, gm_ref[...]).astype(BF16)
    u = jnp.maximum(jnp.dot(xn, wup_ref[...], preferred_element_type=F32), 0.0)
    out_ref[0] = x1 + jnp.dot((u * u).astype(BF16), wdn_ref[...], preferred_element_type=F32)


def _tail(x, oat, obt, dil, od, gg, wo, gm, wup, wdn):
    B, S, _ = x.shape
    tm = TAIL_TM
    nat = lambda w: pl.BlockSpec((1, tm, w), lambda b, t: (b, t, 0))
    ft = pl.BlockSpec((1, GROUP, tm), lambda b, t: (b, 0, t))
    const = lambda a: pl.BlockSpec(a.shape, lambda b, t: (0,) * a.ndim, pipeline_mode=pl.Buffered(1))
    dil_flat = [a for br in dil for a in br]
    return pl.pallas_call(
        _tail_kernel,
        grid=(B, S // tm),
        in_specs=[nat(D_MODEL), ft, ft] + [nat(GROUP)] * 9 + [nat(GROUP), const(gg), const(wo), const(gm),
                                                             const(wup), const(wdn)],
        out_specs=nat(D_MODEL),
        out_shape=jax.ShapeDtypeStruct((B, S, D_MODEL), F32),
        compiler_params=_params(2),
        name="tail",
    )(x, oat, obt, *dil_flat, od, gg, wo, gm, wup, wdn)


def _rope_tables_t(S):
    inv = 1.0 / (ROPE_THETA ** (jnp.arange(0, MLA_ROPE, 2, dtype=F32) / MLA_ROPE))
    ang = inv[:, None] * jnp.arange(S, dtype=F32)[None, :]
    return jnp.cos(ang), jnp.sin(ang)


def _layer(x, cos_t, sin_t, attn_norm_g, w_in, moba_q_g, moba_k_g, mla_qlat_g, mla_kvlat_g, mla_w_uq, mla_w_ukv,
           mla_q_g, mla_k_g, dil_q_g, dil_k_g, swa_q_g, swa_k_g, swa_sinks, group_out_g, w_o, mlp_norm_g,
           w_up, w_down):
    B, S, _ = x.shape
    col = lambda g: g.reshape(-1, 1)
    (aqt, aq32t, ak, ak32, avt, bqt, bk, bvt, cq, ck, cv, dq, dk, dv) = _inproj(
        x, attn_norm_g.reshape(1, -1), w_in.T.astype(BF16), col(moba_q_g), col(moba_k_g), col(mla_qlat_g),
        col(mla_kvlat_g), mla_w_uq.T.astype(BF16), mla_w_ukv.T.astype(BF16), col(mla_q_g), col(mla_k_g),
        col(dil_q_g), col(dil_k_g), col(swa_q_g), col(swa_k_g), cos_t, sin_t)
    kmean = _kmean(ak32)
    nb = S // MOBA_BLOCK
    kmean_h = kmean.reshape(B, nb, N_HEADS, HEAD_DIM).transpose(0, 2, 1, 3)
    oat = _moba(aqt, aq32t, ak, avt, kmean_h).reshape(B, GROUP, S)
    obt = _mla(bqt, bk, bvt).reshape(B, GROUP, S)
    dil = [_dilated_branch(cq, ck, cv, w, d) for (w, d) in DILATED_BRANCHES]
    od = _swa(dq, dk, dv, swa_sinks)
    return _tail(x, oat, obt, dil, od, group_out_g, w_o.astype(BF16), mlp_norm_g.reshape(1, -1),
                 w_up.astype(BF16), w_down.astype(BF16))


def kernel(x, attn_norm_g, w_in, moba_q_g, moba_k_g, mla_qlat_g, mla_kvlat_g, mla_w_uq, mla_w_ukv, mla_q_g,
           mla_k_g, dil_q_g, dil_k_g, swa_q_g, swa_k_g, swa_sinks, group_out_g, w_o, mlp_norm_g, w_up, w_down):
    S = x.shape[1]
    assert S % max(d * Q_BLOCK for _, d in DILATED_BRANCHES) == 0 and S % INPROJ_TM == 0
    cos_t, sin_t = _rope_tables_t(S)
    params = (attn_norm_g, w_in, moba_q_g, moba_k_g, mla_qlat_g, mla_kvlat_g, mla_w_uq, mla_w_ukv, mla_q_g,
              mla_k_g, dil_q_g, dil_k_g, swa_q_g, swa_k_g, swa_sinks, group_out_g, w_o, mlp_norm_g, w_up, w_down)
    for l in range(attn_norm_g.shape[0]):
        x = _layer(x, cos_t, sin_t, *[p[l] for p in params])
    return x
```

```python
import functools

import numpy as np
import jax
import jax.numpy as jnp
from jax import lax
from jax.experimental import pallas as pl
from jax.experimental.pallas import tpu as pltpu

F32 = jnp.float32
BF16 = jnp.bfloat16

D_MODEL = 1024
HEAD_DIM = 64
N_HEADS = 4
GROUP = N_HEADS * HEAD_DIM
LANES = 128
N_HALF = GROUP // LANES
MOBA_BLOCK = 256
MOBA_TOPK = 3
MLA_Q_RANK = 256
MLA_KV_RANK = 128
MLA_NOPE = 64
MLA_ROPE = 32
MLA_QK = MLA_NOPE + MLA_ROPE
MLA_QK_PAD = 128
ROPE_THETA = 10000.0
DILATED_BRANCHES = ((128, 1), (512, 4), (2048, 16))
Q_BLOCK = 128
SWA_WINDOW = 128
SWA_KV_HEADS = 2
SWA_KV_WIDTH = SWA_KV_HEADS * HEAD_DIM
D_FF = 4 * D_MODEL
EPS = 1e-6
NEG = -1e30

_WIDTHS = (GROUP, GROUP, GROUP, MLA_Q_RANK, MLA_KV_RANK, MLA_ROPE,
           GROUP, GROUP, GROUP, GROUP, SWA_KV_WIDTH, SWA_KV_WIDTH)
_OFFS = tuple(int(v) for v in np.cumsum((0,) + _WIDTHS))
IN_COLS = _OFFS[-1]
(_A_Q, _A_K, _A_V, _B_QL, _B_KVL, _B_KR, _C_Q, _C_K, _C_V, _D_Q, _D_K, _D_V) = _OFFS[:-1]

VMEM_LIMIT = 56 * 1024 * 1024

INPROJ_TM = 512
TAIL_TM = 256
ATT_TQ = 256
ATT_TK = 256

_NT = (((1,), (1,)), ((), ()))


def _alibi_slopes():
    n = 3 * N_HEADS
    idx = np.arange(1, n + 1, dtype=np.float32).reshape(N_HEADS, 3)
    s = np.exp2(-8.0 * idx / n).astype(np.float32)
    return s[:, 0], s[:, 1], s[:, 2]


SLOPE_A, SLOPE_C, SLOPE_D = _alibi_slopes()


def _params(n_axes):
    return pltpu.CompilerParams(dimension_semantics=("arbitrary",) * n_axes,
                                vmem_limit_bytes=VMEM_LIMIT)


def _head_norm_t(sec, g_col, n_heads, width):
    outs = []
    for h in range(n_heads):
        s = sec[h * width:(h + 1) * width, :]
        ms = jnp.sum(s * s, axis=0, keepdims=True) * (1.0 / width)
        outs.append(s * lax.rsqrt(ms + EPS) * g_col)
    return outs


def _inproj_kernel(x_ref, gx_ref, w1t_ref, gaq_ref, gak_ref, gql_ref, gkvl_ref, wuqt_ref, wukvt_ref,
                   gbq_ref, gbk_ref, gcq_ref, gck_ref, gdq_ref, gdk_ref, cos_ref, sin_ref,
                   aqt_ref, aq32t_ref, ak_ref, ak32_ref, avt_ref,
                   bqt_ref, bk_ref, bvt_ref,
                   cq_ref, ck_ref, cv_ref, dq_ref, dk_ref, dv_ref,
                   h_scr):
    tm = x_ref.shape[1]
    x = x_ref[0]
    ms = jnp.mean(x * x, axis=-1, keepdims=True)
    xn = (x * lax.rsqrt(ms + EPS) * gx_ref[...]).astype(BF16)
    h_scr[...] = lax.dot_general(w1t_ref[...], xn, _NT, preferred_element_type=F32)

    scale = HEAD_DIM ** -0.5

    qa = _head_norm_t(h_scr[_A_Q:_A_Q + GROUP, :], gaq_ref[...], N_HEADS, HEAD_DIM)
    for h in range(N_HEADS):
        aq32t_ref[0, h] = qa[h]
        aqt_ref[0, h] = (qa[h] * scale).astype(BF16)
    ka = jnp.concatenate(_head_norm_t(h_scr[_A_K:_A_K + GROUP, :], gak_ref[...], N_HEADS, HEAD_DIM), axis=0)
    ka_nat = ka.T
    ak32_ref[0] = ka_nat
    ka_bf = ka_nat.astype(BF16)
    for h in range(N_HEADS):
        ak_ref[0, h] = ka_bf[:, h * HEAD_DIM:(h + 1) * HEAD_DIM]
    for h in range(N_HEADS):
        for c in range(tm // ATT_TK):
            avt_ref[0, h, c] = h_scr[_A_V + h * HEAD_DIM:_A_V + (h + 1) * HEAD_DIM,
                                     c * ATT_TK:(c + 1) * ATT_TK].astype(BF16)

    cos = cos_ref[...]
    sin = sin_ref[...]
    half = MLA_ROPE // 2

    def rope_pad(t, sc):
        x1 = t[MLA_NOPE:MLA_NOPE + half, :]
        x2 = t[MLA_NOPE + half:MLA_QK, :]
        return jnp.concatenate([t[:MLA_NOPE, :] * sc, (x1 * cos - x2 * sin) * sc, (x1 * sin + x2 * cos) * sc,
                                jnp.zeros((MLA_QK_PAD - MLA_QK, tm), F32)], axis=0)

    ql = h_scr[_B_QL:_B_QL + MLA_Q_RANK, :]
    ql = ql * lax.rsqrt(jnp.sum(ql * ql, axis=0, keepdims=True) * (1.0 / MLA_Q_RANK) + EPS) * gql_ref[...]
    qb = jnp.dot(wuqt_ref[...], ql.astype(BF16), preferred_element_type=F32)
    qb = _head_norm_t(qb, gbq_ref[...], N_HEADS, MLA_QK)
    kvl = h_scr[_B_KVL:_B_KVL + MLA_KV_RANK, :]
    kvl = kvl * lax.rsqrt(jnp.sum(kvl * kvl, axis=0, keepdims=True) * (1.0 / MLA_KV_RANK) + EPS) * gkvl_ref[...]
    kvb = jnp.dot(wukvt_ref[...], kvl.astype(BF16), preferred_element_type=F32)
    kr = h_scr[_B_KR:_B_KR + MLA_ROPE, :]
    gbk = gbk_ref[...]
    for h in range(N_HEADS):
        bqt_ref[0, h] = rope_pad(qb[h], MLA_QK ** -0.5).astype(BF16)
        kh = jnp.concatenate([kvb[h * 2 * HEAD_DIM:h * 2 * HEAD_DIM + MLA_NOPE, :], kr], axis=0)
        kh = kh * lax.rsqrt(jnp.sum(kh * kh, axis=0, keepdims=True) * (1.0 / MLA_QK) + EPS) * gbk
        bk_ref[0, h] = rope_pad(kh, 1.0).T.astype(BF16)
        vh = kvb[h * 2 * HEAD_DIM + MLA_NOPE:(h + 1) * 2 * HEAD_DIM, :]
        for c in range(tm // ATT_TK):
            bvt_ref[0, h, c] = vh[:, c * ATT_TK:(c + 1) * ATT_TK].astype(BF16)

    qc = jnp.concatenate(_head_norm_t(h_scr[_C_Q:_C_Q + GROUP, :], gcq_ref[...], N_HEADS, HEAD_DIM), axis=0)
    kc = jnp.concatenate(_head_norm_t(h_scr[_C_K:_C_K + GROUP, :], gck_ref[...], N_HEADS, HEAD_DIM), axis=0)
    for ref, val in ((cq_ref, qc * scale), (ck_ref, kc), (cv_ref, h_scr[_C_V:_C_V + GROUP, :])):
        for c in range(N_HALF):
            ref[0, c] = val[c * LANES:(c + 1) * LANES, :].T

    qd = jnp.concatenate(_head_norm_t(h_scr[_D_Q:_D_Q + GROUP, :], gdq_ref[...], N_HEADS, HEAD_DIM), axis=0)
    dq_ref[0] = (qd * scale).T.astype(BF16)
    kd = jnp.concatenate(_head_norm_t(h_scr[_D_K:_D_K + SWA_KV_WIDTH, :], gdk_ref[...], SWA_KV_HEADS, HEAD_DIM),
                         axis=0)
    dk_ref[0] = kd.T.astype(BF16)
    dv_ref[0] = h_scr[_D_V:_D_V + SWA_KV_WIDTH, :].T.astype(BF16)


def _inproj(x, gx, w1t, gaq, gak, gql, gkvl, wuqt, wukvt, gbq, gbk, gcq, gck, gdq, gdk, cos_t, sin_t):
    B, S, _ = x.shape
    tm = INPROJ_TM
    nb = S // ATT_TK
    cpt = tm // ATT_TK
    H = N_HEADS
    full = lambda a: pl.BlockSpec(a.shape, lambda b, t: (0,) * a.ndim)
    in_specs = [pl.BlockSpec((1, tm, D_MODEL), lambda b, t: (b, t, 0)), full(gx), full(w1t), full(gaq), full(gak),
                full(gql), full(gkvl), full(wuqt), full(wukvt), full(gbq), full(gbk), full(gcq), full(gck),
                full(gdq), full(gdk),
                pl.BlockSpec((MLA_ROPE // 2, tm), lambda b, t: (0, t)),
                pl.BlockSpec((MLA_ROPE // 2, tm), lambda b, t: (0, t))]
    head_t = lambda w: pl.BlockSpec((1, H, w, tm), lambda b, t: (b, 0, 0, t))
    vt_spec = pl.BlockSpec((1, H, cpt, HEAD_DIM, ATT_TK), lambda b, t: (b, 0, t, 0, 0))
    nat = lambda w: pl.BlockSpec((1, tm, w), lambda b, t: (b, t, 0))
    halves = pl.BlockSpec((1, N_HALF, tm, LANES), lambda b, t: (b, 0, t, 0))
    out_shape = [
        jax.ShapeDtypeStruct((B, H, HEAD_DIM, S), BF16),
        jax.ShapeDtypeStruct((B, H, HEAD_DIM, S), F32),
        jax.ShapeDtypeStruct((B, H, S, HEAD_DIM), BF16),
        jax.ShapeDtypeStruct((B, S, GROUP), F32),
        jax.ShapeDtypeStruct((B, H, nb, HEAD_DIM, ATT_TK), BF16),
        jax.ShapeDtypeStruct((B, H, MLA_QK_PAD, S), BF16),
        jax.ShapeDtypeStruct((B, H, S, MLA_QK_PAD), BF16),
        jax.ShapeDtypeStruct((B, H, nb, HEAD_DIM, ATT_TK), BF16),
        jax.ShapeDtypeStruct((B, N_HALF, S, LANES), F32),
        jax.ShapeDtypeStruct((B, N_HALF, S, LANES), F32),
        jax.ShapeDtypeStruct((B, N_HALF, S, LANES), F32),
        jax.ShapeDtypeStruct((B, S, GROUP), BF16),
        jax.ShapeDtypeStruct((B, S, SWA_KV_WIDTH), BF16),
        jax.ShapeDtypeStruct((B, S, SWA_KV_WIDTH), BF16),
    ]
    out_specs = [head_t(HEAD_DIM), head_t(HEAD_DIM),
                 pl.BlockSpec((1, H, tm, HEAD_DIM), lambda b, t: (b, 0, t, 0)),
                 nat(GROUP), vt_spec,
                 head_t(MLA_QK_PAD),
                 pl.BlockSpec((1, H, tm, MLA_QK_PAD), lambda b, t: (b, 0, t, 0)),
                 vt_spec,
                 halves, halves, halves, nat(GROUP), nat(SWA_KV_WIDTH), nat(SWA_KV_WIDTH)]
    return pl.pallas_call(
        _inproj_kernel,
        grid=(B, S // tm),
        in_specs=in_specs,
        out_specs=out_specs,
        out_shape=out_shape,
        scratch_shapes=[pltpu.VMEM((IN_COLS, tm), F32)],
        compiler_params=_params(2),
        name="inproj",
    )(x, gx, w1t, gaq, gak, gql, gkvl, wuqt, wukvt, gbq, gbk, gcq, gck, gdq, gdk, cos_t, sin_t)


def _kmean_kernel(k_ref, o_ref):
    S = k_ref.shape[1]
    k = k_ref[0].reshape(S // MOBA_BLOCK, MOBA_BLOCK, GROUP)
    o_ref[0] = jnp.sum(k, axis=1) * (1.0 / MOBA_BLOCK)


def _kmean(ak32):
    B, S, _ = ak32.shape
    nb = S // MOBA_BLOCK
    return pl.pallas_call(
        _kmean_kernel,
        grid=(B,),
        in_specs=[pl.BlockSpec((1, S, GROUP), lambda b: (b, 0, 0))],
        out_specs=pl.BlockSpec((1, nb, GROUP), lambda b: (b, 0, 0)),
        out_shape=jax.ShapeDtypeStruct((B, nb, GROUP), F32),
        compiler_params=_params(1),
        name="kmean",
    )(ak32)


def _flash_scratch(tq):
    return [pltpu.VMEM((2, N_HEADS, ATT_TK, tq), F32), pltpu.VMEM((N_HEADS, HEAD_DIM, tq), F32)]


def _flash_heads(raw_scores, past_bias, diag_bias, vt_ref, o_ref, s_scr, acc_scr, i):
    tq = o_ref.shape[3]
    for h in range(N_HEADS):
        s_scr[0, h] = raw_scores(h, 0)
        acc_scr[h] = jnp.zeros((HEAD_DIM, tq), F32)

    def step(st, m, l):
        m_new = jnp.maximum(m, jnp.max(st, axis=0, keepdims=True))
        alpha = jnp.exp(m - m_new)
        p = jnp.exp(st - m_new)
        l_new = alpha * l + jnp.sum(p, axis=0, keepdims=True)
        return m_new, l_new, alpha, p.astype(BF16)

    def body(j, carry):
        cur = j % 2
        out = []
        for h in range(N_HEADS):
            m, l = carry[h]
            m, l, alpha, p = step(past_bias(h, j, s_scr[cur, h]), m, l)
            s_scr[1 - cur, h] = raw_scores(h, j + 1)
            acc_scr[h] = alpha * acc_scr[h] + jnp.dot(vt_ref[0, h, j], p, preferred_element_type=F32)
            out.append((m, l))
        return tuple(out)

    init = tuple((jnp.full((1, tq), NEG, F32), jnp.zeros((1, tq), F32)) for _ in range(N_HEADS))
    carry = lax.fori_loop(0, i, body, init)
    for h in range(N_HEADS):
        m, l = carry[h]
        m, l, alpha, p = step(diag_bias(h, s_scr[i % 2, h]), m, l)
        acc = alpha * acc_scr[h] + jnp.dot(vt_ref[0, h, i], p, preferred_element_type=F32)
        o_ref[0, h] = acc / l


def _moba_kernel(qt_ref, q32t_ref, k_ref, vt_ref, kmean_ref, o_ref, selb_scr, s_scr, acc_scr):
    i = pl.program_id(1)
    nb = kmean_ref.shape[2]
    tq = qt_ref.shape[3]
    blk = lax.broadcasted_iota(jnp.int32, (nb, tq), 0)
    blk_dist = ((i - blk) * ATT_TK).astype(F32)
    for h in range(N_HEADS):
        gate = jnp.dot(kmean_ref[0, h], q32t_ref[0, h], preferred_element_type=F32,
                       precision=lax.Precision.HIGHEST)
        gate = jnp.where(blk < i, gate, NEG)
        sel = jnp.zeros((nb, tq), jnp.bool_)
        for _ in range(MOBA_TOPK):
            best = jnp.max(gate, axis=0, keepdims=True)
            first = jnp.min(jnp.where(gate == best, blk, nb), axis=0, keepdims=True)
            pick = blk == first
            sel = jnp.logical_or(sel, pick)
            gate = jnp.where(pick, -jnp.inf, gate)
        sel = jnp.logical_and(sel, blk < i)
        selb_scr[h] = jnp.where(sel, -float(SLOPE_A[h]) * blk_dist, 2 * NEG)

    kj = lax.broadcasted_iota(jnp.int32, (ATT_TK, tq), 0)
    ti = lax.broadcasted_iota(jnp.int32, (ATT_TK, tq), 1)
    causal = ti >= kj
    local = (ti - kj).astype(F32)

    def raw_scores(h, j):
        kb = k_ref[0, h, pl.ds(pl.multiple_of(j * ATT_TK, ATT_TK), ATT_TK), :]
        return jnp.dot(kb, qt_ref[0, h], preferred_element_type=F32)

    def past_bias(h, j, s):
        return s - float(SLOPE_A[h]) * local + selb_scr[h, pl.ds(j, 1), :]

    def diag_bias(h, s):
        return jnp.where(causal, s - float(SLOPE_A[h]) * local, NEG)

    _flash_heads(raw_scores, past_bias, diag_bias, vt_ref, o_ref, s_scr, acc_scr, i)


def _moba(aqt, aq32t, ak, avt, kmean_h):
    B, H, _, S = aqt.shape
    tq = ATT_TQ
    nb = S // ATT_TK
    return pl.pallas_call(
        _moba_kernel,
        grid=(B, S // tq),
        in_specs=[pl.BlockSpec((1, H, HEAD_DIM, tq), lambda b, i: (b, 0, 0, i)),
                  pl.BlockSpec((1, H, HEAD_DIM, tq), lambda b, i: (b, 0, 0, i)),
                  pl.BlockSpec((1, H, S, HEAD_DIM), lambda b, i: (b, 0, 0, 0)),
                  pl.BlockSpec((1, H, nb, HEAD_DIM, ATT_TK), lambda b, i: (b, 0, 0, 0, 0)),
                  pl.BlockSpec((1, H, nb, HEAD_DIM), lambda b, i: (b, 0, 0, 0))],
        out_specs=pl.BlockSpec((1, H, HEAD_DIM, tq), lambda b, i: (b, 0, 0, i)),
        out_shape=jax.ShapeDtypeStruct((B, H, HEAD_DIM, S), F32),
        scratch_shapes=[pltpu.VMEM((H, nb, tq), F32)] + _flash_scratch(tq),
        compiler_params=_params(2),
        name="moba",
    )(aqt, aq32t, ak, avt, kmean_h)


def _mla_kernel(qt_ref, k_ref, vt_ref, o_ref, s_scr, acc_scr):
    i = pl.program_id(1)
    tq = qt_ref.shape[3]

    def raw_scores(h, j):
        kb = k_ref[0, h, pl.ds(pl.multiple_of(j * ATT_TK, ATT_TK), ATT_TK), :]
        return jnp.dot(kb, qt_ref[0, h], preferred_element_type=F32)

    kj = lax.broadcasted_iota(jnp.int32, (ATT_TK, tq), 0)
    ti = lax.broadcasted_iota(jnp.int32, (ATT_TK, tq), 1)
    causal = ti >= kj
    _flash_heads(raw_scores, lambda h, j, s: s, lambda h, s: jnp.where(causal, s, NEG),
                 vt_ref, o_ref, s_scr, acc_scr, i)


def _mla(bqt, bk, bvt):
    B, H, _, S = bqt.shape
    tq = ATT_TQ
    nb = S // ATT_TK
    return pl.pallas_call(
        _mla_kernel,
        grid=(B, S // tq),
        in_specs=[pl.BlockSpec((1, H, MLA_QK_PAD, tq), lambda b, i: (b, 0, 0, i)),
                  pl.BlockSpec((1, H, S, MLA_QK_PAD), lambda b, i: (b, 0, 0, 0)),
                  pl.BlockSpec((1, H, nb, HEAD_DIM, ATT_TK), lambda b, i: (b, 0, 0, 0, 0))],
        out_specs=pl.BlockSpec((1, H, HEAD_DIM, tq), lambda b, i: (b, 0, 0, i)),
        out_shape=jax.ShapeDtypeStruct((B, H, HEAD_DIM, S), F32),
        scratch_shapes=_flash_scratch(tq),
        compiler_params=_params(2),
        name="mla",
    )(bqt, bk, bvt)


DIL_SPAN = max(d for _, d in DILATED_BRANCHES) * Q_BLOCK
DIL_UNITS = DIL_SPAN // Q_BLOCK
DIL_GROUP = 2


def _dilated_bias():
    qi = np.arange(Q_BLOCK)[:, None]
    kidx = np.arange(2 * Q_BLOCK)[None, :]
    rel = qi + Q_BLOCK - kidx
    out = np.empty((len(DILATED_BRANCHES), N_HEADS, Q_BLOCK, 2 * Q_BLOCK), np.float32)
    for bi, (window, d) in enumerate(DILATED_BRANCHES):
        valid = (rel >= 0) & (rel <= window // d)
        for h in range(N_HEADS):
            out[bi, h] = np.where(valid, -SLOPE_C[h] * (d * rel).astype(np.float32), NEG)
    return out


def _rows_load(ref, lead, start, size, stride):
    return jnp.concatenate([ref[lead + (c, pl.ds(start, size, stride=stride), slice(None))]
                            for c in range(N_HALF)], axis=-1)


def _rows_store(ref, lead, start, size, stride, val):
    for c in range(N_HALF):
        ref[lead + (c, pl.ds(start, size, stride=stride), slice(None))] = val[:, c * LANES:(c + 1) * LANES]


def _dilated_kernel(q_ref, kp_ref, kc_ref, vp_ref, vc_ref, bias_ref, o_ref, kbuf, vbuf, m_scr, den_scr, num_scr):
    span = pl.program_id(1)
    kbuf[:, 0:DIL_SPAN, :] = kp_ref[0]
    kbuf[:, DIL_SPAN:, :] = kc_ref[0]
    vbuf[:, 0:DIL_SPAN, :] = vp_ref[0]
    vbuf[:, DIL_SPAN:, :] = vc_ref[0]
    lane_head = lax.broadcasted_iota(jnp.int32, (1, GROUP), 1) // HEAD_DIM
    hmask = [lane_head == h for h in range(N_HEADS)]
    hmask_f = [m.astype(F32) for m in hmask]
    low_neg = jnp.where(lax.broadcasted_iota(jnp.int32, (Q_BLOCK, 2 * Q_BLOCK), 1) < Q_BLOCK, NEG, 0.0)
    first_span = (span == 0).astype(F32)

    def per_head(cols):
        out = cols[N_HEADS - 1]
        for h in range(N_HEADS - 2, -1, -1):
            out = jnp.where(hmask[h], cols[h], out)
        return out

    for bi, (window, d) in enumerate(DILATED_BRANCHES):
        last = bi == len(DILATED_BRANCHES) - 1

        def group(g, _, bi=bi, d=d, last=last):
            fronts = []
            for uu in range(DIL_GROUP):
                u = g * DIL_GROUP + uu
                r, n = u % d, u // d
                qstart = n * (Q_BLOCK * d) + r
                kstart = DIL_SPAN + qstart - Q_BLOCK * d
                q = _rows_load(q_ref, (0,), qstart, Q_BLOCK, d)
                k2 = _rows_load(kbuf, (), kstart, 2 * Q_BLOCK, d).astype(BF16)
                v2 = _rows_load(vbuf, (), kstart, 2 * Q_BLOCK, d).astype(BF16)
                q4 = jnp.concatenate([(q * hmask_f[h]).astype(BF16) for h in range(N_HEADS)], axis=0)
                s4 = lax.dot_general(q4, k2, _NT, preferred_element_type=F32)
                penalty = low_neg * (first_span * (n == 0).astype(F32))
                fronts.append((qstart, s4, v2, penalty))
            for qstart, s4, v2, penalty in fronts:
                es, ms, ls = [], [], []
                for h in range(N_HEADS):
                    s = s4[h * Q_BLOCK:(h + 1) * Q_BLOCK, :] + (bias_ref[bi, h] + penalty)
                    m = jnp.max(s, axis=-1, keepdims=True)
                    e = jnp.exp(s - m)
                    ls.append(jnp.sum(e, axis=-1, keepdims=True))
                    ms.append(m)
                    es.append(e.astype(BF16))
                o4 = jnp.dot(jnp.concatenate(es, axis=0), v2, preferred_element_type=F32)
                o = o4[(N_HEADS - 1) * Q_BLOCK:, :]
                for h in range(N_HEADS - 2, -1, -1):
                    o = jnp.where(hmask[h], o4[h * Q_BLOCK:(h + 1) * Q_BLOCK, :], o)
                m_b, l_b = per_head(ms), per_head(ls)
                at = ((), qstart, Q_BLOCK, d)
                if bi == 0:
                    _rows_store(m_scr, *at, m_b)
                    _rows_store(num_scr, *at, o)
                    _rows_store(den_scr, *at, l_b)
                else:
                    m_old = _rows_load(m_scr, *at)
                    m_new = jnp.maximum(m_old, m_b)
                    a, b = jnp.exp(m_old - m_new), jnp.exp(m_b - m_new)
                    num = a * _rows_load(num_scr, *at) + b * o
                    den = a * _rows_load(den_scr, *at) + b * l_b
                    if last:
                        _rows_store(o_ref, (0,), qstart, Q_BLOCK, d, num / den)
                    else:
                        _rows_store(m_scr, *at, m_new)
                        _rows_store(num_scr, *at, num)
                        _rows_store(den_scr, *at, den)
            return 0

        lax.fori_loop(0, DIL_UNITS // DIL_GROUP, group, 0)


def _dilated(cq, ck, cv):
    B, _, S, _ = cq.shape
    cur = pl.BlockSpec((1, N_HALF, DIL_SPAN, LANES), lambda b, s: (b, 0, s, 0))
    prev = pl.BlockSpec((1, N_HALF, DIL_SPAN, LANES), lambda b, s: (b, 0, jnp.maximum(s - 1, 0), 0))
    bias = jnp.asarray(_dilated_bias())
    return pl.pallas_call(
        _dilated_kernel,
        grid=(B, S // DIL_SPAN),
        in_specs=[cur, prev, cur, prev, cur, pl.BlockSpec(bias.shape, lambda b, s: (0, 0, 0, 0))],
        out_specs=cur,
        out_shape=jax.ShapeDtypeStruct((B, N_HALF, S, LANES), F32),
        scratch_shapes=[pltpu.VMEM((N_HALF, 2 * DIL_SPAN, LANES), F32),
                        pltpu.VMEM((N_HALF, 2 * DIL_SPAN, LANES), F32),
                        pltpu.VMEM((N_HALF, DIL_SPAN, LANES), F32), pltpu.VMEM((N_HALF, DIL_SPAN, LANES), F32),
                        pltpu.VMEM((N_HALF, DIL_SPAN, LANES), F32)],
        compiler_params=_params(2),
        name="dilated",
    )(cq, ck, ck, cv, cv, bias)


def _swa_kernel(sink_ref, q_ref, k_ref, v_ref, o_ref):
    i = pl.program_id(1)
    G = N_HEADS // SWA_KV_HEADS
    start = pl.multiple_of(jnp.maximum(i - 1, 0) * Q_BLOCK, Q_BLOCK)
    q = q_ref[0]
    kw = k_ref[0, pl.ds(start, 2 * Q_BLOCK), :]
    vw = v_ref[0, pl.ds(start, 2 * Q_BLOCK), :]
    tpos = i * Q_BLOCK + lax.broadcasted_iota(jnp.int32, (Q_BLOCK, 2 * Q_BLOCK), 0)
    kpos = start + lax.broadcasted_iota(jnp.int32, (Q_BLOCK, 2 * Q_BLOCK), 1)
    rel = tpos - kpos
    valid = (rel >= 0) & (rel < SWA_WINDOW)
    dist = rel.astype(F32)
    outs = []
    for h in range(N_HEADS):
        g = h // G
        ksl = slice(g * HEAD_DIM, (g + 1) * HEAD_DIM)
        s = lax.dot_general(q[:, h * HEAD_DIM:(h + 1) * HEAD_DIM], kw[:, ksl], _NT, preferred_element_type=F32)
        s = jnp.where(valid, s - float(SLOPE_D[h]) * dist, NEG)
        sink = sink_ref[h]
        m = jnp.maximum(jnp.max(s, axis=-1, keepdims=True), sink)
        p = jnp.exp(s - m)
        l = jnp.sum(p, axis=-1, keepdims=True) + jnp.exp(sink - m)
        outs.append(jnp.dot(p.astype(BF16), vw[:, ksl], preferred_element_type=F32) / l)
    o_ref[0] = jnp.concatenate(outs, axis=-1)


def _swa(dq, dk, dv, sinks):
    B, S, _ = dq.shape
    return pl.pallas_call(
        _swa_kernel,
        grid=(B, S // Q_BLOCK),
        in_specs=[pl.BlockSpec(memory_space=pltpu.SMEM),
                  pl.BlockSpec((1, Q_BLOCK, GROUP), lambda b, i: (b, i, 0)),
                  pl.BlockSpec((1, S, SWA_KV_WIDTH), lambda b, i: (b, 0, 0)),
                  pl.BlockSpec((1, S, SWA_KV_WIDTH), lambda b, i: (b, 0, 0))],
        out_specs=pl.BlockSpec((1, Q_BLOCK, GROUP), lambda b, i: (b, i, 0)),
        out_shape=jax.ShapeDtypeStruct((B, S, GROUP), F32),
        compiler_params=_params(2),
        name="swa",
    )(sinks, dq, dk, dv)


def _row_norm(y, g_row):
    return y * lax.rsqrt(jnp.mean(y * y, axis=-1, keepdims=True) + EPS) * g_row


def _tail_kernel(x_ref, oat_ref, obt_ref, oc_ref, od_ref, gg_ref, wo_ref, gm_ref, wup_ref, wdn_ref, out_ref):
    gg = gg_ref[...]

    def col_norm_t(yt, g_row):
        y = (yt * lax.rsqrt(jnp.mean(yt * yt, axis=0, keepdims=True) + EPS)).T
        return y * g_row

    ga = col_norm_t(oat_ref[0], gg[0:1, :])
    gb = col_norm_t(obt_ref[0], gg[1:2, :])
    gc = _row_norm(jnp.concatenate([oc_ref[0, c] for c in range(N_HALF)], axis=-1), gg[2:3, :])
    gd = _row_norm(od_ref[0], gg[3:4, :])
    mixed = jnp.concatenate([ga, gb, gc, gd], axis=-1).astype(BF16)
    x1 = x_ref[0] + jnp.dot(mixed, wo_ref[...], preferred_element_type=F32)
    xn = _row_norm(x1, gm_ref[...]).astype(BF16)
    u = jnp.maximum(jnp.dot(xn, wup_ref[...], preferred_element_type=F32), 0.0)
    out_ref[0] = x1 + jnp.dot((u * u).astype(BF16), wdn_ref[...], preferred_element_type=F32)


def _tail(x, oat, obt, oc, od, gg, wo, gm, wup, wdn):
    B, S, _ = x.shape
    tm = TAIL_TM
    nat = lambda w: pl.BlockSpec((1, tm, w), lambda b, t: (b, t, 0))
    ft = pl.BlockSpec((1, GROUP, tm), lambda b, t: (b, 0, t))
    const = lambda a: pl.BlockSpec(a.shape, lambda b, t: (0,) * a.ndim, pipeline_mode=pl.Buffered(1))
    return pl.pallas_call(
        _tail_kernel,
        grid=(B, S // tm),
        in_specs=[nat(D_MODEL), ft, ft, pl.BlockSpec((1, N_HALF, tm, LANES), lambda b, t: (b, 0, t, 0)),
                  nat(GROUP), const(gg), const(wo), const(gm),
                  const(wup), const(wdn)],
        out_specs=nat(D_MODEL),
        out_shape=jax.ShapeDtypeStruct((B, S, D_MODEL), F32),
        compiler_params=_params(2),
        name="tail",
    )(x, oat, obt, oc, od, gg, wo, gm, wup, wdn)


def _rope_tables_t(S):
    inv = 1.0 / (ROPE_THETA ** (jnp.arange(0, MLA_ROPE, 2, dtype=F32) / MLA_ROPE))
    ang = inv[:, None] * jnp.arange(S, dtype=F32)[None, :]
    return jnp.cos(ang), jnp.sin(ang)


def _layer(x, cos_t, sin_t, attn_norm_g, w_in, moba_q_g, moba_k_g, mla_qlat_g, mla_kvlat_g, mla_w_uq, mla_w_ukv,
           mla_q_g, mla_k_g, dil_q_g, dil_k_g, swa_q_g, swa_k_g, swa_sinks, group_out_g, w_o, mlp_norm_g,
           w_up, w_down):
    B, S, _ = x.shape
    col = lambda g: g.reshape(-1, 1)
    (aqt, aq32t, ak, ak32, avt, bqt, bk, bvt, cq, ck, cv, dq, dk, dv) = _inproj(
        x, attn_norm_g.reshape(1, -1), w_in.T.astype(BF16), col(moba_q_g), col(moba_k_g), col(mla_qlat_g),
        col(mla_kvlat_g), mla_w_uq.T.astype(BF16), mla_w_ukv.T.astype(BF16), col(mla_q_g), col(mla_k_g),
        col(dil_q_g), col(dil_k_g), col(swa_q_g), col(swa_k_g), cos_t, sin_t)
    kmean = _kmean(ak32)
    nb = S // MOBA_BLOCK
    kmean_h = kmean.reshape(B, nb, N_HEADS, HEAD_DIM).transpose(0, 2, 1, 3)
    oat = _moba(aqt, aq32t, ak, avt, kmean_h).reshape(B, GROUP, S)
    obt = _mla(bqt, bk, bvt).reshape(B, GROUP, S)
    oc = _dilated(cq, ck, cv)
    od = _swa(dq, dk, dv, swa_sinks)
    return _tail(x, oat, obt, oc, od, group_out_g, w_o.astype(BF16), mlp_norm_g.reshape(1, -1),
                 w_up.astype(BF16), w_down.astype(BF16))


def kernel(x, attn_norm_g, w_in, moba_q_g, moba_k_g, mla_qlat_g, mla_kvlat_g, mla_w_uq, mla_w_ukv, mla_q_g,
           mla_k_g, dil_q_g, dil_k_g, swa_q_g, swa_k_g, swa_sinks, group_out_g, w_o, mlp_norm_g, w_up, w_down):
    S = x.shape[1]
    assert S % max(d * Q_BLOCK for _, d in DILATED_BRANCHES) == 0 and S % INPROJ_TM == 0
    cos_t, sin_t = _rope_tables_t(S)
    params = (attn_norm_g, w_in, moba_q_g, moba_k_g, mla_qlat_g, mla_kvlat_g, mla_w_uq, mla_w_ukv, mla_q_g,
              mla_k_g, dil_q_g, dil_k_g, swa_q_g, swa_k_g, swa_sinks, group_out_g, w_o, mlp_norm_g, w_up, w_down)
    for l in range(attn_norm_g.shape[0]):
        x = _layer(x, cos_t, sin_t, *[p[l] for p in params])
    return x
```

```python
import functools

import numpy as np
import jax
import jax.numpy as jnp
from jax import lax
from jax.experimental import pallas as pl
from jax.experimental.pallas import tpu as pltpu

F32 = jnp.float32
BF16 = jnp.bfloat16

D_MODEL = 1024
HEAD_DIM = 64
N_HEADS = 4
GROUP = N_HEADS * HEAD_DIM
LANES = 128
N_HALF = GROUP // LANES
MOBA_BLOCK = 256
MOBA_TOPK = 3
MLA_Q_RANK = 256
MLA_KV_RANK = 128
MLA_NOPE = 64
MLA_ROPE = 32
MLA_QK = MLA_NOPE + MLA_ROPE
MLA_QK_PAD = 128
ROPE_THETA = 10000.0
DILATED_BRANCHES = ((128, 1), (512, 4), (2048, 16))
Q_BLOCK = 128
SWA_WINDOW = 128
SWA_KV_HEADS = 2
SWA_KV_WIDTH = SWA_KV_HEADS * HEAD_DIM
D_FF = 4 * D_MODEL
EPS = 1e-6
NEG = -1e30

_WIDTHS = (GROUP, GROUP, GROUP, MLA_Q_RANK, MLA_KV_RANK, MLA_ROPE,
           GROUP, GROUP, GROUP, GROUP, SWA_KV_WIDTH, SWA_KV_WIDTH)
_OFFS = tuple(int(v) for v in np.cumsum((0,) + _WIDTHS))
IN_COLS = _OFFS[-1]
(_A_Q, _A_K, _A_V, _B_QL, _B_KVL, _B_KR, _C_Q, _C_K, _C_V, _D_Q, _D_K, _D_V) = _OFFS[:-1]

VMEM_LIMIT = 56 * 1024 * 1024

INPROJ_TM = 512
TAIL_TM = 256
ATT_TK = 256
ATT_TQ = 2 * ATT_TK
ATT_DK = 128
ATT_DV = HEAD_DIM + 16
AUG_ROWS = 8
LOG2E = 1.4426950408889634

_NT = (((1,), (1,)), ((), ()))


def _alibi_slopes():
    n = 3 * N_HEADS
    idx = np.arange(1, n + 1, dtype=np.float32).reshape(N_HEADS, 3)
    s = np.exp2(-8.0 * idx / n).astype(np.float32)
    return s[:, 0], s[:, 1], s[:, 2]


SLOPE_A, SLOPE_C, SLOPE_D = _alibi_slopes()


def _params(n_axes):
    return pltpu.CompilerParams(dimension_semantics=("arbitrary",) * n_axes,
                                vmem_limit_bytes=VMEM_LIMIT)


def _head_norm_t(sec, g_col, n_heads, width):
    outs = []
    for h in range(n_heads):
        s = sec[h * width:(h + 1) * width, :]
        ms = jnp.sum(s * s, axis=0, keepdims=True) * (1.0 / width)
        outs.append(s * lax.rsqrt(ms + EPS) * g_col)
    return outs


def _inproj_kernel(x_ref, gx_ref, w1t_ref, gaq_ref, gak_ref, gql_ref, gkvl_ref, wuqt_ref, wukvt_ref,
                   gbq_ref, gbk_ref, gcq_ref, gck_ref, gdq_ref, gdk_ref, cos_ref, sin_ref, kpos_ref, aslope_ref,
                   aqt_ref, aq32t_ref, ak_ref, ak32_ref, avt_ref,
                   bqt_ref, bk_ref, bvt_ref,
                   cq_ref, ck_ref, cv_ref, dq_ref, dk_ref, dv_ref,
                   h_scr):
    tm = x_ref.shape[1]
    x = x_ref[0]
    ms = jnp.mean(x * x, axis=-1, keepdims=True)
    xn = (x * lax.rsqrt(ms + EPS) * gx_ref[...]).astype(BF16)
    h_scr[...] = lax.dot_general(w1t_ref[...], xn, _NT, preferred_element_type=F32)

    scale = HEAD_DIM ** -0.5

    ones_rows = jnp.ones((ATT_DV - HEAD_DIM, ATT_TK), F32)

    def store_vt(ref, h, vh):
        for c in range(tm // ATT_TK):
            ref[0, h, c] = jnp.concatenate([vh[:, c * ATT_TK:(c + 1) * ATT_TK], ones_rows], axis=0).astype(BF16)

    pad_rows = jnp.zeros((ATT_DK - HEAD_DIM - AUG_ROWS, tm), F32)
    qa = _head_norm_t(h_scr[_A_Q:_A_Q + GROUP, :], gaq_ref[...], N_HEADS, HEAD_DIM)
    ka = _head_norm_t(h_scr[_A_K:_A_K + GROUP, :], gak_ref[...], N_HEADS, HEAD_DIM)
    ak32_ref[0] = jnp.concatenate(ka, axis=0).T
    kpos = kpos_ref[...]
    for h in range(N_HEADS):
        aq32t_ref[0, h] = qa[h]
        slope_rows = jnp.broadcast_to(aslope_ref[h], (AUG_ROWS, tm))
        aqt_ref[0, h] = jnp.concatenate([qa[h] * (scale * LOG2E), slope_rows, pad_rows], axis=0).astype(BF16)
        ak_ref[0, h] = jnp.concatenate([ka[h], kpos, pad_rows], axis=0).T.astype(BF16)
        store_vt(avt_ref, h, h_scr[_A_V + h * HEAD_DIM:_A_V + (h + 1) * HEAD_DIM, :])

    cos = cos_ref[...]
    sin = sin_ref[...]
    half = MLA_ROPE // 2

    def rope_pad(t, sc):
        x1 = t[MLA_NOPE:MLA_NOPE + half, :]
        x2 = t[MLA_NOPE + half:MLA_QK, :]
        return jnp.concatenate([t[:MLA_NOPE, :] * sc, (x1 * cos - x2 * sin) * sc, (x1 * sin + x2 * cos) * sc,
                                jnp.zeros((MLA_QK_PAD - MLA_QK, tm), F32)], axis=0)

    ql = h_scr[_B_QL:_B_QL + MLA_Q_RANK, :]
    ql = ql * lax.rsqrt(jnp.sum(ql * ql, axis=0, keepdims=True) * (1.0 / MLA_Q_RANK) + EPS) * gql_ref[...]
    qb = jnp.dot(wuqt_ref[...], ql.astype(BF16), preferred_element_type=F32)
    qb = _head_norm_t(qb, gbq_ref[...], N_HEADS, MLA_QK)
    kvl = h_scr[_B_KVL:_B_KVL + MLA_KV_RANK, :]
    kvl = kvl * lax.rsqrt(jnp.sum(kvl * kvl, axis=0, keepdims=True) * (1.0 / MLA_KV_RANK) + EPS) * gkvl_ref[...]
    kvb = jnp.dot(wukvt_ref[...], kvl.astype(BF16), preferred_element_type=F32)
    kr = h_scr[_B_KR:_B_KR + MLA_ROPE, :]
    gbk = gbk_ref[...]
    for h in range(N_HEADS):
        bqt_ref[0, h] = rope_pad(qb[h], MLA_QK ** -0.5 * LOG2E).astype(BF16)
        kh = jnp.concatenate([kvb[h * 2 * HEAD_DIM:h * 2 * HEAD_DIM + MLA_NOPE, :], kr], axis=0)
        kh = kh * lax.rsqrt(jnp.sum(kh * kh, axis=0, keepdims=True) * (1.0 / MLA_QK) + EPS) * gbk
        bk_ref[0, h] = rope_pad(kh, 1.0).T.astype(BF16)
        store_vt(bvt_ref, h, kvb[h * 2 * HEAD_DIM + MLA_NOPE:(h + 1) * 2 * HEAD_DIM, :])

    qc = jnp.concatenate(_head_norm_t(h_scr[_C_Q:_C_Q + GROUP, :], gcq_ref[...], N_HEADS, HEAD_DIM), axis=0)
    kc = jnp.concatenate(_head_norm_t(h_scr[_C_K:_C_K + GROUP, :], gck_ref[...], N_HEADS, HEAD_DIM), axis=0)
    for ref, val in ((cq_ref, qc * scale), (ck_ref, kc), (cv_ref, h_scr[_C_V:_C_V + GROUP, :])):
        for c in range(N_HALF):
            ref[0, c] = val[c * LANES:(c + 1) * LANES, :].T

    qd = jnp.concatenate(_head_norm_t(h_scr[_D_Q:_D_Q + GROUP, :], gdq_ref[...], N_HEADS, HEAD_DIM), axis=0)
    dq_ref[0] = (qd * scale).T.astype(BF16)
    kd = jnp.concatenate(_head_norm_t(h_scr[_D_K:_D_K + SWA_KV_WIDTH, :], gdk_ref[...], SWA_KV_HEADS, HEAD_DIM),
                         axis=0)
    dk_ref[0] = kd.T.astype(BF16)
    dv_ref[0] = h_scr[_D_V:_D_V + SWA_KV_WIDTH, :].T.astype(BF16)


def _inproj(x, gx, w1t, gaq, gak, gql, gkvl, wuqt, wukvt, gbq, gbk, gcq, gck, gdq, gdk, cos_t, sin_t, kpos_t,
            aslope):
    B, S, _ = x.shape
    tm = INPROJ_TM
    nb = S // ATT_TK
    cpt = tm // ATT_TK
    H = N_HEADS
    full = lambda a: pl.BlockSpec(a.shape, lambda b, t: (0,) * a.ndim)
    in_specs = [pl.BlockSpec((1, tm, D_MODEL), lambda b, t: (b, t, 0)), full(gx), full(w1t), full(gaq), full(gak),
                full(gql), full(gkvl), full(wuqt), full(wukvt), full(gbq), full(gbk), full(gcq), full(gck),
                full(gdq), full(gdk),
                pl.BlockSpec((MLA_ROPE // 2, tm), lambda b, t: (0, t)),
                pl.BlockSpec((MLA_ROPE // 2, tm), lambda b, t: (0, t)),
                pl.BlockSpec((AUG_ROWS, tm), lambda b, t: (0, t)), full(aslope)]
    head_t = lambda w: pl.BlockSpec((1, H, w, tm), lambda b, t: (b, 0, 0, t))
    head_n = pl.BlockSpec((1, H, tm, ATT_DK), lambda b, t: (b, 0, t, 0))
    vt_spec = pl.BlockSpec((1, H, cpt, ATT_DV, ATT_TK), lambda b, t: (b, 0, t, 0, 0))
    nat = lambda w: pl.BlockSpec((1, tm, w), lambda b, t: (b, t, 0))
    halves = pl.BlockSpec((1, N_HALF, tm, LANES), lambda b, t: (b, 0, t, 0))
    out_shape = [
        jax.ShapeDtypeStruct((B, H, ATT_DK, S), BF16),
        jax.ShapeDtypeStruct((B, H, HEAD_DIM, S), F32),
        jax.ShapeDtypeStruct((B, H, S, ATT_DK), BF16),
        jax.ShapeDtypeStruct((B, S, GROUP), F32),
        jax.ShapeDtypeStruct((B, H, nb, ATT_DV, ATT_TK), BF16),
        jax.ShapeDtypeStruct((B, H, ATT_DK, S), BF16),
        jax.ShapeDtypeStruct((B, H, S, ATT_DK), BF16),
        jax.ShapeDtypeStruct((B, H, nb, ATT_DV, ATT_TK), BF16),
        jax.ShapeDtypeStruct((B, N_HALF, S, LANES), F32),
        jax.ShapeDtypeStruct((B, N_HALF, S, LANES), F32),
        jax.ShapeDtypeStruct((B, N_HALF, S, LANES), F32),
        jax.ShapeDtypeStruct((B, S, GROUP), BF16),
        jax.ShapeDtypeStruct((B, S, SWA_KV_WIDTH), BF16),
        jax.ShapeDtypeStruct((B, S, SWA_KV_WIDTH), BF16),
    ]
    out_specs = [head_t(ATT_DK), head_t(HEAD_DIM), head_n, nat(GROUP), vt_spec,
                 head_t(ATT_DK), head_n, vt_spec,
                 halves, halves, halves, nat(GROUP), nat(SWA_KV_WIDTH), nat(SWA_KV_WIDTH)]
    return pl.pallas_call(
        _inproj_kernel,
        grid=(B, S // tm),
        in_specs=in_specs,
        out_specs=out_specs,
        out_shape=out_shape,
        scratch_shapes=[pltpu.VMEM((IN_COLS, tm), F32)],
        compiler_params=_params(2),
        name="inproj",
    )(x, gx, w1t, gaq, gak, gql, gkvl, wuqt, wukvt, gbq, gbk, gcq, gck, gdq, gdk, cos_t, sin_t, kpos_t, aslope)


def _kmean_kernel(k_ref, o_ref):
    S = k_ref.shape[1]
    k = k_ref[0].reshape(S // MOBA_BLOCK, MOBA_BLOCK, GROUP)
    o_ref[0] = jnp.sum(k, axis=1) * (1.0 / MOBA_BLOCK)


def _kmean(ak32):
    B, S, _ = ak32.shape
    nb = S // MOBA_BLOCK
    return pl.pallas_call(
        _kmean_kernel,
        grid=(B,),
        in_specs=[pl.BlockSpec((1, S, GROUP), lambda b: (b, 0, 0))],
        out_specs=pl.BlockSpec((1, nb, GROUP), lambda b: (b, 0, 0)),
        out_shape=jax.ShapeDtypeStruct((B, nb, GROUP), F32),
        compiler_params=_params(1),
        name="kmean",
    )(ak32)


def _flash_scratch(tq):
    s_buf, p_buf = pltpu.VMEM((N_HEADS, ATT_TK, tq), F32), pltpu.VMEM((N_HEADS, ATT_TK, tq), BF16)
    return [s_buf, s_buf, p_buf, p_buf, pltpu.VMEM((N_HEADS, ATT_DV, tq), F32)]


def _flash_heads(qt_ref, k_ref, vt_ref, o_ref, s_bufs, p_bufs, acc_scr, rowb_scr, i):
    tq = o_ref.shape[3]
    n_tail = tq // ATT_TK
    assert n_tail == 2
    n_past = i * n_tail
    heads = range(N_HEADS)

    def block_of(pos):
        return jnp.where(pos < n_tail, n_past + pos, pos - n_tail)

    def stage_scores(slot, blk):
        kstart = pl.multiple_of(blk * ATT_TK, ATT_TK)
        for h in heads:
            kb = k_ref[0, h, pl.ds(kstart, ATT_TK), :]
            s_bufs[slot][h] = jnp.dot(kb, qt_ref[0, h], preferred_element_type=F32)

    def stage_softmax(slot, blk, ms, causal=None):
        new_ms, alphas = [], []
        for h in heads:
            m_parts, a_parts = [], []
            rv_row = None if rowb_scr is None else rowb_scr[h, pl.ds(blk, 1), :]
            for c in range(tq // LANES):
                cols = slice(c * LANES, (c + 1) * LANES)
                st = s_bufs[slot][h, :, cols]
                if causal is not None:
                    st = jnp.where(causal[:, cols], st, 2 * NEG)
                cm = jnp.max(st, axis=0, keepdims=True)
                m_old = ms[h][:, cols]
                if rowb_scr is None:
                    m_new = jnp.maximum(m_old, cm)
                    shift = m_new
                else:
                    rv = rv_row[:, cols]
                    m_new = jnp.maximum(m_old, cm + rv)
                    shift = m_new - rv
                a_parts.append(jnp.exp2(m_old - m_new))
                m_parts.append(m_new)
                p_bufs[slot][h, :, cols] = jnp.exp2(st - shift).astype(BF16)
            new_ms.append(jnp.concatenate(m_parts, axis=1))
            alphas.append(jnp.concatenate(a_parts, axis=1))
        return tuple(new_ms), tuple(alphas)

    def stage_values(slot, blk, alphas):
        for h in heads:
            acc_scr[h] = alphas[h] * acc_scr[h] + jnp.dot(vt_ref[0, h, blk], p_bufs[slot][h],
                                                          preferred_element_type=F32)

    krow = lax.broadcasted_iota(jnp.int32, (ATT_TK, tq), 0)
    ti = lax.broadcasted_iota(jnp.int32, (ATT_TK, tq), 1)
    for h in heads:
        acc_scr[h] = jnp.zeros((ATT_DV, tq), F32)
    stage_scores(0, n_past)
    stage_scores(1, n_past + 1)
    ms = tuple(jnp.full((1, tq), NEG, F32) for _ in heads)
    ms, alphas = stage_softmax(0, n_past, ms, causal=ti >= krow)
    stage_values(0, n_past, alphas)
    ms, alphas = stage_softmax(1, n_past + 1, ms, causal=ti >= krow + ATT_TK)
    stage_scores(0, 0)

    def step(pos, slot, ms, alphas):
        ms, new_alphas = stage_softmax(1 - slot, pos + 1 - n_tail, ms)
        stage_values(slot, block_of(pos), alphas)
        stage_scores(slot, pos)
        return ms, new_alphas

    def body(t, carry):
        ms, alphas = step(2 * t + 1, 1, *carry)
        return step(2 * t + 2, 0, ms, alphas)

    ms, alphas = lax.fori_loop(0, i, body, (ms, alphas))
    last = n_past + 1
    stage_values(1, block_of(last), alphas)
    for h in heads:
        acc = acc_scr[h]
        o_ref[0, h] = acc[:HEAD_DIM, :] / acc[HEAD_DIM:HEAD_DIM + 1, :]


def _moba_kernel(qt_ref, q32t_ref, k_ref, vt_ref, kmean_ref, o_ref, rowb_scr, s0, s1, p0, p1, acc_scr):
    i = pl.program_id(1)
    nb = kmean_ref.shape[2]
    tq = qt_ref.shape[3]
    blk = lax.broadcasted_iota(jnp.int32, (nb, tq), 0)
    col = lax.broadcasted_iota(jnp.int32, (nb, tq), 1)
    qblk = i * (tq // MOBA_BLOCK) + col // MOBA_BLOCK
    past = blk < qblk
    dist0 = (i * tq + col - blk * MOBA_BLOCK).astype(F32)
    for h in range(N_HEADS):
        gate = jnp.dot(kmean_ref[0, h], q32t_ref[0, h], preferred_element_type=F32,
                       precision=lax.Precision.HIGHEST)
        gate = jnp.where(past, gate, NEG)
        sel = blk == qblk
        for _ in range(MOBA_TOPK):
            best = jnp.max(gate, axis=0, keepdims=True)
            first = jnp.min(jnp.where(gate == best, blk, nb), axis=0, keepdims=True)
            pick = blk == first
            sel = jnp.logical_or(sel, jnp.logical_and(pick, past))
            gate = jnp.where(pick, -jnp.inf, gate)
        rowb_scr[h] = jnp.where(sel, (-float(SLOPE_A[h]) * LOG2E) * dist0, 2 * NEG)
    _flash_heads(qt_ref, k_ref, vt_ref, o_ref, (s0, s1), (p0, p1), acc_scr, rowb_scr, i)


def _flash_specs(S, tq):
    H, nb = N_HEADS, S // ATT_TK
    return ([pl.BlockSpec((1, H, ATT_DK, tq), lambda b, i: (b, 0, 0, i)),
             pl.BlockSpec((1, H, S, ATT_DK), lambda b, i: (b, 0, 0, 0)),
             pl.BlockSpec((1, H, nb, ATT_DV, ATT_TK), lambda b, i: (b, 0, 0, 0, 0))],
            pl.BlockSpec((1, H, HEAD_DIM, tq), lambda b, i: (b, 0, 0, i)))


def _moba(aqt, aq32t, ak, avt, kmean_h):
    B, H, _, S = aqt.shape
    tq = ATT_TQ
    nb = S // ATT_TK
    (q_spec, k_spec, vt_spec), o_spec = _flash_specs(S, tq)
    return pl.pallas_call(
        _moba_kernel,
        grid=(B, S // tq),
        in_specs=[q_spec, pl.BlockSpec((1, H, HEAD_DIM, tq), lambda b, i: (b, 0, 0, i)), k_spec, vt_spec,
                  pl.BlockSpec((1, H, nb, HEAD_DIM), lambda b, i: (b, 0, 0, 0))],
        out_specs=o_spec,
        out_shape=jax.ShapeDtypeStruct((B, H, HEAD_DIM, S), F32),
        scratch_shapes=[pltpu.VMEM((H, nb, tq), F32)] + _flash_scratch(tq),
        compiler_params=_params(2),
        name="moba",
    )(aqt, aq32t, ak, avt, kmean_h)


def _mla_kernel(qt_ref, k_ref, vt_ref, o_ref, s0, s1, p0, p1, acc_scr):
    _flash_heads(qt_ref, k_ref, vt_ref, o_ref, (s0, s1), (p0, p1), acc_scr, None, pl.program_id(1))


def _mla(bqt, bk, bvt):
    B, H, _, S = bqt.shape
    tq = ATT_TQ
    in_specs, o_spec = _flash_specs(S, tq)
    return pl.pallas_call(
        _mla_kernel,
        grid=(B, S // tq),
        in_specs=in_specs,
        out_specs=o_spec,
        out_shape=jax.ShapeDtypeStruct((B, H, HEAD_DIM, S), F32),
        scratch_shapes=_flash_scratch(tq),
        compiler_params=_params(2),
        name="mla",
    )(bqt, bk, bvt)


DIL_SPAN = max(d for _, d in DILATED_BRANCHES) * Q_BLOCK
DIL_UNITS = DIL_SPAN // Q_BLOCK
DIL_GROUP = 2


def _dilated_bias():
    qi = np.arange(Q_BLOCK)[:, None]
    kidx = np.arange(2 * Q_BLOCK)[None, :]
    rel = qi + Q_BLOCK - kidx
    out = np.empty((len(DILATED_BRANCHES), N_HEADS, Q_BLOCK, 2 * Q_BLOCK), np.float32)
    for bi, (window, d) in enumerate(DILATED_BRANCHES):
        valid = (rel >= 0) & (rel <= window // d)
        for h in range(N_HEADS):
            out[bi, h] = np.where(valid, -SLOPE_C[h] * (d * rel).astype(np.float32), NEG)
    return out


def _rows_load(ref, lead, start, size, stride):
    return jnp.concatenate([ref[lead + (c, pl.ds(start, size, stride=stride), slice(None))]
                            for c in range(N_HALF)], axis=-1)


def _rows_store(ref, lead, start, size, stride, val):
    for c in range(N_HALF):
        ref[lead + (c, pl.ds(start, size, stride=stride), slice(None))] = val[:, c * LANES:(c + 1) * LANES]


def _dilated_kernel(q_ref, kp_ref, kc_ref, vp_ref, vc_ref, bias_ref, o_ref, kbuf, vbuf, m_scr, den_scr, num_scr):
    span = pl.program_id(1)
    kbuf[:, 0:DIL_SPAN, :] = kp_ref[0]
    kbuf[:, DIL_SPAN:, :] = kc_ref[0]
    vbuf[:, 0:DIL_SPAN, :] = vp_ref[0]
    vbuf[:, DIL_SPAN:, :] = vc_ref[0]
    lane_head = lax.broadcasted_iota(jnp.int32, (1, GROUP), 1) // HEAD_DIM
    hmask = [lane_head == h for h in range(N_HEADS)]
    hmask_f = [m.astype(F32) for m in hmask]
    low_neg = jnp.where(lax.broadcasted_iota(jnp.int32, (Q_BLOCK, 2 * Q_BLOCK), 1) < Q_BLOCK, NEG, 0.0)
    first_span = jnp.where(span == 0, 1.0, 0.0)

    def per_head(cols):
        out = cols[N_HEADS - 1]
        for h in range(N_HEADS - 2, -1, -1):
            out = jnp.where(hmask[h], cols[h], out)
        return out

    for bi, (window, d) in enumerate(DILATED_BRANCHES):
        last = bi == len(DILATED_BRANCHES) - 1

        def group(g, _, bi=bi, d=d, last=last):
            fronts = []
            for uu in range(DIL_GROUP):
                u = g * DIL_GROUP + uu
                r, n = u % d, u // d
                qstart = n * (Q_BLOCK * d) + r
                kstart = DIL_SPAN + qstart - Q_BLOCK * d
                q = _rows_load(q_ref, (0,), qstart, Q_BLOCK, d)
                k2 = _rows_load(kbuf, (), kstart, 2 * Q_BLOCK, d).astype(BF16)
                v2 = _rows_load(vbuf, (), kstart, 2 * Q_BLOCK, d).astype(BF16)
                q4 = jnp.concatenate([(q * hmask_f[h]).astype(BF16) for h in range(N_HEADS)], axis=0)
                s4 = lax.dot_general(q4, k2, _NT, preferred_element_type=F32)
                penalty = low_neg * jnp.where(n == 0, first_span, 0.0)
                fronts.append((qstart, s4, v2, penalty))
            for qstart, s4, v2, penalty in fronts:
                es, ms, ls = [], [], []
                for h in range(N_HEADS):
                    s = s4[h * Q_BLOCK:(h + 1) * Q_BLOCK, :] + (bias_ref[bi, h] + penalty)
                    m = jnp.max(s, axis=-1, keepdims=True)
                    e = jnp.exp(s - m)
                    ls.append(jnp.sum(e, axis=-1, keepdims=True))
                    ms.append(m)
                    es.append(e.astype(BF16))
                o4 = jnp.dot(jnp.concatenate(es, axis=0), v2, preferred_element_type=F32)
                o = o4[(N_HEADS - 1) * Q_BLOCK:, :]
                for h in range(N_HEADS - 2, -1, -1):
                    o = jnp.where(hmask[h], o4[h * Q_BLOCK:(h + 1) * Q_BLOCK, :], o)
                m_b, l_b = per_head(ms), per_head(ls)
                at = ((), qstart, Q_BLOCK, d)
                if bi == 0:
                    _rows_store(m_scr, *at, m_b)
                    _rows_store(num_scr, *at, o)
                    _rows_store(den_scr, *at, l_b)
                else:
                    m_old = _rows_load(m_scr, *at)
                    m_new = jnp.maximum(m_old, m_b)
                    a, b = jnp.exp(m_old - m_new), jnp.exp(m_b - m_new)
                    num = a * _rows_load(num_scr, *at) + b * o
                    den = a * _rows_load(den_scr, *at) + b * l_b
                    if last:
                        _rows_store(o_ref, (0,), qstart, Q_BLOCK, d, num / den)
                    else:
                        _rows_store(m_scr, *at, m_new)
                        _rows_store(num_scr, *at, num)
                        _rows_store(den_scr, *at, den)
            return 0

        lax.fori_loop(0, DIL_UNITS // DIL_GROUP, group, 0)


def _dilated(cq, ck, cv):
    B, _, S, _ = cq.shape
    cur = pl.BlockSpec((1, N_HALF, DIL_SPAN, LANES), lambda b, s: (b, 0, s, 0))
    prev = pl.BlockSpec((1, N_HALF, DIL_SPAN, LANES), lambda b, s: (b, 0, jnp.maximum(s - 1, 0), 0))
    bias = jnp.asarray(_dilated_bias())
    return pl.pallas_call(
        _dilated_kernel,
        grid=(B, S // DIL_SPAN),
        in_specs=[cur, prev, cur, prev, cur, pl.BlockSpec(bias.shape, lambda b, s: (0, 0, 0, 0))],
        out_specs=cur,
        out_shape=jax.ShapeDtypeStruct((B, N_HALF, S, LANES), F32),
        scratch_shapes=[pltpu.VMEM((N_HALF, 2 * DIL_SPAN, LANES), F32),
                        pltpu.VMEM((N_HALF, 2 * DIL_SPAN, LANES), F32),
                        pltpu.VMEM((N_HALF, DIL_SPAN, LANES), F32), pltpu.VMEM((N_HALF, DIL_SPAN, LANES), F32),
                        pltpu.VMEM((N_HALF, DIL_SPAN, LANES), F32)],
        compiler_params=_params(2),
        name="dilated",
    )(cq, ck, ck, cv, cv, bias)


def _swa_kernel(sink_ref, q_ref, k_ref, v_ref, o_ref):
    i = pl.program_id(1)
    G = N_HEADS // SWA_KV_HEADS
    start = pl.multiple_of(jnp.maximum(i - 1, 0) * Q_BLOCK, Q_BLOCK)
    q = q_ref[0]
    kw = k_ref[0, pl.ds(start, 2 * Q_BLOCK), :]
    vw = v_ref[0, pl.ds(start, 2 * Q_BLOCK), :]
    tpos = i * Q_BLOCK + lax.broadcasted_iota(jnp.int32, (Q_BLOCK, 2 * Q_BLOCK), 0)
    kpos = start + lax.broadcasted_iota(jnp.int32, (Q_BLOCK, 2 * Q_BLOCK), 1)
    rel = tpos - kpos
    valid = (rel >= 0) & (rel < SWA_WINDOW)
    dist = rel.astype(F32)
    outs = []
    for h in range(N_HEADS):
        g = h // G
        ksl = slice(g * HEAD_DIM, (g + 1) * HEAD_DIM)
        s = lax.dot_general(q[:, h * HEAD_DIM:(h + 1) * HEAD_DIM], kw[:, ksl], _NT, preferred_element_type=F32)
        s = jnp.where(valid, s - float(SLOPE_D[h]) * dist, NEG)
        sink = sink_ref[h]
        m = jnp.maximum(jnp.max(s, axis=-1, keepdims=True), sink)
        p = jnp.exp(s - m)
        l = jnp.sum(p, axis=-1, keepdims=True) + jnp.exp(sink - m)
        outs.append(jnp.dot(p.astype(BF16), vw[:, ksl], preferred_element_type=F32) / l)
    o_ref[0] = jnp.concatenate(outs, axis=-1)


def _swa(dq, dk, dv, sinks):
    B, S, _ = dq.shape
    return pl.pallas_call(
        _swa_kernel,
        grid=(B, S // Q_BLOCK),
        in_specs=[pl.BlockSpec(memory_space=pltpu.SMEM),
                  pl.BlockSpec((1, Q_BLOCK, GROUP), lambda b, i: (b, i, 0)),
                  pl.BlockSpec((1, S, SWA_KV_WIDTH), lambda b, i: (b, 0, 0)),
                  pl.BlockSpec((1, S, SWA_KV_WIDTH), lambda b, i: (b, 0, 0))],
        out_specs=pl.BlockSpec((1, Q_BLOCK, GROUP), lambda b, i: (b, i, 0)),
        out_shape=jax.ShapeDtypeStruct((B, S, GROUP), F32),
        compiler_params=_params(2),
        name="swa",
    )(sinks, dq, dk, dv)


def _row_norm(y, g_row):
    return y * lax.rsqrt(jnp.mean(y * y, axis=-1, keepdims=True) + EPS) * g_row


def _tail_kernel(x_ref, oat_ref, obt_ref, oc_ref, od_ref, gg_ref, wo_ref, gm_ref, wup_ref, wdn_ref, out_ref):
    gg = gg_ref[...]

    def col_norm_t(yt, g_row):
        y = (yt * lax.rsqrt(jnp.mean(yt * yt, axis=0, keepdims=True) + EPS)).T
        return y * g_row

    ga = col_norm_t(oat_ref[0], gg[0:1, :])
    gb = col_norm_t(obt_ref[0], gg[1:2, :])
    gc = _row_norm(jnp.concatenate([oc_ref[0, c] for c in range(N_HALF)], axis=-1), gg[2:3, :])
    gd = _row_norm(od_ref[0], gg[3:4, :])
    mixed = jnp.concatenate([ga, gb, gc, gd], axis=-1).astype(BF16)
    x1 = x_ref[0] + jnp.dot(mixed, wo_ref[...], preferred_element_type=F32)
    xn = _row_norm(x1, gm_ref[...]).astype(BF16)
    u = jnp.maximum(jnp.dot(xn, wup_ref[...], preferred_element_type=F32), 0.0)
    out_ref[0] = x1 + jnp.dot((u * u).astype(BF16), wdn_ref[...], preferred_element_type=F32)


def _tail(x, oat, obt, oc, od, gg, wo, gm, wup, wdn):
    B, S, _ = x.shape
    tm = TAIL_TM
    nat = lambda w: pl.BlockSpec((1, tm, w), lambda b, t: (b, t, 0))
    ft = pl.BlockSpec((1, GROUP, tm), lambda b, t: (b, 0, t))
    const = lambda a: pl.BlockSpec(a.shape, lambda b, t: (0,) * a.ndim, pipeline_mode=pl.Buffered(1))
    return pl.pallas_call(
        _tail_kernel,
        grid=(B, S // tm),
        in_specs=[nat(D_MODEL), ft, ft, pl.BlockSpec((1, N_HALF, tm, LANES), lambda b, t: (b, 0, t, 0)),
                  nat(GROUP), const(gg), const(wo), const(gm),
                  const(wup), const(wdn)],
        out_specs=nat(D_MODEL),
        out_shape=jax.ShapeDtypeStruct((B, S, D_MODEL), F32),
        compiler_params=_params(2),
        name="tail",
    )(x, oat, obt, oc, od, gg, wo, gm, wup, wdn)


def _rope_tables_t(S):
    inv = 1.0 / (ROPE_THETA ** (jnp.arange(0, MLA_ROPE, 2, dtype=F32) / MLA_ROPE))
    ang = inv[:, None] * jnp.arange(S, dtype=F32)[None, :]
    return jnp.cos(ang), jnp.sin(ang)


def _moba_aug_tables(S):
    pos = np.arange(S, dtype=np.float32) % MOBA_BLOCK
    kpos = np.zeros((AUG_ROWS, S), np.float32)
    kpos[:3] = pos
    aslope = np.zeros((N_HEADS, AUG_ROWS, 1), np.float32)
    rest = (SLOPE_A * np.float32(LOG2E)).astype(np.float32)
    for r in range(3):
        piece = rest.astype(BF16).astype(np.float32)
        aslope[:, r, 0] = piece
        rest = rest - piece
    return jnp.asarray(kpos), jnp.asarray(aslope)


def _layer(x, cos_t, sin_t, kpos_t, aslope, attn_norm_g, w_in, moba_q_g, moba_k_g, mla_qlat_g, mla_kvlat_g, mla_w_uq, mla_w_ukv,
           mla_q_g, mla_k_g, dil_q_g, dil_k_g, swa_q_g, swa_k_g, swa_sinks, group_out_g, w_o, mlp_norm_g,
           w_up, w_down):
    B, S, _ = x.shape
    col = lambda g: g.reshape(-1, 1)
    (aqt, aq32t, ak, ak32, avt, bqt, bk, bvt, cq, ck, cv, dq, dk, dv) = _inproj(
        x, attn_norm_g.reshape(1, -1), w_in.T.astype(BF16), col(moba_q_g), col(moba_k_g), col(mla_qlat_g),
        col(mla_kvlat_g), mla_w_uq.T.astype(BF16), mla_w_ukv.T.astype(BF16), col(mla_q_g), col(mla_k_g),
        col(dil_q_g), col(dil_k_g), col(swa_q_g), col(swa_k_g), cos_t, sin_t, kpos_t, aslope)
    kmean = _kmean(ak32)
    nb = S // MOBA_BLOCK
    kmean_h = kmean.reshape(B, nb, N_HEADS, HEAD_DIM).transpose(0, 2, 1, 3)
    oat = _moba(aqt, aq32t, ak, avt, kmean_h).reshape(B, GROUP, S)
    obt = _mla(bqt, bk, bvt).reshape(B, GROUP, S)
    oc = _dilated(cq, ck, cv)
    od = _swa(dq, dk, dv, swa_sinks)
    return _tail(x, oat, obt, oc, od, group_out_g, w_o.astype(BF16), mlp_norm_g.reshape(1, -1),
                 w_up.astype(BF16), w_down.astype(BF16))


def kernel(x, attn_norm_g, w_in, moba_q_g, moba_k_g, mla_qlat_g, mla_kvlat_g, mla_w_uq, mla_w_ukv, mla_q_g,
           mla_k_g, dil_q_g, dil_k_g, swa_q_g, swa_k_g, swa_sinks, group_out_g, w_o, mlp_norm_g, w_up, w_down):
    S = x.shape[1]
    assert S % max(d * Q_BLOCK for _, d in DILATED_BRANCHES) == 0 and S % INPROJ_TM == 0
    cos_t, sin_t = _rope_tables_t(S)
    kpos_t, aslope = _moba_aug_tables(S)
    params = (attn_norm_g, w_in, moba_q_g, moba_k_g, mla_qlat_g, mla_kvlat_g, mla_w_uq, mla_w_ukv, mla_q_g,
              mla_k_g, dil_q_g, dil_k_g, swa_q_g, swa_k_g, swa_sinks, group_out_g, w_o, mlp_norm_g, w_up, w_down)
    for l in range(attn_norm_g.shape[0]):
        x = _layer(x, cos_t, sin_t, kpos_t, aslope, *[p[l] for p in params])
    return x
```

```python
import functools

import numpy as np
import jax
import jax.numpy as jnp
from jax import lax
from jax.experimental import pallas as pl
from jax.experimental.pallas import tpu as pltpu

F32 = jnp.float32
BF16 = jnp.bfloat16

D_MODEL = 1024
HEAD_DIM = 64
N_HEADS = 4
GROUP = N_HEADS * HEAD_DIM
LANES = 128
N_HALF = GROUP // LANES
MOBA_BLOCK = 256
MOBA_TOPK = 3
MLA_Q_RANK = 256
MLA_KV_RANK = 128
MLA_NOPE = 64
MLA_ROPE = 32
MLA_QK = MLA_NOPE + MLA_ROPE
MLA_QK_PAD = 128
ROPE_THETA = 10000.0
DILATED_BRANCHES = ((128, 1), (512, 4), (2048, 16))
Q_BLOCK = 128
SWA_WINDOW = 128
SWA_KV_HEADS = 2
SWA_KV_WIDTH = SWA_KV_HEADS * HEAD_DIM
D_FF = 4 * D_MODEL
EPS = 1e-6
NEG = -1e30

_WIDTHS = (GROUP, GROUP, GROUP, MLA_Q_RANK, MLA_KV_RANK, MLA_ROPE,
           GROUP, GROUP, GROUP, GROUP, SWA_KV_WIDTH, SWA_KV_WIDTH)
_OFFS = tuple(int(v) for v in np.cumsum((0,) + _WIDTHS))
IN_COLS = _OFFS[-1]
(_A_Q, _A_K, _A_V, _B_QL, _B_KVL, _B_KR, _C_Q, _C_K, _C_V, _D_Q, _D_K, _D_V) = _OFFS[:-1]

VMEM_LIMIT = 56 * 1024 * 1024

INPROJ_TM = 512
TAIL_TM = 256
ATT_TK = 256
ATT_TQ = 2 * ATT_TK
ATT_DK = 128
ATT_DV = HEAD_DIM + 16
AUG_ROWS = 8
LOG2E = 1.4426950408889634

_NT = (((1,), (1,)), ((), ()))


def _alibi_slopes():
    n = 3 * N_HEADS
    idx = np.arange(1, n + 1, dtype=np.float32).reshape(N_HEADS, 3)
    s = np.exp2(-8.0 * idx / n).astype(np.float32)
    return s[:, 0], s[:, 1], s[:, 2]


SLOPE_A, SLOPE_C, SLOPE_D = _alibi_slopes()


def _params(n_axes):
    return pltpu.CompilerParams(dimension_semantics=("arbitrary",) * n_axes,
                                vmem_limit_bytes=VMEM_LIMIT)


def _head_norm_t(sec, g_col, n_heads, width):
    outs = []
    for h in range(n_heads):
        s = sec[h * width:(h + 1) * width, :]
        ms = jnp.sum(s * s, axis=0, keepdims=True) * (1.0 / width)
        outs.append(s * lax.rsqrt(ms + EPS) * g_col)
    return outs


def _inproj_kernel(x_ref, gx_ref, w1t_ref, gaq_ref, gak_ref, gql_ref, gkvl_ref, wuqt_ref, wukvt_ref,
                   gbq_ref, gbk_ref, gcq_ref, gck_ref, gdq_ref, gdk_ref, cos_ref, sin_ref, kpos_ref, aslope_ref,
                   aqt_ref, aq32t_ref, ak_ref, ak32_ref, avt_ref,
                   bqt_ref, bk_ref, bvt_ref,
                   cq_ref, ck_ref, cv_ref, dq_ref, dk_ref, dv_ref,
                   h_scr):
    tm = x_ref.shape[1]
    x = x_ref[0]
    ms = jnp.mean(x * x, axis=-1, keepdims=True)
    xn = (x * lax.rsqrt(ms + EPS) * gx_ref[...]).astype(BF16)
    h_scr[...] = lax.dot_general(w1t_ref[...], xn, _NT, preferred_element_type=F32)

    scale = HEAD_DIM ** -0.5

    ones_rows = jnp.ones((ATT_DV - HEAD_DIM, ATT_TK), F32)

    def store_vt(ref, h, vh):
        for c in range(tm // ATT_TK):
            ref[0, h, c] = jnp.concatenate([vh[:, c * ATT_TK:(c + 1) * ATT_TK], ones_rows], axis=0).astype(BF16)

    pad_rows = jnp.zeros((ATT_DK - HEAD_DIM - AUG_ROWS, tm), F32)
    qa = _head_norm_t(h_scr[_A_Q:_A_Q + GROUP, :], gaq_ref[...], N_HEADS, HEAD_DIM)
    ka = _head_norm_t(h_scr[_A_K:_A_K + GROUP, :], gak_ref[...], N_HEADS, HEAD_DIM)
    ak32_ref[0] = jnp.concatenate(ka, axis=0).T
    kpos = kpos_ref[...]
    for h in range(N_HEADS):
        aq32t_ref[0, h] = qa[h]
        slope_rows = jnp.broadcast_to(aslope_ref[h], (AUG_ROWS, tm))
        aqt_ref[0, h] = jnp.concatenate([qa[h] * (scale * LOG2E), slope_rows, pad_rows], axis=0).astype(BF16)
        ak_ref[0, h] = jnp.concatenate([ka[h], kpos, pad_rows], axis=0).T.astype(BF16)
        store_vt(avt_ref, h, h_scr[_A_V + h * HEAD_DIM:_A_V + (h + 1) * HEAD_DIM, :])

    cos = cos_ref[...]
    sin = sin_ref[...]
    half = MLA_ROPE // 2

    def rope_pad(t, sc):
        x1 = t[MLA_NOPE:MLA_NOPE + half, :]
        x2 = t[MLA_NOPE + half:MLA_QK, :]
        return jnp.concatenate([t[:MLA_NOPE, :] * sc, (x1 * cos - x2 * sin) * sc, (x1 * sin + x2 * cos) * sc,
                                jnp.zeros((MLA_QK_PAD - MLA_QK, tm), F32)], axis=0)

    ql = h_scr[_B_QL:_B_QL + MLA_Q_RANK, :]
    ql = ql * lax.rsqrt(jnp.sum(ql * ql, axis=0, keepdims=True) * (1.0 / MLA_Q_RANK) + EPS) * gql_ref[...]
    qb = jnp.dot(wuqt_ref[...], ql.astype(BF16), preferred_element_type=F32)
    qb = _head_norm_t(qb, gbq_ref[...], N_HEADS, MLA_QK)
    kvl = h_scr[_B_KVL:_B_KVL + MLA_KV_RANK, :]
    kvl = kvl * lax.rsqrt(jnp.sum(kvl * kvl, axis=0, keepdims=True) * (1.0 / MLA_KV_RANK) + EPS) * gkvl_ref[...]
    kvb = jnp.dot(wukvt_ref[...], kvl.astype(BF16), preferred_element_type=F32)
    kr = h_scr[_B_KR:_B_KR + MLA_ROPE, :]
    gbk = gbk_ref[...]
    for h in range(N_HEADS):
        bqt_ref[0, h] = rope_pad(qb[h], MLA_QK ** -0.5 * LOG2E).astype(BF16)
        kh = jnp.concatenate([kvb[h * 2 * HEAD_DIM:h * 2 * HEAD_DIM + MLA_NOPE, :], kr], axis=0)
        kh = kh * lax.rsqrt(jnp.sum(kh * kh, axis=0, keepdims=True) * (1.0 / MLA_QK) + EPS) * gbk
        bk_ref[0, h] = rope_pad(kh, 1.0).T.astype(BF16)
        store_vt(bvt_ref, h, kvb[h * 2 * HEAD_DIM + MLA_NOPE:(h + 1) * 2 * HEAD_DIM, :])

    qc = jnp.concatenate(_head_norm_t(h_scr[_C_Q:_C_Q + GROUP, :], gcq_ref[...], N_HEADS, HEAD_DIM), axis=0)
    kc = jnp.concatenate(_head_norm_t(h_scr[_C_K:_C_K + GROUP, :], gck_ref[...], N_HEADS, HEAD_DIM), axis=0)
    for ref, val in ((cq_ref, qc * scale), (ck_ref, kc), (cv_ref, h_scr[_C_V:_C_V + GROUP, :])):
        for c in range(N_HALF):
            ref[0, c] = val[c * LANES:(c + 1) * LANES, :].T

    qd = jnp.concatenate(_head_norm_t(h_scr[_D_Q:_D_Q + GROUP, :], gdq_ref[...], N_HEADS, HEAD_DIM), axis=0)
    dq_ref[0] = (qd * scale).T.astype(BF16)
    kd = jnp.concatenate(_head_norm_t(h_scr[_D_K:_D_K + SWA_KV_WIDTH, :], gdk_ref[...], SWA_KV_HEADS, HEAD_DIM),
                         axis=0)
    dk_ref[0] = kd.T.astype(BF16)
    dv_ref[0] = h_scr[_D_V:_D_V + SWA_KV_WIDTH, :].T.astype(BF16)


def _inproj(x, gx, w1t, gaq, gak, gql, gkvl, wuqt, wukvt, gbq, gbk, gcq, gck, gdq, gdk, cos_t, sin_t, kpos_t,
            aslope):
    B, S, _ = x.shape
    tm = INPROJ_TM
    nb = S // ATT_TK
    cpt = tm // ATT_TK
    H = N_HEADS
    full = lambda a: pl.BlockSpec(a.shape, lambda b, t: (0,) * a.ndim)
    in_specs = [pl.BlockSpec((1, tm, D_MODEL), lambda b, t: (b, t, 0)), full(gx), full(w1t), full(gaq), full(gak),
                full(gql), full(gkvl), full(wuqt), full(wukvt), full(gbq), full(gbk), full(gcq), full(gck),
                full(gdq), full(gdk),
                pl.BlockSpec((MLA_ROPE // 2, tm), lambda b, t: (0, t)),
                pl.BlockSpec((MLA_ROPE // 2, tm), lambda b, t: (0, t)),
                pl.BlockSpec((AUG_ROWS, tm), lambda b, t: (0, t)), full(aslope)]
    head_t = lambda w: pl.BlockSpec((1, H, w, tm), lambda b, t: (b, 0, 0, t))
    head_n = pl.BlockSpec((1, H, tm, ATT_DK), lambda b, t: (b, 0, t, 0))
    vt_spec = pl.BlockSpec((1, H, cpt, ATT_DV, ATT_TK), lambda b, t: (b, 0, t, 0, 0))
    nat = lambda w: pl.BlockSpec((1, tm, w), lambda b, t: (b, t, 0))
    halves = pl.BlockSpec((1, N_HALF, tm, LANES), lambda b, t: (b, 0, t, 0))
    out_shape = [
        jax.ShapeDtypeStruct((B, H, ATT_DK, S), BF16),
        jax.ShapeDtypeStruct((B, H, HEAD_DIM, S), F32),
        jax.ShapeDtypeStruct((B, H, S, ATT_DK), BF16),
        jax.ShapeDtypeStruct((B, S, GROUP), F32),
        jax.ShapeDtypeStruct((B, H, nb, ATT_DV, ATT_TK), BF16),
        jax.ShapeDtypeStruct((B, H, ATT_DK, S), BF16),
        jax.ShapeDtypeStruct((B, H, S, ATT_DK), BF16),
        jax.ShapeDtypeStruct((B, H, nb, ATT_DV, ATT_TK), BF16),
        jax.ShapeDtypeStruct((B, N_HALF, S, LANES), F32),
        jax.ShapeDtypeStruct((B, N_HALF, S, LANES), F32),
        jax.ShapeDtypeStruct((B, N_HALF, S, LANES), F32),
        jax.ShapeDtypeStruct((B, S, GROUP), BF16),
        jax.ShapeDtypeStruct((B, S, SWA_KV_WIDTH), BF16),
        jax.ShapeDtypeStruct((B, S, SWA_KV_WIDTH), BF16),
    ]
    out_specs = [head_t(ATT_DK), head_t(HEAD_DIM), head_n, nat(GROUP), vt_spec,
                 head_t(ATT_DK), head_n, vt_spec,
                 halves, halves, halves, nat(GROUP), nat(SWA_KV_WIDTH), nat(SWA_KV_WIDTH)]
    return pl.pallas_call(
        _inproj_kernel,
        grid=(B, S // tm),
        in_specs=in_specs,
        out_specs=out_specs,
        out_shape=out_shape,
        scratch_shapes=[pltpu.VMEM((IN_COLS, tm), F32)],
        compiler_params=_params(2),
        name="inproj",
    )(x, gx, w1t, gaq, gak, gql, gkvl, wuqt, wukvt, gbq, gbk, gcq, gck, gdq, gdk, cos_t, sin_t, kpos_t, aslope)


def _kmean_kernel(k_ref, o_ref):
    S = k_ref.shape[1]
    k = k_ref[0].reshape(S // MOBA_BLOCK, MOBA_BLOCK, GROUP)
    o_ref[0] = jnp.sum(k, axis=1) * (1.0 / MOBA_BLOCK)


def _kmean(ak32):
    B, S, _ = ak32.shape
    nb = S // MOBA_BLOCK
    return pl.pallas_call(
        _kmean_kernel,
        grid=(B,),
        in_specs=[pl.BlockSpec((1, S, GROUP), lambda b: (b, 0, 0))],
        out_specs=pl.BlockSpec((1, nb, GROUP), lambda b: (b, 0, 0)),
        out_shape=jax.ShapeDtypeStruct((B, nb, GROUP), F32),
        compiler_params=_params(1),
        name="kmean",
    )(ak32)


def _flash_scratch(tq):
    s_buf, p_buf = pltpu.VMEM((N_HEADS, ATT_TK, tq), F32), pltpu.VMEM((N_HEADS, ATT_TK, tq), BF16)
    return [s_buf, s_buf, p_buf, p_buf, pltpu.VMEM((N_HEADS, ATT_DV, tq), F32)]


def _flash_heads(qt_ref, k_ref, vt_ref, o_ref, s_bufs, p_bufs, acc_scr, rowb_scr, i):
    tq = o_ref.shape[3]
    n_tail = tq // ATT_TK
    assert n_tail == 2
    n_past = i * n_tail
    heads = range(N_HEADS)

    def block_of(pos):
        return jnp.where(pos < n_tail, n_past + pos, pos - n_tail)

    def stage_scores(slot, blk):
        kstart = pl.multiple_of(blk * ATT_TK, ATT_TK)
        for h in heads:
            kb = k_ref[0, h, pl.ds(kstart, ATT_TK), :]
            s_bufs[slot][h] = jnp.dot(kb, qt_ref[0, h], preferred_element_type=F32)

    def stage_softmax(slot, blk, ms, causal=None):
        new_ms, alphas = [], []
        for h in heads:
            m_parts, a_parts = [], []
            rv_row = None if rowb_scr is None else rowb_scr[h, pl.ds(blk, 1), :]
            for c in range(tq // LANES):
                cols = slice(c * LANES, (c + 1) * LANES)
                st = s_bufs[slot][h, :, cols]
                if causal is not None:
                    st = jnp.where(causal[:, cols], st, 2 * NEG)
                cm = jnp.max(st, axis=0, keepdims=True)
                m_old = ms[h][:, cols]
                if rowb_scr is None:
                    m_new = jnp.maximum(m_old, cm)
                    shift = m_new
                else:
                    rv = rv_row[:, cols]
                    m_new = jnp.maximum(m_old, cm + rv)
                    shift = m_new - rv
                a_parts.append(jnp.exp2(m_old - m_new))
                m_parts.append(m_new)
                p_bufs[slot][h, :, cols] = jnp.exp2(st - shift).astype(BF16)
            new_ms.append(jnp.concatenate(m_parts, axis=1))
            alphas.append(jnp.concatenate(a_parts, axis=1))
        return tuple(new_ms), tuple(alphas)

    def stage_values(slot, blk, alphas):
        for h in heads:
            acc_scr[h] = alphas[h] * acc_scr[h] + jnp.dot(vt_ref[0, h, blk], p_bufs[slot][h],
                                                          preferred_element_type=F32)

    krow = lax.broadcasted_iota(jnp.int32, (ATT_TK, tq), 0)
    ti = lax.broadcasted_iota(jnp.int32, (ATT_TK, tq), 1)
    for h in heads:
        acc_scr[h] = jnp.zeros((ATT_DV, tq), F32)
    stage_scores(0, n_past)
    stage_scores(1, n_past + 1)
    ms = tuple(jnp.full((1, tq), NEG, F32) for _ in heads)
    ms, alphas = stage_softmax(0, n_past, ms, causal=ti >= krow)
    stage_values(0, n_past, alphas)
    ms, alphas = stage_softmax(1, n_past + 1, ms, causal=ti >= krow + ATT_TK)
    stage_scores(0, 0)

    def step(pos, slot, ms, alphas):
        ms, new_alphas = stage_softmax(1 - slot, pos + 1 - n_tail, ms)
        stage_values(slot, block_of(pos), alphas)
        stage_scores(slot, pos)
        return ms, new_alphas

    def pair(first, carry):
        ms, alphas = step(first, 1, *carry)
        return step(first + 1, 0, ms, alphas)

    carry = lax.fori_loop(0, i // 2, lambda t, c: pair(4 * t + 3, pair(4 * t + 1, c)), (ms, alphas))
    ms, alphas = lax.cond(i % 2 == 1, lambda c: pair(n_past - 1, c), lambda c: c, carry)
    last = n_past + 1
    stage_values(1, block_of(last), alphas)
    for h in heads:
        acc = acc_scr[h]
        o_ref[0, h] = acc[:HEAD_DIM, :] / acc[HEAD_DIM:HEAD_DIM + 1, :]


def _moba_kernel(qt_ref, q32t_ref, k_ref, vt_ref, kmean_ref, o_ref, rowb_scr, s0, s1, p0, p1, acc_scr):
    i = pl.program_id(1)
    nb = kmean_ref.shape[2]
    tq = qt_ref.shape[3]
    blk = lax.broadcasted_iota(jnp.int32, (nb, tq), 0)
    col = lax.broadcasted_iota(jnp.int32, (nb, tq), 1)
    qblk = i * (tq // MOBA_BLOCK) + col // MOBA_BLOCK
    past = blk < qblk
    dist0 = (i * tq + col - blk * MOBA_BLOCK).astype(F32)
    for h in range(N_HEADS):
        gate = jnp.dot(kmean_ref[0, h], q32t_ref[0, h], preferred_element_type=F32,
                       precision=lax.Precision.HIGHEST)
        gate = jnp.where(past, gate, NEG)
        sel = blk == qblk
        for _ in range(MOBA_TOPK):
            best = jnp.max(gate, axis=0, keepdims=True)
            first = jnp.min(jnp.where(gate == best, blk, nb), axis=0, keepdims=True)
            pick = blk == first
            sel = jnp.logical_or(sel, jnp.logical_and(pick, past))
            gate = jnp.where(pick, -jnp.inf, gate)
        rowb_scr[h] = jnp.where(sel, (-float(SLOPE_A[h]) * LOG2E) * dist0, 2 * NEG)
    _flash_heads(qt_ref, k_ref, vt_ref, o_ref, (s0, s1), (p0, p1), acc_scr, rowb_scr, i)


def _flash_specs(S, tq):
    H, nb = N_HEADS, S // ATT_TK
    return ([pl.BlockSpec((1, H, ATT_DK, tq), lambda b, i: (b, 0, 0, i)),
             pl.BlockSpec((1, H, S, ATT_DK), lambda b, i: (b, 0, 0, 0)),
             pl.BlockSpec((1, H, nb, ATT_DV, ATT_TK), lambda b, i: (b, 0, 0, 0, 0))],
            pl.BlockSpec((1, H, HEAD_DIM, tq), lambda b, i: (b, 0, 0, i)))


def _moba(aqt, aq32t, ak, avt, kmean_h):
    B, H, _, S = aqt.shape
    tq = ATT_TQ
    nb = S // ATT_TK
    (q_spec, k_spec, vt_spec), o_spec = _flash_specs(S, tq)
    return pl.pallas_call(
        _moba_kernel,
        grid=(B, S // tq),
        in_specs=[q_spec, pl.BlockSpec((1, H, HEAD_DIM, tq), lambda b, i: (b, 0, 0, i)), k_spec, vt_spec,
                  pl.BlockSpec((1, H, nb, HEAD_DIM), lambda b, i: (b, 0, 0, 0))],
        out_specs=o_spec,
        out_shape=jax.ShapeDtypeStruct((B, H, HEAD_DIM, S), F32),
        scratch_shapes=[pltpu.VMEM((H, nb, tq), F32)] + _flash_scratch(tq),
        compiler_params=_params(2),
        name="moba",
    )(aqt, aq32t, ak, avt, kmean_h)


def _mla_kernel(qt_ref, k_ref, vt_ref, o_ref, s0, s1, p0, p1, acc_scr):
    _flash_heads(qt_ref, k_ref, vt_ref, o_ref, (s0, s1), (p0, p1), acc_scr, None, pl.program_id(1))


def _mla(bqt, bk, bvt):
    B, H, _, S = bqt.shape
    tq = ATT_TQ
    in_specs, o_spec = _flash_specs(S, tq)
    return pl.pallas_call(
        _mla_kernel,
        grid=(B, S // tq),
        in_specs=in_specs,
        out_specs=o_spec,
        out_shape=jax.ShapeDtypeStruct((B, H, HEAD_DIM, S), F32),
        scratch_shapes=_flash_scratch(tq),
        compiler_params=_params(2),
        name="mla",
    )(bqt, bk, bvt)


DIL_SPAN = max(d for _, d in DILATED_BRANCHES) * Q_BLOCK
DIL_UNITS = DIL_SPAN // Q_BLOCK
DIL_GROUP = 2


def _dilated_bias():
    qi = np.arange(Q_BLOCK)[:, None]
    kidx = np.arange(2 * Q_BLOCK)[None, :]
    rel = qi + Q_BLOCK - kidx
    out = np.empty((len(DILATED_BRANCHES), N_HEADS, Q_BLOCK, 2 * Q_BLOCK), np.float32)
    for bi, (window, d) in enumerate(DILATED_BRANCHES):
        valid = (rel >= 0) & (rel <= window // d)
        for h in range(N_HEADS):
            out[bi, h] = np.where(valid, -SLOPE_C[h] * (d * rel).astype(np.float32), NEG)
    return out


def _rows_load(ref, lead, start, size, stride):
    return jnp.concatenate([ref[lead + (c, pl.ds(start, size, stride=stride), slice(None))]
                            for c in range(N_HALF)], axis=-1)


def _rows_store(ref, lead, start, size, stride, val):
    for c in range(N_HALF):
        ref[lead + (c, pl.ds(start, size, stride=stride), slice(None))] = val[:, c * LANES:(c + 1) * LANES]


def _dilated_kernel(q_ref, kp_ref, kc_ref, vp_ref, vc_ref, bias_ref, o_ref, kbuf, vbuf, m_scr, den_scr, num_scr):
    span = pl.program_id(1)
    kbuf[:, 0:DIL_SPAN, :] = kp_ref[0]
    kbuf[:, DIL_SPAN:, :] = kc_ref[0]
    vbuf[:, 0:DIL_SPAN, :] = vp_ref[0]
    vbuf[:, DIL_SPAN:, :] = vc_ref[0]
    lane_head = lax.broadcasted_iota(jnp.int32, (1, GROUP), 1) // HEAD_DIM
    hmask = [lane_head == h for h in range(N_HEADS)]
    hmask_f = [m.astype(F32) for m in hmask]
    low_neg = jnp.where(lax.broadcasted_iota(jnp.int32, (Q_BLOCK, 2 * Q_BLOCK), 1) < Q_BLOCK, NEG, 0.0)
    first_span = jnp.where(span == 0, 1.0, 0.0)

    def per_head(cols):
        out = cols[N_HEADS - 1]
        for h in range(N_HEADS - 2, -1, -1):
            out = jnp.where(hmask[h], cols[h], out)
        return out

    for bi, (window, d) in enumerate(DILATED_BRANCHES):
        last = bi == len(DILATED_BRANCHES) - 1

        def group(g, _, bi=bi, d=d, last=last):
            fronts = []
            for uu in range(DIL_GROUP):
                u = g * DIL_GROUP + uu
                r, n = u % d, u // d
                qstart = n * (Q_BLOCK * d) + r
                kstart = DIL_SPAN + qstart - Q_BLOCK * d
                q = _rows_load(q_ref, (0,), qstart, Q_BLOCK, d)
                k2 = _rows_load(kbuf, (), kstart, 2 * Q_BLOCK, d).astype(BF16)
                v2 = _rows_load(vbuf, (), kstart, 2 * Q_BLOCK, d).astype(BF16)
                q4 = jnp.concatenate([(q * hmask_f[h]).astype(BF16) for h in range(N_HEADS)], axis=0)
                s4 = lax.dot_general(q4, k2, _NT, preferred_element_type=F32)
                penalty = low_neg * jnp.where(n == 0, first_span, 0.0)
                fronts.append((qstart, s4, v2, penalty))
            for qstart, s4, v2, penalty in fronts:
                es, ms, ls = [], [], []
                for h in range(N_HEADS):
                    s = s4[h * Q_BLOCK:(h + 1) * Q_BLOCK, :] + (bias_ref[bi, h] + penalty)
                    m = jnp.max(s, axis=-1, keepdims=True)
                    e = jnp.exp(s - m)
                    ls.append(jnp.sum(e, axis=-1, keepdims=True))
                    ms.append(m)
                    es.append(e.astype(BF16))
                o4 = jnp.dot(jnp.concatenate(es, axis=0), v2, preferred_element_type=F32)
                o = o4[(N_HEADS - 1) * Q_BLOCK:, :]
                for h in range(N_HEADS - 2, -1, -1):
                    o = jnp.where(hmask[h], o4[h * Q_BLOCK:(h + 1) * Q_BLOCK, :], o)
                m_b, l_b = per_head(ms), per_head(ls)
                at = ((), qstart, Q_BLOCK, d)
                if bi == 0:
                    _rows_store(m_scr, *at, m_b)
                    _rows_store(num_scr, *at, o)
                    _rows_store(den_scr, *at, l_b)
                else:
                    m_old = _rows_load(m_scr, *at)
                    m_new = jnp.maximum(m_old, m_b)
                    a, b = jnp.exp(m_old - m_new), jnp.exp(m_b - m_new)
                    num = a * _rows_load(num_scr, *at) + b * o
                    den = a * _rows_load(den_scr, *at) + b * l_b
                    if last:
                        _rows_store(o_ref, (0,), qstart, Q_BLOCK, d, num / den)
                    else:
                        _rows_store(m_scr, *at, m_new)
                        _rows_store(num_scr, *at, num)
                        _rows_store(den_scr, *at, den)
            return 0

        lax.fori_loop(0, DIL_UNITS // DIL_GROUP, group, 0)


def _dilated(cq, ck, cv):
    B, _, S, _ = cq.shape
    cur = pl.BlockSpec((1, N_HALF, DIL_SPAN, LANES), lambda b, s: (b, 0, s, 0))
    prev = pl.BlockSpec((1, N_HALF, DIL_SPAN, LANES), lambda b, s: (b, 0, jnp.maximum(s - 1, 0), 0))
    bias = jnp.asarray(_dilated_bias())
    return pl.pallas_call(
        _dilated_kernel,
        grid=(B, S // DIL_SPAN),
        in_specs=[cur, prev, cur, prev, cur, pl.BlockSpec(bias.shape, lambda b, s: (0, 0, 0, 0))],
        out_specs=cur,
        out_shape=jax.ShapeDtypeStruct((B, N_HALF, S, LANES), F32),
        scratch_shapes=[pltpu.VMEM((N_HALF, 2 * DIL_SPAN, LANES), F32),
                        pltpu.VMEM((N_HALF, 2 * DIL_SPAN, LANES), F32),
                        pltpu.VMEM((N_HALF, DIL_SPAN, LANES), F32), pltpu.VMEM((N_HALF, DIL_SPAN, LANES), F32),
                        pltpu.VMEM((N_HALF, DIL_SPAN, LANES), F32)],
        compiler_params=_params(2),
        name="dilated",
    )(cq, ck, ck, cv, cv, bias)


def _swa_kernel(sink_ref, q_ref, k_ref, v_ref, o_ref):
    i = pl.program_id(1)
    G = N_HEADS // SWA_KV_HEADS
    start = pl.multiple_of(jnp.maximum(i - 1, 0) * Q_BLOCK, Q_BLOCK)
    q = q_ref[0]
    kw = k_ref[0, pl.ds(start, 2 * Q_BLOCK), :]
    vw = v_ref[0, pl.ds(start, 2 * Q_BLOCK), :]
    tpos = i * Q_BLOCK + lax.broadcasted_iota(jnp.int32, (Q_BLOCK, 2 * Q_BLOCK), 0)
    kpos = start + lax.broadcasted_iota(jnp.int32, (Q_BLOCK, 2 * Q_BLOCK), 1)
    rel = tpos - kpos
    valid = (rel >= 0) & (rel < SWA_WINDOW)
    dist = rel.astype(F32)
    outs = []
    for h in range(N_HEADS):
        g = h // G
        ksl = slice(g * HEAD_DIM, (g + 1) * HEAD_DIM)
        s = lax.dot_general(q[:, h * HEAD_DIM:(h + 1) * HEAD_DIM], kw[:, ksl], _NT, preferred_element_type=F32)
        s = jnp.where(valid, s - float(SLOPE_D[h]) * dist, NEG)
        sink = sink_ref[h]
        m = jnp.maximum(jnp.max(s, axis=-1, keepdims=True), sink)
        p = jnp.exp(s - m)
        l = jnp.sum(p, axis=-1, keepdims=True) + jnp.exp(sink - m)
        outs.append(jnp.dot(p.astype(BF16), vw[:, ksl], preferred_element_type=F32) / l)
    o_ref[0] = jnp.concatenate(outs, axis=-1)


def _swa(dq, dk, dv, sinks):
    B, S, _ = dq.shape
    return pl.pallas_call(
        _swa_kernel,
        grid=(B, S // Q_BLOCK),
        in_specs=[pl.BlockSpec(memory_space=pltpu.SMEM),
                  pl.BlockSpec((1, Q_BLOCK, GROUP), lambda b, i: (b, i, 0)),
                  pl.BlockSpec((1, S, SWA_KV_WIDTH), lambda b, i: (b, 0, 0)),
                  pl.BlockSpec((1, S, SWA_KV_WIDTH), lambda b, i: (b, 0, 0))],
        out_specs=pl.BlockSpec((1, Q_BLOCK, GROUP), lambda b, i: (b, i, 0)),
        out_shape=jax.ShapeDtypeStruct((B, S, GROUP), F32),
        compiler_params=_params(2),
        name="swa",
    )(sinks, dq, dk, dv)


def _row_norm(y, g_row):
    return y * lax.rsqrt(jnp.mean(y * y, axis=-1, keepdims=True) + EPS) * g_row


def _tail_kernel(x_ref, oat_ref, obt_ref, oc_ref, od_ref, gg_ref, wo_ref, gm_ref, wup_ref, wdn_ref, out_ref):
    gg = gg_ref[...]

    def col_norm_t(yt, g_row):
        y = (yt * lax.rsqrt(jnp.mean(yt * yt, axis=0, keepdims=True) + EPS)).T
        return y * g_row

    ga = col_norm_t(oat_ref[0], gg[0:1, :])
    gb = col_norm_t(obt_ref[0], gg[1:2, :])
    gc = _row_norm(jnp.concatenate([oc_ref[0, c] for c in range(N_HALF)], axis=-1), gg[2:3, :])
    gd = _row_norm(od_ref[0], gg[3:4, :])
    mixed = jnp.concatenate([ga, gb, gc, gd], axis=-1).astype(BF16)
    x1 = x_ref[0] + jnp.dot(mixed, wo_ref[...], preferred_element_type=F32)
    xn = _row_norm(x1, gm_ref[...]).astype(BF16)
    u = jnp.maximum(jnp.dot(xn, wup_ref[...], preferred_element_type=F32), 0.0)
    out_ref[0] = x1 + jnp.dot((u * u).astype(BF16), wdn_ref[...], preferred_element_type=F32)


def _tail(x, oat, obt, oc, od, gg, wo, gm, wup, wdn):
    B, S, _ = x.shape
    tm = TAIL_TM
    nat = lambda w: pl.BlockSpec((1, tm, w), lambda b, t: (b, t, 0))
    ft = pl.BlockSpec((1, GROUP, tm), lambda b, t: (b, 0, t))
    const = lambda a: pl.BlockSpec(a.shape, lambda b, t: (0,) * a.ndim, pipeline_mode=pl.Buffered(1))
    return pl.pallas_call(
        _tail_kernel,
        grid=(B, S // tm),
        in_specs=[nat(D_MODEL), ft, ft, pl.BlockSpec((1, N_HALF, tm, LANES), lambda b, t: (b, 0, t, 0)),
                  nat(GROUP), const(gg), const(wo), const(gm),
                  const(wup), const(wdn)],
        out_specs=nat(D_MODEL),
        out_shape=jax.ShapeDtypeStruct((B, S, D_MODEL), F32),
        compiler_params=_params(2),
        name="tail",
    )(x, oat, obt, oc, od, gg, wo, gm, wup, wdn)


def _rope_tables_t(S):
    inv = 1.0 / (ROPE_THETA ** (jnp.arange(0, MLA_ROPE, 2, dtype=F32) / MLA_ROPE))
    ang = inv[:, None] * jnp.arange(S, dtype=F32)[None, :]
    return jnp.cos(ang), jnp.sin(ang)


def _moba_aug_tables(S):
    pos = np.arange(S, dtype=np.float32) % MOBA_BLOCK
    kpos = np.zeros((AUG_ROWS, S), np.float32)
    kpos[:3] = pos
    aslope = np.zeros((N_HEADS, AUG_ROWS, 1), np.float32)
    rest = (SLOPE_A * np.float32(LOG2E)).astype(np.float32)
    for r in range(3):
        piece = rest.astype(BF16).astype(np.float32)
        aslope[:, r, 0] = piece
        rest = rest - piece
    return jnp.asarray(kpos), jnp.asarray(aslope)


def _layer(x, cos_t, sin_t, kpos_t, aslope, attn_norm_g, w_in, moba_q_g, moba_k_g, mla_qlat_g, mla_kvlat_g, mla_w_uq, mla_w_ukv,
           mla_q_g, mla_k_g, dil_q_g, dil_k_g, swa_q_g, swa_k_g, swa_sinks, group_out_g, w_o, mlp_norm_g,
           w_up, w_down):
    B, S, _ = x.shape
    col = lambda g: g.reshape(-1, 1)
    (aqt, aq32t, ak, ak32, avt, bqt, bk, bvt, cq, ck, cv, dq, dk, dv) = _inproj(
        x, attn_norm_g.reshape(1, -1), w_in.T.astype(BF16), col(moba_q_g), col(moba_k_g), col(mla_qlat_g),
        col(mla_kvlat_g), mla_w_uq.T.astype(BF16), mla_w_ukv.T.astype(BF16), col(mla_q_g), col(mla_k_g),
        col(dil_q_g), col(dil_k_g), col(swa_q_g), col(swa_k_g), cos_t, sin_t, kpos_t, aslope)
    kmean = _kmean(ak32)
    nb = S // MOBA_BLOCK
    kmean_h = kmean.reshape(B, nb, N_HEADS, HEAD_DIM).transpose(0, 2, 1, 3)
    oat = _moba(aqt, aq32t, ak, avt, kmean_h).reshape(B, GROUP, S)
    obt = _mla(bqt, bk, bvt).reshape(B, GROUP, S)
    oc = _dilated(cq, ck, cv)
    od = _swa(dq, dk, dv, swa_sinks)
    return _tail(x, oat, obt, oc, od, group_out_g, w_o.astype(BF16), mlp_norm_g.reshape(1, -1),
                 w_up.astype(BF16), w_down.astype(BF16))


def kernel(x, attn_norm_g, w_in, moba_q_g, moba_k_g, mla_qlat_g, mla_kvlat_g, mla_w_uq, mla_w_ukv, mla_q_g,
           mla_k_g, dil_q_g, dil_k_g, swa_q_g, swa_k_g, swa_sinks, group_out_g, w_o, mlp_norm_g, w_up, w_down):
    S = x.shape[1]
    assert S % max(d * Q_BLOCK for _, d in DILATED_BRANCHES) == 0 and S % INPROJ_TM == 0
    cos_t, sin_t = _rope_tables_t(S)
    kpos_t, aslope = _moba_aug_tables(S)
    params = (attn_norm_g, w_in, moba_q_g, moba_k_g, mla_qlat_g, mla_kvlat_g, mla_w_uq, mla_w_ukv, mla_q_g,
              mla_k_g, dil_q_g, dil_k_g, swa_q_g, swa_k_g, swa_sinks, group_out_g, w_o, mlp_norm_g, w_up, w_down)
    for l in range(attn_norm_g.shape[0]):
        x = _layer(x, cos_t, sin_t, kpos_t, aslope, *[p[l] for p in params])
    return x
```

```python
import functools

import numpy as np
import jax
import jax.numpy as jnp
from jax import lax
from jax.experimental import pallas as pl
from jax.experimental.pallas import tpu as pltpu

F32 = jnp.float32
BF16 = jnp.bfloat16

D_MODEL = 1024
HEAD_DIM = 64
N_HEADS = 4
GROUP = N_HEADS * HEAD_DIM
LANES = 128
N_HALF = GROUP // LANES
MOBA_BLOCK = 256
MOBA_TOPK = 3
MLA_Q_RANK = 256
MLA_KV_RANK = 128
MLA_NOPE = 64
MLA_ROPE = 32
MLA_QK = MLA_NOPE + MLA_ROPE
MLA_QK_PAD = 128
ROPE_THETA = 10000.0
DILATED_BRANCHES = ((128, 1), (512, 4), (2048, 16))
Q_BLOCK = 128
SWA_WINDOW = 128
SWA_KV_HEADS = 2
SWA_KV_WIDTH = SWA_KV_HEADS * HEAD_DIM
D_FF = 4 * D_MODEL
EPS = 1e-6
NEG = -1e30

_WIDTHS = (GROUP, GROUP, GROUP, MLA_Q_RANK, MLA_KV_RANK, MLA_ROPE,
           GROUP, GROUP, GROUP, GROUP, SWA_KV_WIDTH, SWA_KV_WIDTH)
_OFFS = tuple(int(v) for v in np.cumsum((0,) + _WIDTHS))
IN_COLS = _OFFS[-1]
(_A_Q, _A_K, _A_V, _B_QL, _B_KVL, _B_KR, _C_Q, _C_K, _C_V, _D_Q, _D_K, _D_V) = _OFFS[:-1]

VMEM_LIMIT = 56 * 1024 * 1024

INPROJ_TM = 512
TAIL_TM = 256
ATT_TK = 256
ATT_TQ = 2 * ATT_TK
ATT_DK = 128
ATT_DV = HEAD_DIM + 16
AUG_ROWS = 8
LOG2E = 1.4426950408889634
FLASH_PAIRS = 2

_NT = (((1,), (1,)), ((), ()))


def _alibi_slopes():
    n = 3 * N_HEADS
    idx = np.arange(1, n + 1, dtype=np.float32).reshape(N_HEADS, 3)
    s = np.exp2(-8.0 * idx / n).astype(np.float32)
    return s[:, 0], s[:, 1], s[:, 2]


SLOPE_A, SLOPE_C, SLOPE_D = _alibi_slopes()


def _params(n_axes):
    return pltpu.CompilerParams(dimension_semantics=("arbitrary",) * n_axes,
                                vmem_limit_bytes=VMEM_LIMIT)


def _head_norm_t(sec, g_col, n_heads, width):
    outs = []
    for h in range(n_heads):
        s = sec[h * width:(h + 1) * width, :]
        ms = jnp.sum(s * s, axis=0, keepdims=True) * (1.0 / width)
        outs.append(s * lax.rsqrt(ms + EPS) * g_col)
    return outs


def _inproj_kernel(x_ref, gx_ref, w1t_ref, gaq_ref, gak_ref, gql_ref, gkvl_ref, wuqt_ref, wukvt_ref,
                   gbq_ref, gbk_ref, gcq_ref, gck_ref, gdq_ref, gdk_ref, cos_ref, sin_ref, kpos_ref, aslope_ref,
                   aqt_ref, aq32t_ref, ak_ref, ak32_ref, avt_ref,
                   bqt_ref, bk_ref, bvt_ref,
                   cq_ref, ck_ref, cv_ref, dq_ref, dk_ref, dv_ref,
                   h_scr):
    tm = x_ref.shape[1]
    x = x_ref[0]
    ms = jnp.mean(x * x, axis=-1, keepdims=True)
    xn = (x * lax.rsqrt(ms + EPS) * gx_ref[...]).astype(BF16)
    h_scr[...] = lax.dot_general(w1t_ref[...], xn, _NT, preferred_element_type=F32)

    scale = HEAD_DIM ** -0.5

    ones_rows = jnp.ones((ATT_DV - HEAD_DIM, ATT_TK), F32)

    def store_vt(ref, h, vh):
        for c in range(tm // ATT_TK):
            ref[0, h, c] = jnp.concatenate([vh[:, c * ATT_TK:(c + 1) * ATT_TK], ones_rows], axis=0).astype(BF16)

    pad_rows = jnp.zeros((ATT_DK - HEAD_DIM - AUG_ROWS, tm), F32)
    qa = _head_norm_t(h_scr[_A_Q:_A_Q + GROUP, :], gaq_ref[...], N_HEADS, HEAD_DIM)
    ka = _head_norm_t(h_scr[_A_K:_A_K + GROUP, :], gak_ref[...], N_HEADS, HEAD_DIM)
    ak32_ref[0] = jnp.concatenate(ka, axis=0).T
    kpos = kpos_ref[...]
    for h in range(N_HEADS):
        aq32t_ref[0, h] = qa[h]
        slope_rows = jnp.broadcast_to(aslope_ref[h], (AUG_ROWS, tm))
        aqt_ref[0, h] = jnp.concatenate([qa[h] * (scale * LOG2E), slope_rows, pad_rows], axis=0).astype(BF16)
        ak_ref[0, h] = jnp.concatenate([ka[h], kpos, pad_rows], axis=0).T.astype(BF16)
        store_vt(avt_ref, h, h_scr[_A_V + h * HEAD_DIM:_A_V + (h + 1) * HEAD_DIM, :])

    cos = cos_ref[...]
    sin = sin_ref[...]
    half = MLA_ROPE // 2

    def rope_pad(t, sc):
        x1 = t[MLA_NOPE:MLA_NOPE + half, :]
        x2 = t[MLA_NOPE + half:MLA_QK, :]
        return jnp.concatenate([t[:MLA_NOPE, :] * sc, (x1 * cos - x2 * sin) * sc, (x1 * sin + x2 * cos) * sc,
                                jnp.zeros((MLA_QK_PAD - MLA_QK, tm), F32)], axis=0)

    ql = h_scr[_B_QL:_B_QL + MLA_Q_RANK, :]
    ql = ql * lax.rsqrt(jnp.sum(ql * ql, axis=0, keepdims=True) * (1.0 / MLA_Q_RANK) + EPS) * gql_ref[...]
    qb = jnp.dot(wuqt_ref[...], ql.astype(BF16), preferred_element_type=F32)
    qb = _head_norm_t(qb, gbq_ref[...], N_HEADS, MLA_QK)
    kvl = h_scr[_B_KVL:_B_KVL + MLA_KV_RANK, :]
    kvl = kvl * lax.rsqrt(jnp.sum(kvl * kvl, axis=0, keepdims=True) * (1.0 / MLA_KV_RANK) + EPS) * gkvl_ref[...]
    kvb = jnp.dot(wukvt_ref[...], kvl.astype(BF16), preferred_element_type=F32)
    kr = h_scr[_B_KR:_B_KR + MLA_ROPE, :]
    gbk = gbk_ref[...]
    for h in range(N_HEADS):
        bqt_ref[0, h] = rope_pad(qb[h], MLA_QK ** -0.5 * LOG2E).astype(BF16)
        kh = jnp.concatenate([kvb[h * 2 * HEAD_DIM:h * 2 * HEAD_DIM + MLA_NOPE, :], kr], axis=0)
        kh = kh * lax.rsqrt(jnp.sum(kh * kh, axis=0, keepdims=True) * (1.0 / MLA_QK) + EPS) * gbk
        bk_ref[0, h] = rope_pad(kh, 1.0).T.astype(BF16)
        store_vt(bvt_ref, h, kvb[h * 2 * HEAD_DIM + MLA_NOPE:(h + 1) * 2 * HEAD_DIM, :])

    qc = jnp.concatenate(_head_norm_t(h_scr[_C_Q:_C_Q + GROUP, :], gcq_ref[...], N_HEADS, HEAD_DIM), axis=0)
    kc = jnp.concatenate(_head_norm_t(h_scr[_C_K:_C_K + GROUP, :], gck_ref[...], N_HEADS, HEAD_DIM), axis=0)
    for ref, val in ((cq_ref, qc * (scale * LOG2E)), (ck_ref, kc), (cv_ref, h_scr[_C_V:_C_V + GROUP, :])):
        for c in range(N_HALF):
            ref[0, c] = val[c * LANES:(c + 1) * LANES, :].T

    qd = _head_norm_t(h_scr[_D_Q:_D_Q + GROUP, :], gdq_ref[...], N_HEADS, HEAD_DIM)
    qd = jnp.concatenate([qd[h] for h in SWA_HEAD_ORDER], axis=0)
    dq_ref[0] = (qd * (scale * LOG2E)).T.astype(BF16)
    kd = jnp.concatenate(_head_norm_t(h_scr[_D_K:_D_K + SWA_KV_WIDTH, :], gdk_ref[...], SWA_KV_HEADS, HEAD_DIM),
                         axis=0)
    dk_ref[0] = kd.T.astype(BF16)
    dv_ref[0] = h_scr[_D_V:_D_V + SWA_KV_WIDTH, :].T.astype(BF16)


def _inproj(x, gx, w1t, gaq, gak, gql, gkvl, wuqt, wukvt, gbq, gbk, gcq, gck, gdq, gdk, cos_t, sin_t, kpos_t,
            aslope):
    B, S, _ = x.shape
    tm = INPROJ_TM
    nb = S // ATT_TK
    cpt = tm // ATT_TK
    H = N_HEADS
    full = lambda a: pl.BlockSpec(a.shape, lambda b, t: (0,) * a.ndim)
    in_specs = [pl.BlockSpec((1, tm, D_MODEL), lambda b, t: (b, t, 0)), full(gx), full(w1t), full(gaq), full(gak),
                full(gql), full(gkvl), full(wuqt), full(wukvt), full(gbq), full(gbk), full(gcq), full(gck),
                full(gdq), full(gdk),
                pl.BlockSpec((MLA_ROPE // 2, tm), lambda b, t: (0, t)),
                pl.BlockSpec((MLA_ROPE // 2, tm), lambda b, t: (0, t)),
                pl.BlockSpec((AUG_ROWS, tm), lambda b, t: (0, t)), full(aslope)]
    head_t = lambda w: pl.BlockSpec((1, H, w, tm), lambda b, t: (b, 0, 0, t))
    head_n = pl.BlockSpec((1, H, tm, ATT_DK), lambda b, t: (b, 0, t, 0))
    vt_spec = pl.BlockSpec((1, H, cpt, ATT_DV, ATT_TK), lambda b, t: (b, 0, t, 0, 0))
    nat = lambda w: pl.BlockSpec((1, tm, w), lambda b, t: (b, t, 0))
    halves = pl.BlockSpec((1, N_HALF, tm, LANES), lambda b, t: (b, 0, t, 0))
    out_shape = [
        jax.ShapeDtypeStruct((B, H, ATT_DK, S), BF16),
        jax.ShapeDtypeStruct((B, H, HEAD_DIM, S), F32),
        jax.ShapeDtypeStruct((B, H, S, ATT_DK), BF16),
        jax.ShapeDtypeStruct((B, S, GROUP), F32),
        jax.ShapeDtypeStruct((B, H, nb, ATT_DV, ATT_TK), BF16),
        jax.ShapeDtypeStruct((B, H, ATT_DK, S), BF16),
        jax.ShapeDtypeStruct((B, H, S, ATT_DK), BF16),
        jax.ShapeDtypeStruct((B, H, nb, ATT_DV, ATT_TK), BF16),
        jax.ShapeDtypeStruct((B, N_HALF, S, LANES), F32),
        jax.ShapeDtypeStruct((B, N_HALF, S, LANES), F32),
        jax.ShapeDtypeStruct((B, N_HALF, S, LANES), F32),
        jax.ShapeDtypeStruct((B, S, GROUP), BF16),
        jax.ShapeDtypeStruct((B, S, SWA_KV_WIDTH), BF16),
        jax.ShapeDtypeStruct((B, S, SWA_KV_WIDTH), BF16),
    ]
    out_specs = [head_t(ATT_DK), head_t(HEAD_DIM), head_n, nat(GROUP), vt_spec,
                 head_t(ATT_DK), head_n, vt_spec,
                 halves, halves, halves, nat(GROUP), nat(SWA_KV_WIDTH), nat(SWA_KV_WIDTH)]
    return pl.pallas_call(
        _inproj_kernel,
        grid=(B, S // tm),
        in_specs=in_specs,
        out_specs=out_specs,
        out_shape=out_shape,
        scratch_shapes=[pltpu.VMEM((IN_COLS, tm), F32)],
        compiler_params=_params(2),
        name="inproj",
    )(x, gx, w1t, gaq, gak, gql, gkvl, wuqt, wukvt, gbq, gbk, gcq, gck, gdq, gdk, cos_t, sin_t, kpos_t, aslope)


def _kmean_kernel(k_ref, o_ref):
    S = k_ref.shape[1]
    k = k_ref[0].reshape(S // MOBA_BLOCK, MOBA_BLOCK, GROUP)
    o_ref[0] = jnp.sum(k, axis=1) * (1.0 / MOBA_BLOCK)


def _kmean(ak32):
    B, S, _ = ak32.shape
    nb = S // MOBA_BLOCK
    return pl.pallas_call(
        _kmean_kernel,
        grid=(B,),
        in_specs=[pl.BlockSpec((1, S, GROUP), lambda b: (b, 0, 0))],
        out_specs=pl.BlockSpec((1, nb, GROUP), lambda b: (b, 0, 0)),
        out_shape=jax.ShapeDtypeStruct((B, nb, GROUP), F32),
        compiler_params=_params(1),
        name="kmean",
    )(ak32)


def _flash_scratch(tq):
    s_buf, p_buf = pltpu.VMEM((N_HEADS, ATT_TK, tq), F32), pltpu.VMEM((N_HEADS, ATT_TK, tq), BF16)
    return [s_buf, s_buf, p_buf, p_buf, pltpu.VMEM((N_HEADS, ATT_DV, tq), F32)]


def _flash_heads(qt_ref, k_ref, vt_ref, o_ref, s_bufs, p_bufs, acc_scr, rowb_scr, i):
    tq = o_ref.shape[3]
    n_tail = tq // ATT_TK
    assert n_tail == 2
    n_past = i * n_tail
    heads = range(N_HEADS)

    def block_of(pos):
        return jnp.where(pos < n_tail, n_past + pos, pos - n_tail)

    def stage_scores(slot, blk):
        kstart = pl.multiple_of(blk * ATT_TK, ATT_TK)
        for h in heads:
            kb = k_ref[0, h, pl.ds(kstart, ATT_TK), :]
            s_bufs[slot][h] = jnp.dot(kb, qt_ref[0, h], preferred_element_type=F32)

    def stage_softmax(slot, blk, ms, causal=None):
        new_ms, alphas = [], []
        for h in heads:
            m_parts, a_parts = [], []
            rv_row = None if rowb_scr is None else rowb_scr[h, pl.ds(blk, 1), :]
            for c in range(tq // LANES):
                cols = slice(c * LANES, (c + 1) * LANES)
                st = s_bufs[slot][h, :, cols]
                if causal is not None:
                    st = jnp.where(causal[:, cols], st, 2 * NEG)
                cm = jnp.max(st, axis=0, keepdims=True)
                m_old = ms[h][:, cols]
                if rowb_scr is None:
                    m_new = jnp.maximum(m_old, cm)
                    shift = m_new
                else:
                    rv = rv_row[:, cols]
                    m_new = jnp.maximum(m_old, cm + rv)
                    shift = m_new - rv
                a_parts.append(jnp.exp2(m_old - m_new))
                m_parts.append(m_new)
                p_bufs[slot][h, :, cols] = jnp.exp2(st - shift).astype(BF16)
            new_ms.append(jnp.concatenate(m_parts, axis=1))
            alphas.append(jnp.concatenate(a_parts, axis=1))
        return tuple(new_ms), tuple(alphas)

    def stage_values(slot, blk, alphas):
        for h in heads:
            acc_scr[h] = alphas[h] * acc_scr[h] + jnp.dot(vt_ref[0, h, blk], p_bufs[slot][h],
                                                          preferred_element_type=F32)

    krow = lax.broadcasted_iota(jnp.int32, (ATT_TK, tq), 0)
    ti = lax.broadcasted_iota(jnp.int32, (ATT_TK, tq), 1)
    for h in heads:
        acc_scr[h] = jnp.zeros((ATT_DV, tq), F32)
    stage_scores(0, n_past)
    stage_scores(1, n_past + 1)
    ms = tuple(jnp.full((1, tq), NEG, F32) for _ in heads)
    ms, alphas = stage_softmax(0, n_past, ms, causal=ti >= krow)
    stage_values(0, n_past, alphas)
    ms, alphas = stage_softmax(1, n_past + 1, ms, causal=ti >= krow + ATT_TK)
    stage_scores(0, 0)

    def step(pos, slot, ms, alphas):
        ms, new_alphas = stage_softmax(1 - slot, pos + 1 - n_tail, ms)
        stage_values(slot, block_of(pos), alphas)
        stage_scores(slot, pos)
        return ms, new_alphas

    def pair(first, carry):
        ms, alphas = step(first, 1, *carry)
        return step(first + 1, 0, ms, alphas)

    def trip(t, carry):
        for u in range(FLASH_PAIRS):
            carry = pair(2 * (FLASH_PAIRS * t + u) + 1, carry)
        return carry

    carry = lax.fori_loop(0, i // FLASH_PAIRS, trip, (ms, alphas))
    ms, alphas = lax.fori_loop(i - i % FLASH_PAIRS, i, lambda u, c: pair(2 * u + 1, c), carry)
    last = n_past + 1
    stage_values(1, block_of(last), alphas)
    for h in heads:
        acc = acc_scr[h]
        o_ref[0, h] = acc[:HEAD_DIM, :] / acc[HEAD_DIM:HEAD_DIM + 1, :]


def _moba_kernel(qt_ref, q32t_ref, k_ref, vt_ref, kmean_ref, o_ref, rowb_scr, s0, s1, p0, p1, acc_scr):
    i = pl.program_id(1)
    nb = kmean_ref.shape[2]
    tq = qt_ref.shape[3]
    blk = lax.broadcasted_iota(jnp.int32, (nb, tq), 0)
    col = lax.broadcasted_iota(jnp.int32, (nb, tq), 1)
    qblk = i * (tq // MOBA_BLOCK) + col // MOBA_BLOCK
    past = blk < qblk
    dist0 = (i * tq + col - blk * MOBA_BLOCK).astype(F32)
    for h in range(N_HEADS):
        gate = jnp.dot(kmean_ref[0, h], q32t_ref[0, h], preferred_element_type=F32,
                       precision=lax.Precision.HIGHEST)
        gate = jnp.where(past, gate, NEG)
        sel = blk == qblk
        for _ in range(MOBA_TOPK):
            best = jnp.max(gate, axis=0, keepdims=True)
            first = jnp.min(jnp.where(gate == best, blk, nb), axis=0, keepdims=True)
            pick = blk == first
            sel = jnp.logical_or(sel, jnp.logical_and(pick, past))
            gate = jnp.where(pick, -jnp.inf, gate)
        rowb_scr[h] = jnp.where(sel, (-float(SLOPE_A[h]) * LOG2E) * dist0, 2 * NEG)
    _flash_heads(qt_ref, k_ref, vt_ref, o_ref, (s0, s1), (p0, p1), acc_scr, rowb_scr, i)


def _flash_specs(S, tq):
    H, nb = N_HEADS, S // ATT_TK
    return ([pl.BlockSpec((1, H, ATT_DK, tq), lambda b, i: (b, 0, 0, i)),
             pl.BlockSpec((1, H, S, ATT_DK), lambda b, i: (b, 0, 0, 0)),
             pl.BlockSpec((1, H, nb, ATT_DV, ATT_TK), lambda b, i: (b, 0, 0, 0, 0))],
            pl.BlockSpec((1, H, HEAD_DIM, tq), lambda b, i: (b, 0, 0, i)))


def _moba(aqt, aq32t, ak, avt, kmean_h):
    B, H, _, S = aqt.shape
    tq = ATT_TQ
    nb = S // ATT_TK
    (q_spec, k_spec, vt_spec), o_spec = _flash_specs(S, tq)
    return pl.pallas_call(
        _moba_kernel,
        grid=(B, S // tq),
        in_specs=[q_spec, pl.BlockSpec((1, H, HEAD_DIM, tq), lambda b, i: (b, 0, 0, i)), k_spec, vt_spec,
                  pl.BlockSpec((1, H, nb, HEAD_DIM), lambda b, i: (b, 0, 0, 0))],
        out_specs=o_spec,
        out_shape=jax.ShapeDtypeStruct((B, H, HEAD_DIM, S), F32),
        scratch_shapes=[pltpu.VMEM((H, nb, tq), F32)] + _flash_scratch(tq),
        compiler_params=_params(2),
        name="moba",
    )(aqt, aq32t, ak, avt, kmean_h)


def _mla_kernel(qt_ref, k_ref, vt_ref, o_ref, s0, s1, p0, p1, acc_scr):
    _flash_heads(qt_ref, k_ref, vt_ref, o_ref, (s0, s1), (p0, p1), acc_scr, None, pl.program_id(1))


def _mla(bqt, bk, bvt):
    B, H, _, S = bqt.shape
    tq = ATT_TQ
    in_specs, o_spec = _flash_specs(S, tq)
    return pl.pallas_call(
        _mla_kernel,
        grid=(B, S // tq),
        in_specs=in_specs,
        out_specs=o_spec,
        out_shape=jax.ShapeDtypeStruct((B, H, HEAD_DIM, S), F32),
        scratch_shapes=_flash_scratch(tq),
        compiler_params=_params(2),
        name="mla",
    )(bqt, bk, bvt)


DIL_SPAN = max(d for _, d in DILATED_BRANCHES) * Q_BLOCK
DIL_UNITS = DIL_SPAN // Q_BLOCK
DIL_GROUP = 2


def _dilated_bias():
    qi = np.arange(Q_BLOCK)[:, None]
    kidx = np.arange(2 * Q_BLOCK)[None, :]
    rel = qi + Q_BLOCK - kidx
    out = np.empty((len(DILATED_BRANCHES), N_HEADS, Q_BLOCK, 2 * Q_BLOCK), np.float32)
    for bi, (window, d) in enumerate(DILATED_BRANCHES):
        valid = (rel >= 0) & (rel <= window // d)
        for h in range(N_HEADS):
            out[bi, h] = np.where(valid, -SLOPE_C[h] * np.float32(LOG2E) * (d * rel).astype(np.float32), NEG)
    return out


def _rows_load(ref, lead, start, size, stride):
    return jnp.concatenate([ref[lead + (c, pl.ds(start, size, stride=stride), slice(None))]
                            for c in range(N_HALF)], axis=-1)


def _rows_store(ref, lead, start, size, stride, val):
    for c in range(N_HALF):
        ref[lead + (c, pl.ds(start, size, stride=stride), slice(None))] = val[:, c * LANES:(c + 1) * LANES]


def _dilated_kernel(q_ref, kp_ref, kc_ref, vp_ref, vc_ref, bias_ref, o_ref, kbuf, vbuf, m_scr, den_scr, num_scr):
    span = pl.program_id(1)
    kbuf[:, 0:DIL_SPAN, :] = kp_ref[0]
    kbuf[:, DIL_SPAN:, :] = kc_ref[0]
    vbuf[:, 0:DIL_SPAN, :] = vp_ref[0]
    vbuf[:, DIL_SPAN:, :] = vc_ref[0]
    lane_head = lax.broadcasted_iota(jnp.int32, (1, GROUP), 1) // HEAD_DIM
    hmask = [lane_head == h for h in range(N_HEADS)]
    hmask_f = [m.astype(F32) for m in hmask]
    low_neg = jnp.where(lax.broadcasted_iota(jnp.int32, (Q_BLOCK, 2 * Q_BLOCK), 1) < Q_BLOCK, NEG, 0.0)
    first_span = jnp.where(span == 0, 1.0, 0.0)

    def per_head(cols):
        out = cols[N_HEADS - 1]
        for h in range(N_HEADS - 2, -1, -1):
            out = jnp.where(hmask[h], cols[h], out)
        return out

    for bi, (window, d) in enumerate(DILATED_BRANCHES):
        last = bi == len(DILATED_BRANCHES) - 1

        def group(g, _, bi=bi, d=d, last=last):
            fronts = []
            for uu in range(DIL_GROUP):
                u = g * DIL_GROUP + uu
                r, n = u % d, u // d
                qstart = n * (Q_BLOCK * d) + r
                kstart = DIL_SPAN + qstart - Q_BLOCK * d
                q = _rows_load(q_ref, (0,), qstart, Q_BLOCK, d)
                k2 = _rows_load(kbuf, (), kstart, 2 * Q_BLOCK, d).astype(BF16)
                v2 = _rows_load(vbuf, (), kstart, 2 * Q_BLOCK, d).astype(BF16)
                q4 = jnp.concatenate([(q * hmask_f[h]).astype(BF16) for h in range(N_HEADS)], axis=0)
                s4 = lax.dot_general(q4, k2, _NT, preferred_element_type=F32)
                penalty = low_neg * jnp.where(n == 0, first_span, 0.0)
                fronts.append((qstart, s4, v2, penalty))
            for qstart, s4, v2, penalty in fronts:
                es, ms, ls = [], [], []
                for h in range(N_HEADS):
                    s = s4[h * Q_BLOCK:(h + 1) * Q_BLOCK, :] + (bias_ref[bi, h] + penalty)
                    m = jnp.max(s, axis=-1, keepdims=True)
                    e = jnp.exp2(s - m)
                    ls.append(jnp.sum(e, axis=-1, keepdims=True))
                    ms.append(m)
                    es.append(e.astype(BF16))
                o4 = jnp.dot(jnp.concatenate(es, axis=0), v2, preferred_element_type=F32)
                o = o4[(N_HEADS - 1) * Q_BLOCK:, :]
                for h in range(N_HEADS - 2, -1, -1):
                    o = jnp.where(hmask[h], o4[h * Q_BLOCK:(h + 1) * Q_BLOCK, :], o)
                m_b, l_b = per_head(ms), per_head(ls)
                at = ((), qstart, Q_BLOCK, d)
                if bi == 0:
                    _rows_store(m_scr, *at, m_b)
                    _rows_store(num_scr, *at, o)
                    _rows_store(den_scr, *at, l_b)
                else:
                    m_old = _rows_load(m_scr, *at)
                    m_new = jnp.maximum(m_old, m_b)
                    a, b = jnp.exp2(m_old - m_new), jnp.exp2(m_b - m_new)
                    num = a * _rows_load(num_scr, *at) + b * o
                    den = a * _rows_load(den_scr, *at) + b * l_b
                    if last:
                        _rows_store(o_ref, (0,), qstart, Q_BLOCK, d, num / den)
                    else:
                        _rows_store(m_scr, *at, m_new)
                        _rows_store(num_scr, *at, num)
                        _rows_store(den_scr, *at, den)
            return 0

        lax.fori_loop(0, DIL_UNITS // DIL_GROUP, group, 0)


def _dilated(cq, ck, cv):
    B, _, S, _ = cq.shape
    cur = pl.BlockSpec((1, N_HALF, DIL_SPAN, LANES), lambda b, s: (b, 0, s, 0))
    prev = pl.BlockSpec((1, N_HALF, DIL_SPAN, LANES), lambda b, s: (b, 0, jnp.maximum(s - 1, 0), 0))
    bias = jnp.asarray(_dilated_bias())
    return pl.pallas_call(
        _dilated_kernel,
        grid=(B, S // DIL_SPAN),
        in_specs=[cur, prev, cur, prev, cur, pl.BlockSpec(bias.shape, lambda b, s: (0, 0, 0, 0))],
        out_specs=cur,
        out_shape=jax.ShapeDtypeStruct((B, N_HALF, S, LANES), F32),
        scratch_shapes=[pltpu.VMEM((N_HALF, 2 * DIL_SPAN, LANES), F32),
                        pltpu.VMEM((N_HALF, 2 * DIL_SPAN, LANES), F32),
                        pltpu.VMEM((N_HALF, DIL_SPAN, LANES), F32), pltpu.VMEM((N_HALF, DIL_SPAN, LANES), F32),
                        pltpu.VMEM((N_HALF, DIL_SPAN, LANES), F32)],
        compiler_params=_params(2),
        name="dilated",
    )(cq, ck, ck, cv, cv, bias)


SWA_SPAN = 1024
SWA_GROUP = 2
SWA_HEAD_ORDER = (0, 2, 1, 3)


def _swa_bias():
    qi = np.arange(Q_BLOCK)[:, None]
    kidx = np.arange(2 * Q_BLOCK)[None, :]
    rel = qi + Q_BLOCK - kidx
    valid = (rel >= 0) & (rel < SWA_WINDOW)
    out = np.empty((N_HEADS, Q_BLOCK, 2 * Q_BLOCK), np.float32)
    for h in range(N_HEADS):
        out[h] = np.where(valid, -SLOPE_D[h] * np.float32(LOG2E) * rel.astype(np.float32), NEG)
    return out


def _swa_kernel(sink_ref, q_ref, k_ref, v_ref, bias_ref, o_ref):
    span = pl.program_id(1)
    units = SWA_SPAN // Q_BLOCK
    half = lax.broadcasted_iota(jnp.int32, (1, LANES), 1) // HEAD_DIM
    lo = half == 0
    kv_mask = [jnp.where(half == g, 1.0, 0.0).astype(BF16) for g in range(SWA_KV_HEADS)]
    low_neg = jnp.where(lax.broadcasted_iota(jnp.int32, (Q_BLOCK, 2 * Q_BLOCK), 1) < Q_BLOCK, NEG, 0.0)

    def group(g, _):
        fronts = []
        for uu in range(SWA_GROUP):
            u = g * SWA_GROUP + uu
            n = span * units + u
            lo_start = pl.multiple_of(jnp.maximum(n - 1, 0) * Q_BLOCK, Q_BLOCK)
            hi_start = pl.multiple_of(n * Q_BLOCK, Q_BLOCK)
            qstart = pl.multiple_of(u * Q_BLOCK, Q_BLOCK)
            q = q_ref[0, pl.ds(qstart, Q_BLOCK), :]
            k2 = jnp.concatenate([k_ref[0, pl.ds(lo_start, Q_BLOCK), :], k_ref[0, pl.ds(hi_start, Q_BLOCK), :]],
                                 axis=0)
            v2 = jnp.concatenate([v_ref[0, pl.ds(lo_start, Q_BLOCK), :], v_ref[0, pl.ds(hi_start, Q_BLOCK), :]],
                                 axis=0)
            q4 = jnp.concatenate([q[:, (h % 2) * LANES:(h % 2 + 1) * LANES] * kv_mask[h // 2]
                                  for h in range(N_HEADS)], axis=0)
            s4 = lax.dot_general(q4, k2, _NT, preferred_element_type=F32)
            penalty = low_neg * jnp.where(n == 0, 1.0, 0.0)
            fronts.append((qstart, s4, v2, penalty))
        for qstart, s4, v2, penalty in fronts:
            es, ls = [], []
            for h in range(N_HEADS):
                s = s4[h * Q_BLOCK:(h + 1) * Q_BLOCK, :] + (bias_ref[h] + penalty)
                sink = sink_ref[h] * LOG2E
                m = jnp.maximum(jnp.max(s, axis=-1, keepdims=True), sink)
                e = jnp.exp2(s - m)
                ls.append(jnp.sum(e, axis=-1, keepdims=True) + jnp.exp2(sink - m))
                es.append(e.astype(BF16))
            o4 = jnp.dot(jnp.concatenate(es, axis=0), v2, preferred_element_type=F32)
            tiles = []
            for t in range(N_HALF):
                a, b = t, t + 2
                tiles.append(jnp.where(lo, o4[a * Q_BLOCK:(a + 1) * Q_BLOCK, :] / ls[a],
                                       o4[b * Q_BLOCK:(b + 1) * Q_BLOCK, :] / ls[b]))
            o_ref[0, pl.ds(qstart, Q_BLOCK), :] = jnp.concatenate(tiles, axis=-1)
        return 0

    lax.fori_loop(0, units // SWA_GROUP, group, 0)


def _swa(dq, dk, dv, sinks):
    B, S, _ = dq.shape
    bias = jnp.asarray(_swa_bias())
    return pl.pallas_call(
        _swa_kernel,
        grid=(B, S // SWA_SPAN),
        in_specs=[pl.BlockSpec(memory_space=pltpu.SMEM),
                  pl.BlockSpec((1, SWA_SPAN, GROUP), lambda b, i: (b, i, 0)),
                  pl.BlockSpec((1, S, SWA_KV_WIDTH), lambda b, i: (b, 0, 0)),
                  pl.BlockSpec((1, S, SWA_KV_WIDTH), lambda b, i: (b, 0, 0)),
                  pl.BlockSpec(bias.shape, lambda b, i: (0, 0, 0))],
        out_specs=pl.BlockSpec((1, SWA_SPAN, GROUP), lambda b, i: (b, i, 0)),
        out_shape=jax.ShapeDtypeStruct((B, S, GROUP), F32),
        compiler_params=_params(2),
        name="swa",
    )(sinks, dq, dk, dv, bias)


def _row_norm(y, g_row):
    return y * lax.rsqrt(jnp.mean(y * y, axis=-1, keepdims=True) + EPS) * g_row


def _tail_kernel(x_ref, oat_ref, obt_ref, oc_ref, od_ref, gg_ref, wo_ref, gm_ref, wup_ref, wdn_ref, out_ref):
    gg = gg_ref[...]

    def col_norm_t(yt, g_row):
        y = (yt * lax.rsqrt(jnp.mean(yt * yt, axis=0, keepdims=True) + EPS)).T
        return y * g_row

    ga = col_norm_t(oat_ref[0], gg[0:1, :])
    gb = col_norm_t(obt_ref[0], gg[1:2, :])
    gc = _row_norm(jnp.concatenate([oc_ref[0, c] for c in range(N_HALF)], axis=-1), gg[2:3, :])
    gd = _row_norm(od_ref[0], gg[3:4, :])
    mixed = jnp.concatenate([ga, gb, gc, gd], axis=-1).astype(BF16)
    x1 = x_ref[0] + jnp.dot(mixed, wo_ref[...], preferred_element_type=F32)
    xn = _row_norm(x1, gm_ref[...]).astype(BF16)
    u = jnp.maximum(jnp.dot(xn, wup_ref[...], preferred_element_type=F32), 0.0)
    out_ref[0] = x1 + jnp.dot((u * u).astype(BF16), wdn_ref[...], preferred_element_type=F32)


def _tail(x, oat, obt, oc, od, gg, wo, gm, wup, wdn):
    B, S, _ = x.shape
    tm = TAIL_TM
    nat = lambda w: pl.BlockSpec((1, tm, w), lambda b, t: (b, t, 0))
    ft = pl.BlockSpec((1, GROUP, tm), lambda b, t: (b, 0, t))
    const = lambda a: pl.BlockSpec(a.shape, lambda b, t: (0,) * a.ndim, pipeline_mode=pl.Buffered(1))
    return pl.pallas_call(
        _tail_kernel,
        grid=(B, S // tm),
        in_specs=[nat(D_MODEL), ft, ft, pl.BlockSpec((1, N_HALF, tm, LANES), lambda b, t: (b, 0, t, 0)),
                  nat(GROUP), const(gg), const(wo), const(gm),
                  const(wup), const(wdn)],
        out_specs=nat(D_MODEL),
        out_shape=jax.ShapeDtypeStruct((B, S, D_MODEL), F32),
        compiler_params=_params(2),
        name="tail",
    )(x, oat, obt, oc, od, gg, wo, gm, wup, wdn)


def _rope_tables_t(S):
    inv = 1.0 / (ROPE_THETA ** (jnp.arange(0, MLA_ROPE, 2, dtype=F32) / MLA_ROPE))
    ang = inv[:, None] * jnp.arange(S, dtype=F32)[None, :]
    return jnp.cos(ang), jnp.sin(ang)


def _moba_aug_tables(S):
    pos = np.arange(S, dtype=np.float32) % MOBA_BLOCK
    kpos = np.zeros((AUG_ROWS, S), np.float32)
    kpos[:3] = pos
    aslope = np.zeros((N_HEADS, AUG_ROWS, 1), np.float32)
    rest = (SLOPE_A * np.float32(LOG2E)).astype(np.float32)
    for r in range(3):
        piece = rest.astype(BF16).astype(np.float32)
        aslope[:, r, 0] = piece
        rest = rest - piece
    return jnp.asarray(kpos), jnp.asarray(aslope)


def _layer(x, cos_t, sin_t, kpos_t, aslope, attn_norm_g, w_in, moba_q_g, moba_k_g, mla_qlat_g, mla_kvlat_g, mla_w_uq, mla_w_ukv,
           mla_q_g, mla_k_g, dil_q_g, dil_k_g, swa_q_g, swa_k_g, swa_sinks, group_out_g, w_o, mlp_norm_g,
           w_up, w_down):
    B, S, _ = x.shape
    col = lambda g: g.reshape(-1, 1)
    (aqt, aq32t, ak, ak32, avt, bqt, bk, bvt, cq, ck, cv, dq, dk, dv) = _inproj(
        x, attn_norm_g.reshape(1, -1), w_in.T.astype(BF16), col(moba_q_g), col(moba_k_g), col(mla_qlat_g),
        col(mla_kvlat_g), mla_w_uq.T.astype(BF16), mla_w_ukv.T.astype(BF16), col(mla_q_g), col(mla_k_g),
        col(dil_q_g), col(dil_k_g), col(swa_q_g), col(swa_k_g), cos_t, sin_t, kpos_t, aslope)
    kmean = _kmean(ak32)
    nb = S // MOBA_BLOCK
    kmean_h = kmean.reshape(B, nb, N_HEADS, HEAD_DIM).transpose(0, 2, 1, 3)
    oat = _moba(aqt, aq32t, ak, avt, kmean_h).reshape(B, GROUP, S)
    obt = _mla(bqt, bk, bvt).reshape(B, GROUP, S)
    oc = _dilated(cq, ck, cv)
    od = _swa(dq, dk, dv, swa_sinks)
    perm = np.concatenate([np.arange(HEAD_DIM) + HEAD_DIM * h for h in SWA_HEAD_ORDER])
    gg = group_out_g.at[3].set(group_out_g[3][perm])
    wo = jnp.concatenate([w_o[:3 * GROUP], w_o[3 * GROUP + perm]], axis=0)
    return _tail(x, oat, obt, oc, od, gg, wo.astype(BF16), mlp_norm_g.reshape(1, -1),
                 w_up.astype(BF16), w_down.astype(BF16))


def kernel(x, attn_norm_g, w_in, moba_q_g, moba_k_g, mla_qlat_g, mla_kvlat_g, mla_w_uq, mla_w_ukv, mla_q_g,
           mla_k_g, dil_q_g, dil_k_g, swa_q_g, swa_k_g, swa_sinks, group_out_g, w_o, mlp_norm_g, w_up, w_down):
    S = x.shape[1]
    assert S % max(d * Q_BLOCK for _, d in DILATED_BRANCHES) == 0 and S % INPROJ_TM == 0
    cos_t, sin_t = _rope_tables_t(S)
    kpos_t, aslope = _moba_aug_tables(S)
    params = (attn_norm_g, w_in, moba_q_g, moba_k_g, mla_qlat_g, mla_kvlat_g, mla_w_uq, mla_w_ukv, mla_q_g,
              mla_k_g, dil_q_g, dil_k_g, swa_q_g, swa_k_g, swa_sinks, group_out_g, w_o, mlp_norm_g, w_up, w_down)
    for l in range(attn_norm_g.shape[0]):
        x = _layer(x, cos_t, sin_t, kpos_t, aslope, *[p[l] for p in params])
    return x
```

```python
import functools

import numpy as np
import jax
import jax.numpy as jnp
from jax import lax
from jax.experimental import pallas as pl
from jax.experimental.pallas import tpu as pltpu

F32 = jnp.float32
BF16 = jnp.bfloat16

D_MODEL = 1024
HEAD_DIM = 64
N_HEADS = 4
GROUP = N_HEADS * HEAD_DIM
LANES = 128
N_HALF = GROUP // LANES
MOBA_BLOCK = 256
MOBA_TOPK = 3
MLA_Q_RANK = 256
MLA_KV_RANK = 128
MLA_NOPE = 64
MLA_ROPE = 32
MLA_QK = MLA_NOPE + MLA_ROPE
MLA_QK_PAD = 128
ROPE_THETA = 10000.0
DILATED_BRANCHES = ((128, 1), (512, 4), (2048, 16))
Q_BLOCK = 128
SWA_WINDOW = 128
SWA_KV_HEADS = 2
SWA_KV_WIDTH = SWA_KV_HEADS * HEAD_DIM
D_FF = 4 * D_MODEL
EPS = 1e-6
NEG = -1e30

_WIDTHS = (GROUP, GROUP, GROUP, MLA_Q_RANK, MLA_KV_RANK, MLA_ROPE,
           GROUP, GROUP, GROUP, GROUP, SWA_KV_WIDTH, SWA_KV_WIDTH)
_OFFS = tuple(int(v) for v in np.cumsum((0,) + _WIDTHS))
IN_COLS = _OFFS[-1]
(_A_Q, _A_K, _A_V, _B_QL, _B_KVL, _B_KR, _C_Q, _C_K, _C_V, _D_Q, _D_K, _D_V) = _OFFS[:-1]

VMEM_LIMIT = 56 * 1024 * 1024

INPROJ_TM = 512
TAIL_TM = 512
ATT_TK = 256
ATT_TQ = 2 * ATT_TK
ATT_DK = 128
ATT_DV = HEAD_DIM + 16
AUG_ROWS = 8
LOG2E = 1.4426950408889634
FLASH_PAIRS = 2

_NT = (((1,), (1,)), ((), ()))


def _alibi_slopes():
    n = 3 * N_HEADS
    idx = np.arange(1, n + 1, dtype=np.float32).reshape(N_HEADS, 3)
    s = np.exp2(-8.0 * idx / n).astype(np.float32)
    return s[:, 0], s[:, 1], s[:, 2]


SLOPE_A, SLOPE_C, SLOPE_D = _alibi_slopes()


def _params(n_axes):
    return pltpu.CompilerParams(dimension_semantics=("arbitrary",) * n_axes,
                                vmem_limit_bytes=VMEM_LIMIT)


def _head_norm_t(sec, g_col, n_heads, width):
    outs = []
    for h in range(n_heads):
        s = sec[h * width:(h + 1) * width, :]
        ms = jnp.sum(s * s, axis=0, keepdims=True) * (1.0 / width)
        outs.append(s * lax.rsqrt(ms + EPS) * g_col)
    return outs


def _inproj_kernel(x_ref, gx_ref, w1t_ref, gaq_ref, gak_ref, gql_ref, gkvl_ref, wuqt_ref, wukvt_ref,
                   gbq_ref, gbk_ref, gcq_ref, gck_ref, gdq_ref, gdk_ref, cos_ref, sin_ref, kpos_ref, aslope_ref,
                   aqt_ref, aq32t_ref, ak_ref, ak32_ref, avt_ref,
                   bqt_ref, bk_ref, bvt_ref,
                   cq_ref, ck_ref, cv_ref, dq_ref, dk_ref, dv_ref,
                   h_scr):
    tm = x_ref.shape[1]
    x = x_ref[0]
    ms = jnp.mean(x * x, axis=-1, keepdims=True)
    xn = (x * lax.rsqrt(ms + EPS) * gx_ref[...]).astype(BF16)
    h_scr[...] = lax.dot_general(w1t_ref[...], xn, _NT, preferred_element_type=F32)

    scale = HEAD_DIM ** -0.5

    ones_rows = jnp.ones((ATT_DV - HEAD_DIM, ATT_TK), F32)

    def store_vt(ref, h, vh):
        for c in range(tm // ATT_TK):
            ref[0, h, c] = jnp.concatenate([vh[:, c * ATT_TK:(c + 1) * ATT_TK], ones_rows], axis=0).astype(BF16)

    pad_rows = jnp.zeros((ATT_DK - HEAD_DIM - AUG_ROWS, tm), F32)
    qa = _head_norm_t(h_scr[_A_Q:_A_Q + GROUP, :], gaq_ref[...], N_HEADS, HEAD_DIM)
    ka = _head_norm_t(h_scr[_A_K:_A_K + GROUP, :], gak_ref[...], N_HEADS, HEAD_DIM)
    ak32_ref[0] = jnp.concatenate(ka, axis=0).T
    kpos = kpos_ref[...]
    for h in range(N_HEADS):
        aq32t_ref[0, h] = qa[h]
        slope_rows = jnp.broadcast_to(aslope_ref[h], (AUG_ROWS, tm))
        aqt_ref[0, h] = jnp.concatenate([qa[h] * (scale * LOG2E), slope_rows, pad_rows], axis=0).astype(BF16)
        ak_ref[0, h] = jnp.concatenate([ka[h], kpos, pad_rows], axis=0).T.astype(BF16)
        store_vt(avt_ref, h, h_scr[_A_V + h * HEAD_DIM:_A_V + (h + 1) * HEAD_DIM, :])

    cos = cos_ref[...]
    sin = sin_ref[...]
    half = MLA_ROPE // 2

    def rope_pad(t, sc):
        x1 = t[MLA_NOPE:MLA_NOPE + half, :]
        x2 = t[MLA_NOPE + half:MLA_QK, :]
        return jnp.concatenate([t[:MLA_NOPE, :] * sc, (x1 * cos - x2 * sin) * sc, (x1 * sin + x2 * cos) * sc,
                                jnp.zeros((MLA_QK_PAD - MLA_QK, tm), F32)], axis=0)

    ql = h_scr[_B_QL:_B_QL + MLA_Q_RANK, :]
    ql = ql * lax.rsqrt(jnp.sum(ql * ql, axis=0, keepdims=True) * (1.0 / MLA_Q_RANK) + EPS) * gql_ref[...]
    qb = jnp.dot(wuqt_ref[...], ql.astype(BF16), preferred_element_type=F32)
    qb = _head_norm_t(qb, gbq_ref[...], N_HEADS, MLA_QK)
    kvl = h_scr[_B_KVL:_B_KVL + MLA_KV_RANK, :]
    kvl = kvl * lax.rsqrt(jnp.sum(kvl * kvl, axis=0, keepdims=True) * (1.0 / MLA_KV_RANK) + EPS) * gkvl_ref[...]
    kvb = jnp.dot(wukvt_ref[...], kvl.astype(BF16), preferred_element_type=F32)
    kr = h_scr[_B_KR:_B_KR + MLA_ROPE, :]
    gbk = gbk_ref[...]
    for h in range(N_HEADS):
        bqt_ref[0, h] = rope_pad(qb[h], MLA_QK ** -0.5 * LOG2E).astype(BF16)
        kh = jnp.concatenate([kvb[h * 2 * HEAD_DIM:h * 2 * HEAD_DIM + MLA_NOPE, :], kr], axis=0)
        kh = kh * lax.rsqrt(jnp.sum(kh * kh, axis=0, keepdims=True) * (1.0 / MLA_QK) + EPS) * gbk
        bk_ref[0, h] = rope_pad(kh, 1.0).T.astype(BF16)
        store_vt(bvt_ref, h, kvb[h * 2 * HEAD_DIM + MLA_NOPE:(h + 1) * 2 * HEAD_DIM, :])

    qc = jnp.concatenate(_head_norm_t(h_scr[_C_Q:_C_Q + GROUP, :], gcq_ref[...], N_HEADS, HEAD_DIM), axis=0)
    kc = jnp.concatenate(_head_norm_t(h_scr[_C_K:_C_K + GROUP, :], gck_ref[...], N_HEADS, HEAD_DIM), axis=0)
    for ref, val in ((cq_ref, qc * (scale * LOG2E)), (ck_ref, kc), (cv_ref, h_scr[_C_V:_C_V + GROUP, :])):
        for c in range(N_HALF):
            ref[0, c] = val[c * LANES:(c + 1) * LANES, :].T

    qd = _head_norm_t(h_scr[_D_Q:_D_Q + GROUP, :], gdq_ref[...], N_HEADS, HEAD_DIM)
    qd = jnp.concatenate([qd[h] for h in SWA_HEAD_ORDER], axis=0)
    dq_ref[0] = (qd * (scale * LOG2E)).T.astype(BF16)
    kd = jnp.concatenate(_head_norm_t(h_scr[_D_K:_D_K + SWA_KV_WIDTH, :], gdk_ref[...], SWA_KV_HEADS, HEAD_DIM),
                         axis=0)
    dk_ref[0] = kd.T.astype(BF16)
    dv_ref[0] = h_scr[_D_V:_D_V + SWA_KV_WIDTH, :].T.astype(BF16)


def _inproj(x, gx, w1t, gaq, gak, gql, gkvl, wuqt, wukvt, gbq, gbk, gcq, gck, gdq, gdk, cos_t, sin_t, kpos_t,
            aslope):
    B, S, _ = x.shape
    tm = INPROJ_TM
    nb = S // ATT_TK
    cpt = tm // ATT_TK
    H = N_HEADS
    full = lambda a: pl.BlockSpec(a.shape, lambda b, t: (0,) * a.ndim)
    in_specs = [pl.BlockSpec((1, tm, D_MODEL), lambda b, t: (b, t, 0)), full(gx), full(w1t), full(gaq), full(gak),
                full(gql), full(gkvl), full(wuqt), full(wukvt), full(gbq), full(gbk), full(gcq), full(gck),
                full(gdq), full(gdk),
                pl.BlockSpec((MLA_ROPE // 2, tm), lambda b, t: (0, t)),
                pl.BlockSpec((MLA_ROPE // 2, tm), lambda b, t: (0, t)),
                pl.BlockSpec((AUG_ROWS, tm), lambda b, t: (0, t)), full(aslope)]
    head_t = lambda w: pl.BlockSpec((1, H, w, tm), lambda b, t: (b, 0, 0, t))
    head_n = pl.BlockSpec((1, H, tm, ATT_DK), lambda b, t: (b, 0, t, 0))
    vt_spec = pl.BlockSpec((1, H, cpt, ATT_DV, ATT_TK), lambda b, t: (b, 0, t, 0, 0))
    nat = lambda w: pl.BlockSpec((1, tm, w), lambda b, t: (b, t, 0))
    halves = pl.BlockSpec((1, N_HALF, tm, LANES), lambda b, t: (b, 0, t, 0))
    out_shape = [
        jax.ShapeDtypeStruct((B, H, ATT_DK, S), BF16),
        jax.ShapeDtypeStruct((B, H, HEAD_DIM, S), F32),
        jax.ShapeDtypeStruct((B, H, S, ATT_DK), BF16),
        jax.ShapeDtypeStruct((B, S, GROUP), F32),
        jax.ShapeDtypeStruct((B, H, nb, ATT_DV, ATT_TK), BF16),
        jax.ShapeDtypeStruct((B, H, ATT_DK, S), BF16),
        jax.ShapeDtypeStruct((B, H, S, ATT_DK), BF16),
        jax.ShapeDtypeStruct((B, H, nb, ATT_DV, ATT_TK), BF16),
        jax.ShapeDtypeStruct((B, N_HALF, S, LANES), F32),
        jax.ShapeDtypeStruct((B, N_HALF, S, LANES), F32),
        jax.ShapeDtypeStruct((B, N_HALF, S, LANES), F32),
        jax.ShapeDtypeStruct((B, S, GROUP), BF16),
        jax.ShapeDtypeStruct((B, S, SWA_KV_WIDTH), BF16),
        jax.ShapeDtypeStruct((B, S, SWA_KV_WIDTH), BF16),
    ]
    out_specs = [head_t(ATT_DK), head_t(HEAD_DIM), head_n, nat(GROUP), vt_spec,
                 head_t(ATT_DK), head_n, vt_spec,
                 halves, halves, halves, nat(GROUP), nat(SWA_KV_WIDTH), nat(SWA_KV_WIDTH)]
    return pl.pallas_call(
        _inproj_kernel,
        grid=(B, S // tm),
        in_specs=in_specs,
        out_specs=out_specs,
        out_shape=out_shape,
        scratch_shapes=[pltpu.VMEM((IN_COLS, tm), F32)],
        compiler_params=_params(2),
        name="inproj",
    )(x, gx, w1t, gaq, gak, gql, gkvl, wuqt, wukvt, gbq, gbk, gcq, gck, gdq, gdk, cos_t, sin_t, kpos_t, aslope)


def _kmean_kernel(k_ref, o_ref):
    S = k_ref.shape[1]
    k = k_ref[0].reshape(S // MOBA_BLOCK, MOBA_BLOCK, GROUP)
    o_ref[0] = jnp.sum(k, axis=1) * (1.0 / MOBA_BLOCK)


def _kmean(ak32):
    B, S, _ = ak32.shape
    nb = S // MOBA_BLOCK
    return pl.pallas_call(
        _kmean_kernel,
        grid=(B,),
        in_specs=[pl.BlockSpec((1, S, GROUP), lambda b: (b, 0, 0))],
        out_specs=pl.BlockSpec((1, nb, GROUP), lambda b: (b, 0, 0)),
        out_shape=jax.ShapeDtypeStruct((B, nb, GROUP), F32),
        compiler_params=_params(1),
        name="kmean",
    )(ak32)


def _flash_scratch(tq):
    s_buf, p_buf = pltpu.VMEM((N_HEADS, ATT_TK, tq), F32), pltpu.VMEM((N_HEADS, ATT_TK, tq), BF16)
    return [s_buf, s_buf, p_buf, p_buf, pltpu.VMEM((N_HEADS, ATT_DV, tq), F32)]


def _flash_heads(qt_ref, k_ref, vt_ref, o_ref, s_bufs, p_bufs, acc_scr, rowb_scr, i):
    tq = o_ref.shape[3]
    n_tail = tq // ATT_TK
    assert n_tail == 2
    n_past = i * n_tail
    heads = range(N_HEADS)

    def block_of(pos):
        return jnp.where(pos < n_tail, n_past + pos, pos - n_tail)

    def stage_scores(slot, blk):
        kstart = pl.multiple_of(blk * ATT_TK, ATT_TK)
        for h in heads:
            kb = k_ref[0, h, pl.ds(kstart, ATT_TK), :]
            s_bufs[slot][h] = jnp.dot(kb, qt_ref[0, h], preferred_element_type=F32)

    def stage_softmax(slot, blk, ms, causal=None):
        new_ms, alphas = [], []
        for h in heads:
            m_parts, a_parts = [], []
            rv_row = None if rowb_scr is None else rowb_scr[h, pl.ds(blk, 1), :]
            for c in range(tq // LANES):
                cols = slice(c * LANES, (c + 1) * LANES)
                st = s_bufs[slot][h, :, cols]
                if causal is not None:
                    st = jnp.where(causal[:, cols], st, 2 * NEG)
                cm = jnp.max(st, axis=0, keepdims=True)
                m_old = ms[h][:, cols]
                if rowb_scr is None:
                    m_new = jnp.maximum(m_old, cm)
                    shift = m_new
                else:
                    rv = rv_row[:, cols]
                    m_new = jnp.maximum(m_old, cm + rv)
                    shift = m_new - rv
                a_parts.append(jnp.exp2(m_old - m_new))
                m_parts.append(m_new)
                p_bufs[slot][h, :, cols] = jnp.exp2(st - shift).astype(BF16)
            new_ms.append(jnp.concatenate(m_parts, axis=1))
            alphas.append(jnp.concatenate(a_parts, axis=1))
        return tuple(new_ms), tuple(alphas)

    def stage_values(slot, blk, alphas):
        for h in heads:
            acc_scr[h] = alphas[h] * acc_scr[h] + jnp.dot(vt_ref[0, h, blk], p_bufs[slot][h],
                                                          preferred_element_type=F32)

    krow = lax.broadcasted_iota(jnp.int32, (ATT_TK, tq), 0)
    ti = lax.broadcasted_iota(jnp.int32, (ATT_TK, tq), 1)
    for h in heads:
        acc_scr[h] = jnp.zeros((ATT_DV, tq), F32)
    stage_scores(0, n_past)
    stage_scores(1, n_past + 1)
    ms = tuple(jnp.full((1, tq), NEG, F32) for _ in heads)
    ms, alphas = stage_softmax(0, n_past, ms, causal=ti >= krow)
    stage_values(0, n_past, alphas)
    ms, alphas = stage_softmax(1, n_past + 1, ms, causal=ti >= krow + ATT_TK)
    stage_scores(0, 0)

    def step(pos, slot, ms, alphas):
        ms, new_alphas = stage_softmax(1 - slot, pos + 1 - n_tail, ms)
        stage_values(slot, block_of(pos), alphas)
        stage_scores(slot, pos)
        return ms, new_alphas

    def pair(first, carry):
        ms, alphas = step(first, 1, *carry)
        return step(first + 1, 0, ms, alphas)

    def trip(t, carry):
        for u in range(FLASH_PAIRS):
            carry = pair(2 * (FLASH_PAIRS * t + u) + 1, carry)
        return carry

    carry = lax.fori_loop(0, i // FLASH_PAIRS, trip, (ms, alphas))
    ms, alphas = lax.fori_loop(i - i % FLASH_PAIRS, i, lambda u, c: pair(2 * u + 1, c), carry)
    last = n_past + 1
    stage_values(1, block_of(last), alphas)
    for h in heads:
        acc = acc_scr[h]
        o_ref[0, h] = acc[:HEAD_DIM, :] / acc[HEAD_DIM:HEAD_DIM + 1, :]


def _moba_kernel(qt_ref, q32t_ref, k_ref, vt_ref, kmean_ref, o_ref, rowb_scr, s0, s1, p0, p1, acc_scr):
    i = pl.program_id(1)
    nb = kmean_ref.shape[2]
    tq = qt_ref.shape[3]
    blk = lax.broadcasted_iota(jnp.int32, (nb, tq), 0)
    col = lax.broadcasted_iota(jnp.int32, (nb, tq), 1)
    qblk = i * (tq // MOBA_BLOCK) + col // MOBA_BLOCK
    past = blk < qblk
    dist0 = (i * tq + col - blk * MOBA_BLOCK).astype(F32)
    for h in range(N_HEADS):
        gate = jnp.dot(kmean_ref[0, h], q32t_ref[0, h], preferred_element_type=F32,
                       precision=lax.Precision.HIGHEST)
        gate = jnp.where(past, gate, NEG)
        sel = blk == qblk
        for _ in range(MOBA_TOPK):
            best = jnp.max(gate, axis=0, keepdims=True)
            first = jnp.min(jnp.where(gate == best, blk, nb), axis=0, keepdims=True)
            pick = blk == first
            sel = jnp.logical_or(sel, jnp.logical_and(pick, past))
            gate = jnp.where(pick, -jnp.inf, gate)
        rowb_scr[h] = jnp.where(sel, (-float(SLOPE_A[h]) * LOG2E) * dist0, 2 * NEG)
    _flash_heads(qt_ref, k_ref, vt_ref, o_ref, (s0, s1), (p0, p1), acc_scr, rowb_scr, i)


def _flash_specs(S, tq):
    H, nb = N_HEADS, S // ATT_TK
    return ([pl.BlockSpec((1, H, ATT_DK, tq), lambda b, i: (b, 0, 0, i)),
             pl.BlockSpec((1, H, S, ATT_DK), lambda b, i: (b, 0, 0, 0)),
             pl.BlockSpec((1, H, nb, ATT_DV, ATT_TK), lambda b, i: (b, 0, 0, 0, 0))],
            pl.BlockSpec((1, H, HEAD_DIM, tq), lambda b, i: (b, 0, 0, i)))


def _moba(aqt, aq32t, ak, avt, kmean_h):
    B, H, _, S = aqt.shape
    tq = ATT_TQ
    nb = S // ATT_TK
    (q_spec, k_spec, vt_spec), o_spec = _flash_specs(S, tq)
    return pl.pallas_call(
        _moba_kernel,
        grid=(B, S // tq),
        in_specs=[q_spec, pl.BlockSpec((1, H, HEAD_DIM, tq), lambda b, i: (b, 0, 0, i)), k_spec, vt_spec,
                  pl.BlockSpec((1, H, nb, HEAD_DIM), lambda b, i: (b, 0, 0, 0))],
        out_specs=o_spec,
        out_shape=jax.ShapeDtypeStruct((B, H, HEAD_DIM, S), F32),
        scratch_shapes=[pltpu.VMEM((H, nb, tq), F32)] + _flash_scratch(tq),
        compiler_params=_params(2),
        name="moba",
    )(aqt, aq32t, ak, avt, kmean_h)


def _mla_kernel(qt_ref, k_ref, vt_ref, o_ref, s0, s1, p0, p1, acc_scr):
    _flash_heads(qt_ref, k_ref, vt_ref, o_ref, (s0, s1), (p0, p1), acc_scr, None, pl.program_id(1))


def _mla(bqt, bk, bvt):
    B, H, _, S = bqt.shape
    tq = ATT_TQ
    in_specs, o_spec = _flash_specs(S, tq)
    return pl.pallas_call(
        _mla_kernel,
        grid=(B, S // tq),
        in_specs=in_specs,
        out_specs=o_spec,
        out_shape=jax.ShapeDtypeStruct((B, H, HEAD_DIM, S), F32),
        scratch_shapes=_flash_scratch(tq),
        compiler_params=_params(2),
        name="mla",
    )(bqt, bk, bvt)


DIL_SPAN = max(d for _, d in DILATED_BRANCHES) * Q_BLOCK
DIL_UNITS = DIL_SPAN // Q_BLOCK
DIL_GROUP = 2


def _dilated_bias():
    qi = np.arange(Q_BLOCK)[:, None]
    kidx = np.arange(2 * Q_BLOCK)[None, :]
    rel = qi + Q_BLOCK - kidx
    out = np.empty((len(DILATED_BRANCHES), N_HEADS, Q_BLOCK, 2 * Q_BLOCK), np.float32)
    for bi, (window, d) in enumerate(DILATED_BRANCHES):
        valid = (rel >= 0) & (rel <= window // d)
        for h in range(N_HEADS):
            out[bi, h] = np.where(valid, -SLOPE_C[h] * np.float32(LOG2E) * (d * rel).astype(np.float32), NEG)
    return out


def _rows_load(ref, lead, start, size, stride):
    return jnp.concatenate([ref[lead + (c, pl.ds(start, size, stride=stride), slice(None))]
                            for c in range(N_HALF)], axis=-1)


def _rows_store(ref, lead, start, size, stride, val):
    for c in range(N_HALF):
        ref[lead + (c, pl.ds(start, size, stride=stride), slice(None))] = val[:, c * LANES:(c + 1) * LANES]


def _dilated_kernel(q_ref, kp_ref, kc_ref, vp_ref, vc_ref, bias_ref, o_ref, kbuf, vbuf, m_scr, den_scr, num_scr):
    span = pl.program_id(1)
    kbuf[:, 0:DIL_SPAN, :] = kp_ref[0]
    kbuf[:, DIL_SPAN:, :] = kc_ref[0]
    vbuf[:, 0:DIL_SPAN, :] = vp_ref[0]
    vbuf[:, DIL_SPAN:, :] = vc_ref[0]
    lane_head = lax.broadcasted_iota(jnp.int32, (1, GROUP), 1) // HEAD_DIM
    hmask = [lane_head == h for h in range(N_HEADS)]
    hmask_f = [m.astype(F32) for m in hmask]
    low_neg = jnp.where(lax.broadcasted_iota(jnp.int32, (Q_BLOCK, 2 * Q_BLOCK), 1) < Q_BLOCK, NEG, 0.0)
    first_span = jnp.where(span == 0, 1.0, 0.0)

    def per_head(cols):
        out = cols[N_HEADS - 1]
        for h in range(N_HEADS - 2, -1, -1):
            out = jnp.where(hmask[h], cols[h], out)
        return out

    order = sorted(range(len(DILATED_BRANCHES)), key=lambda b: -DILATED_BRANCHES[b][1])
    for bi in order:
        d = DILATED_BRANCHES[bi][1]
        first, last = bi == order[0], bi == order[-1]

        def group(g, _, bi=bi, d=d, first=first, last=last):
            fronts = []
            for uu in range(DIL_GROUP):
                u = g * DIL_GROUP + uu
                r, n = u % d, u // d
                qstart = n * (Q_BLOCK * d) + r
                kstart = DIL_SPAN + qstart - Q_BLOCK * d
                q = _rows_load(q_ref, (0,), qstart, Q_BLOCK, d)
                k2 = _rows_load(kbuf, (), kstart, 2 * Q_BLOCK, d).astype(BF16)
                v2 = _rows_load(vbuf, (), kstart, 2 * Q_BLOCK, d).astype(BF16)
                q4 = jnp.concatenate([(q * hmask_f[h]).astype(BF16) for h in range(N_HEADS)], axis=0)
                s4 = lax.dot_general(q4, k2, _NT, preferred_element_type=F32)
                penalty = low_neg * jnp.where(n == 0, first_span, 0.0)
                fronts.append((qstart, s4, v2, penalty))
            for qstart, s4, v2, penalty in fronts:
                es, ms, ls = [], [], []
                for h in range(N_HEADS):
                    s = s4[h * Q_BLOCK:(h + 1) * Q_BLOCK, :] + (bias_ref[bi, h] + penalty)
                    m = jnp.max(s, axis=-1, keepdims=True)
                    e = jnp.exp2(s - m)
                    ls.append(jnp.sum(e, axis=-1, keepdims=True))
                    ms.append(m)
                    es.append(e.astype(BF16))
                o4 = jnp.dot(jnp.concatenate(es, axis=0), v2, preferred_element_type=F32)
                o = o4[(N_HEADS - 1) * Q_BLOCK:, :]
                for h in range(N_HEADS - 2, -1, -1):
                    o = jnp.where(hmask[h], o4[h * Q_BLOCK:(h + 1) * Q_BLOCK, :], o)
                m_b, l_b = per_head(ms), per_head(ls)
                at = ((), qstart, Q_BLOCK, d)
                if first:
                    _rows_store(m_scr, *at, m_b)
                    _rows_store(num_scr, *at, o)
                    _rows_store(den_scr, *at, l_b)
                else:
                    m_old = _rows_load(m_scr, *at)
                    m_new = jnp.maximum(m_old, m_b)
                    a, b = jnp.exp2(m_old - m_new), jnp.exp2(m_b - m_new)
                    num = a * _rows_load(num_scr, *at) + b * o
                    den = a * _rows_load(den_scr, *at) + b * l_b
                    if last:
                        _rows_store(o_ref, (0,), qstart, Q_BLOCK, d, num / den)
                    else:
                        _rows_store(m_scr, *at, m_new)
                        _rows_store(num_scr, *at, num)
                        _rows_store(den_scr, *at, den)
            return 0

        lax.fori_loop(0, DIL_UNITS // DIL_GROUP, group, 0)


def _dilated(cq, ck, cv):
    B, _, S, _ = cq.shape
    cur = pl.BlockSpec((1, N_HALF, DIL_SPAN, LANES), lambda b, s: (b, 0, s, 0))
    prev = pl.BlockSpec((1, N_HALF, DIL_SPAN, LANES), lambda b, s: (b, 0, jnp.maximum(s - 1, 0), 0))
    bias = jnp.asarray(_dilated_bias())
    return pl.pallas_call(
        _dilated_kernel,
        grid=(B, S // DIL_SPAN),
        in_specs=[cur, prev, cur, prev, cur, pl.BlockSpec(bias.shape, lambda b, s: (0, 0, 0, 0))],
        out_specs=cur,
        out_shape=jax.ShapeDtypeStruct((B, N_HALF, S, LANES), F32),
        scratch_shapes=[pltpu.VMEM((N_HALF, 2 * DIL_SPAN, LANES), F32),
                        pltpu.VMEM((N_HALF, 2 * DIL_SPAN, LANES), F32),
                        pltpu.VMEM((N_HALF, DIL_SPAN, LANES), F32), pltpu.VMEM((N_HALF, DIL_SPAN, LANES), F32),
                        pltpu.VMEM((N_HALF, DIL_SPAN, LANES), F32)],
        compiler_params=_params(2),
        name="dilated",
    )(cq, ck, ck, cv, cv, bias)


SWA_SPAN = 1024
SWA_GROUP = 2
SWA_HEAD_ORDER = (0, 2, 1, 3)


def _swa_bias():
    qi = np.arange(Q_BLOCK)[:, None]
    kidx = np.arange(2 * Q_BLOCK)[None, :]
    rel = qi + Q_BLOCK - kidx
    valid = (rel >= 0) & (rel < SWA_WINDOW)
    out = np.empty((N_HEADS, Q_BLOCK, 2 * Q_BLOCK), np.float32)
    for h in range(N_HEADS):
        out[h] = np.where(valid, -SLOPE_D[h] * np.float32(LOG2E) * rel.astype(np.float32), NEG)
    return out


def _swa_kernel(sink_ref, q_ref, k_ref, v_ref, bias_ref, o_ref):
    span = pl.program_id(1)
    units = SWA_SPAN // Q_BLOCK
    half = lax.broadcasted_iota(jnp.int32, (1, LANES), 1) // HEAD_DIM
    lo = half == 0
    kv_mask = [jnp.where(half == g, 1.0, 0.0).astype(BF16) for g in range(SWA_KV_HEADS)]
    low_neg = jnp.where(lax.broadcasted_iota(jnp.int32, (Q_BLOCK, 2 * Q_BLOCK), 1) < Q_BLOCK, NEG, 0.0)

    def group(g, _):
        fronts = []
        for uu in range(SWA_GROUP):
            u = g * SWA_GROUP + uu
            n = span * units + u
            lo_start = pl.multiple_of(jnp.maximum(n - 1, 0) * Q_BLOCK, Q_BLOCK)
            hi_start = pl.multiple_of(n * Q_BLOCK, Q_BLOCK)
            qstart = pl.multiple_of(u * Q_BLOCK, Q_BLOCK)
            q = q_ref[0, pl.ds(qstart, Q_BLOCK), :]
            k2 = jnp.concatenate([k_ref[0, pl.ds(lo_start, Q_BLOCK), :], k_ref[0, pl.ds(hi_start, Q_BLOCK), :]],
                                 axis=0)
            v2 = jnp.concatenate([v_ref[0, pl.ds(lo_start, Q_BLOCK), :], v_ref[0, pl.ds(hi_start, Q_BLOCK), :]],
                                 axis=0)
            q4 = jnp.concatenate([q[:, (h % 2) * LANES:(h % 2 + 1) * LANES] * kv_mask[h // 2]
                                  for h in range(N_HEADS)], axis=0)
            s4 = lax.dot_general(q4, k2, _NT, preferred_element_type=F32)
            penalty = low_neg * jnp.where(n == 0, 1.0, 0.0)
            fronts.append((qstart, s4, v2, penalty))
        for qstart, s4, v2, penalty in fronts:
            es, ls = [], []
            for h in range(N_HEADS):
                s = s4[h * Q_BLOCK:(h + 1) * Q_BLOCK, :] + (bias_ref[h] + penalty)
                sink = sink_ref[h] * LOG2E
                m = jnp.maximum(jnp.max(s, axis=-1, keepdims=True), sink)
                e = jnp.exp2(s - m)
                ls.append(jnp.sum(e, axis=-1, keepdims=True) + jnp.exp2(sink - m))
                es.append(e.astype(BF16))
            o4 = jnp.dot(jnp.concatenate(es, axis=0), v2, preferred_element_type=F32)
            tiles = []
            for t in range(N_HALF):
                a, b = t, t + 2
                tiles.append(jnp.where(lo, o4[a * Q_BLOCK:(a + 1) * Q_BLOCK, :] / ls[a],
                                       o4[b * Q_BLOCK:(b + 1) * Q_BLOCK, :] / ls[b]))
            o_ref[0, pl.ds(qstart, Q_BLOCK), :] = jnp.concatenate(tiles, axis=-1)
        return 0

    lax.fori_loop(0, units // SWA_GROUP, group, 0)


def _swa(dq, dk, dv, sinks):
    B, S, _ = dq.shape
    bias = jnp.asarray(_swa_bias())
    return pl.pallas_call(
        _swa_kernel,
        grid=(B, S // SWA_SPAN),
        in_specs=[pl.BlockSpec(memory_space=pltpu.SMEM),
                  pl.BlockSpec((1, SWA_SPAN, GROUP), lambda b, i: (b, i, 0)),
                  pl.BlockSpec((1, S, SWA_KV_WIDTH), lambda b, i: (b, 0, 0)),
                  pl.BlockSpec((1, S, SWA_KV_WIDTH), lambda b, i: (b, 0, 0)),
                  pl.BlockSpec(bias.shape, lambda b, i: (0, 0, 0))],
        out_specs=pl.BlockSpec((1, SWA_SPAN, GROUP), lambda b, i: (b, i, 0)),
        out_shape=jax.ShapeDtypeStruct((B, S, GROUP), F32),
        compiler_params=_params(2),
        name="swa",
    )(sinks, dq, dk, dv, bias)


def _row_norm(y, g_row):
    return y * lax.rsqrt(jnp.mean(y * y, axis=-1, keepdims=True) + EPS) * g_row


def _tail_kernel(x_ref, oat_ref, obt_ref, oc_ref, od_ref, gg_ref, wo_ref, gm_ref, wup_ref, wdn_ref, out_ref):
    gg = gg_ref[...]

    def col_norm_t(yt, g_row):
        y = (yt * lax.rsqrt(jnp.mean(yt * yt, axis=0, keepdims=True) + EPS)).T
        return y * g_row

    ga = col_norm_t(oat_ref[0], gg[0:1, :])
    gb = col_norm_t(obt_ref[0], gg[1:2, :])
    gc = _row_norm(jnp.concatenate([oc_ref[0, c] for c in range(N_HALF)], axis=-1), gg[2:3, :])
    gd = _row_norm(od_ref[0], gg[3:4, :])
    mixed = jnp.concatenate([ga, gb, gc, gd], axis=-1).astype(BF16)
    x1 = x_ref[0] + jnp.dot(mixed, wo_ref[...], preferred_element_type=F32)
    xn = _row_norm(x1, gm_ref[...]).astype(BF16)
    u = jnp.maximum(jnp.dot(xn, wup_ref[...], preferred_element_type=F32), 0.0)
    out_ref[0] = x1 + jnp.dot((u * u).astype(BF16), wdn_ref[...], preferred_element_type=F32)


def _tail(x, oat, obt, oc, od, gg, wo, gm, wup, wdn):
    B, S, _ = x.shape
    tm = TAIL_TM
    nat = lambda w: pl.BlockSpec((1, tm, w), lambda b, t: (b, t, 0))
    ft = pl.BlockSpec((1, GROUP, tm), lambda b, t: (b, 0, t))
    const = lambda a: pl.BlockSpec(a.shape, lambda b, t: (0,) * a.ndim, pipeline_mode=pl.Buffered(1))
    return pl.pallas_call(
        _tail_kernel,
        grid=(B, S // tm),
        in_specs=[nat(D_MODEL), ft, ft, pl.BlockSpec((1, N_HALF, tm, LANES), lambda b, t: (b, 0, t, 0)),
                  nat(GROUP), const(gg), const(wo), const(gm),
                  const(wup), const(wdn)],
        out_specs=nat(D_MODEL),
        out_shape=jax.ShapeDtypeStruct((B, S, D_MODEL), F32),
        compiler_params=_params(2),
        name="tail",
    )(x, oat, obt, oc, od, gg, wo, gm, wup, wdn)


def _rope_tables_t(S):
    inv = 1.0 / (ROPE_THETA ** (jnp.arange(0, MLA_ROPE, 2, dtype=F32) / MLA_ROPE))
    ang = inv[:, None] * jnp.arange(S, dtype=F32)[None, :]
    return jnp.cos(ang), jnp.sin(ang)


def _moba_aug_tables(S):
    pos = np.arange(S, dtype=np.float32) % MOBA_BLOCK
    kpos = np.zeros((AUG_ROWS, S), np.float32)
    kpos[:3] = pos
    aslope = np.zeros((N_HEADS, AUG_ROWS, 1), np.float32)
    rest = (SLOPE_A * np.float32(LOG2E)).astype(np.float32)
    for r in range(3):
        piece = rest.astype(BF16).astype(np.float32)
        aslope[:, r, 0] = piece
        rest = rest - piece
    return jnp.asarray(kpos), jnp.asarray(aslope)


def _layer(x, cos_t, sin_t, kpos_t, aslope, attn_norm_g, w_in, moba_q_g, moba_k_g, mla_qlat_g, mla_kvlat_g, mla_w_uq, mla_w_ukv,
           mla_q_g, mla_k_g, dil_q_g, dil_k_g, swa_q_g, swa_k_g, swa_sinks, group_out_g, w_o, mlp_norm_g,
           w_up, w_down):
    B, S, _ = x.shape
    col = lambda g: g.reshape(-1, 1)
    (aqt, aq32t, ak, ak32, avt, bqt, bk, bvt, cq, ck, cv, dq, dk, dv) = _inproj(
        x, attn_norm_g.reshape(1, -1), w_in.T.astype(BF16), col(moba_q_g), col(moba_k_g), col(mla_qlat_g),
        col(mla_kvlat_g), mla_w_uq.T.astype(BF16), mla_w_ukv.T.astype(BF16), col(mla_q_g), col(mla_k_g),
        col(dil_q_g), col(dil_k_g), col(swa_q_g), col(swa_k_g), cos_t, sin_t, kpos_t, aslope)
    kmean = _kmean(ak32)
    nb = S // MOBA_BLOCK
    kmean_h = kmean.reshape(B, nb, N_HEADS, HEAD_DIM).transpose(0, 2, 1, 3)
    oat = _moba(aqt, aq32t, ak, avt, kmean_h).reshape(B, GROUP, S)
    obt = _mla(bqt, bk, bvt).reshape(B, GROUP, S)
    oc = _dilated(cq, ck, cv)
    od = _swa(dq, dk, dv, swa_sinks)
    perm = np.concatenate([np.arange(HEAD_DIM) + HEAD_DIM * h for h in SWA_HEAD_ORDER])
    gg = group_out_g.at[3].set(group_out_g[3][perm])
    wo = jnp.concatenate([w_o[:3 * GROUP], w_o[3 * GROUP + perm]], axis=0)
    return _tail(x, oat, obt, oc, od, gg, wo.astype(BF16), mlp_norm_g.reshape(1, -1),
                 w_up.astype(BF16), w_down.astype(BF16))


def kernel(x, attn_norm_g, w_in, moba_q_g, moba_k_g, mla_qlat_g, mla_kvlat_g, mla_w_uq, mla_w_ukv, mla_q_g,
           mla_k_g, dil_q_g, dil_k_g, swa_q_g, swa_k_g, swa_sinks, group_out_g, w_o, mlp_norm_g, w_up, w_down):
    S = x.shape[1]
    assert S % max(d * Q_BLOCK for _, d in DILATED_BRANCHES) == 0 and S % INPROJ_TM == 0
    cos_t, sin_t = _rope_tables_t(S)
    kpos_t, aslope = _moba_aug_tables(S)
    params = (attn_norm_g, w_in, moba_q_g, moba_k_g, mla_qlat_g, mla_kvlat_g, mla_w_uq, mla_w_ukv, mla_q_g,
              mla_k_g, dil_q_g, dil_k_g, swa_q_g, swa_k_g, swa_sinks, group_out_g, w_o, mlp_norm_g, w_up, w_down)
    for l in range(attn_norm_g.shape[0]):
        x = _layer(x, cos_t, sin_t, kpos_t, aslope, *[p[l] for p in params])
    return x
```

```python
import functools

import numpy as np
import jax
import jax.numpy as jnp
from jax import lax
from jax.experimental import pallas as pl
from jax.experimental.pallas import tpu as pltpu

F32 = jnp.float32
BF16 = jnp.bfloat16

D_MODEL = 1024
HEAD_DIM = 64
N_HEADS = 4
GROUP = N_HEADS * HEAD_DIM
LANES = 128
N_HALF = GROUP // LANES
MOBA_BLOCK = 256
MOBA_TOPK = 3
MLA_Q_RANK = 256
MLA_KV_RANK = 128
MLA_NOPE = 64
MLA_ROPE = 32
MLA_QK = MLA_NOPE + MLA_ROPE
MLA_QK_PAD = 128
ROPE_THETA = 10000.0
DILATED_BRANCHES = ((128, 1), (512, 4), (2048, 16))
Q_BLOCK = 128
SWA_WINDOW = 128
SWA_KV_HEADS = 2
SWA_KV_WIDTH = SWA_KV_HEADS * HEAD_DIM
D_FF = 4 * D_MODEL
EPS = 1e-6
NEG = -1e30

_WIDTHS = (GROUP, GROUP, GROUP, MLA_Q_RANK, MLA_KV_RANK, MLA_ROPE,
           GROUP, GROUP, GROUP, GROUP, SWA_KV_WIDTH, SWA_KV_WIDTH)
_OFFS = tuple(int(v) for v in np.cumsum((0,) + _WIDTHS))
IN_COLS = _OFFS[-1]
(_A_Q, _A_K, _A_V, _B_QL, _B_KVL, _B_KR, _C_Q, _C_K, _C_V, _D_Q, _D_K, _D_V) = _OFFS[:-1]

VMEM_LIMIT = 56 * 1024 * 1024

INPROJ_TM = 512
TAIL_TM = 512
ATT_TK = 256
ATT_TQ = 2 * ATT_TK
ATT_DK = 128
ATT_DV = HEAD_DIM + 16
AUG_ROWS = 8
LOG2E = 1.4426950408889634
FLASH_PAIRS = 2

_NT = (((1,), (1,)), ((), ()))


def _alibi_slopes():
    n = 3 * N_HEADS
    idx = np.arange(1, n + 1, dtype=np.float32).reshape(N_HEADS, 3)
    s = np.exp2(-8.0 * idx / n).astype(np.float32)
    return s[:, 0], s[:, 1], s[:, 2]


SLOPE_A, SLOPE_C, SLOPE_D = _alibi_slopes()


def _params(n_axes):
    return pltpu.CompilerParams(dimension_semantics=("arbitrary",) * n_axes,
                                vmem_limit_bytes=VMEM_LIMIT)


def _head_norm_t(sec, g_col, n_heads, width):
    outs = []
    for h in range(n_heads):
        s = sec[h * width:(h + 1) * width, :]
        ms = jnp.sum(s * s, axis=0, keepdims=True) * (1.0 / width)
        outs.append(s * lax.rsqrt(ms + EPS) * g_col)
    return outs


def _inproj_kernel(x_ref, gx_ref, w1t_ref, gaq_ref, gak_ref, gql_ref, gkvl_ref, wuqt_ref, wukvt_ref,
                   gbq_ref, gbk_ref, gcq_ref, gck_ref, gdq_ref, gdk_ref, cos_ref, sin_ref, kpos_ref, aslope_ref,
                   aqt_ref, aq32t_ref, ak_ref, ak32_ref, avt_ref,
                   bqt_ref, bk_ref, bvt_ref,
                   cq_ref, ck_ref, cv_ref, dq_ref, dk_ref, dv_ref,
                   h_scr):
    tm = x_ref.shape[1]
    x = x_ref[0]
    ms = jnp.mean(x * x, axis=-1, keepdims=True)
    xn = (x * lax.rsqrt(ms + EPS) * gx_ref[...]).astype(BF16)
    h_scr[...] = lax.dot_general(w1t_ref[...], xn, _NT, preferred_element_type=F32)

    scale = HEAD_DIM ** -0.5

    ones_rows = jnp.ones((ATT_DV - HEAD_DIM, ATT_TK), F32)

    def store_vt(ref, h, vh):
        for c in range(tm // ATT_TK):
            ref[0, h, c] = jnp.concatenate([vh[:, c * ATT_TK:(c + 1) * ATT_TK], ones_rows], axis=0).astype(BF16)

    pad_rows = jnp.zeros((ATT_DK - HEAD_DIM - AUG_ROWS, tm), F32)
    qa = _head_norm_t(h_scr[_A_Q:_A_Q + GROUP, :], gaq_ref[...], N_HEADS, HEAD_DIM)
    ka = _head_norm_t(h_scr[_A_K:_A_K + GROUP, :], gak_ref[...], N_HEADS, HEAD_DIM)
    ak32_ref[0] = jnp.concatenate(ka, axis=0).T
    kpos = kpos_ref[...]
    for h in range(N_HEADS):
        aq32t_ref[0, h] = qa[h]
        slope_rows = jnp.broadcast_to(aslope_ref[h], (AUG_ROWS, tm))
        aqt_ref[0, h] = jnp.concatenate([qa[h] * (scale * LOG2E), slope_rows, pad_rows], axis=0).astype(BF16)
        ak_ref[0, h] = jnp.concatenate([ka[h], kpos, pad_rows], axis=0).T.astype(BF16)
        store_vt(avt_ref, h, h_scr[_A_V + h * HEAD_DIM:_A_V + (h + 1) * HEAD_DIM, :])

    cos = cos_ref[...]
    sin = sin_ref[...]
    half = MLA_ROPE // 2

    def rope_pad(t, sc):
        x1 = t[MLA_NOPE:MLA_NOPE + half, :]
        x2 = t[MLA_NOPE + half:MLA_QK, :]
        return jnp.concatenate([t[:MLA_NOPE, :] * sc, (x1 * cos - x2 * sin) * sc, (x1 * sin + x2 * cos) * sc,
                                jnp.zeros((MLA_QK_PAD - MLA_QK, tm), F32)], axis=0)

    ql = h_scr[_B_QL:_B_QL + MLA_Q_RANK, :]
    ql = ql * lax.rsqrt(jnp.sum(ql * ql, axis=0, keepdims=True) * (1.0 / MLA_Q_RANK) + EPS) * gql_ref[...]
    qb = jnp.dot(wuqt_ref[...], ql.astype(BF16), preferred_element_type=F32)
    qb = _head_norm_t(qb, gbq_ref[...], N_HEADS, MLA_QK)
    kvl = h_scr[_B_KVL:_B_KVL + MLA_KV_RANK, :]
    kvl = kvl * lax.rsqrt(jnp.sum(kvl * kvl, axis=0, keepdims=True) * (1.0 / MLA_KV_RANK) + EPS) * gkvl_ref[...]
    kvb = jnp.dot(wukvt_ref[...], kvl.astype(BF16), preferred_element_type=F32)
    kr = h_scr[_B_KR:_B_KR + MLA_ROPE, :]
    gbk = gbk_ref[...]
    for h in range(N_HEADS):
        bqt_ref[0, h] = rope_pad(qb[h], MLA_QK ** -0.5 * LOG2E).astype(BF16)
        kh = jnp.concatenate([kvb[h * 2 * HEAD_DIM:h * 2 * HEAD_DIM + MLA_NOPE, :], kr], axis=0)
        kh = kh * lax.rsqrt(jnp.sum(kh * kh, axis=0, keepdims=True) * (1.0 / MLA_QK) + EPS) * gbk
        bk_ref[0, h] = rope_pad(kh, 1.0).T.astype(BF16)
        store_vt(bvt_ref, h, kvb[h * 2 * HEAD_DIM + MLA_NOPE:(h + 1) * 2 * HEAD_DIM, :])

    qc = jnp.concatenate(_head_norm_t(h_scr[_C_Q:_C_Q + GROUP, :], gcq_ref[...], N_HEADS, HEAD_DIM), axis=0)
    kc = jnp.concatenate(_head_norm_t(h_scr[_C_K:_C_K + GROUP, :], gck_ref[...], N_HEADS, HEAD_DIM), axis=0)
    for ref, val in ((cq_ref, qc * (scale * LOG2E)), (ck_ref, kc), (cv_ref, h_scr[_C_V:_C_V + GROUP, :])):
        for c in range(N_HALF):
            ref[0, c] = val[c * LANES:(c + 1) * LANES, :].T

    qd = _head_norm_t(h_scr[_D_Q:_D_Q + GROUP, :], gdq_ref[...], N_HEADS, HEAD_DIM)
    qd = jnp.concatenate([qd[h] for h in SWA_HEAD_ORDER], axis=0)
    dq_ref[0] = (qd * (scale * LOG2E)).T.astype(BF16)
    kd = jnp.concatenate(_head_norm_t(h_scr[_D_K:_D_K + SWA_KV_WIDTH, :], gdk_ref[...], SWA_KV_HEADS, HEAD_DIM),
                         axis=0)
    dk_ref[0] = kd.T.astype(BF16)
    dv_ref[0] = h_scr[_D_V:_D_V + SWA_KV_WIDTH, :].T.astype(BF16)


def _inproj(x, gx, w1t, gaq, gak, gql, gkvl, wuqt, wukvt, gbq, gbk, gcq, gck, gdq, gdk, cos_t, sin_t, kpos_t,
            aslope):
    B, S, _ = x.shape
    tm = INPROJ_TM
    nb = S // ATT_TK
    cpt = tm // ATT_TK
    H = N_HEADS
    full = lambda a: pl.BlockSpec(a.shape, lambda b, t: (0,) * a.ndim)
    in_specs = [pl.BlockSpec((1, tm, D_MODEL), lambda b, t: (b, t, 0)), full(gx), full(w1t), full(gaq), full(gak),
                full(gql), full(gkvl), full(wuqt), full(wukvt), full(gbq), full(gbk), full(gcq), full(gck),
                full(gdq), full(gdk),
                pl.BlockSpec((MLA_ROPE // 2, tm), lambda b, t: (0, t)),
                pl.BlockSpec((MLA_ROPE // 2, tm), lambda b, t: (0, t)),
                pl.BlockSpec((AUG_ROWS, tm), lambda b, t: (0, t)), full(aslope)]
    head_t = lambda w: pl.BlockSpec((1, H, w, tm), lambda b, t: (b, 0, 0, t))
    head_n = pl.BlockSpec((1, H, tm, ATT_DK), lambda b, t: (b, 0, t, 0))
    vt_spec = pl.BlockSpec((1, H, cpt, ATT_DV, ATT_TK), lambda b, t: (b, 0, t, 0, 0))
    nat = lambda w: pl.BlockSpec((1, tm, w), lambda b, t: (b, t, 0))
    halves = pl.BlockSpec((1, N_HALF, tm, LANES), lambda b, t: (b, 0, t, 0))
    out_shape = [
        jax.ShapeDtypeStruct((B, H, ATT_DK, S), BF16),
        jax.ShapeDtypeStruct((B, H, HEAD_DIM, S), F32),
        jax.ShapeDtypeStruct((B, H, S, ATT_DK), BF16),
        jax.ShapeDtypeStruct((B, S, GROUP), F32),
        jax.ShapeDtypeStruct((B, H, nb, ATT_DV, ATT_TK), BF16),
        jax.ShapeDtypeStruct((B, H, ATT_DK, S), BF16),
        jax.ShapeDtypeStruct((B, H, S, ATT_DK), BF16),
        jax.ShapeDtypeStruct((B, H, nb, ATT_DV, ATT_TK), BF16),
        jax.ShapeDtypeStruct((B, N_HALF, S, LANES), F32),
        jax.ShapeDtypeStruct((B, N_HALF, S, LANES), F32),
        jax.ShapeDtypeStruct((B, N_HALF, S, LANES), F32),
        jax.ShapeDtypeStruct((B, S, GROUP), BF16),
        jax.ShapeDtypeStruct((B, S, SWA_KV_WIDTH), BF16),
        jax.ShapeDtypeStruct((B, S, SWA_KV_WIDTH), BF16),
    ]
    out_specs = [head_t(ATT_DK), head_t(HEAD_DIM), head_n, nat(GROUP), vt_spec,
                 head_t(ATT_DK), head_n, vt_spec,
                 halves, halves, halves, nat(GROUP), nat(SWA_KV_WIDTH), nat(SWA_KV_WIDTH)]
    return pl.pallas_call(
        _inproj_kernel,
        grid=(B, S // tm),
        in_specs=in_specs,
        out_specs=out_specs,
        out_shape=out_shape,
        scratch_shapes=[pltpu.VMEM((IN_COLS, tm), F32)],
        compiler_params=_params(2),
        name="inproj",
    )(x, gx, w1t, gaq, gak, gql, gkvl, wuqt, wukvt, gbq, gbk, gcq, gck, gdq, gdk, cos_t, sin_t, kpos_t, aslope)


def _kmean_kernel(k_ref, o_ref):
    S = k_ref.shape[1]
    k = k_ref[0].reshape(S // MOBA_BLOCK, MOBA_BLOCK, GROUP)
    o_ref[0] = jnp.sum(k, axis=1) * (1.0 / MOBA_BLOCK)


def _kmean(ak32):
    B, S, _ = ak32.shape
    nb = S // MOBA_BLOCK
    return pl.pallas_call(
        _kmean_kernel,
        grid=(B,),
        in_specs=[pl.BlockSpec((1, S, GROUP), lambda b: (b, 0, 0))],
        out_specs=pl.BlockSpec((1, nb, GROUP), lambda b: (b, 0, 0)),
        out_shape=jax.ShapeDtypeStruct((B, nb, GROUP), F32),
        compiler_params=_params(1),
        name="kmean",
    )(ak32)


def _flash_scratch(tq):
    s_buf, p_buf = pltpu.VMEM((N_HEADS, ATT_TK, tq), F32), pltpu.VMEM((N_HEADS, ATT_TK, tq), BF16)
    return [s_buf, s_buf, p_buf, p_buf, pltpu.VMEM((N_HEADS, ATT_DV, tq), F32)]


def _flash_heads(qt_ref, k_ref, vt_ref, o_ref, s_bufs, p_bufs, acc_scr, rowb_scr, i):
    tq = o_ref.shape[3]
    n_tail = tq // ATT_TK
    assert n_tail == 2
    n_past = i * n_tail
    heads = range(N_HEADS)

    def block_of(pos):
        return jnp.where(pos < n_tail, n_past + pos, pos - n_tail)

    def head_scores(h, slot, blk):
        kb = k_ref[0, h, pl.ds(pl.multiple_of(blk * ATT_TK, ATT_TK), ATT_TK), :]
        s_bufs[slot][h] = jnp.dot(kb, qt_ref[0, h], preferred_element_type=F32)

    def stage_scores(slot, blk):
        for h in heads:
            head_scores(h, slot, blk)

    def stage_softmax(slot, blk, ms, causal=None):
        new_ms, alphas = [], []
        for h in heads:
            m_parts, a_parts = [], []
            rv_row = None if rowb_scr is None else rowb_scr[h, pl.ds(blk, 1), :]
            for c in range(tq // LANES):
                cols = slice(c * LANES, (c + 1) * LANES)
                st = s_bufs[slot][h, :, cols]
                if causal is not None:
                    st = jnp.where(causal[:, cols], st, 2 * NEG)
                cm = jnp.max(st, axis=0, keepdims=True)
                m_old = ms[h][:, cols]
                if rowb_scr is None:
                    m_new = jnp.maximum(m_old, cm)
                    shift = m_new
                else:
                    rv = rv_row[:, cols]
                    m_new = jnp.maximum(m_old, cm + rv)
                    shift = m_new - rv
                a_parts.append(jnp.exp2(m_old - m_new))
                m_parts.append(m_new)
                p_bufs[slot][h, :, cols] = jnp.exp2(st - shift).astype(BF16)
            new_ms.append(jnp.concatenate(m_parts, axis=1))
            alphas.append(jnp.concatenate(a_parts, axis=1))
        return tuple(new_ms), tuple(alphas)

    def head_values(h, slot, blk, alphas):
        acc_scr[h] = alphas[h] * acc_scr[h] + jnp.dot(vt_ref[0, h, blk], p_bufs[slot][h],
                                                      preferred_element_type=F32)

    def stage_values(slot, blk, alphas):
        for h in heads:
            head_values(h, slot, blk, alphas)

    krow = lax.broadcasted_iota(jnp.int32, (ATT_TK, tq), 0)
    ti = lax.broadcasted_iota(jnp.int32, (ATT_TK, tq), 1)
    for h in heads:
        acc_scr[h] = jnp.zeros((ATT_DV, tq), F32)
    stage_scores(0, n_past)
    stage_scores(1, n_past + 1)
    ms = tuple(jnp.full((1, tq), NEG, F32) for _ in heads)
    ms, alphas = stage_softmax(0, n_past, ms, causal=ti >= krow)
    stage_values(0, n_past, alphas)
    ms, alphas = stage_softmax(1, n_past + 1, ms, causal=ti >= krow + ATT_TK)
    stage_scores(0, 0)

    def step(pos, slot, ms, alphas):
        ms, new_alphas = stage_softmax(1 - slot, pos + 1 - n_tail, ms)
        blk = block_of(pos)
        for h in heads:
            head_values(h, slot, blk, alphas)
            head_scores(h, slot, pos)
        return ms, new_alphas

    def pair(first, carry):
        ms, alphas = step(first, 1, *carry)
        return step(first + 1, 0, ms, alphas)

    def trip(t, carry):
        for u in range(FLASH_PAIRS):
            carry = pair(2 * (FLASH_PAIRS * t + u) + 1, carry)
        return carry

    carry = lax.fori_loop(0, i // FLASH_PAIRS, trip, (ms, alphas))
    ms, alphas = lax.fori_loop(i - i % FLASH_PAIRS, i, lambda u, c: pair(2 * u + 1, c), carry)
    last = n_past + 1
    stage_values(1, block_of(last), alphas)
    for h in heads:
        acc = acc_scr[h]
        o_ref[0, h] = acc[:HEAD_DIM, :] / acc[HEAD_DIM:HEAD_DIM + 1, :]


def _moba_kernel(qt_ref, q32t_ref, k_ref, vt_ref, kmean_ref, o_ref, rowb_scr, s0, s1, p0, p1, acc_scr):
    i = pl.program_id(1)
    nb = kmean_ref.shape[2]
    tq = qt_ref.shape[3]
    blk = lax.broadcasted_iota(jnp.int32, (nb, tq), 0)
    col = lax.broadcasted_iota(jnp.int32, (nb, tq), 1)
    qblk = i * (tq // MOBA_BLOCK) + col // MOBA_BLOCK
    past = blk < qblk
    dist0 = (i * tq + col - blk * MOBA_BLOCK).astype(F32)
    for h in range(N_HEADS):
        gate = jnp.dot(kmean_ref[0, h], q32t_ref[0, h], preferred_element_type=F32,
                       precision=lax.Precision.HIGHEST)
        gate = jnp.where(past, gate, NEG)
        sel = blk == qblk
        for _ in range(MOBA_TOPK):
            best = jnp.max(gate, axis=0, keepdims=True)
            first = jnp.min(jnp.where(gate == best, blk, nb), axis=0, keepdims=True)
            pick = blk == first
            sel = jnp.logical_or(sel, jnp.logical_and(pick, past))
            gate = jnp.where(pick, -jnp.inf, gate)
        rowb_scr[h] = jnp.where(sel, (-float(SLOPE_A[h]) * LOG2E) * dist0, 2 * NEG)
    _flash_heads(qt_ref, k_ref, vt_ref, o_ref, (s0, s1), (p0, p1), acc_scr, rowb_scr, i)


def _flash_specs(S, tq):
    H, nb = N_HEADS, S // ATT_TK
    return ([pl.BlockSpec((1, H, ATT_DK, tq), lambda b, i: (b, 0, 0, i)),
             pl.BlockSpec((1, H, S, ATT_DK), lambda b, i: (b, 0, 0, 0)),
             pl.BlockSpec((1, H, nb, ATT_DV, ATT_TK), lambda b, i: (b, 0, 0, 0, 0))],
            pl.BlockSpec((1, H, HEAD_DIM, tq), lambda b, i: (b, 0, 0, i)))


def _moba(aqt, aq32t, ak, avt, kmean_h):
    B, H, _, S = aqt.shape
    tq = ATT_TQ
    nb = S // ATT_TK
    (q_spec, k_spec, vt_spec), o_spec = _flash_specs(S, tq)
    return pl.pallas_call(
        _moba_kernel,
        grid=(B, S // tq),
        in_specs=[q_spec, pl.BlockSpec((1, H, HEAD_DIM, tq), lambda b, i: (b, 0, 0, i)), k_spec, vt_spec,
                  pl.BlockSpec((1, H, nb, HEAD_DIM), lambda b, i: (b, 0, 0, 0))],
        out_specs=o_spec,
        out_shape=jax.ShapeDtypeStruct((B, H, HEAD_DIM, S), F32),
        scratch_shapes=[pltpu.VMEM((H, nb, tq), F32)] + _flash_scratch(tq),
        compiler_params=_params(2),
        name="moba",
    )(aqt, aq32t, ak, avt, kmean_h)


def _mla_kernel(qt_ref, k_ref, vt_ref, o_ref, s0, s1, p0, p1, acc_scr):
    _flash_heads(qt_ref, k_ref, vt_ref, o_ref, (s0, s1), (p0, p1), acc_scr, None, pl.program_id(1))


def _mla(bqt, bk, bvt):
    B, H, _, S = bqt.shape
    tq = ATT_TQ
    in_specs, o_spec = _flash_specs(S, tq)
    return pl.pallas_call(
        _mla_kernel,
        grid=(B, S // tq),
        in_specs=in_specs,
        out_specs=o_spec,
        out_shape=jax.ShapeDtypeStruct((B, H, HEAD_DIM, S), F32),
        scratch_shapes=_flash_scratch(tq),
        compiler_params=_params(2),
        name="mla",
    )(bqt, bk, bvt)


DIL_SPAN = max(d for _, d in DILATED_BRANCHES) * Q_BLOCK
DIL_UNITS = DIL_SPAN // Q_BLOCK
DIL_GROUP = 2


def _dilated_bias():
    qi = np.arange(Q_BLOCK)[:, None]
    kidx = np.arange(2 * Q_BLOCK)[None, :]
    rel = qi + Q_BLOCK - kidx
    out = np.empty((2, len(DILATED_BRANCHES), N_HEADS, Q_BLOCK, 2 * Q_BLOCK), np.float32)
    for bi, (window, d) in enumerate(DILATED_BRANCHES):
        valid = (rel >= 0) & (rel <= window // d)
        for h in range(N_HEADS):
            bias = -SLOPE_C[h] * np.float32(LOG2E) * (d * rel).astype(np.float32)
            out[0, bi, h] = np.where(valid, bias, NEG)
            out[1, bi, h] = np.where(valid & (kidx >= Q_BLOCK), bias, NEG)
    return out


def _rows_load(ref, lead, start, size, stride):
    return jnp.concatenate([ref[lead + (c, pl.ds(start, size, stride=stride), slice(None))]
                            for c in range(N_HALF)], axis=-1)


def _rows_store(ref, lead, start, size, stride, val):
    for c in range(N_HALF):
        ref[lead + (c, pl.ds(start, size, stride=stride), slice(None))] = val[:, c * LANES:(c + 1) * LANES]


def _dilated_kernel(q_ref, kp_ref, kc_ref, vp_ref, vc_ref, bias_ref, o_ref, kbuf, vbuf, m_scr, den_scr, num_scr):
    span = pl.program_id(1)
    kbuf[:, 0:DIL_SPAN, :] = kp_ref[0]
    kbuf[:, DIL_SPAN:, :] = kc_ref[0]
    vbuf[:, 0:DIL_SPAN, :] = vp_ref[0]
    vbuf[:, DIL_SPAN:, :] = vc_ref[0]
    lane_head = lax.broadcasted_iota(jnp.int32, (1, GROUP), 1) // HEAD_DIM
    hmask = [lane_head == h for h in range(N_HEADS)]
    hmask_f = [m.astype(F32) for m in hmask]
    first_span = jnp.where(span == 0, 1, 0)

    def per_head(cols):
        out = cols[N_HEADS - 1]
        for h in range(N_HEADS - 2, -1, -1):
            out = jnp.where(hmask[h], cols[h], out)
        return out

    order = sorted(range(len(DILATED_BRANCHES)), key=lambda b: -DILATED_BRANCHES[b][1])
    for bi in order:
        d = DILATED_BRANCHES[bi][1]
        first, last = bi == order[0], bi == order[-1]

        def group(g, _, bi=bi, d=d, first=first, last=last):
            fronts = []
            for uu in range(DIL_GROUP):
                u = g * DIL_GROUP + uu
                r, n = u % d, u // d
                qstart = n * (Q_BLOCK * d) + r
                kstart = DIL_SPAN + qstart - Q_BLOCK * d
                q = _rows_load(q_ref, (0,), qstart, Q_BLOCK, d)
                k2 = _rows_load(kbuf, (), kstart, 2 * Q_BLOCK, d).astype(BF16)
                v2 = _rows_load(vbuf, (), kstart, 2 * Q_BLOCK, d).astype(BF16)
                q4 = jnp.concatenate([(q * hmask_f[h]).astype(BF16) for h in range(N_HEADS)], axis=0)
                s4 = lax.dot_general(q4, k2, _NT, preferred_element_type=F32)
                variant = jnp.where(n == 0, first_span, 0)
                fronts.append((qstart, s4, v2, variant))
            for qstart, s4, v2, variant in fronts:
                es, ms, ls = [], [], []
                for h in range(N_HEADS):
                    s = s4[h * Q_BLOCK:(h + 1) * Q_BLOCK, :] + bias_ref[variant, bi, h]
                    m = jnp.max(s, axis=-1, keepdims=True)
                    e = jnp.exp2(s - m)
                    ls.append(jnp.sum(e, axis=-1, keepdims=True))
                    ms.append(m)
                    es.append(e.astype(BF16))
                o4 = jnp.dot(jnp.concatenate(es, axis=0), v2, preferred_element_type=F32)
                o = o4[(N_HEADS - 1) * Q_BLOCK:, :]
                for h in range(N_HEADS - 2, -1, -1):
                    o = jnp.where(hmask[h], o4[h * Q_BLOCK:(h + 1) * Q_BLOCK, :], o)
                m_b, l_b = per_head(ms), per_head(ls)
                at = ((), qstart, Q_BLOCK, d)
                if first:
                    _rows_store(m_scr, *at, m_b)
                    _rows_store(num_scr, *at, o)
                    _rows_store(den_scr, *at, l_b)
                else:
                    m_old = _rows_load(m_scr, *at)
                    m_new = jnp.maximum(m_old, m_b)
                    a, b = jnp.exp2(m_old - m_new), jnp.exp2(m_b - m_new)
                    num = a * _rows_load(num_scr, *at) + b * o
                    den = a * _rows_load(den_scr, *at) + b * l_b
                    if last:
                        _rows_store(o_ref, (0,), qstart, Q_BLOCK, d, num / den)
                    else:
                        _rows_store(m_scr, *at, m_new)
                        _rows_store(num_scr, *at, num)
                        _rows_store(den_scr, *at, den)
            return 0

        lax.fori_loop(0, DIL_UNITS // DIL_GROUP, group, 0)


def _dilated(cq, ck, cv):
    B, _, S, _ = cq.shape
    cur = pl.BlockSpec((1, N_HALF, DIL_SPAN, LANES), lambda b, s: (b, 0, s, 0))
    prev = pl.BlockSpec((1, N_HALF, DIL_SPAN, LANES), lambda b, s: (b, 0, jnp.maximum(s - 1, 0), 0))
    bias = jnp.asarray(_dilated_bias())
    return pl.pallas_call(
        _dilated_kernel,
        grid=(B, S // DIL_SPAN),
        in_specs=[cur, prev, cur, prev, cur, pl.BlockSpec(bias.shape, lambda b, s: (0,) * bias.ndim)],
        out_specs=cur,
        out_shape=jax.ShapeDtypeStruct((B, N_HALF, S, LANES), F32),
        scratch_shapes=[pltpu.VMEM((N_HALF, 2 * DIL_SPAN, LANES), F32),
                        pltpu.VMEM((N_HALF, 2 * DIL_SPAN, LANES), F32),
                        pltpu.VMEM((N_HALF, DIL_SPAN, LANES), F32), pltpu.VMEM((N_HALF, DIL_SPAN, LANES), F32),
                        pltpu.VMEM((N_HALF, DIL_SPAN, LANES), F32)],
        compiler_params=_params(2),
        name="dilated",
    )(cq, ck, ck, cv, cv, bias)


SWA_SPAN = 1024
SWA_GROUP = 2
SWA_HEAD_ORDER = (0, 2, 1, 3)


def _swa_bias():
    qi = np.arange(Q_BLOCK)[:, None]
    kidx = np.arange(2 * Q_BLOCK)[None, :]
    rel = qi + Q_BLOCK - kidx
    valid = (rel >= 0) & (rel < SWA_WINDOW)
    out = np.empty((2, N_HEADS, Q_BLOCK, 2 * Q_BLOCK), np.float32)
    for h in range(N_HEADS):
        bias = -SLOPE_D[h] * np.float32(LOG2E) * rel.astype(np.float32)
        out[0, h] = np.where(valid, bias, NEG)
        out[1, h] = np.where(valid & (kidx >= Q_BLOCK), bias, NEG)
    return out


def _swa_kernel(sink_ref, q_ref, k_ref, v_ref, bias_ref, o_ref):
    span = pl.program_id(1)
    units = SWA_SPAN // Q_BLOCK
    half = lax.broadcasted_iota(jnp.int32, (1, LANES), 1) // HEAD_DIM
    lo = half == 0
    kv_mask = [jnp.where(half == g, 1.0, 0.0).astype(BF16) for g in range(SWA_KV_HEADS)]

    def group(g, _):
        fronts = []
        for uu in range(SWA_GROUP):
            u = g * SWA_GROUP + uu
            n = span * units + u
            lo_start = pl.multiple_of(jnp.maximum(n - 1, 0) * Q_BLOCK, Q_BLOCK)
            hi_start = pl.multiple_of(n * Q_BLOCK, Q_BLOCK)
            qstart = pl.multiple_of(u * Q_BLOCK, Q_BLOCK)
            q = q_ref[0, pl.ds(qstart, Q_BLOCK), :]
            k2 = jnp.concatenate([k_ref[0, pl.ds(lo_start, Q_BLOCK), :], k_ref[0, pl.ds(hi_start, Q_BLOCK), :]],
                                 axis=0)
            v2 = jnp.concatenate([v_ref[0, pl.ds(lo_start, Q_BLOCK), :], v_ref[0, pl.ds(hi_start, Q_BLOCK), :]],
                                 axis=0)
            q4 = jnp.concatenate([q[:, (h % 2) * LANES:(h % 2 + 1) * LANES] * kv_mask[h // 2]
                                  for h in range(N_HEADS)], axis=0)
            s4 = lax.dot_general(q4, k2, _NT, preferred_element_type=F32)
            fronts.append((qstart, s4, v2, jnp.where(n == 0, 1, 0)))
        for qstart, s4, v2, variant in fronts:
            es, ls = [], []
            for h in range(N_HEADS):
                s = s4[h * Q_BLOCK:(h + 1) * Q_BLOCK, :] + bias_ref[variant, h]
                sink = sink_ref[h] * LOG2E
                m = jnp.maximum(jnp.max(s, axis=-1, keepdims=True), sink)
                e = jnp.exp2(s - m)
                ls.append(jnp.sum(e, axis=-1, keepdims=True) + jnp.exp2(sink - m))
                es.append(e.astype(BF16))
            o4 = jnp.dot(jnp.concatenate(es, axis=0), v2, preferred_element_type=F32)
            tiles = []
            for t in range(N_HALF):
                a, b = t, t + 2
                tiles.append(jnp.where(lo, o4[a * Q_BLOCK:(a + 1) * Q_BLOCK, :] / ls[a],
                                       o4[b * Q_BLOCK:(b + 1) * Q_BLOCK, :] / ls[b]))
            o_ref[0, pl.ds(qstart, Q_BLOCK), :] = jnp.concatenate(tiles, axis=-1)
        return 0

    lax.fori_loop(0, units // SWA_GROUP, group, 0)


def _swa(dq, dk, dv, sinks):
    B, S, _ = dq.shape
    bias = jnp.asarray(_swa_bias())
    return pl.pallas_call(
        _swa_kernel,
        grid=(B, S // SWA_SPAN),
        in_specs=[pl.BlockSpec(memory_space=pltpu.SMEM),
                  pl.BlockSpec((1, SWA_SPAN, GROUP), lambda b, i: (b, i, 0)),
                  pl.BlockSpec((1, S, SWA_KV_WIDTH), lambda b, i: (b, 0, 0)),
                  pl.BlockSpec((1, S, SWA_KV_WIDTH), lambda b, i: (b, 0, 0)),
                  pl.BlockSpec(bias.shape, lambda b, i: (0,) * bias.ndim)],
        out_specs=pl.BlockSpec((1, SWA_SPAN, GROUP), lambda b, i: (b, i, 0)),
        out_shape=jax.ShapeDtypeStruct((B, S, GROUP), F32),
        compiler_params=_params(2),
        name="swa",
    )(sinks, dq, dk, dv, bias)


def _row_norm(y, g_row):
    return y * lax.rsqrt(jnp.mean(y * y, axis=-1, keepdims=True) + EPS) * g_row


def _tail_kernel(x_ref, oat_ref, obt_ref, oc_ref, od_ref, gg_ref, wo_ref, gm_ref, wup_ref, wdn_ref, out_ref):
    gg = gg_ref[...]

    def col_norm_t(yt, g_row):
        y = (yt * lax.rsqrt(jnp.mean(yt * yt, axis=0, keepdims=True) + EPS)).T
        return y * g_row

    ga = col_norm_t(oat_ref[0], gg[0:1, :])
    gb = col_norm_t(obt_ref[0], gg[1:2, :])
    gc = _row_norm(jnp.concatenate([oc_ref[0, c] for c in range(N_HALF)], axis=-1), gg[2:3, :])
    gd = _row_norm(od_ref[0], gg[3:4, :])
    mixed = jnp.concatenate([ga, gb, gc, gd], axis=-1).astype(BF16)
    x1 = x_ref[0] + jnp.dot(mixed, wo_ref[...], preferred_element_type=F32)
    xn = _row_norm(x1, gm_ref[...]).astype(BF16)
    u = jnp.maximum(jnp.dot(xn, wup_ref[...], preferred_element_type=F32), 0.0)
    out_ref[0] = x1 + jnp.dot((u * u).astype(BF16), wdn_ref[...], preferred_element_type=F32)


def _tail(x, oat, obt, oc, od, gg, wo, gm, wup, wdn):
    B, S, _ = x.shape
    tm = TAIL_TM
    nat = lambda w: pl.BlockSpec((1, tm, w), lambda b, t: (b, t, 0))
    ft = pl.BlockSpec((1, GROUP, tm), lambda b, t: (b, 0, t))
    const = lambda a: pl.BlockSpec(a.shape, lambda b, t: (0,) * a.ndim, pipeline_mode=pl.Buffered(1))
    return pl.pallas_call(
        _tail_kernel,
        grid=(B, S // tm),
        in_specs=[nat(D_MODEL), ft, ft, pl.BlockSpec((1, N_HALF, tm, LANES), lambda b, t: (b, 0, t, 0)),
                  nat(GROUP), const(gg), const(wo), const(gm),
                  const(wup), const(wdn)],
        out_specs=nat(D_MODEL),
        out_shape=jax.ShapeDtypeStruct((B, S, D_MODEL), F32),
        compiler_params=_params(2),
        name="tail",
    )(x, oat, obt, oc, od, gg, wo, gm, wup, wdn)


def _rope_tables_t(S):
    inv = 1.0 / (ROPE_THETA ** (jnp.arange(0, MLA_ROPE, 2, dtype=F32) / MLA_ROPE))
    ang = inv[:, None] * jnp.arange(S, dtype=F32)[None, :]
    return jnp.cos(ang), jnp.sin(ang)


def _moba_aug_tables(S):
    pos = np.arange(S, dtype=np.float32) % MOBA_BLOCK
    kpos = np.zeros((AUG_ROWS, S), np.float32)
    kpos[:3] = pos
    aslope = np.zeros((N_HEADS, AUG_ROWS, 1), np.float32)
    rest = (SLOPE_A * np.float32(LOG2E)).astype(np.float32)
    for r in range(3):
        piece = rest.astype(BF16).astype(np.float32)
        aslope[:, r, 0] = piece
        rest = rest - piece
    return jnp.asarray(kpos), jnp.asarray(aslope)


def _layer(x, cos_t, sin_t, kpos_t, aslope, attn_norm_g, w_in, moba_q_g, moba_k_g, mla_qlat_g, mla_kvlat_g, mla_w_uq, mla_w_ukv,
           mla_q_g, mla_k_g, dil_q_g, dil_k_g, swa_q_g, swa_k_g, swa_sinks, group_out_g, w_o, mlp_norm_g,
           w_up, w_down):
    B, S, _ = x.shape
    col = lambda g: g.reshape(-1, 1)
    (aqt, aq32t, ak, ak32, avt, bqt, bk, bvt, cq, ck, cv, dq, dk, dv) = _inproj(
        x, attn_norm_g.reshape(1, -1), w_in.T.astype(BF16), col(moba_q_g), col(moba_k_g), col(mla_qlat_g),
        col(mla_kvlat_g), mla_w_uq.T.astype(BF16), mla_w_ukv.T.astype(BF16), col(mla_q_g), col(mla_k_g),
        col(dil_q_g), col(dil_k_g), col(swa_q_g), col(swa_k_g), cos_t, sin_t, kpos_t, aslope)
    kmean = _kmean(ak32)
    nb = S // MOBA_BLOCK
    kmean_h = kmean.reshape(B, nb, N_HEADS, HEAD_DIM).transpose(0, 2, 1, 3)
    oat = _moba(aqt, aq32t, ak, avt, kmean_h).reshape(B, GROUP, S)
    obt = _mla(bqt, bk, bvt).reshape(B, GROUP, S)
    oc = _dilated(cq, ck, cv)
    od = _swa(dq, dk, dv, swa_sinks)
    perm = np.concatenate([np.arange(HEAD_DIM) + HEAD_DIM * h for h in SWA_HEAD_ORDER])
    gg = group_out_g.at[3].set(group_out_g[3][perm])
    wo = jnp.concatenate([w_o[:3 * GROUP], w_o[3 * GROUP + perm]], axis=0)
    return _tail(x, oat, obt, oc, od, gg, wo.astype(BF16), mlp_norm_g.reshape(1, -1),
                 w_up.astype(BF16), w_down.astype(BF16))


def kernel(x, attn_norm_g, w_in, moba_q_g, moba_k_g, mla_qlat_g, mla_kvlat_g, mla_w_uq, mla_w_ukv, mla_q_g,
           mla_k_g, dil_q_g, dil_k_g, swa_q_g, swa_k_g, swa_sinks, group_out_g, w_o, mlp_norm_g, w_up, w_down):
    S = x.shape[1]
    assert S % max(d * Q_BLOCK for _, d in DILATED_BRANCHES) == 0 and S % INPROJ_TM == 0
    cos_t, sin_t = _rope_tables_t(S)
    kpos_t, aslope = _moba_aug_tables(S)
    params = (attn_norm_g, w_in, moba_q_g, moba_k_g, mla_qlat_g, mla_kvlat_g, mla_w_uq, mla_w_ukv, mla_q_g,
              mla_k_g, dil_q_g, dil_k_g, swa_q_g, swa_k_g, swa_sinks, group_out_g, w_o, mlp_norm_g, w_up, w_down)
    for l in range(attn_norm_g.shape[0]):
        x = _layer(x, cos_t, sin_t, kpos_t, aslope, *[p[l] for p in params])
    return x
```

```python
import functools

import numpy as np
import jax
import jax.numpy as jnp
from jax import lax
from jax.experimental import pallas as pl
from jax.experimental.pallas import tpu as pltpu

F32 = jnp.float32
BF16 = jnp.bfloat16

D_MODEL = 1024
HEAD_DIM = 64
N_HEADS = 4
GROUP = N_HEADS * HEAD_DIM
LANES = 128
N_HALF = GROUP // LANES
MOBA_BLOCK = 256
MOBA_TOPK = 3
MLA_Q_RANK = 256
MLA_KV_RANK = 128
MLA_NOPE = 64
MLA_ROPE = 32
MLA_QK = MLA_NOPE + MLA_ROPE
MLA_QK_PAD = 128
ROPE_THETA = 10000.0
DILATED_BRANCHES = ((128, 1), (512, 4), (2048, 16))
Q_BLOCK = 128
SWA_WINDOW = 128
SWA_KV_HEADS = 2
SWA_KV_WIDTH = SWA_KV_HEADS * HEAD_DIM
D_FF = 4 * D_MODEL
EPS = 1e-6
NEG = -1e30

_WIDTHS = (GROUP, GROUP, GROUP, MLA_Q_RANK, MLA_KV_RANK, MLA_ROPE,
           GROUP, GROUP, GROUP, GROUP, SWA_KV_WIDTH, SWA_KV_WIDTH)
_OFFS = tuple(int(v) for v in np.cumsum((0,) + _WIDTHS))
IN_COLS = _OFFS[-1]
(_A_Q, _A_K, _A_V, _B_QL, _B_KVL, _B_KR, _C_Q, _C_K, _C_V, _D_Q, _D_K, _D_V) = _OFFS[:-1]

VMEM_LIMIT = 56 * 1024 * 1024

INPROJ_TM = 512
TAIL_TM = 512
ATT_TK = 256
ATT_TQ = 2 * ATT_TK
ATT_DK = 128
ATT_DV = HEAD_DIM + 16
AUG_ROWS = 8
LOG2E = 1.4426950408889634
FLASH_PAIRS = 2

_NT = (((1,), (1,)), ((), ()))


def _alibi_slopes():
    n = 3 * N_HEADS
    idx = np.arange(1, n + 1, dtype=np.float32).reshape(N_HEADS, 3)
    s = np.exp2(-8.0 * idx / n).astype(np.float32)
    return s[:, 0], s[:, 1], s[:, 2]


SLOPE_A, SLOPE_C, SLOPE_D = _alibi_slopes()


def _params(n_axes):
    return pltpu.CompilerParams(dimension_semantics=("arbitrary",) * n_axes,
                                vmem_limit_bytes=VMEM_LIMIT)


def _head_norm_t(sec, g_col, n_heads, width):
    outs = []
    for h in range(n_heads):
        s = sec[h * width:(h + 1) * width, :]
        ms = jnp.sum(s * s, axis=0, keepdims=True) * (1.0 / width)
        outs.append(s * lax.rsqrt(ms + EPS) * g_col)
    return outs


def _inproj_kernel(x_ref, gx_ref, w1t_ref, gaq_ref, gak_ref, gql_ref, gkvl_ref, wuqt_ref, wukvt_ref,
                   gbq_ref, gbk_ref, gcq_ref, gck_ref, gdq_ref, gdk_ref, cos_ref, sin_ref, kpos_ref, aslope_ref,
                   aqt_ref, aq32t_ref, ak_ref, ak32_ref, avt_ref,
                   bqt_ref, bk_ref, bvt_ref,
                   cq_ref, ck_ref, cv_ref, dq_ref, dk_ref, dv_ref,
                   h_scr):
    tm = x_ref.shape[1]
    x = x_ref[0]
    ms = jnp.mean(x * x, axis=-1, keepdims=True)
    xn = (x * lax.rsqrt(ms + EPS) * gx_ref[...]).astype(BF16)
    h_scr[...] = lax.dot_general(w1t_ref[...], xn, _NT, preferred_element_type=F32)

    scale = HEAD_DIM ** -0.5

    ones_rows = jnp.ones((ATT_DV - HEAD_DIM, ATT_TK), F32)

    def store_vt(ref, h, vh):
        for c in range(tm // ATT_TK):
            ref[0, h, c] = jnp.concatenate([vh[:, c * ATT_TK:(c + 1) * ATT_TK], ones_rows], axis=0).astype(BF16)

    pad_rows = jnp.zeros((ATT_DK - HEAD_DIM - AUG_ROWS, tm), F32)
    qa = _head_norm_t(h_scr[_A_Q:_A_Q + GROUP, :], gaq_ref[...], N_HEADS, HEAD_DIM)
    ka = _head_norm_t(h_scr[_A_K:_A_K + GROUP, :], gak_ref[...], N_HEADS, HEAD_DIM)
    ak32_ref[0] = jnp.concatenate(ka, axis=0).T
    kpos = kpos_ref[...]
    for h in range(N_HEADS):
        aq32t_ref[0, h] = qa[h]
        slope_rows = jnp.broadcast_to(aslope_ref[h], (AUG_ROWS, tm))
        aqt_ref[0, h] = jnp.concatenate([qa[h] * (scale * LOG2E), slope_rows, pad_rows], axis=0).astype(BF16)
        ak_ref[0, h] = jnp.concatenate([ka[h], kpos, pad_rows], axis=0).T.astype(BF16)
        store_vt(avt_ref, h, h_scr[_A_V + h * HEAD_DIM:_A_V + (h + 1) * HEAD_DIM, :])

    cos = cos_ref[...]
    sin = sin_ref[...]
    half = MLA_ROPE // 2

    def rope_pad(t, sc):
        x1 = t[MLA_NOPE:MLA_NOPE + half, :]
        x2 = t[MLA_NOPE + half:MLA_QK, :]
        return jnp.concatenate([t[:MLA_NOPE, :] * sc, (x1 * cos - x2 * sin) * sc, (x1 * sin + x2 * cos) * sc,
                                jnp.zeros((MLA_QK_PAD - MLA_QK, tm), F32)], axis=0)

    ql = h_scr[_B_QL:_B_QL + MLA_Q_RANK, :]
    ql = ql * lax.rsqrt(jnp.sum(ql * ql, axis=0, keepdims=True) * (1.0 / MLA_Q_RANK) + EPS) * gql_ref[...]
    qb = jnp.dot(wuqt_ref[...], ql.astype(BF16), preferred_element_type=F32)
    qb = _head_norm_t(qb, gbq_ref[...], N_HEADS, MLA_QK)
    kvl = h_scr[_B_KVL:_B_KVL + MLA_KV_RANK, :]
    kvl = kvl * lax.rsqrt(jnp.sum(kvl * kvl, axis=0, keepdims=True) * (1.0 / MLA_KV_RANK) + EPS) * gkvl_ref[...]
    kvb = jnp.dot(wukvt_ref[...], kvl.astype(BF16), preferred_element_type=F32)
    kr = h_scr[_B_KR:_B_KR + MLA_ROPE, :]
    gbk = gbk_ref[...]
    for h in range(N_HEADS):
        bqt_ref[0, h] = rope_pad(qb[h], MLA_QK ** -0.5 * LOG2E).astype(BF16)
        kh = jnp.concatenate([kvb[h * 2 * HEAD_DIM:h * 2 * HEAD_DIM + MLA_NOPE, :], kr], axis=0)
        kh = kh * lax.rsqrt(jnp.sum(kh * kh, axis=0, keepdims=True) * (1.0 / MLA_QK) + EPS) * gbk
        bk_ref[0, h] = rope_pad(kh, 1.0).T.astype(BF16)
        store_vt(bvt_ref, h, kvb[h * 2 * HEAD_DIM + MLA_NOPE:(h + 1) * 2 * HEAD_DIM, :])

    qc = jnp.concatenate(_head_norm_t(h_scr[_C_Q:_C_Q + GROUP, :], gcq_ref[...], N_HEADS, HEAD_DIM), axis=0)
    kc = jnp.concatenate(_head_norm_t(h_scr[_C_K:_C_K + GROUP, :], gck_ref[...], N_HEADS, HEAD_DIM), axis=0)
    for ref, val in ((cq_ref, qc * (scale * LOG2E)), (ck_ref, kc), (cv_ref, h_scr[_C_V:_C_V + GROUP, :])):
        for c in range(N_HALF):
            ref[0, c] = val[c * LANES:(c + 1) * LANES, :].T

    qd = _head_norm_t(h_scr[_D_Q:_D_Q + GROUP, :], gdq_ref[...], N_HEADS, HEAD_DIM)
    qd = jnp.concatenate([qd[h] for h in SWA_HEAD_ORDER], axis=0)
    dq_ref[0] = (qd * (scale * LOG2E)).T.astype(BF16)
    kd = jnp.concatenate(_head_norm_t(h_scr[_D_K:_D_K + SWA_KV_WIDTH, :], gdk_ref[...], SWA_KV_HEADS, HEAD_DIM),
                         axis=0)
    dk_ref[0] = kd.T.astype(BF16)
    dv_ref[0] = h_scr[_D_V:_D_V + SWA_KV_WIDTH, :].T.astype(BF16)


def _inproj(x, gx, w1t, gaq, gak, gql, gkvl, wuqt, wukvt, gbq, gbk, gcq, gck, gdq, gdk, cos_t, sin_t, kpos_t,
            aslope):
    B, S, _ = x.shape
    tm = INPROJ_TM
    nb = S // ATT_TK
    cpt = tm // ATT_TK
    H = N_HEADS
    full = lambda a: pl.BlockSpec(a.shape, lambda b, t: (0,) * a.ndim)
    in_specs = [pl.BlockSpec((1, tm, D_MODEL), lambda b, t: (b, t, 0)), full(gx), full(w1t), full(gaq), full(gak),
                full(gql), full(gkvl), full(wuqt), full(wukvt), full(gbq), full(gbk), full(gcq), full(gck),
                full(gdq), full(gdk),
                pl.BlockSpec((MLA_ROPE // 2, tm), lambda b, t: (0, t)),
                pl.BlockSpec((MLA_ROPE // 2, tm), lambda b, t: (0, t)),
                pl.BlockSpec((AUG_ROWS, tm), lambda b, t: (0, t)), full(aslope)]
    head_t = lambda w: pl.BlockSpec((1, H, w, tm), lambda b, t: (b, 0, 0, t))
    head_n = pl.BlockSpec((1, H, tm, ATT_DK), lambda b, t: (b, 0, t, 0))
    vt_spec = pl.BlockSpec((1, H, cpt, ATT_DV, ATT_TK), lambda b, t: (b, 0, t, 0, 0))
    nat = lambda w: pl.BlockSpec((1, tm, w), lambda b, t: (b, t, 0))
    halves = pl.BlockSpec((1, N_HALF, tm, LANES), lambda b, t: (b, 0, t, 0))
    out_shape = [
        jax.ShapeDtypeStruct((B, H, ATT_DK, S), BF16),
        jax.ShapeDtypeStruct((B, H, HEAD_DIM, S), F32),
        jax.ShapeDtypeStruct((B, H, S, ATT_DK), BF16),
        jax.ShapeDtypeStruct((B, S, GROUP), F32),
        jax.ShapeDtypeStruct((B, H, nb, ATT_DV, ATT_TK), BF16),
        jax.ShapeDtypeStruct((B, H, ATT_DK, S), BF16),
        jax.ShapeDtypeStruct((B, H, S, ATT_DK), BF16),
        jax.ShapeDtypeStruct((B, H, nb, ATT_DV, ATT_TK), BF16),
        jax.ShapeDtypeStruct((B, N_HALF, S, LANES), F32),
        jax.ShapeDtypeStruct((B, N_HALF, S, LANES), F32),
        jax.ShapeDtypeStruct((B, N_HALF, S, LANES), F32),
        jax.ShapeDtypeStruct((B, S, GROUP), BF16),
        jax.ShapeDtypeStruct((B, S, SWA_KV_WIDTH), BF16),
        jax.ShapeDtypeStruct((B, S, SWA_KV_WIDTH), BF16),
    ]
    out_specs = [head_t(ATT_DK), head_t(HEAD_DIM), head_n, nat(GROUP), vt_spec,
                 head_t(ATT_DK), head_n, vt_spec,
                 halves, halves, halves, nat(GROUP), nat(SWA_KV_WIDTH), nat(SWA_KV_WIDTH)]
    return pl.pallas_call(
        _inproj_kernel,
        grid=(B, S // tm),
        in_specs=in_specs,
        out_specs=out_specs,
        out_shape=out_shape,
        scratch_shapes=[pltpu.VMEM((IN_COLS, tm), F32)],
        compiler_params=_params(2),
        name="inproj",
    )(x, gx, w1t, gaq, gak, gql, gkvl, wuqt, wukvt, gbq, gbk, gcq, gck, gdq, gdk, cos_t, sin_t, kpos_t, aslope)


def _kmean_kernel(k_ref, o_ref):
    S = k_ref.shape[1]
    k = k_ref[0].reshape(S // MOBA_BLOCK, MOBA_BLOCK, GROUP)
    o_ref[0] = jnp.sum(k, axis=1) * (1.0 / MOBA_BLOCK)


def _kmean(ak32):
    B, S, _ = ak32.shape
    nb = S // MOBA_BLOCK
    return pl.pallas_call(
        _kmean_kernel,
        grid=(B,),
        in_specs=[pl.BlockSpec((1, S, GROUP), lambda b: (b, 0, 0))],
        out_specs=pl.BlockSpec((1, nb, GROUP), lambda b: (b, 0, 0)),
        out_shape=jax.ShapeDtypeStruct((B, nb, GROUP), F32),
        compiler_params=_params(1),
        name="kmean",
    )(ak32)


def _flash_scratch(tq):
    s_buf, p_buf = pltpu.VMEM((N_HEADS, ATT_TK, tq), F32), pltpu.VMEM((N_HEADS, ATT_TK, tq), BF16)
    return [s_buf, s_buf, p_buf, p_buf, pltpu.VMEM((N_HEADS, ATT_DV, tq), F32)]


def _flash_heads(qt_ref, k_ref, vt_ref, o_ref, s_bufs, p_bufs, acc_scr, rowb_scr, i):
    tq = o_ref.shape[3]
    n_tail = tq // ATT_TK
    assert n_tail == 2
    n_past = i * n_tail
    heads = range(N_HEADS)

    def block_of(pos):
        return jnp.where(pos < n_tail, n_past + pos, pos - n_tail)

    def head_scores(h, slot, blk):
        kb = k_ref[0, h, pl.ds(pl.multiple_of(blk * ATT_TK, ATT_TK), ATT_TK), :]
        s_bufs[slot][h] = jnp.dot(kb, qt_ref[0, h], preferred_element_type=F32)

    def stage_scores(slot, blk):
        for h in heads:
            head_scores(h, slot, blk)

    def stage_softmax(slot, blk, ms, causal=None):
        new_ms, alphas = [], []
        for h in heads:
            m_parts, a_parts = [], []
            rv_row = None if rowb_scr is None else rowb_scr[h, pl.ds(blk, 1), :]
            for c in range(tq // LANES):
                cols = slice(c * LANES, (c + 1) * LANES)
                st = s_bufs[slot][h, :, cols]
                if causal is not None:
                    st = jnp.where(causal[:, cols], st, 2 * NEG)
                cm = jnp.max(st, axis=0, keepdims=True)
                m_old = ms[h][:, cols]
                if rowb_scr is None:
                    m_new = jnp.maximum(m_old, cm)
                    shift = m_new
                else:
                    rv = rv_row[:, cols]
                    m_new = jnp.maximum(m_old, cm + rv)
                    shift = m_new - rv
                a_parts.append(jnp.exp2(m_old - m_new))
                m_parts.append(m_new)
                p_bufs[slot][h, :, cols] = jnp.exp2((st - shift).astype(BF16))
            new_ms.append(jnp.concatenate(m_parts, axis=1))
            alphas.append(jnp.concatenate(a_parts, axis=1))
        return tuple(new_ms), tuple(alphas)

    def head_values(h, slot, blk, alphas):
        acc_scr[h] = alphas[h] * acc_scr[h] + jnp.dot(vt_ref[0, h, blk], p_bufs[slot][h],
                                                      preferred_element_type=F32)

    def stage_values(slot, blk, alphas):
        for h in heads:
            head_values(h, slot, blk, alphas)

    krow = lax.broadcasted_iota(jnp.int32, (ATT_TK, tq), 0)
    ti = lax.broadcasted_iota(jnp.int32, (ATT_TK, tq), 1)
    for h in heads:
        acc_scr[h] = jnp.zeros((ATT_DV, tq), F32)
    stage_scores(0, n_past)
    stage_scores(1, n_past + 1)
    ms = tuple(jnp.full((1, tq), NEG, F32) for _ in heads)
    ms, alphas = stage_softmax(0, n_past, ms, causal=ti >= krow)
    stage_values(0, n_past, alphas)
    ms, alphas = stage_softmax(1, n_past + 1, ms, causal=ti >= krow + ATT_TK)
    stage_scores(0, 0)

    def step(pos, slot, ms, alphas):
        ms, new_alphas = stage_softmax(1 - slot, pos + 1 - n_tail, ms)
        blk = block_of(pos)
        for h in heads:
            head_values(h, slot, blk, alphas)
            head_scores(h, slot, pos)
        return ms, new_alphas

    def pair(first, carry):
        ms, alphas = step(first, 1, *carry)
        return step(first + 1, 0, ms, alphas)

    def trip(t, carry):
        for u in range(FLASH_PAIRS):
            carry = pair(2 * (FLASH_PAIRS * t + u) + 1, carry)
        return carry

    carry = lax.fori_loop(0, i // FLASH_PAIRS, trip, (ms, alphas))
    ms, alphas = lax.fori_loop(i - i % FLASH_PAIRS, i, lambda u, c: pair(2 * u + 1, c), carry)
    last = n_past + 1
    stage_values(1, block_of(last), alphas)
    for h in heads:
        acc = acc_scr[h]
        o_ref[0, h] = acc[:HEAD_DIM, :] / acc[HEAD_DIM:HEAD_DIM + 1, :]


def _moba_kernel(qt_ref, q32t_ref, k_ref, vt_ref, kmean_ref, o_ref, rowb_scr, s0, s1, p0, p1, acc_scr):
    i = pl.program_id(1)
    nb = kmean_ref.shape[2]
    tq = qt_ref.shape[3]
    blk = lax.broadcasted_iota(jnp.int32, (nb, tq), 0)
    col = lax.broadcasted_iota(jnp.int32, (nb, tq), 1)
    qblk = i * (tq // MOBA_BLOCK) + col // MOBA_BLOCK
    past = blk < qblk
    dist0 = (i * tq + col - blk * MOBA_BLOCK).astype(F32)
    for h in range(N_HEADS):
        gate = jnp.dot(kmean_ref[0, h], q32t_ref[0, h], preferred_element_type=F32,
                       precision=lax.Precision.HIGHEST)
        gate = jnp.where(past, gate, NEG)
        sel = blk == qblk
        for _ in range(MOBA_TOPK):
            best = jnp.max(gate, axis=0, keepdims=True)
            first = jnp.min(jnp.where(gate == best, blk, nb), axis=0, keepdims=True)
            pick = blk == first
            sel = jnp.logical_or(sel, jnp.logical_and(pick, past))
            gate = jnp.where(pick, -jnp.inf, gate)
        rowb_scr[h] = jnp.where(sel, (-float(SLOPE_A[h]) * LOG2E) * dist0, 2 * NEG)
    _flash_heads(qt_ref, k_ref, vt_ref, o_ref, (s0, s1), (p0, p1), acc_scr, rowb_scr, i)


def _flash_specs(S, tq):
    H, nb = N_HEADS, S // ATT_TK
    return ([pl.BlockSpec((1, H, ATT_DK, tq), lambda b, i: (b, 0, 0, i)),
             pl.BlockSpec((1, H, S, ATT_DK), lambda b, i: (b, 0, 0, 0)),
             pl.BlockSpec((1, H, nb, ATT_DV, ATT_TK), lambda b, i: (b, 0, 0, 0, 0))],
            pl.BlockSpec((1, H, HEAD_DIM, tq), lambda b, i: (b, 0, 0, i)))


def _moba(aqt, aq32t, ak, avt, kmean_h):
    B, H, _, S = aqt.shape
    tq = ATT_TQ
    nb = S // ATT_TK
    (q_spec, k_spec, vt_spec), o_spec = _flash_specs(S, tq)
    return pl.pallas_call(
        _moba_kernel,
        grid=(B, S // tq),
        in_specs=[q_spec, pl.BlockSpec((1, H, HEAD_DIM, tq), lambda b, i: (b, 0, 0, i)), k_spec, vt_spec,
                  pl.BlockSpec((1, H, nb, HEAD_DIM), lambda b, i: (b, 0, 0, 0))],
        out_specs=o_spec,
        out_shape=jax.ShapeDtypeStruct((B, H, HEAD_DIM, S), F32),
        scratch_shapes=[pltpu.VMEM((H, nb, tq), F32)] + _flash_scratch(tq),
        compiler_params=_params(2),
        name="moba",
    )(aqt, aq32t, ak, avt, kmean_h)


def _mla_kernel(qt_ref, k_ref, vt_ref, o_ref, s0, s1, p0, p1, acc_scr):
    _flash_heads(qt_ref, k_ref, vt_ref, o_ref, (s0, s1), (p0, p1), acc_scr, None, pl.program_id(1))


def _mla(bqt, bk, bvt):
    B, H, _, S = bqt.shape
    tq = ATT_TQ
    in_specs, o_spec = _flash_specs(S, tq)
    return pl.pallas_call(
        _mla_kernel,
        grid=(B, S // tq),
        in_specs=in_specs,
        out_specs=o_spec,
        out_shape=jax.ShapeDtypeStruct((B, H, HEAD_DIM, S), F32),
        scratch_shapes=_flash_scratch(tq),
        compiler_params=_params(2),
        name="mla",
    )(bqt, bk, bvt)


DIL_SPAN = max(d for _, d in DILATED_BRANCHES) * Q_BLOCK
DIL_UNITS = DIL_SPAN // Q_BLOCK
DIL_GROUP = 2


def _dilated_bias():
    qi = np.arange(Q_BLOCK)[:, None]
    kidx = np.arange(2 * Q_BLOCK)[None, :]
    rel = qi + Q_BLOCK - kidx
    out = np.empty((2, len(DILATED_BRANCHES), N_HEADS, Q_BLOCK, 2 * Q_BLOCK), np.float32)
    for bi, (window, d) in enumerate(DILATED_BRANCHES):
        valid = (rel >= 0) & (rel <= window // d)
        for h in range(N_HEADS):
            bias = -SLOPE_C[h] * np.float32(LOG2E) * (d * rel).astype(np.float32)
            out[0, bi, h] = np.where(valid, bias, NEG)
            out[1, bi, h] = np.where(valid & (kidx >= Q_BLOCK), bias, NEG)
    return out


def _rows_load(ref, lead, start, size, stride):
    return jnp.concatenate([ref[lead + (c, pl.ds(start, size, stride=stride), slice(None))]
                            for c in range(N_HALF)], axis=-1)


def _rows_store(ref, lead, start, size, stride, val):
    for c in range(N_HALF):
        ref[lead + (c, pl.ds(start, size, stride=stride), slice(None))] = val[:, c * LANES:(c + 1) * LANES]


def _dilated_kernel(q_ref, kp_ref, kc_ref, vp_ref, vc_ref, bias_ref, o_ref, kbuf, vbuf, m_scr, den_scr, num_scr):
    span = pl.program_id(1)
    kbuf[:, 0:DIL_SPAN, :] = kp_ref[0]
    kbuf[:, DIL_SPAN:, :] = kc_ref[0]
    vbuf[:, 0:DIL_SPAN, :] = vp_ref[0]
    vbuf[:, DIL_SPAN:, :] = vc_ref[0]
    lane_head = lax.broadcasted_iota(jnp.int32, (1, GROUP), 1) // HEAD_DIM
    hmask = [lane_head == h for h in range(N_HEADS)]
    hmask_f = [m.astype(F32) for m in hmask]
    first_span = jnp.where(span == 0, 1, 0)

    def per_head(cols):
        out = cols[N_HEADS - 1]
        for h in range(N_HEADS - 2, -1, -1):
            out = jnp.where(hmask[h], cols[h], out)
        return out

    order = sorted(range(len(DILATED_BRANCHES)), key=lambda b: -DILATED_BRANCHES[b][1])
    for bi in order:
        d = DILATED_BRANCHES[bi][1]
        first, last = bi == order[0], bi == order[-1]

        def group(g, _, bi=bi, d=d, first=first, last=last):
            fronts = []
            for uu in range(DIL_GROUP):
                u = g * DIL_GROUP + uu
                r, n = u % d, u // d
                qstart = n * (Q_BLOCK * d) + r
                kstart = DIL_SPAN + qstart - Q_BLOCK * d
                q = _rows_load(q_ref, (0,), qstart, Q_BLOCK, d)
                k2 = _rows_load(kbuf, (), kstart, 2 * Q_BLOCK, d).astype(BF16)
                v2 = _rows_load(vbuf, (), kstart, 2 * Q_BLOCK, d).astype(BF16)
                q4 = jnp.concatenate([(q * hmask_f[h]).astype(BF16) for h in range(N_HEADS)], axis=0)
                s4 = lax.dot_general(q4, k2, _NT, preferred_element_type=F32)
                variant = jnp.where(n == 0, first_span, 0)
                fronts.append((qstart, s4, v2, variant))
            for qstart, s4, v2, variant in fronts:
                es, ms, ls = [], [], []
                for h in range(N_HEADS):
                    s = s4[h * Q_BLOCK:(h + 1) * Q_BLOCK, :] + bias_ref[variant, bi, h]
                    m = jnp.max(s, axis=-1, keepdims=True)
                    e = jnp.exp2(s - m)
                    ls.append(jnp.sum(e, axis=-1, keepdims=True))
                    ms.append(m)
                    es.append(e.astype(BF16))
                o4 = jnp.dot(jnp.concatenate(es, axis=0), v2, preferred_element_type=F32)
                o = o4[(N_HEADS - 1) * Q_BLOCK:, :]
                for h in range(N_HEADS - 2, -1, -1):
                    o = jnp.where(hmask[h], o4[h * Q_BLOCK:(h + 1) * Q_BLOCK, :], o)
                m_b, l_b = per_head(ms), per_head(ls)
                at = ((), qstart, Q_BLOCK, d)
                if first:
                    _rows_store(m_scr, *at, m_b)
                    _rows_store(num_scr, *at, o)
                    _rows_store(den_scr, *at, l_b)
                else:
                    m_old = _rows_load(m_scr, *at)
                    m_new = jnp.maximum(m_old, m_b)
                    a, b = jnp.exp2(m_old - m_new), jnp.exp2(m_b - m_new)
                    num = a * _rows_load(num_scr, *at) + b * o
                    den = a * _rows_load(den_scr, *at) + b * l_b
                    if last:
                        _rows_store(o_ref, (0,), qstart, Q_BLOCK, d, num / den)
                    else:
                        _rows_store(m_scr, *at, m_new)
                        _rows_store(num_scr, *at, num)
                        _rows_store(den_scr, *at, den)
            return 0

        lax.fori_loop(0, DIL_UNITS // DIL_GROUP, group, 0)


def _dilated(cq, ck, cv):
    B, _, S, _ = cq.shape
    cur = pl.BlockSpec((1, N_HALF, DIL_SPAN, LANES), lambda b, s: (b, 0, s, 0))
    prev = pl.BlockSpec((1, N_HALF, DIL_SPAN, LANES), lambda b, s: (b, 0, jnp.maximum(s - 1, 0), 0))
    bias = jnp.asarray(_dilated_bias())
    return pl.pallas_call(
        _dilated_kernel,
        grid=(B, S // DIL_SPAN),
        in_specs=[cur, prev, cur, prev, cur, pl.BlockSpec(bias.shape, lambda b, s: (0,) * bias.ndim)],
        out_specs=cur,
        out_shape=jax.ShapeDtypeStruct((B, N_HALF, S, LANES), F32),
        scratch_shapes=[pltpu.VMEM((N_HALF, 2 * DIL_SPAN, LANES), F32),
                        pltpu.VMEM((N_HALF, 2 * DIL_SPAN, LANES), F32),
                        pltpu.VMEM((N_HALF, DIL_SPAN, LANES), F32), pltpu.VMEM((N_HALF, DIL_SPAN, LANES), F32),
                        pltpu.VMEM((N_HALF, DIL_SPAN, LANES), F32)],
        compiler_params=_params(2),
        name="dilated",
    )(cq, ck, ck, cv, cv, bias)


SWA_SPAN = 1024
SWA_GROUP = 2
SWA_HEAD_ORDER = (0, 2, 1, 3)


def _swa_bias():
    qi = np.arange(Q_BLOCK)[:, None]
    kidx = np.arange(2 * Q_BLOCK)[None, :]
    rel = qi + Q_BLOCK - kidx
    valid = (rel >= 0) & (rel < SWA_WINDOW)
    out = np.empty((2, N_HEADS, Q_BLOCK, 2 * Q_BLOCK), np.float32)
    for h in range(N_HEADS):
        bias = -SLOPE_D[h] * np.float32(LOG2E) * rel.astype(np.float32)
        out[0, h] = np.where(valid, bias, NEG)
        out[1, h] = np.where(valid & (kidx >= Q_BLOCK), bias, NEG)
    return out


def _swa_kernel(sink_ref, q_ref, k_ref, v_ref, bias_ref, o_ref):
    span = pl.program_id(1)
    units = SWA_SPAN // Q_BLOCK
    half = lax.broadcasted_iota(jnp.int32, (1, LANES), 1) // HEAD_DIM
    lo = half == 0
    kv_mask = [jnp.where(half == g, 1.0, 0.0).astype(BF16) for g in range(SWA_KV_HEADS)]

    def group(g, _):
        fronts = []
        for uu in range(SWA_GROUP):
            u = g * SWA_GROUP + uu
            n = span * units + u
            lo_start = pl.multiple_of(jnp.maximum(n - 1, 0) * Q_BLOCK, Q_BLOCK)
            hi_start = pl.multiple_of(n * Q_BLOCK, Q_BLOCK)
            qstart = pl.multiple_of(u * Q_BLOCK, Q_BLOCK)
            q = q_ref[0, pl.ds(qstart, Q_BLOCK), :]
            k2 = jnp.concatenate([k_ref[0, pl.ds(lo_start, Q_BLOCK), :], k_ref[0, pl.ds(hi_start, Q_BLOCK), :]],
                                 axis=0)
            v2 = jnp.concatenate([v_ref[0, pl.ds(lo_start, Q_BLOCK), :], v_ref[0, pl.ds(hi_start, Q_BLOCK), :]],
                                 axis=0)
            q4 = jnp.concatenate([q[:, (h % 2) * LANES:(h % 2 + 1) * LANES] * kv_mask[h // 2]
                                  for h in range(N_HEADS)], axis=0)
            s4 = lax.dot_general(q4, k2, _NT, preferred_element_type=F32)
            fronts.append((qstart, s4, v2, jnp.where(n == 0, 1, 0)))
        for qstart, s4, v2, variant in fronts:
            es, ls = [], []
            for h in range(N_HEADS):
                s = s4[h * Q_BLOCK:(h + 1) * Q_BLOCK, :] + bias_ref[variant, h]
                sink = sink_ref[h] * LOG2E
                m = jnp.maximum(jnp.max(s, axis=-1, keepdims=True), sink)
                e = jnp.exp2(s - m)
                ls.append(jnp.sum(e, axis=-1, keepdims=True) + jnp.exp2(sink - m))
                es.append(e.astype(BF16))
            o4 = jnp.dot(jnp.concatenate(es, axis=0), v2, preferred_element_type=F32)
            tiles = []
            for t in range(N_HALF):
                a, b = t, t + 2
                tiles.append(jnp.where(lo, o4[a * Q_BLOCK:(a + 1) * Q_BLOCK, :] / ls[a],
                                       o4[b * Q_BLOCK:(b + 1) * Q_BLOCK, :] / ls[b]))
            o_ref[0, pl.ds(qstart, Q_BLOCK), :] = jnp.concatenate(tiles, axis=-1)
        return 0

    lax.fori_loop(0, units // SWA_GROUP, group, 0)


def _swa(dq, dk, dv, sinks):
    B, S, _ = dq.shape
    bias = jnp.asarray(_swa_bias())
    return pl.pallas_call(
        _swa_kernel,
        grid=(B, S // SWA_SPAN),
        in_specs=[pl.BlockSpec(memory_space=pltpu.SMEM),
                  pl.BlockSpec((1, SWA_SPAN, GROUP), lambda b, i: (b, i, 0)),
                  pl.BlockSpec((1, S, SWA_KV_WIDTH), lambda b, i: (b, 0, 0)),
                  pl.BlockSpec((1, S, SWA_KV_WIDTH), lambda b, i: (b, 0, 0)),
                  pl.BlockSpec(bias.shape, lambda b, i: (0,) * bias.ndim)],
        out_specs=pl.BlockSpec((1, SWA_SPAN, GROUP), lambda b, i: (b, i, 0)),
        out_shape=jax.ShapeDtypeStruct((B, S, GROUP), F32),
        compiler_params=_params(2),
        name="swa",
    )(sinks, dq, dk, dv, bias)


def _row_norm(y, g_row):
    return y * lax.rsqrt(jnp.mean(y * y, axis=-1, keepdims=True) + EPS) * g_row


def _tail_kernel(x_ref, oat_ref, obt_ref, oc_ref, od_ref, gg_ref, wo_ref, gm_ref, wup_ref, wdn_ref, out_ref):
    gg = gg_ref[...]

    def col_norm_t(yt, g_row):
        y = (yt * lax.rsqrt(jnp.mean(yt * yt, axis=0, keepdims=True) + EPS)).T
        return y * g_row

    ga = col_norm_t(oat_ref[0], gg[0:1, :])
    gb = col_norm_t(obt_ref[0], gg[1:2, :])
    gc = _row_norm(jnp.concatenate([oc_ref[0, c] for c in range(N_HALF)], axis=-1), gg[2:3, :])
    gd = _row_norm(od_ref[0], gg[3:4, :])
    mixed = jnp.concatenate([ga, gb, gc, gd], axis=-1).astype(BF16)
    x1 = x_ref[0] + jnp.dot(mixed, wo_ref[...], preferred_element_type=F32)
    xn = _row_norm(x1, gm_ref[...]).astype(BF16)
    u = jnp.maximum(jnp.dot(xn, wup_ref[...], preferred_element_type=F32), 0.0)
    out_ref[0] = x1 + jnp.dot((u * u).astype(BF16), wdn_ref[...], preferred_element_type=F32)


def _tail(x, oat, obt, oc, od, gg, wo, gm, wup, wdn):
    B, S, _ = x.shape
    tm = TAIL_TM
    nat = lambda w: pl.BlockSpec((1, tm, w), lambda b, t: (b, t, 0))
    ft = pl.BlockSpec((1, GROUP, tm), lambda b, t: (b, 0, t))
    const = lambda a: pl.BlockSpec(a.shape, lambda b, t: (0,) * a.ndim, pipeline_mode=pl.Buffered(1))
    return pl.pallas_call(
        _tail_kernel,
        grid=(B, S // tm),
        in_specs=[nat(D_MODEL), ft, ft, pl.BlockSpec((1, N_HALF, tm, LANES), lambda b, t: (b, 0, t, 0)),
                  nat(GROUP), const(gg), const(wo), const(gm),
                  const(wup), const(wdn)],
        out_specs=nat(D_MODEL),
        out_shape=jax.ShapeDtypeStruct((B, S, D_MODEL), F32),
        compiler_params=_params(2),
        name="tail",
    )(x, oat, obt, oc, od, gg, wo, gm, wup, wdn)


def _rope_tables_t(S):
    inv = 1.0 / (ROPE_THETA ** (jnp.arange(0, MLA_ROPE, 2, dtype=F32) / MLA_ROPE))
    ang = inv[:, None] * jnp.arange(S, dtype=F32)[None, :]
    return jnp.cos(ang), jnp.sin(ang)


def _moba_aug_tables(S):
    pos = np.arange(S, dtype=np.float32) % MOBA_BLOCK
    kpos = np.zeros((AUG_ROWS, S), np.float32)
    kpos[:3] = pos
    aslope = np.zeros((N_HEADS, AUG_ROWS, 1), np.float32)
    rest = (SLOPE_A * np.float32(LOG2E)).astype(np.float32)
    for r in range(3):
        piece = rest.astype(BF16).astype(np.float32)
        aslope[:, r, 0] = piece
        rest = rest - piece
    return jnp.asarray(kpos), jnp.asarray(aslope)


def _layer(x, cos_t, sin_t, kpos_t, aslope, attn_norm_g, w_in, moba_q_g, moba_k_g, mla_qlat_g, mla_kvlat_g, mla_w_uq, mla_w_ukv,
           mla_q_g, mla_k_g, dil_q_g, dil_k_g, swa_q_g, swa_k_g, swa_sinks, group_out_g, w_o, mlp_norm_g,
           w_up, w_down):
    B, S, _ = x.shape
    col = lambda g: g.reshape(-1, 1)
    (aqt, aq32t, ak, ak32, avt, bqt, bk, bvt, cq, ck, cv, dq, dk, dv) = _inproj(
        x, attn_norm_g.reshape(1, -1), w_in.T.astype(BF16), col(moba_q_g), col(moba_k_g), col(mla_qlat_g),
        col(mla_kvlat_g), mla_w_uq.T.astype(BF16), mla_w_ukv.T.astype(BF16), col(mla_q_g), col(mla_k_g),
        col(dil_q_g), col(dil_k_g), col(swa_q_g), col(swa_k_g), cos_t, sin_t, kpos_t, aslope)
    kmean = _kmean(ak32)
    nb = S // MOBA_BLOCK
    kmean_h = kmean.reshape(B, nb, N_HEADS, HEAD_DIM).transpose(0, 2, 1, 3)
    oat = _moba(aqt, aq32t, ak, avt, kmean_h).reshape(B, GROUP, S)
    obt = _mla(bqt, bk, bvt).reshape(B, GROUP, S)
    oc = _dilated(cq, ck, cv)
    od = _swa(dq, dk, dv, swa_sinks)
    perm = np.concatenate([np.arange(HEAD_DIM) + HEAD_DIM * h for h in SWA_HEAD_ORDER])
    gg = group_out_g.at[3].set(group_out_g[3][perm])
    wo = jnp.concatenate([w_o[:3 * GROUP], w_o[3 * GROUP + perm]], axis=0)
    return _tail(x, oat, obt, oc, od, gg, wo.astype(BF16), mlp_norm_g.reshape(1, -1),
                 w_up.astype(BF16), w_down.astype(BF16))


def kernel(x, attn_norm_g, w_in, moba_q_g, moba_k_g, mla_qlat_g, mla_kvlat_g, mla_w_uq, mla_w_ukv, mla_q_g,
           mla_k_g, dil_q_g, dil_k_g, swa_q_g, swa_k_g, swa_sinks, group_out_g, w_o, mlp_norm_g, w_up, w_down):
    S = x.shape[1]
    assert S % max(d * Q_BLOCK for _, d in DILATED_BRANCHES) == 0 and S % INPROJ_TM == 0
    cos_t, sin_t = _rope_tables_t(S)
    kpos_t, aslope = _moba_aug_tables(S)
    params = (attn_norm_g, w_in, moba_q_g, moba_k_g, mla_qlat_g, mla_kvlat_g, mla_w_uq, mla_w_ukv, mla_q_g,
              mla_k_g, dil_q_g, dil_k_g, swa_q_g, swa_k_g, swa_sinks, group_out_g, w_o, mlp_norm_g, w_up, w_down)
    for l in range(attn_norm_g.shape[0]):
        x = _layer(x, cos_t, sin_t, kpos_t, aslope, *[p[l] for p in params])
    return x
```

```python
import functools

import numpy as np
import jax
import jax.numpy as jnp
from jax import lax
from jax.experimental import pallas as pl
from jax.experimental.pallas import tpu as pltpu

F32 = jnp.float32
BF16 = jnp.bfloat16

D_MODEL = 1024
HEAD_DIM = 64
N_HEADS = 4
GROUP = N_HEADS * HEAD_DIM
LANES = 128
N_HALF = GROUP // LANES
MOBA_BLOCK = 256
MOBA_TOPK = 3
MLA_Q_RANK = 256
MLA_KV_RANK = 128
MLA_NOPE = 64
MLA_ROPE = 32
MLA_QK = MLA_NOPE + MLA_ROPE
MLA_QK_PAD = 128
ROPE_THETA = 10000.0
DILATED_BRANCHES = ((128, 1), (512, 4), (2048, 16))
Q_BLOCK = 128
SWA_WINDOW = 128
SWA_KV_HEADS = 2
SWA_KV_WIDTH = SWA_KV_HEADS * HEAD_DIM
D_FF = 4 * D_MODEL
EPS = 1e-6
NEG = -1e30

_WIDTHS = (GROUP, GROUP, GROUP, MLA_Q_RANK, MLA_KV_RANK, MLA_ROPE,
           GROUP, GROUP, GROUP, GROUP, SWA_KV_WIDTH, SWA_KV_WIDTH)
_OFFS = tuple(int(v) for v in np.cumsum((0,) + _WIDTHS))
IN_COLS = _OFFS[-1]
(_A_Q, _A_K, _A_V, _B_QL, _B_KVL, _B_KR, _C_Q, _C_K, _C_V, _D_Q, _D_K, _D_V) = _OFFS[:-1]

VMEM_LIMIT = 56 * 1024 * 1024

INPROJ_TM = 512
TAIL_TM = 512
ATT_TK = 256
ATT_TQ = 2 * ATT_TK
ATT_DK = 128
ATT_DV = HEAD_DIM + 16
AUG_ROWS = 8
LOG2E = 1.4426950408889634
FLASH_PAIRS = 2

_NT = (((1,), (1,)), ((), ()))


def _alibi_slopes():
    n = 3 * N_HEADS
    idx = np.arange(1, n + 1, dtype=np.float32).reshape(N_HEADS, 3)
    s = np.exp2(-8.0 * idx / n).astype(np.float32)
    return s[:, 0], s[:, 1], s[:, 2]


SLOPE_A, SLOPE_C, SLOPE_D = _alibi_slopes()


def _params(n_axes):
    return pltpu.CompilerParams(dimension_semantics=("arbitrary",) * n_axes,
                                vmem_limit_bytes=VMEM_LIMIT)


def _head_norm_t(sec, g_col, n_heads, width):
    outs = []
    for h in range(n_heads):
        s = sec[h * width:(h + 1) * width, :]
        ms = jnp.sum(s * s, axis=0, keepdims=True) * (1.0 / width)
        outs.append(s * lax.rsqrt(ms + EPS) * g_col)
    return outs


def _inproj_kernel(x_ref, gx_ref, w1t_ref, gaq_ref, gak_ref, gql_ref, gkvl_ref, wuqt_ref, wukvt_ref,
                   gbq_ref, gbk_ref, gcq_ref, gck_ref, gdq_ref, gdk_ref, cos_ref, sin_ref, kpos_ref, aslope_ref,
                   aqt_ref, aq32t_ref, ak_ref, akm_ref, avt_ref,
                   bqt_ref, bk_ref, bvt_ref,
                   cq_ref, ck_ref, cv_ref, dq_ref, dk_ref, dv_ref,
                   h_scr):
    tm = x_ref.shape[1]
    x = x_ref[0]
    ms = jnp.mean(x * x, axis=-1, keepdims=True)
    xn = (x * lax.rsqrt(ms + EPS) * gx_ref[...]).astype(BF16)
    h_scr[...] = lax.dot_general(w1t_ref[...], xn, _NT, preferred_element_type=F32)

    scale = HEAD_DIM ** -0.5

    ones_rows = jnp.ones((ATT_DV - HEAD_DIM, ATT_TK), F32)

    def store_vt(ref, h, vh):
        for c in range(tm // ATT_TK):
            ref[0, h, c] = jnp.concatenate([vh[:, c * ATT_TK:(c + 1) * ATT_TK], ones_rows], axis=0).astype(BF16)

    pad_rows = jnp.zeros((ATT_DK - HEAD_DIM - AUG_ROWS, tm), F32)
    qa = _head_norm_t(h_scr[_A_Q:_A_Q + GROUP, :], gaq_ref[...], N_HEADS, HEAD_DIM)
    ka = _head_norm_t(h_scr[_A_K:_A_K + GROUP, :], gak_ref[...], N_HEADS, HEAD_DIM)
    ka_nat = jnp.concatenate(ka, axis=0).T
    for c in range(tm // MOBA_BLOCK):
        akm_ref[0, c] = jnp.sum(ka_nat[c * MOBA_BLOCK:(c + 1) * MOBA_BLOCK, :], axis=0,
                                keepdims=True) * (1.0 / MOBA_BLOCK)
    kpos = kpos_ref[...]
    for h in range(N_HEADS):
        aq32t_ref[0, h] = qa[h]
        slope_rows = jnp.broadcast_to(aslope_ref[h], (AUG_ROWS, tm))
        aqt_ref[0, h] = jnp.concatenate([qa[h] * (scale * LOG2E), slope_rows, pad_rows], axis=0).astype(BF16)
        ak_ref[0, h] = jnp.concatenate([ka[h], kpos, pad_rows], axis=0).T.astype(BF16)
        store_vt(avt_ref, h, h_scr[_A_V + h * HEAD_DIM:_A_V + (h + 1) * HEAD_DIM, :])

    cos = cos_ref[...]
    sin = sin_ref[...]
    half = MLA_ROPE // 2

    def rope_pad(t, sc):
        x1 = t[MLA_NOPE:MLA_NOPE + half, :]
        x2 = t[MLA_NOPE + half:MLA_QK, :]
        return jnp.concatenate([t[:MLA_NOPE, :] * sc, (x1 * cos - x2 * sin) * sc, (x1 * sin + x2 * cos) * sc,
                                jnp.zeros((MLA_QK_PAD - MLA_QK, tm), F32)], axis=0)

    ql = h_scr[_B_QL:_B_QL + MLA_Q_RANK, :]
    ql = ql * lax.rsqrt(jnp.sum(ql * ql, axis=0, keepdims=True) * (1.0 / MLA_Q_RANK) + EPS) * gql_ref[...]
    qb = jnp.dot(wuqt_ref[...], ql.astype(BF16), preferred_element_type=F32)
    qb = _head_norm_t(qb, gbq_ref[...], N_HEADS, MLA_QK)
    kvl = h_scr[_B_KVL:_B_KVL + MLA_KV_RANK, :]
    kvl = kvl * lax.rsqrt(jnp.sum(kvl * kvl, axis=0, keepdims=True) * (1.0 / MLA_KV_RANK) + EPS) * gkvl_ref[...]
    kvb = jnp.dot(wukvt_ref[...], kvl.astype(BF16), preferred_element_type=F32)
    kr = h_scr[_B_KR:_B_KR + MLA_ROPE, :]
    gbk = gbk_ref[...]
    for h in range(N_HEADS):
        bqt_ref[0, h] = rope_pad(qb[h], MLA_QK ** -0.5 * LOG2E).astype(BF16)
        kh = jnp.concatenate([kvb[h * 2 * HEAD_DIM:h * 2 * HEAD_DIM + MLA_NOPE, :], kr], axis=0)
        kh = kh * lax.rsqrt(jnp.sum(kh * kh, axis=0, keepdims=True) * (1.0 / MLA_QK) + EPS) * gbk
        bk_ref[0, h] = rope_pad(kh, 1.0).T.astype(BF16)
        store_vt(bvt_ref, h, kvb[h * 2 * HEAD_DIM + MLA_NOPE:(h + 1) * 2 * HEAD_DIM, :])

    qc = jnp.concatenate(_head_norm_t(h_scr[_C_Q:_C_Q + GROUP, :], gcq_ref[...], N_HEADS, HEAD_DIM), axis=0)
    kc = jnp.concatenate(_head_norm_t(h_scr[_C_K:_C_K + GROUP, :], gck_ref[...], N_HEADS, HEAD_DIM), axis=0)
    for ref, val in ((cq_ref, qc * (scale * LOG2E)), (ck_ref, kc), (cv_ref, h_scr[_C_V:_C_V + GROUP, :])):
        for c in range(N_HALF):
            ref[0, c] = val[c * LANES:(c + 1) * LANES, :].T

    qd = _head_norm_t(h_scr[_D_Q:_D_Q + GROUP, :], gdq_ref[...], N_HEADS, HEAD_DIM)
    qd = jnp.concatenate([qd[h] for h in SWA_HEAD_ORDER], axis=0)
    dq_ref[0] = (qd * (scale * LOG2E)).T.astype(BF16)
    kd = jnp.concatenate(_head_norm_t(h_scr[_D_K:_D_K + SWA_KV_WIDTH, :], gdk_ref[...], SWA_KV_HEADS, HEAD_DIM),
                         axis=0)
    dk_ref[0] = kd.T.astype(BF16)
    dv_ref[0] = h_scr[_D_V:_D_V + SWA_KV_WIDTH, :].T.astype(BF16)


def _inproj(x, gx, w1t, gaq, gak, gql, gkvl, wuqt, wukvt, gbq, gbk, gcq, gck, gdq, gdk, cos_t, sin_t, kpos_t,
            aslope):
    B, S, _ = x.shape
    tm = INPROJ_TM
    nb = S // ATT_TK
    cpt = tm // ATT_TK
    H = N_HEADS
    full = lambda a: pl.BlockSpec(a.shape, lambda b, t: (0,) * a.ndim)
    in_specs = [pl.BlockSpec((1, tm, D_MODEL), lambda b, t: (b, t, 0)), full(gx), full(w1t), full(gaq), full(gak),
                full(gql), full(gkvl), full(wuqt), full(wukvt), full(gbq), full(gbk), full(gcq), full(gck),
                full(gdq), full(gdk),
                pl.BlockSpec((MLA_ROPE // 2, tm), lambda b, t: (0, t)),
                pl.BlockSpec((MLA_ROPE // 2, tm), lambda b, t: (0, t)),
                pl.BlockSpec((AUG_ROWS, tm), lambda b, t: (0, t)), full(aslope)]
    head_t = lambda w: pl.BlockSpec((1, H, w, tm), lambda b, t: (b, 0, 0, t))
    head_n = pl.BlockSpec((1, H, tm, ATT_DK), lambda b, t: (b, 0, t, 0))
    vt_spec = pl.BlockSpec((1, H, cpt, ATT_DV, ATT_TK), lambda b, t: (b, 0, t, 0, 0))
    nat = lambda w: pl.BlockSpec((1, tm, w), lambda b, t: (b, t, 0))
    halves = pl.BlockSpec((1, N_HALF, tm, LANES), lambda b, t: (b, 0, t, 0))
    out_shape = [
        jax.ShapeDtypeStruct((B, H, ATT_DK, S), BF16),
        jax.ShapeDtypeStruct((B, H, HEAD_DIM, S), F32),
        jax.ShapeDtypeStruct((B, H, S, ATT_DK), BF16),
        jax.ShapeDtypeStruct((B, nb, 1, GROUP), F32),
        jax.ShapeDtypeStruct((B, H, nb, ATT_DV, ATT_TK), BF16),
        jax.ShapeDtypeStruct((B, H, ATT_DK, S), BF16),
        jax.ShapeDtypeStruct((B, H, S, ATT_DK), BF16),
        jax.ShapeDtypeStruct((B, H, nb, ATT_DV, ATT_TK), BF16),
        jax.ShapeDtypeStruct((B, N_HALF, S, LANES), F32),
        jax.ShapeDtypeStruct((B, N_HALF, S, LANES), F32),
        jax.ShapeDtypeStruct((B, N_HALF, S, LANES), F32),
        jax.ShapeDtypeStruct((B, S, GROUP), BF16),
        jax.ShapeDtypeStruct((B, S, SWA_KV_WIDTH), BF16),
        jax.ShapeDtypeStruct((B, S, SWA_KV_WIDTH), BF16),
    ]
    out_specs = [head_t(ATT_DK), head_t(HEAD_DIM), head_n,
                 pl.BlockSpec((1, tm // MOBA_BLOCK, 1, GROUP), lambda b, t: (b, t, 0, 0)), vt_spec,
                 head_t(ATT_DK), head_n, vt_spec,
                 halves, halves, halves, nat(GROUP), nat(SWA_KV_WIDTH), nat(SWA_KV_WIDTH)]
    return pl.pallas_call(
        _inproj_kernel,
        grid=(B, S // tm),
        in_specs=in_specs,
        out_specs=out_specs,
        out_shape=out_shape,
        scratch_shapes=[pltpu.VMEM((IN_COLS, tm), F32)],
        compiler_params=_params(2),
        name="inproj",
    )(x, gx, w1t, gaq, gak, gql, gkvl, wuqt, wukvt, gbq, gbk, gcq, gck, gdq, gdk, cos_t, sin_t, kpos_t, aslope)


def _flash_scratch(tq):
    s_buf, p_buf = pltpu.VMEM((N_HEADS, ATT_TK, tq), F32), pltpu.VMEM((N_HEADS, ATT_TK, tq), BF16)
    return [s_buf, s_buf, p_buf, p_buf, pltpu.VMEM((N_HEADS, ATT_DV, tq), F32)]


def _flash_heads(qt_ref, k_ref, vt_ref, o_ref, s_bufs, p_bufs, acc_scr, rowb_scr, i):
    tq = o_ref.shape[3]
    n_tail = tq // ATT_TK
    assert n_tail == 2
    n_past = i * n_tail
    heads = range(N_HEADS)

    def block_of(pos):
        return jnp.where(pos < n_tail, n_past + pos, pos - n_tail)

    def head_scores(h, slot, blk):
        kb = k_ref[0, h, pl.ds(pl.multiple_of(blk * ATT_TK, ATT_TK), ATT_TK), :]
        s_bufs[slot][h] = jnp.dot(kb, qt_ref[0, h], preferred_element_type=F32)

    def stage_scores(slot, blk):
        for h in heads:
            head_scores(h, slot, blk)

    def stage_softmax(slot, blk, ms, causal=None):
        new_ms, alphas = [], []
        for h in heads:
            m_parts, a_parts = [], []
            rv_row = None if rowb_scr is None else rowb_scr[h, pl.ds(blk, 1), :]
            for c in range(tq // LANES):
                cols = slice(c * LANES, (c + 1) * LANES)
                st = s_bufs[slot][h, :, cols]
                if causal is not None:
                    st = jnp.where(causal[:, cols], st, 2 * NEG)
                cm = jnp.max(st, axis=0, keepdims=True)
                m_old = ms[h][:, cols]
                if rowb_scr is None:
                    m_new = jnp.maximum(m_old, cm)
                    shift = m_new
                else:
                    rv = rv_row[:, cols]
                    m_new = jnp.maximum(m_old, cm + rv)
                    shift = m_new - rv
                a_parts.append(jnp.exp2(m_old - m_new))
                m_parts.append(m_new)
                p_bufs[slot][h, :, cols] = jnp.exp2(st - shift).astype(BF16)
            new_ms.append(jnp.concatenate(m_parts, axis=1))
            alphas.append(jnp.concatenate(a_parts, axis=1))
        return tuple(new_ms), tuple(alphas)

    def head_values(h, slot, blk, alphas):
        acc_scr[h] = alphas[h] * acc_scr[h] + jnp.dot(vt_ref[0, h, blk], p_bufs[slot][h],
                                                      preferred_element_type=F32)

    def stage_values(slot, blk, alphas):
        for h in heads:
            head_values(h, slot, blk, alphas)

    krow = lax.broadcasted_iota(jnp.int32, (ATT_TK, tq), 0)
    ti = lax.broadcasted_iota(jnp.int32, (ATT_TK, tq), 1)
    for h in heads:
        acc_scr[h] = jnp.zeros((ATT_DV, tq), F32)
    stage_scores(0, n_past)
    stage_scores(1, n_past + 1)
    ms = tuple(jnp.full((1, tq), NEG, F32) for _ in heads)
    ms, alphas = stage_softmax(0, n_past, ms, causal=ti >= krow)
    stage_values(0, n_past, alphas)
    ms, alphas = stage_softmax(1, n_past + 1, ms, causal=ti >= krow + ATT_TK)
    stage_scores(0, 0)

    def step(pos, slot, ms, alphas):
        ms, new_alphas = stage_softmax(1 - slot, pos + 1 - n_tail, ms)
        blk = block_of(pos)
        for h in heads:
            head_values(h, slot, blk, alphas)
            head_scores(h, slot, pos)
        return ms, new_alphas

    def pair(first, carry):
        ms, alphas = step(first, 1, *carry)
        return step(first + 1, 0, ms, alphas)

    def trip(t, carry):
        for u in range(FLASH_PAIRS):
            carry = pair(2 * (FLASH_PAIRS * t + u) + 1, carry)
        return carry

    carry = lax.fori_loop(0, i // FLASH_PAIRS, trip, (ms, alphas))
    ms, alphas = lax.fori_loop(i - i % FLASH_PAIRS, i, lambda u, c: pair(2 * u + 1, c), carry)
    last = n_past + 1
    stage_values(1, block_of(last), alphas)
    for h in heads:
        acc = acc_scr[h]
        o_ref[0, h] = acc[:HEAD_DIM, :] / acc[HEAD_DIM:HEAD_DIM + 1, :]


def _moba_kernel(qt_ref, q32t_ref, k_ref, vt_ref, kmean_ref, o_ref, rowb_scr, s0, s1, p0, p1, acc_scr):
    i = pl.program_id(1)
    nb = kmean_ref.shape[2]
    tq = qt_ref.shape[3]
    blk = lax.broadcasted_iota(jnp.int32, (nb, tq), 0)
    col = lax.broadcasted_iota(jnp.int32, (nb, tq), 1)
    qblk = i * (tq // MOBA_BLOCK) + col // MOBA_BLOCK
    past = blk < qblk
    dist0 = (i * tq + col - blk * MOBA_BLOCK).astype(F32)
    for h in range(N_HEADS):
        gate = jnp.dot(kmean_ref[0, h], q32t_ref[0, h], preferred_element_type=F32,
                       precision=lax.Precision.HIGHEST)
        gate = jnp.where(past, gate, NEG)
        sel = blk == qblk
        for _ in range(MOBA_TOPK):
            best = jnp.max(gate, axis=0, keepdims=True)
            first = jnp.min(jnp.where(gate == best, blk, nb), axis=0, keepdims=True)
            pick = blk == first
            sel = jnp.logical_or(sel, jnp.logical_and(pick, past))
            gate = jnp.where(pick, -jnp.inf, gate)
        rowb_scr[h] = jnp.where(sel, (-float(SLOPE_A[h]) * LOG2E) * dist0, 2 * NEG)
    _flash_heads(qt_ref, k_ref, vt_ref, o_ref, (s0, s1), (p0, p1), acc_scr, rowb_scr, i)


def _flash_specs(S, tq):
    H, nb = N_HEADS, S // ATT_TK
    return ([pl.BlockSpec((1, H, ATT_DK, tq), lambda b, i: (b, 0, 0, i)),
             pl.BlockSpec((1, H, S, ATT_DK), lambda b, i: (b, 0, 0, 0)),
             pl.BlockSpec((1, H, nb, ATT_DV, ATT_TK), lambda b, i: (b, 0, 0, 0, 0))],
            pl.BlockSpec((1, H, HEAD_DIM, tq), lambda b, i: (b, 0, 0, i)))


def _moba(aqt, aq32t, ak, avt, kmean_h):
    B, H, _, S = aqt.shape
    tq = ATT_TQ
    nb = S // ATT_TK
    (q_spec, k_spec, vt_spec), o_spec = _flash_specs(S, tq)
    return pl.pallas_call(
        _moba_kernel,
        grid=(B, S // tq),
        in_specs=[q_spec, pl.BlockSpec((1, H, HEAD_DIM, tq), lambda b, i: (b, 0, 0, i)), k_spec, vt_spec,
                  pl.BlockSpec((1, H, nb, HEAD_DIM), lambda b, i: (b, 0, 0, 0))],
        out_specs=o_spec,
        out_shape=jax.ShapeDtypeStruct((B, H, HEAD_DIM, S), F32),
        scratch_shapes=[pltpu.VMEM((H, nb, tq), F32)] + _flash_scratch(tq),
        compiler_params=_params(2),
        name="moba",
    )(aqt, aq32t, ak, avt, kmean_h)


def _mla_kernel(qt_ref, k_ref, vt_ref, o_ref, s0, s1, p0, p1, acc_scr):
    _flash_heads(qt_ref, k_ref, vt_ref, o_ref, (s0, s1), (p0, p1), acc_scr, None, pl.program_id(1))


def _mla(bqt, bk, bvt):
    B, H, _, S = bqt.shape
    tq = ATT_TQ
    in_specs, o_spec = _flash_specs(S, tq)
    return pl.pallas_call(
        _mla_kernel,
        grid=(B, S // tq),
        in_specs=in_specs,
        out_specs=o_spec,
        out_shape=jax.ShapeDtypeStruct((B, H, HEAD_DIM, S), F32),
        scratch_shapes=_flash_scratch(tq),
        compiler_params=_params(2),
        name="mla",
    )(bqt, bk, bvt)


DIL_SPAN = max(d for _, d in DILATED_BRANCHES) * Q_BLOCK
DIL_UNITS = DIL_SPAN // Q_BLOCK
DIL_GROUP = 2


def _dilated_bias():
    qi = np.arange(Q_BLOCK)[:, None]
    kidx = np.arange(2 * Q_BLOCK)[None, :]
    rel = qi + Q_BLOCK - kidx
    out = np.empty((2, len(DILATED_BRANCHES), N_HEADS, Q_BLOCK, 2 * Q_BLOCK), np.float32)
    for bi, (window, d) in enumerate(DILATED_BRANCHES):
        valid = (rel >= 0) & (rel <= window // d)
        for h in range(N_HEADS):
            bias = -SLOPE_C[h] * np.float32(LOG2E) * (d * rel).astype(np.float32)
            out[0, bi, h] = np.where(valid, bias, NEG)
            out[1, bi, h] = np.where(valid & (kidx >= Q_BLOCK), bias, NEG)
    return out


def _rows_load(ref, lead, start, size, stride):
    return jnp.concatenate([ref[lead + (c, pl.ds(start, size, stride=stride), slice(None))]
                            for c in range(N_HALF)], axis=-1)


def _rows_store(ref, lead, start, size, stride, val):
    for c in range(N_HALF):
        ref[lead + (c, pl.ds(start, size, stride=stride), slice(None))] = val[:, c * LANES:(c + 1) * LANES]


def _dilated_kernel(q_ref, kp_ref, kc_ref, vp_ref, vc_ref, bias_ref, o_ref, kbuf, vbuf, m_scr, den_scr, num_scr):
    span = pl.program_id(1)
    kbuf[:, 0:DIL_SPAN, :] = kp_ref[0]
    kbuf[:, DIL_SPAN:, :] = kc_ref[0]
    vbuf[:, 0:DIL_SPAN, :] = vp_ref[0]
    vbuf[:, DIL_SPAN:, :] = vc_ref[0]
    lane_head = lax.broadcasted_iota(jnp.int32, (1, GROUP), 1) // HEAD_DIM
    hmask = [lane_head == h for h in range(N_HEADS)]
    hmask_f = [m.astype(F32) for m in hmask]
    first_span = jnp.where(span == 0, 1, 0)

    def per_head(cols):
        out = cols[N_HEADS - 1]
        for h in range(N_HEADS - 2, -1, -1):
            out = jnp.where(hmask[h], cols[h], out)
        return out

    order = sorted(range(len(DILATED_BRANCHES)), key=lambda b: -DILATED_BRANCHES[b][1])
    for bi in order:
        d = DILATED_BRANCHES[bi][1]
        first, last = bi == order[0], bi == order[-1]

        def group(g, _, bi=bi, d=d, first=first, last=last):
            fronts = []
            for uu in range(DIL_GROUP):
                u = g * DIL_GROUP + uu
                r, n = u % d, u // d
                qstart = n * (Q_BLOCK * d) + r
                kstart = DIL_SPAN + qstart - Q_BLOCK * d
                q = _rows_load(q_ref, (0,), qstart, Q_BLOCK, d)
                k2 = _rows_load(kbuf, (), kstart, 2 * Q_BLOCK, d).astype(BF16)
                v2 = _rows_load(vbuf, (), kstart, 2 * Q_BLOCK, d).astype(BF16)
                q4 = jnp.concatenate([(q * hmask_f[h]).astype(BF16) for h in range(N_HEADS)], axis=0)
                s4 = lax.dot_general(q4, k2, _NT, preferred_element_type=F32)
                variant = jnp.where(n == 0, first_span, 0)
                fronts.append((qstart, s4, v2, variant))
            for qstart, s4, v2, variant in fronts:
                es, ms, ls = [], [], []
                for h in range(N_HEADS):
                    s = s4[h * Q_BLOCK:(h + 1) * Q_BLOCK, :] + bias_ref[variant, bi, h]
                    m = jnp.max(s, axis=-1, keepdims=True)
                    e = jnp.exp2(s - m)
                    ls.append(jnp.sum(e, axis=-1, keepdims=True))
                    ms.append(m)
                    es.append(e.astype(BF16))
                o4 = jnp.dot(jnp.concatenate(es, axis=0), v2, preferred_element_type=F32)
                o = o4[(N_HEADS - 1) * Q_BLOCK:, :]
                for h in range(N_HEADS - 2, -1, -1):
                    o = jnp.where(hmask[h], o4[h * Q_BLOCK:(h + 1) * Q_BLOCK, :], o)
                m_b, l_b = per_head(ms), per_head(ls)
                at = ((), qstart, Q_BLOCK, d)
                if first:
                    _rows_store(m_scr, *at, m_b)
                    _rows_store(num_scr, *at, o)
                    _rows_store(den_scr, *at, l_b)
                else:
                    m_old = _rows_load(m_scr, *at)
                    m_new = jnp.maximum(m_old, m_b)
                    a, b = jnp.exp2(m_old - m_new), jnp.exp2(m_b - m_new)
                    num = a * _rows_load(num_scr, *at) + b * o
                    den = a * _rows_load(den_scr, *at) + b * l_b
                    if last:
                        _rows_store(o_ref, (0,), qstart, Q_BLOCK, d, num / den)
                    else:
                        _rows_store(m_scr, *at, m_new)
                        _rows_store(num_scr, *at, num)
                        _rows_store(den_scr, *at, den)
            return 0

        lax.fori_loop(0, DIL_UNITS // DIL_GROUP, group, 0)


def _dilated(cq, ck, cv):
    B, _, S, _ = cq.shape
    cur = pl.BlockSpec((1, N_HALF, DIL_SPAN, LANES), lambda b, s: (b, 0, s, 0))
    prev = pl.BlockSpec((1, N_HALF, DIL_SPAN, LANES), lambda b, s: (b, 0, jnp.maximum(s - 1, 0), 0))
    bias = jnp.asarray(_dilated_bias())
    return pl.pallas_call(
        _dilated_kernel,
        grid=(B, S // DIL_SPAN),
        in_specs=[cur, prev, cur, prev, cur, pl.BlockSpec(bias.shape, lambda b, s: (0,) * bias.ndim)],
        out_specs=cur,
        out_shape=jax.ShapeDtypeStruct((B, N_HALF, S, LANES), F32),
        scratch_shapes=[pltpu.VMEM((N_HALF, 2 * DIL_SPAN, LANES), F32),
                        pltpu.VMEM((N_HALF, 2 * DIL_SPAN, LANES), F32),
                        pltpu.VMEM((N_HALF, DIL_SPAN, LANES), F32), pltpu.VMEM((N_HALF, DIL_SPAN, LANES), F32),
                        pltpu.VMEM((N_HALF, DIL_SPAN, LANES), F32)],
        compiler_params=_params(2),
        name="dilated",
    )(cq, ck, ck, cv, cv, bias)


SWA_SPAN = 1024
SWA_GROUP = 2
SWA_HEAD_ORDER = (0, 2, 1, 3)


def _swa_bias():
    qi = np.arange(Q_BLOCK)[:, None]
    kidx = np.arange(2 * Q_BLOCK)[None, :]
    rel = qi + Q_BLOCK - kidx
    valid = (rel >= 0) & (rel < SWA_WINDOW)
    out = np.empty((2, N_HEADS, Q_BLOCK, 2 * Q_BLOCK), np.float32)
    for h in range(N_HEADS):
        bias = -SLOPE_D[h] * np.float32(LOG2E) * rel.astype(np.float32)
        out[0, h] = np.where(valid, bias, NEG)
        out[1, h] = np.where(valid & (kidx >= Q_BLOCK), bias, NEG)
    return out


def _swa_kernel(sink_ref, q_ref, k_ref, v_ref, bias_ref, o_ref):
    span = pl.program_id(1)
    units = SWA_SPAN // Q_BLOCK
    half = lax.broadcasted_iota(jnp.int32, (1, LANES), 1) // HEAD_DIM
    lo = half == 0
    kv_mask = [jnp.where(half == g, 1.0, 0.0).astype(BF16) for g in range(SWA_KV_HEADS)]

    def group(g, _):
        fronts = []
        for uu in range(SWA_GROUP):
            u = g * SWA_GROUP + uu
            n = span * units + u
            lo_start = pl.multiple_of(jnp.maximum(n - 1, 0) * Q_BLOCK, Q_BLOCK)
            hi_start = pl.multiple_of(n * Q_BLOCK, Q_BLOCK)
            qstart = pl.multiple_of(u * Q_BLOCK, Q_BLOCK)
            q = q_ref[0, pl.ds(qstart, Q_BLOCK), :]
            k2 = jnp.concatenate([k_ref[0, pl.ds(lo_start, Q_BLOCK), :], k_ref[0, pl.ds(hi_start, Q_BLOCK), :]],
                                 axis=0)
            v2 = jnp.concatenate([v_ref[0, pl.ds(lo_start, Q_BLOCK), :], v_ref[0, pl.ds(hi_start, Q_BLOCK), :]],
                                 axis=0)
            q4 = jnp.concatenate([q[:, (h % 2) * LANES:(h % 2 + 1) * LANES] * kv_mask[h // 2]
                                  for h in range(N_HEADS)], axis=0)
            s4 = lax.dot_general(q4, k2, _NT, preferred_element_type=F32)
            fronts.append((qstart, s4, v2, jnp.where(n == 0, 1, 0)))
        for qstart, s4, v2, variant in fronts:
            es, ls = [], []
            for h in range(N_HEADS):
                s = s4[h * Q_BLOCK:(h + 1) * Q_BLOCK, :] + bias_ref[variant, h]
                sink = sink_ref[h] * LOG2E
                m = jnp.maximum(jnp.max(s, axis=-1, keepdims=True), sink)
                e = jnp.exp2(s - m)
                ls.append(jnp.sum(e, axis=-1, keepdims=True) + jnp.exp2(sink - m))
                es.append(e.astype(BF16))
            o4 = jnp.dot(jnp.concatenate(es, axis=0), v2, preferred_element_type=F32)
            tiles = []
            for t in range(N_HALF):
                a, b = t, t + 2
                tiles.append(jnp.where(lo, o4[a * Q_BLOCK:(a + 1) * Q_BLOCK, :] / ls[a],
                                       o4[b * Q_BLOCK:(b + 1) * Q_BLOCK, :] / ls[b]))
            o_ref[0, pl.ds(qstart, Q_BLOCK), :] = jnp.concatenate(tiles, axis=-1)
        return 0

    lax.fori_loop(0, units // SWA_GROUP, group, 0)


def _swa(dq, dk, dv, sinks):
    B, S, _ = dq.shape
    bias = jnp.asarray(_swa_bias())
    return pl.pallas_call(
        _swa_kernel,
        grid=(B, S // SWA_SPAN),
        in_specs=[pl.BlockSpec(memory_space=pltpu.SMEM),
                  pl.BlockSpec((1, SWA_SPAN, GROUP), lambda b, i: (b, i, 0)),
                  pl.BlockSpec((1, S, SWA_KV_WIDTH), lambda b, i: (b, 0, 0)),
                  pl.BlockSpec((1, S, SWA_KV_WIDTH), lambda b, i: (b, 0, 0)),
                  pl.BlockSpec(bias.shape, lambda b, i: (0,) * bias.ndim)],
        out_specs=pl.BlockSpec((1, SWA_SPAN, GROUP), lambda b, i: (b, i, 0)),
        out_shape=jax.ShapeDtypeStruct((B, S, GROUP), F32),
        compiler_params=_params(2),
        name="swa",
    )(sinks, dq, dk, dv, bias)


def _row_norm(y, g_row):
    return y * lax.rsqrt(jnp.mean(y * y, axis=-1, keepdims=True) + EPS) * g_row


def _tail_kernel(x_ref, oat_ref, obt_ref, oc_ref, od_ref, gg_ref, wo_ref, gm_ref, wup_ref, wdn_ref, out_ref):
    gg = gg_ref[...]

    def col_norm_t(yt, g_row):
        y = (yt * lax.rsqrt(jnp.mean(yt * yt, axis=0, keepdims=True) + EPS)).T
        return y * g_row

    ga = col_norm_t(oat_ref[0], gg[0:1, :])
    gb = col_norm_t(obt_ref[0], gg[1:2, :])
    gc = _row_norm(jnp.concatenate([oc_ref[0, c] for c in range(N_HALF)], axis=-1), gg[2:3, :])
    gd = _row_norm(od_ref[0], gg[3:4, :])
    mixed = jnp.concatenate([ga, gb, gc, gd], axis=-1).astype(BF16)
    x1 = x_ref[0] + jnp.dot(mixed, wo_ref[...], preferred_element_type=F32)
    xn = _row_norm(x1, gm_ref[...]).astype(BF16)
    u = jnp.maximum(jnp.dot(xn, wup_ref[...], preferred_element_type=F32), 0.0)
    out_ref[0] = x1 + jnp.dot((u * u).astype(BF16), wdn_ref[...], preferred_element_type=F32)


def _tail(x, oat, obt, oc, od, gg, wo, gm, wup, wdn):
    B, S, _ = x.shape
    tm = TAIL_TM
    nat = lambda w: pl.BlockSpec((1, tm, w), lambda b, t: (b, t, 0))
    ft = pl.BlockSpec((1, GROUP, tm), lambda b, t: (b, 0, t))
    const = lambda a: pl.BlockSpec(a.shape, lambda b, t: (0,) * a.ndim, pipeline_mode=pl.Buffered(1))
    return pl.pallas_call(
        _tail_kernel,
        grid=(B, S // tm),
        in_specs=[nat(D_MODEL), ft, ft, pl.BlockSpec((1, N_HALF, tm, LANES), lambda b, t: (b, 0, t, 0)),
                  nat(GROUP), const(gg), const(wo), const(gm),
                  const(wup), const(wdn)],
        out_specs=nat(D_MODEL),
        out_shape=jax.ShapeDtypeStruct((B, S, D_MODEL), F32),
        compiler_params=_params(2),
        name="tail",
    )(x, oat, obt, oc, od, gg, wo, gm, wup, wdn)


def _rope_tables_t(S):
    inv = 1.0 / (ROPE_THETA ** (jnp.arange(0, MLA_ROPE, 2, dtype=F32) / MLA_ROPE))
    ang = inv[:, None] * jnp.arange(S, dtype=F32)[None, :]
    return jnp.cos(ang), jnp.sin(ang)


def _moba_aug_tables(S):
    pos = np.arange(S, dtype=np.float32) % MOBA_BLOCK
    kpos = np.zeros((AUG_ROWS, S), np.float32)
    kpos[:3] = pos
    aslope = np.zeros((N_HEADS, AUG_ROWS, 1), np.float32)
    rest = (SLOPE_A * np.float32(LOG2E)).astype(np.float32)
    for r in range(3):
        piece = rest.astype(BF16).astype(np.float32)
        aslope[:, r, 0] = piece
        rest = rest - piece
    return jnp.asarray(kpos), jnp.asarray(aslope)


def _layer(x, cos_t, sin_t, kpos_t, aslope, attn_norm_g, w_in, moba_q_g, moba_k_g, mla_qlat_g, mla_kvlat_g, mla_w_uq, mla_w_ukv,
           mla_q_g, mla_k_g, dil_q_g, dil_k_g, swa_q_g, swa_k_g, swa_sinks, group_out_g, w_o, mlp_norm_g,
           w_up, w_down):
    B, S, _ = x.shape
    col = lambda g: g.reshape(-1, 1)
    (aqt, aq32t, ak, akm, avt, bqt, bk, bvt, cq, ck, cv, dq, dk, dv) = _inproj(
        x, attn_norm_g.reshape(1, -1), w_in.T.astype(BF16), col(moba_q_g), col(moba_k_g), col(mla_qlat_g),
        col(mla_kvlat_g), mla_w_uq.T.astype(BF16), mla_w_ukv.T.astype(BF16), col(mla_q_g), col(mla_k_g),
        col(dil_q_g), col(dil_k_g), col(swa_q_g), col(swa_k_g), cos_t, sin_t, kpos_t, aslope)
    nb = S // MOBA_BLOCK
    kmean_h = akm.reshape(B, nb, N_HEADS, HEAD_DIM).transpose(0, 2, 1, 3)
    oat = _moba(aqt, aq32t, ak, avt, kmean_h).reshape(B, GROUP, S)
    obt = _mla(bqt, bk, bvt).reshape(B, GROUP, S)
    oc = _dilated(cq, ck, cv)
    od = _swa(dq, dk, dv, swa_sinks)
    perm = np.concatenate([np.arange(HEAD_DIM) + HEAD_DIM * h for h in SWA_HEAD_ORDER])
    gg = group_out_g.at[3].set(group_out_g[3][perm])
    wo = jnp.concatenate([w_o[:3 * GROUP], w_o[3 * GROUP + perm]], axis=0)
    return _tail(x, oat, obt, oc, od, gg, wo.astype(BF16), mlp_norm_g.reshape(1, -1),
                 w_up.astype(BF16), w_down.astype(BF16))


def kernel(x, attn_norm_g, w_in, moba_q_g, moba_k_g, mla_qlat_g, mla_kvlat_g, mla_w_uq, mla_w_ukv, mla_q_g,
           mla_k_g, dil_q_g, dil_k_g, swa_q_g, swa_k_g, swa_sinks, group_out_g, w_o, mlp_norm_g, w_up, w_down):
    S = x.shape[1]
    assert S % max(d * Q_BLOCK for _, d in DILATED_BRANCHES) == 0 and S % INPROJ_TM == 0
    cos_t, sin_t = _rope_tables_t(S)
    kpos_t, aslope = _moba_aug_tables(S)
    params = (attn_norm_g, w_in, moba_q_g, moba_k_g, mla_qlat_g, mla_kvlat_g, mla_w_uq, mla_w_ukv, mla_q_g,
              mla_k_g, dil_q_g, dil_k_g, swa_q_g, swa_k_g, swa_sinks, group_out_g, w_o, mlp_norm_g, w_up, w_down)
    for l in range(attn_norm_g.shape[0]):
        x = _layer(x, cos_t, sin_t, kpos_t, aslope, *[p[l] for p in params])
    return x
```

```python
import functools

import numpy as np
import jax
import jax.numpy as jnp
from jax import lax
from jax.experimental import pallas as pl
from jax.experimental.pallas import tpu as pltpu

F32 = jnp.float32
BF16 = jnp.bfloat16

D_MODEL = 1024
HEAD_DIM = 64
N_HEADS = 4
GROUP = N_HEADS * HEAD_DIM
LANES = 128
N_HALF = GROUP // LANES
MOBA_BLOCK = 256
MOBA_TOPK = 3
MLA_Q_RANK = 256
MLA_KV_RANK = 128
MLA_NOPE = 64
MLA_ROPE = 32
MLA_QK = MLA_NOPE + MLA_ROPE
MLA_QK_PAD = 128
ROPE_THETA = 10000.0
DILATED_BRANCHES = ((128, 1), (512, 4), (2048, 16))
Q_BLOCK = 128
SWA_WINDOW = 128
SWA_KV_HEADS = 2
SWA_KV_WIDTH = SWA_KV_HEADS * HEAD_DIM
D_FF = 4 * D_MODEL
EPS = 1e-6
NEG = -1e30

_WIDTHS = (GROUP, GROUP, GROUP, MLA_Q_RANK, MLA_KV_RANK, MLA_ROPE,
           GROUP, GROUP, GROUP, GROUP, SWA_KV_WIDTH, SWA_KV_WIDTH)
_OFFS = tuple(int(v) for v in np.cumsum((0,) + _WIDTHS))
IN_COLS = _OFFS[-1]
(_A_Q, _A_K, _A_V, _B_QL, _B_KVL, _B_KR, _C_Q, _C_K, _C_V, _D_Q, _D_K, _D_V) = _OFFS[:-1]

VMEM_LIMIT = 56 * 1024 * 1024

INPROJ_TM = 512
TAIL_TM = 512
ATT_TK = 256
ATT_TQ = 2 * ATT_TK
ATT_DK = 128
ATT_DV = HEAD_DIM + 16
AUG_ROWS = 8
LOG2E = 1.4426950408889634
FLASH_PAIRS = 2

_NT = (((1,), (1,)), ((), ()))


def _alibi_slopes():
    n = 3 * N_HEADS
    idx = np.arange(1, n + 1, dtype=np.float32).reshape(N_HEADS, 3)
    s = np.exp2(-8.0 * idx / n).astype(np.float32)
    return s[:, 0], s[:, 1], s[:, 2]


SLOPE_A, SLOPE_C, SLOPE_D = _alibi_slopes()


def _params(n_axes):
    return pltpu.CompilerParams(dimension_semantics=("arbitrary",) * n_axes,
                                vmem_limit_bytes=VMEM_LIMIT)


def _head_norm_t(sec, g_col, n_heads, width):
    outs = []
    for h in range(n_heads):
        s = sec[h * width:(h + 1) * width, :]
        ms = jnp.sum(s * s, axis=0, keepdims=True) * (1.0 / width)
        outs.append(s * lax.rsqrt(ms + EPS) * g_col)
    return outs


def _inproj_kernel(x_ref, gx_ref, w1t_ref, gaq_ref, gak_ref, gql_ref, gkvl_ref, wuqt_ref, wukvt_ref,
                   gbq_ref, gbk_ref, gcq_ref, gck_ref, gdq_ref, gdk_ref, cos_ref, sin_ref, kpos_ref, aslope_ref,
                   aqt_ref, aq32t_ref, ak_ref, akm_ref, avt_ref,
                   bqt_ref, bk_ref, bvt_ref,
                   cq_ref, ck_ref, cv_ref, dq_ref, dk_ref, dv_ref,
                   h_scr):
    tm = x_ref.shape[1]
    x = x_ref[0]
    ms = jnp.mean(x * x, axis=-1, keepdims=True)
    xn = (x * lax.rsqrt(ms + EPS) * gx_ref[...]).astype(BF16)
    h_scr[...] = lax.dot_general(w1t_ref[...], xn, _NT, preferred_element_type=F32)

    scale = HEAD_DIM ** -0.5

    ones_rows = jnp.ones((ATT_DV - HEAD_DIM, ATT_TK), F32)

    def store_vt(ref, h, vh):
        for c in range(tm // ATT_TK):
            ref[0, h, c] = jnp.concatenate([vh[:, c * ATT_TK:(c + 1) * ATT_TK], ones_rows], axis=0).astype(BF16)

    pad_rows = jnp.zeros((ATT_DK - HEAD_DIM - AUG_ROWS, tm), F32)
    qa = _head_norm_t(h_scr[_A_Q:_A_Q + GROUP, :], gaq_ref[...], N_HEADS, HEAD_DIM)
    ka = _head_norm_t(h_scr[_A_K:_A_K + GROUP, :], gak_ref[...], N_HEADS, HEAD_DIM)
    ka_nat = jnp.concatenate(ka, axis=0).T
    for c in range(tm // MOBA_BLOCK):
        akm_ref[0, c] = jnp.sum(ka_nat[c * MOBA_BLOCK:(c + 1) * MOBA_BLOCK, :], axis=0,
                                keepdims=True) * (1.0 / MOBA_BLOCK)
    kpos = kpos_ref[...]
    for h in range(N_HEADS):
        aq32t_ref[0, h] = qa[h]
        slope_rows = jnp.broadcast_to(aslope_ref[h], (AUG_ROWS, tm))
        aqt_ref[0, h] = jnp.concatenate([qa[h] * (scale * LOG2E), slope_rows, pad_rows], axis=0).astype(BF16)
        ak_ref[0, h] = jnp.concatenate([ka[h], kpos, pad_rows], axis=0).T.astype(BF16)
        store_vt(avt_ref, h, h_scr[_A_V + h * HEAD_DIM:_A_V + (h + 1) * HEAD_DIM, :])

    cos = cos_ref[...]
    sin = sin_ref[...]
    half = MLA_ROPE // 2

    def rope_pad(t, sc):
        x1 = t[MLA_NOPE:MLA_NOPE + half, :]
        x2 = t[MLA_NOPE + half:MLA_QK, :]
        return jnp.concatenate([t[:MLA_NOPE, :] * sc, (x1 * cos - x2 * sin) * sc, (x1 * sin + x2 * cos) * sc,
                                jnp.zeros((MLA_QK_PAD - MLA_QK, tm), F32)], axis=0)

    ql = h_scr[_B_QL:_B_QL + MLA_Q_RANK, :]
    ql = ql * lax.rsqrt(jnp.sum(ql * ql, axis=0, keepdims=True) * (1.0 / MLA_Q_RANK) + EPS) * gql_ref[...]
    qb = jnp.dot(wuqt_ref[...], ql.astype(BF16), preferred_element_type=F32)
    qb = _head_norm_t(qb, gbq_ref[...], N_HEADS, MLA_QK)
    kvl = h_scr[_B_KVL:_B_KVL + MLA_KV_RANK, :]
    kvl = kvl * lax.rsqrt(jnp.sum(kvl * kvl, axis=0, keepdims=True) * (1.0 / MLA_KV_RANK) + EPS) * gkvl_ref[...]
    kvb = jnp.dot(wukvt_ref[...], kvl.astype(BF16), preferred_element_type=F32)
    kr = h_scr[_B_KR:_B_KR + MLA_ROPE, :]
    gbk = gbk_ref[...]
    for h in range(N_HEADS):
        bqt_ref[0, h] = rope_pad(qb[h], MLA_QK ** -0.5 * LOG2E).astype(BF16)
        kh = jnp.concatenate([kvb[h * 2 * HEAD_DIM:h * 2 * HEAD_DIM + MLA_NOPE, :], kr], axis=0)
        kh = kh * lax.rsqrt(jnp.sum(kh * kh, axis=0, keepdims=True) * (1.0 / MLA_QK) + EPS) * gbk
        bk_ref[0, h] = rope_pad(kh, 1.0).T.astype(BF16)
        store_vt(bvt_ref, h, kvb[h * 2 * HEAD_DIM + MLA_NOPE:(h + 1) * 2 * HEAD_DIM, :])

    qc = jnp.concatenate(_head_norm_t(h_scr[_C_Q:_C_Q + GROUP, :], gcq_ref[...], N_HEADS, HEAD_DIM), axis=0)
    kc = jnp.concatenate(_head_norm_t(h_scr[_C_K:_C_K + GROUP, :], gck_ref[...], N_HEADS, HEAD_DIM), axis=0)
    for ref, val in ((cq_ref, qc * (scale * LOG2E)), (ck_ref, kc), (cv_ref, h_scr[_C_V:_C_V + GROUP, :])):
        for c in range(N_HALF):
            ref[0, c] = val[c * LANES:(c + 1) * LANES, :].T

    qd = _head_norm_t(h_scr[_D_Q:_D_Q + GROUP, :], gdq_ref[...], N_HEADS, HEAD_DIM)
    qd = jnp.concatenate([qd[h] for h in SWA_HEAD_ORDER], axis=0)
    dq_ref[0] = (qd * (scale * LOG2E)).T.astype(BF16)
    kd = jnp.concatenate(_head_norm_t(h_scr[_D_K:_D_K + SWA_KV_WIDTH, :], gdk_ref[...], SWA_KV_HEADS, HEAD_DIM),
                         axis=0)
    dk_ref[0] = kd.T.astype(BF16)
    dv_ref[0] = h_scr[_D_V:_D_V + SWA_KV_WIDTH, :].T.astype(BF16)


def _inproj(x, gx, w1t, gaq, gak, gql, gkvl, wuqt, wukvt, gbq, gbk, gcq, gck, gdq, gdk, cos_t, sin_t, kpos_t,
            aslope):
    B, S, _ = x.shape
    tm = INPROJ_TM
    nb = S // ATT_TK
    cpt = tm // ATT_TK
    H = N_HEADS
    full = lambda a: pl.BlockSpec(a.shape, lambda b, t: (0,) * a.ndim)
    in_specs = [pl.BlockSpec((1, tm, D_MODEL), lambda b, t: (b, t, 0)), full(gx), full(w1t), full(gaq), full(gak),
                full(gql), full(gkvl), full(wuqt), full(wukvt), full(gbq), full(gbk), full(gcq), full(gck),
                full(gdq), full(gdk),
                pl.BlockSpec((MLA_ROPE // 2, tm), lambda b, t: (0, t)),
                pl.BlockSpec((MLA_ROPE // 2, tm), lambda b, t: (0, t)),
                pl.BlockSpec((AUG_ROWS, tm), lambda b, t: (0, t)), full(aslope)]
    head_t = lambda w: pl.BlockSpec((1, H, w, tm), lambda b, t: (b, 0, 0, t))
    head_n = pl.BlockSpec((1, H, tm, ATT_DK), lambda b, t: (b, 0, t, 0))
    vt_spec = pl.BlockSpec((1, H, cpt, ATT_DV, ATT_TK), lambda b, t: (b, 0, t, 0, 0))
    nat = lambda w: pl.BlockSpec((1, tm, w), lambda b, t: (b, t, 0))
    halves = pl.BlockSpec((1, N_HALF, tm, LANES), lambda b, t: (b, 0, t, 0))
    out_shape = [
        jax.ShapeDtypeStruct((B, H, ATT_DK, S), BF16),
        jax.ShapeDtypeStruct((B, H, HEAD_DIM, S), F32),
        jax.ShapeDtypeStruct((B, H, S, ATT_DK), BF16),
        jax.ShapeDtypeStruct((B, nb, 1, GROUP), F32),
        jax.ShapeDtypeStruct((B, H, nb, ATT_DV, ATT_TK), BF16),
        jax.ShapeDtypeStruct((B, H, ATT_DK, S), BF16),
        jax.ShapeDtypeStruct((B, H, S, ATT_DK), BF16),
        jax.ShapeDtypeStruct((B, H, nb, ATT_DV, ATT_TK), BF16),
        jax.ShapeDtypeStruct((B, N_HALF, S, LANES), F32),
        jax.ShapeDtypeStruct((B, N_HALF, S, LANES), F32),
        jax.ShapeDtypeStruct((B, N_HALF, S, LANES), F32),
        jax.ShapeDtypeStruct((B, S, GROUP), BF16),
        jax.ShapeDtypeStruct((B, S, SWA_KV_WIDTH), BF16),
        jax.ShapeDtypeStruct((B, S, SWA_KV_WIDTH), BF16),
    ]
    out_specs = [head_t(ATT_DK), head_t(HEAD_DIM), head_n,
                 pl.BlockSpec((1, tm // MOBA_BLOCK, 1, GROUP), lambda b, t: (b, t, 0, 0)), vt_spec,
                 head_t(ATT_DK), head_n, vt_spec,
                 halves, halves, halves, nat(GROUP), nat(SWA_KV_WIDTH), nat(SWA_KV_WIDTH)]
    return pl.pallas_call(
        _inproj_kernel,
        grid=(B, S // tm),
        in_specs=in_specs,
        out_specs=out_specs,
        out_shape=out_shape,
        scratch_shapes=[pltpu.VMEM((IN_COLS, tm), F32)],
        compiler_params=_params(2),
        name="inproj",
    )(x, gx, w1t, gaq, gak, gql, gkvl, wuqt, wukvt, gbq, gbk, gcq, gck, gdq, gdk, cos_t, sin_t, kpos_t, aslope)


def _flash_scratch(tq):
    s_buf, p_buf = pltpu.VMEM((N_HEADS, ATT_TK, tq), F32), pltpu.VMEM((N_HEADS, ATT_TK, tq), BF16)
    row = pltpu.VMEM((N_HEADS, tq), F32)
    return [s_buf, s_buf, p_buf, p_buf, row, row, row, pltpu.VMEM((N_HEADS, ATT_DV, tq), F32)]


def _flash_heads(qt_ref, k_ref, vt_ref, o_ref, s_bufs, p_bufs, a_bufs, m_scr, acc_scr, rowb_scr, i):
    tq = o_ref.shape[3]
    n_tail = tq // ATT_TK
    assert n_tail == 2
    n_past = i * n_tail
    heads = range(N_HEADS)

    def block_of(pos):
        return jnp.where(pos < n_tail, n_past + pos, pos - n_tail)

    def head_scores(h, slot, blk):
        kb = k_ref[0, h, pl.ds(pl.multiple_of(blk * ATT_TK, ATT_TK), ATT_TK), :]
        s_bufs[slot][h] = jnp.dot(kb, qt_ref[0, h], preferred_element_type=F32)

    def stage_scores(slot, blk):
        for h in heads:
            head_scores(h, slot, blk)

    def stage_softmax(slot, blk, causal=None):
        for h in heads:
            rv_row = None if rowb_scr is None else rowb_scr[h, pl.ds(blk, 1), :]
            for c in range(tq // LANES):
                cols = slice(c * LANES, (c + 1) * LANES)
                st = s_bufs[slot][h, :, cols]
                if causal is not None:
                    st = jnp.where(causal[:, cols], st, 2 * NEG)
                cm = jnp.max(st, axis=0, keepdims=True)
                m_old = m_scr[h:h + 1, cols]
                if rowb_scr is None:
                    m_new = jnp.maximum(m_old, cm)
                    shift = m_new
                else:
                    rv = rv_row[:, cols]
                    m_new = jnp.maximum(m_old, cm + rv)
                    shift = m_new - rv
                m_scr[h:h + 1, cols] = m_new
                a_bufs[slot][h:h + 1, cols] = jnp.exp2(m_old - m_new)
                p_bufs[slot][h, :, cols] = jnp.exp2(st - shift).astype(BF16)

    def head_values(h, slot, blk):
        acc_scr[h] = a_bufs[slot][h:h + 1, :] * acc_scr[h] + jnp.dot(vt_ref[0, h, blk], p_bufs[slot][h],
                                                                     preferred_element_type=F32)

    def stage_values(slot, blk):
        for h in heads:
            head_values(h, slot, blk)

    krow = lax.broadcasted_iota(jnp.int32, (ATT_TK, tq), 0)
    ti = lax.broadcasted_iota(jnp.int32, (ATT_TK, tq), 1)
    for h in heads:
        acc_scr[h] = jnp.zeros((ATT_DV, tq), F32)
    m_scr[...] = jnp.full((N_HEADS, tq), NEG, F32)
    stage_scores(0, n_past)
    stage_scores(1, n_past + 1)
    stage_softmax(0, n_past, causal=ti >= krow)
    stage_values(0, n_past)
    stage_softmax(1, n_past + 1, causal=ti >= krow + ATT_TK)
    stage_scores(0, 0)

    def step(pos, slot):
        stage_softmax(1 - slot, pos + 1 - n_tail)
        blk = block_of(pos)
        for h in heads:
            head_values(h, slot, blk)
            head_scores(h, slot, pos)

    def pair(first):
        step(first, 1)
        step(first + 1, 0)

    def trip(t, carry):
        for u in range(FLASH_PAIRS):
            pair(2 * (FLASH_PAIRS * t + u) + 1)
        return carry

    def single(u, carry):
        pair(2 * u + 1)
        return carry

    lax.fori_loop(0, i // FLASH_PAIRS, trip, 0)
    lax.fori_loop(i - i % FLASH_PAIRS, i, single, 0)
    last = n_past + 1
    stage_values(1, block_of(last))
    for h in heads:
        acc = acc_scr[h]
        o_ref[0, h] = acc[:HEAD_DIM, :] / acc[HEAD_DIM:HEAD_DIM + 1, :]


def _moba_kernel(qt_ref, q32t_ref, k_ref, vt_ref, kmean_ref, o_ref, rowb_scr, s0, s1, p0, p1, a0, a1, m_scr,
                 acc_scr):
    i = pl.program_id(1)
    nb = kmean_ref.shape[2]
    tq = qt_ref.shape[3]
    blk = lax.broadcasted_iota(jnp.int32, (nb, tq), 0)
    col = lax.broadcasted_iota(jnp.int32, (nb, tq), 1)
    qblk = i * (tq // MOBA_BLOCK) + col // MOBA_BLOCK
    past = blk < qblk
    dist0 = (i * tq + col - blk * MOBA_BLOCK).astype(F32)
    for h in range(N_HEADS):
        gate = jnp.dot(kmean_ref[0, h], q32t_ref[0, h], preferred_element_type=F32,
                       precision=lax.Precision.HIGHEST)
        gate = jnp.where(past, gate, NEG)
        sel = blk == qblk
        for _ in range(MOBA_TOPK):
            best = jnp.max(gate, axis=0, keepdims=True)
            first = jnp.min(jnp.where(gate == best, blk, nb), axis=0, keepdims=True)
            pick = blk == first
            sel = jnp.logical_or(sel, jnp.logical_and(pick, past))
            gate = jnp.where(pick, -jnp.inf, gate)
        rowb_scr[h] = jnp.where(sel, (-float(SLOPE_A[h]) * LOG2E) * dist0, 2 * NEG)
    _flash_heads(qt_ref, k_ref, vt_ref, o_ref, (s0, s1), (p0, p1), (a0, a1), m_scr, acc_scr, rowb_scr, i)


def _flash_specs(S, tq):
    H, nb = N_HEADS, S // ATT_TK
    return ([pl.BlockSpec((1, H, ATT_DK, tq), lambda b, i: (b, 0, 0, i)),
             pl.BlockSpec((1, H, S, ATT_DK), lambda b, i: (b, 0, 0, 0)),
             pl.BlockSpec((1, H, nb, ATT_DV, ATT_TK), lambda b, i: (b, 0, 0, 0, 0))],
            pl.BlockSpec((1, H, HEAD_DIM, tq), lambda b, i: (b, 0, 0, i)))


def _moba(aqt, aq32t, ak, avt, kmean_h):
    B, H, _, S = aqt.shape
    tq = ATT_TQ
    nb = S // ATT_TK
    (q_spec, k_spec, vt_spec), o_spec = _flash_specs(S, tq)
    return pl.pallas_call(
        _moba_kernel,
        grid=(B, S // tq),
        in_specs=[q_spec, pl.BlockSpec((1, H, HEAD_DIM, tq), lambda b, i: (b, 0, 0, i)), k_spec, vt_spec,
                  pl.BlockSpec((1, H, nb, HEAD_DIM), lambda b, i: (b, 0, 0, 0))],
        out_specs=o_spec,
        out_shape=jax.ShapeDtypeStruct((B, H, HEAD_DIM, S), F32),
        scratch_shapes=[pltpu.VMEM((H, nb, tq), F32)] + _flash_scratch(tq),
        compiler_params=_params(2),
        name="moba",
    )(aqt, aq32t, ak, avt, kmean_h)


def _mla_kernel(qt_ref, k_ref, vt_ref, o_ref, s0, s1, p0, p1, a0, a1, m_scr, acc_scr):
    _flash_heads(qt_ref, k_ref, vt_ref, o_ref, (s0, s1), (p0, p1), (a0, a1), m_scr, acc_scr, None,
                 pl.program_id(1))


def _mla(bqt, bk, bvt):
    B, H, _, S = bqt.shape
    tq = ATT_TQ
    in_specs, o_spec = _flash_specs(S, tq)
    return pl.pallas_call(
        _mla_kernel,
        grid=(B, S // tq),
        in_specs=in_specs,
        out_specs=o_spec,
        out_shape=jax.ShapeDtypeStruct((B, H, HEAD_DIM, S), F32),
        scratch_shapes=_flash_scratch(tq),
        compiler_params=_params(2),
        name="mla",
    )(bqt, bk, bvt)


DIL_SPAN = max(d for _, d in DILATED_BRANCHES) * Q_BLOCK
DIL_UNITS = DIL_SPAN // Q_BLOCK
DIL_GROUP = 2


def _dilated_bias():
    qi = np.arange(Q_BLOCK)[:, None]
    kidx = np.arange(2 * Q_BLOCK)[None, :]
    rel = qi + Q_BLOCK - kidx
    out = np.empty((2, len(DILATED_BRANCHES), N_HEADS, Q_BLOCK, 2 * Q_BLOCK), np.float32)
    for bi, (window, d) in enumerate(DILATED_BRANCHES):
        valid = (rel >= 0) & (rel <= window // d)
        for h in range(N_HEADS):
            bias = -SLOPE_C[h] * np.float32(LOG2E) * (d * rel).astype(np.float32)
            out[0, bi, h] = np.where(valid, bias, NEG)
            out[1, bi, h] = np.where(valid & (kidx >= Q_BLOCK), bias, NEG)
    return out


def _rows_load(ref, lead, start, size, stride):
    return jnp.concatenate([ref[lead + (c, pl.ds(start, size, stride=stride), slice(None))]
                            for c in range(N_HALF)], axis=-1)


def _rows_store(ref, lead, start, size, stride, val):
    for c in range(N_HALF):
        ref[lead + (c, pl.ds(start, size, stride=stride), slice(None))] = val[:, c * LANES:(c + 1) * LANES]


def _dilated_kernel(q_ref, kp_ref, kc_ref, vp_ref, vc_ref, bias_ref, o_ref, kbuf, vbuf, m_scr, den_scr, num_scr):
    span = pl.program_id(1)
    kbuf[:, 0:DIL_SPAN, :] = kp_ref[0]
    kbuf[:, DIL_SPAN:, :] = kc_ref[0]
    vbuf[:, 0:DIL_SPAN, :] = vp_ref[0]
    vbuf[:, DIL_SPAN:, :] = vc_ref[0]
    lane_head = lax.broadcasted_iota(jnp.int32, (1, GROUP), 1) // HEAD_DIM
    hmask = [lane_head == h for h in range(N_HEADS)]
    hmask_f = [m.astype(F32) for m in hmask]
    first_span = jnp.where(span == 0, 1, 0)

    def per_head(cols):
        out = cols[N_HEADS - 1]
        for h in range(N_HEADS - 2, -1, -1):
            out = jnp.where(hmask[h], cols[h], out)
        return out

    order = sorted(range(len(DILATED_BRANCHES)), key=lambda b: -DILATED_BRANCHES[b][1])
    for bi in order:
        d = DILATED_BRANCHES[bi][1]
        first, last = bi == order[0], bi == order[-1]

        def group(g, _, bi=bi, d=d, first=first, last=last):
            fronts = []
            for uu in range(DIL_GROUP):
                u = g * DIL_GROUP + uu
                r, n = u % d, u // d
                qstart = n * (Q_BLOCK * d) + r
                kstart = DIL_SPAN + qstart - Q_BLOCK * d
                q = _rows_load(q_ref, (0,), qstart, Q_BLOCK, d)
                k2 = _rows_load(kbuf, (), kstart, 2 * Q_BLOCK, d).astype(BF16)
                v2 = _rows_load(vbuf, (), kstart, 2 * Q_BLOCK, d).astype(BF16)
                q4 = jnp.concatenate([(q * hmask_f[h]).astype(BF16) for h in range(N_HEADS)], axis=0)
                s4 = lax.dot_general(q4, k2, _NT, preferred_element_type=F32)
                variant = jnp.where(n == 0, first_span, 0)
                fronts.append((qstart, s4, v2, variant))
            for qstart, s4, v2, variant in fronts:
                es, ms, ls = [], [], []
                for h in range(N_HEADS):
                    s = s4[h * Q_BLOCK:(h + 1) * Q_BLOCK, :] + bias_ref[variant, bi, h]
                    m = jnp.max(s, axis=-1, keepdims=True)
                    e = jnp.exp2(s - m)
                    ls.append(jnp.sum(e, axis=-1, keepdims=True))
                    ms.append(m)
                    es.append(e.astype(BF16))
                o4 = jnp.dot(jnp.concatenate(es, axis=0), v2, preferred_element_type=F32)
                o = o4[(N_HEADS - 1) * Q_BLOCK:, :]
                for h in range(N_HEADS - 2, -1, -1):
                    o = jnp.where(hmask[h], o4[h * Q_BLOCK:(h + 1) * Q_BLOCK, :], o)
                m_b, l_b = per_head(ms), per_head(ls)
                at = ((), qstart, Q_BLOCK, d)
                if first:
                    _rows_store(m_scr, *at, m_b)
                    _rows_store(num_scr, *at, o)
                    _rows_store(den_scr, *at, l_b)
                else:
                    m_old = _rows_load(m_scr, *at)
                    m_new = jnp.maximum(m_old, m_b)
                    a, b = jnp.exp2(m_old - m_new), jnp.exp2(m_b - m_new)
                    num = a * _rows_load(num_scr, *at) + b * o
                    den = a * _rows_load(den_scr, *at) + b * l_b
                    if last:
                        _rows_store(o_ref, (0,), qstart, Q_BLOCK, d, num / den)
                    else:
                        _rows_store(m_scr, *at, m_new)
                        _rows_store(num_scr, *at, num)
                        _rows_store(den_scr, *at, den)
            return 0

        lax.fori_loop(0, DIL_UNITS // DIL_GROUP, group, 0)


def _dilated(cq, ck, cv):
    B, _, S, _ = cq.shape
    cur = pl.BlockSpec((1, N_HALF, DIL_SPAN, LANES), lambda b, s: (b, 0, s, 0))
    prev = pl.BlockSpec((1, N_HALF, DIL_SPAN, LANES), lambda b, s: (b, 0, jnp.maximum(s - 1, 0), 0))
    bias = jnp.asarray(_dilated_bias())
    return pl.pallas_call(
        _dilated_kernel,
        grid=(B, S // DIL_SPAN),
        in_specs=[cur, prev, cur, prev, cur, pl.BlockSpec(bias.shape, lambda b, s: (0,) * bias.ndim)],
        out_specs=cur,
        out_shape=jax.ShapeDtypeStruct((B, N_HALF, S, LANES), F32),
        scratch_shapes=[pltpu.VMEM((N_HALF, 2 * DIL_SPAN, LANES), F32),
                        pltpu.VMEM((N_HALF, 2 * DIL_SPAN, LANES), F32),
                        pltpu.VMEM((N_HALF, DIL_SPAN, LANES), F32), pltpu.VMEM((N_HALF, DIL_SPAN, LANES), F32),
                        pltpu.VMEM((N_HALF, DIL_SPAN, LANES), F32)],
        compiler_params=_params(2),
        name="dilated",
    )(cq, ck, ck, cv, cv, bias)


SWA_SPAN = 1024
SWA_GROUP = 2
SWA_HEAD_ORDER = (0, 2, 1, 3)


def _swa_bias():
    qi = np.arange(Q_BLOCK)[:, None]
    kidx = np.arange(2 * Q_BLOCK)[None, :]
    rel = qi + Q_BLOCK - kidx
    valid = (rel >= 0) & (rel < SWA_WINDOW)
    out = np.empty((2, N_HEADS, Q_BLOCK, 2 * Q_BLOCK), np.float32)
    for h in range(N_HEADS):
        bias = -SLOPE_D[h] * np.float32(LOG2E) * rel.astype(np.float32)
        out[0, h] = np.where(valid, bias, NEG)
        out[1, h] = np.where(valid & (kidx >= Q_BLOCK), bias, NEG)
    return out


def _swa_kernel(sink_ref, q_ref, k_ref, v_ref, bias_ref, o_ref):
    span = pl.program_id(1)
    units = SWA_SPAN // Q_BLOCK
    half = lax.broadcasted_iota(jnp.int32, (1, LANES), 1) // HEAD_DIM
    lo = half == 0
    kv_mask = [jnp.where(half == g, 1.0, 0.0).astype(BF16) for g in range(SWA_KV_HEADS)]

    def group(g, _):
        fronts = []
        for uu in range(SWA_GROUP):
            u = g * SWA_GROUP + uu
            n = span * units + u
            lo_start = pl.multiple_of(jnp.maximum(n - 1, 0) * Q_BLOCK, Q_BLOCK)
            hi_start = pl.multiple_of(n * Q_BLOCK, Q_BLOCK)
            qstart = pl.multiple_of(u * Q_BLOCK, Q_BLOCK)
            q = q_ref[0, pl.ds(qstart, Q_BLOCK), :]
            k2 = jnp.concatenate([k_ref[0, pl.ds(lo_start, Q_BLOCK), :], k_ref[0, pl.ds(hi_start, Q_BLOCK), :]],
                                 axis=0)
            v2 = jnp.concatenate([v_ref[0, pl.ds(lo_start, Q_BLOCK), :], v_ref[0, pl.ds(hi_start, Q_BLOCK), :]],
                                 axis=0)
            q4 = jnp.concatenate([q[:, (h % 2) * LANES:(h % 2 + 1) * LANES] * kv_mask[h // 2]
                                  for h in range(N_HEADS)], axis=0)
            s4 = lax.dot_general(q4, k2, _NT, preferred_element_type=F32)
            fronts.append((qstart, s4, v2, jnp.where(n == 0, 1, 0)))
        for qstart, s4, v2, variant in fronts:
            es, ls = [], []
            for h in range(N_HEADS):
                s = s4[h * Q_BLOCK:(h + 1) * Q_BLOCK, :] + bias_ref[variant, h]
                sink = sink_ref[h] * LOG2E
                m = jnp.maximum(jnp.max(s, axis=-1, keepdims=True), sink)
                e = jnp.exp2(s - m)
                ls.append(jnp.sum(e, axis=-1, keepdims=True) + jnp.exp2(sink - m))
                es.append(e.astype(BF16))
            o4 = jnp.dot(jnp.concatenate(es, axis=0), v2, preferred_element_type=F32)
            tiles = []
            for t in range(N_HALF):
                a, b = t, t + 2
                tiles.append(jnp.where(lo, o4[a * Q_BLOCK:(a + 1) * Q_BLOCK, :] / ls[a],
                                       o4[b * Q_BLOCK:(b + 1) * Q_BLOCK, :] / ls[b]))
            o_ref[0, pl.ds(qstart, Q_BLOCK), :] = jnp.concatenate(tiles, axis=-1)
        return 0

    lax.fori_loop(0, units // SWA_GROUP, group, 0)


def _swa(dq, dk, dv, sinks):
    B, S, _ = dq.shape
    bias = jnp.asarray(_swa_bias())
    return pl.pallas_call(
        _swa_kernel,
        grid=(B, S // SWA_SPAN),
        in_specs=[pl.BlockSpec(memory_space=pltpu.SMEM),
                  pl.BlockSpec((1, SWA_SPAN, GROUP), lambda b, i: (b, i, 0)),
                  pl.BlockSpec((1, S, SWA_KV_WIDTH), lambda b, i: (b, 0, 0)),
                  pl.BlockSpec((1, S, SWA_KV_WIDTH), lambda b, i: (b, 0, 0)),
                  pl.BlockSpec(bias.shape, lambda b, i: (0,) * bias.ndim)],
        out_specs=pl.BlockSpec((1, SWA_SPAN, GROUP), lambda b, i: (b, i, 0)),
        out_shape=jax.ShapeDtypeStruct((B, S, GROUP), F32),
        compiler_params=_params(2),
        name="swa",
    )(sinks, dq, dk, dv, bias)


def _row_norm(y, g_row):
    return y * lax.rsqrt(jnp.mean(y * y, axis=-1, keepdims=True) + EPS) * g_row


def _tail_kernel(x_ref, oat_ref, obt_ref, oc_ref, od_ref, gg_ref, wo_ref, gm_ref, wup_ref, wdn_ref, out_ref):
    gg = gg_ref[...]

    def col_norm_t(yt, g_row):
        y = (yt * lax.rsqrt(jnp.mean(yt * yt, axis=0, keepdims=True) + EPS)).T
        return y * g_row

    ga = col_norm_t(oat_ref[0], gg[0:1, :])
    gb = col_norm_t(obt_ref[0], gg[1:2, :])
    gc = _row_norm(jnp.concatenate([oc_ref[0, c] for c in range(N_HALF)], axis=-1), gg[2:3, :])
    gd = _row_norm(od_ref[0], gg[3:4, :])
    mixed = jnp.concatenate([ga, gb, gc, gd], axis=-1).astype(BF16)
    x1 = x_ref[0] + jnp.dot(mixed, wo_ref[...], preferred_element_type=F32)
    xn = _row_norm(x1, gm_ref[...]).astype(BF16)
    u = jnp.maximum(jnp.dot(xn, wup_ref[...], preferred_element_type=F32), 0.0)
    out_ref[0] = x1 + jnp.dot((u * u).astype(BF16), wdn_ref[...], preferred_element_type=F32)


def _tail(x, oat, obt, oc, od, gg, wo, gm, wup, wdn):
    B, S, _ = x.shape
    tm = TAIL_TM
    nat = lambda w: pl.BlockSpec((1, tm, w), lambda b, t: (b, t, 0))
    ft = pl.BlockSpec((1, GROUP, tm), lambda b, t: (b, 0, t))
    const = lambda a: pl.BlockSpec(a.shape, lambda b, t: (0,) * a.ndim, pipeline_mode=pl.Buffered(1))
    return pl.pallas_call(
        _tail_kernel,
        grid=(B, S // tm),
        in_specs=[nat(D_MODEL), ft, ft, pl.BlockSpec((1, N_HALF, tm, LANES), lambda b, t: (b, 0, t, 0)),
                  nat(GROUP), const(gg), const(wo), const(gm),
                  const(wup), const(wdn)],
        out_specs=nat(D_MODEL),
        out_shape=jax.ShapeDtypeStruct((B, S, D_MODEL), F32),
        compiler_params=_params(2),
        name="tail",
    )(x, oat, obt, oc, od, gg, wo, gm, wup, wdn)


def _rope_tables_t(S):
    inv = 1.0 / (ROPE_THETA ** (jnp.arange(0, MLA_ROPE, 2, dtype=F32) / MLA_ROPE))
    ang = inv[:, None] * jnp.arange(S, dtype=F32)[None, :]
    return jnp.cos(ang), jnp.sin(ang)


def _moba_aug_tables(S):
    pos = np.arange(S, dtype=np.float32) % MOBA_BLOCK
    kpos = np.zeros((AUG_ROWS, S), np.float32)
    kpos[:3] = pos
    aslope = np.zeros((N_HEADS, AUG_ROWS, 1), np.float32)
    rest = (SLOPE_A * np.float32(LOG2E)).astype(np.float32)
    for r in range(3):
        piece = rest.astype(BF16).astype(np.float32)
        aslope[:, r, 0] = piece
        rest = rest - piece
    return jnp.asarray(kpos), jnp.asarray(aslope)


def _layer(x, cos_t, sin_t, kpos_t, aslope, attn_norm_g, w_in, moba_q_g, moba_k_g, mla_qlat_g, mla_kvlat_g, mla_w_uq, mla_w_ukv,
           mla_q_g, mla_k_g, dil_q_g, dil_k_g, swa_q_g, swa_k_g, swa_sinks, group_out_g, w_o, mlp_norm_g,
           w_up, w_down):
    B, S, _ = x.shape
    col = lambda g: g.reshape(-1, 1)
    (aqt, aq32t, ak, akm, avt, bqt, bk, bvt, cq, ck, cv, dq, dk, dv) = _inproj(
        x, attn_norm_g.reshape(1, -1), w_in.T.astype(BF16), col(moba_q_g), col(moba_k_g), col(mla_qlat_g),
        col(mla_kvlat_g), mla_w_uq.T.astype(BF16), mla_w_ukv.T.astype(BF16), col(mla_q_g), col(mla_k_g),
        col(dil_q_g), col(dil_k_g), col(swa_q_g), col(swa_k_g), cos_t, sin_t, kpos_t, aslope)
    nb = S // MOBA_BLOCK
    kmean_h = akm.reshape(B, nb, N_HEADS, HEAD_DIM).transpose(0, 2, 1, 3)
    oat = _moba(aqt, aq32t, ak, avt, kmean_h).reshape(B, GROUP, S)
    obt = _mla(bqt, bk, bvt).reshape(B, GROUP, S)
    oc = _dilated(cq, ck, cv)
    od = _swa(dq, dk, dv, swa_sinks)
    perm = np.concatenate([np.arange(HEAD_DIM) + HEAD_DIM * h for h in SWA_HEAD_ORDER])
    gg = group_out_g.at[3].set(group_out_g[3][perm])
    wo = jnp.concatenate([w_o[:3 * GROUP], w_o[3 * GROUP + perm]], axis=0)
    return _tail(x, oat, obt, oc, od, gg, wo.astype(BF16), mlp_norm_g.reshape(1, -1),
                 w_up.astype(BF16), w_down.astype(BF16))


def kernel(x, attn_norm_g, w_in, moba_q_g, moba_k_g, mla_qlat_g, mla_kvlat_g, mla_w_uq, mla_w_ukv, mla_q_g,
           mla_k_g, dil_q_g, dil_k_g, swa_q_g, swa_k_g, swa_sinks, group_out_g, w_o, mlp_norm_g, w_up, w_down):
    S = x.shape[1]
    assert S % max(d * Q_BLOCK for _, d in DILATED_BRANCHES) == 0 and S % INPROJ_TM == 0
    cos_t, sin_t = _rope_tables_t(S)
    kpos_t, aslope = _moba_aug_tables(S)
    params = (attn_norm_g, w_in, moba_q_g, moba_k_g, mla_qlat_g, mla_kvlat_g, mla_w_uq, mla_w_ukv, mla_q_g,
              mla_k_g, dil_q_g, dil_k_g, swa_q_g, swa_k_g, swa_sinks, group_out_g, w_o, mlp_norm_g, w_up, w_down)
    for l in range(attn_norm_g.shape[0]):
        x = _layer(x, cos_t, sin_t, kpos_t, aslope, *[p[l] for p in params])
    return x
```

```python
import numpy as np
import jax
import jax.numpy as jnp
from jax import lax
from jax.experimental import pallas as pl
from jax.experimental.pallas import tpu as pltpu

F32 = jnp.float32
BF16 = jnp.bfloat16

D_MODEL = 1024
HEAD_DIM = 64
N_HEADS = 4
GROUP = N_HEADS * HEAD_DIM
LANES = 128
N_HALF = GROUP // LANES
MOBA_BLOCK = 256
MOBA_TOPK = 3
MLA_Q_RANK = 256
MLA_KV_RANK = 128
MLA_NOPE = 64
MLA_ROPE = 32
MLA_QK = MLA_NOPE + MLA_ROPE
ROPE_THETA = 10000.0
DILATED_BRANCHES = ((128, 1), (512, 4), (2048, 16))
Q_BLOCK = 128
SWA_WINDOW = 128
SWA_KV_HEADS = 2
SWA_KV_WIDTH = SWA_KV_HEADS * HEAD_DIM
D_FF = 4 * D_MODEL
EPS = 1e-6
NEG = -1e30

_WIDTHS = (GROUP, GROUP, GROUP, MLA_Q_RANK, MLA_KV_RANK, MLA_ROPE,
           GROUP, GROUP, GROUP, GROUP, SWA_KV_WIDTH, SWA_KV_WIDTH)
_OFFS = tuple(int(v) for v in np.cumsum((0,) + _WIDTHS))
IN_COLS = _OFFS[-1]
(_A_Q, _A_K, _A_V, _B_QL, _B_KVL, _B_KR, _C_Q, _C_K, _C_V, _D_Q, _D_K, _D_V) = _OFFS[:-1]

VMEM_LIMIT = 56 * 1024 * 1024

INPROJ_TM = 512
TAIL_TM = 512
ATT_TK = 256
ATT_TQ = 2 * ATT_TK
ATT_DK = 128
ATT_DV = HEAD_DIM + 16
AUG_ROWS = 8
LOG2E = 1.4426950408889634
FLASH_PAIRS = 2

_NT = (((1,), (1,)), ((), ()))


def _alibi_slopes():
    n = 3 * N_HEADS
    idx = np.arange(1, n + 1, dtype=np.float32).reshape(N_HEADS, 3)
    s = np.exp2(-8.0 * idx / n).astype(np.float32)
    return s[:, 0], s[:, 1], s[:, 2]


SLOPE_A, SLOPE_C, SLOPE_D = _alibi_slopes()


def _params(n_axes):
    return pltpu.CompilerParams(dimension_semantics=("arbitrary",) * n_axes,
                                vmem_limit_bytes=VMEM_LIMIT)


def _head_norm_t(sec, g_col, n_heads, width):
    outs = []
    for h in range(n_heads):
        s = sec[h * width:(h + 1) * width, :]
        ms = jnp.sum(s * s, axis=0, keepdims=True) * (1.0 / width)
        outs.append(s * lax.rsqrt(ms + EPS) * g_col)
    return outs


def _inproj_kernel(x_ref, gx_ref, w1t_ref, gaq_ref, gak_ref, gql_ref, gkvl_ref, wuqt_ref, wukvt_ref,
                   gbq_ref, gbk_ref, gcq_ref, gck_ref, gdq_ref, gdk_ref, cos_ref, sin_ref, kpos_ref, aslope_ref,
                   aqt_ref, aq32t_ref, ak_ref, akm_ref, avt_ref,
                   bqt_ref, bk_ref, bvt_ref,
                   cq_ref, ck_ref, cv_ref, dq_ref, dk_ref, dv_ref,
                   h_scr):
    tm = x_ref.shape[1]
    x = x_ref[0]
    ms = jnp.mean(x * x, axis=-1, keepdims=True)
    xn = (x * lax.rsqrt(ms + EPS) * gx_ref[...]).astype(BF16)
    h_scr[...] = lax.dot_general(w1t_ref[...], xn, _NT, preferred_element_type=F32)

    scale = HEAD_DIM ** -0.5

    ones_rows = jnp.ones((ATT_DV - HEAD_DIM, ATT_TK), F32)

    def store_vt(ref, h, vh):
        for c in range(tm // ATT_TK):
            ref[0, h, c] = jnp.concatenate([vh[:, c * ATT_TK:(c + 1) * ATT_TK], ones_rows], axis=0).astype(BF16)

    pad_rows = jnp.zeros((ATT_DK - HEAD_DIM - AUG_ROWS, tm), F32)
    qa = _head_norm_t(h_scr[_A_Q:_A_Q + GROUP, :], gaq_ref[...], N_HEADS, HEAD_DIM)
    ka = _head_norm_t(h_scr[_A_K:_A_K + GROUP, :], gak_ref[...], N_HEADS, HEAD_DIM)
    ka_nat = jnp.concatenate(ka, axis=0).T
    for c in range(tm // MOBA_BLOCK):
        akm_ref[0, c] = jnp.sum(ka_nat[c * MOBA_BLOCK:(c + 1) * MOBA_BLOCK, :], axis=0,
                                keepdims=True) * (1.0 / MOBA_BLOCK)
    kpos = kpos_ref[...]
    for h in range(N_HEADS):
        aq32t_ref[0, h] = qa[h]
        slope_rows = jnp.broadcast_to(aslope_ref[h], (AUG_ROWS, tm))
        aqt_ref[0, h] = jnp.concatenate([qa[h] * (scale * LOG2E), slope_rows, pad_rows], axis=0).astype(BF16)
        ak_ref[0, h] = jnp.concatenate([ka[h], kpos, pad_rows], axis=0).T.astype(BF16)
        store_vt(avt_ref, h, h_scr[_A_V + h * HEAD_DIM:_A_V + (h + 1) * HEAD_DIM, :])

    cos = cos_ref[...]
    sin = sin_ref[...]
    half = MLA_ROPE // 2

    def rope_pad(t, sc):
        x1 = t[MLA_NOPE:MLA_NOPE + half, :]
        x2 = t[MLA_NOPE + half:MLA_QK, :]
        return jnp.concatenate([t[:MLA_NOPE, :] * sc, (x1 * cos - x2 * sin) * sc, (x1 * sin + x2 * cos) * sc,
                                jnp.zeros((ATT_DK - MLA_QK, tm), F32)], axis=0)

    ql = h_scr[_B_QL:_B_QL + MLA_Q_RANK, :]
    ql = ql * lax.rsqrt(jnp.sum(ql * ql, axis=0, keepdims=True) * (1.0 / MLA_Q_RANK) + EPS) * gql_ref[...]
    qb = jnp.dot(wuqt_ref[...], ql.astype(BF16), preferred_element_type=F32)
    qb = _head_norm_t(qb, gbq_ref[...], N_HEADS, MLA_QK)
    kvl = h_scr[_B_KVL:_B_KVL + MLA_KV_RANK, :]
    kvl = kvl * lax.rsqrt(jnp.sum(kvl * kvl, axis=0, keepdims=True) * (1.0 / MLA_KV_RANK) + EPS) * gkvl_ref[...]
    kvb = jnp.dot(wukvt_ref[...], kvl.astype(BF16), preferred_element_type=F32)
    kr = h_scr[_B_KR:_B_KR + MLA_ROPE, :]
    gbk = gbk_ref[...]
    for h in range(N_HEADS):
        bqt_ref[0, h] = rope_pad(qb[h], MLA_QK ** -0.5 * LOG2E).astype(BF16)
        kh = jnp.concatenate([kvb[h * 2 * HEAD_DIM:h * 2 * HEAD_DIM + MLA_NOPE, :], kr], axis=0)
        kh = kh * lax.rsqrt(jnp.sum(kh * kh, axis=0, keepdims=True) * (1.0 / MLA_QK) + EPS) * gbk
        bk_ref[0, h] = rope_pad(kh, 1.0).T.astype(BF16)
        store_vt(bvt_ref, h, kvb[h * 2 * HEAD_DIM + MLA_NOPE:(h + 1) * 2 * HEAD_DIM, :])

    qc = jnp.concatenate(_head_norm_t(h_scr[_C_Q:_C_Q + GROUP, :], gcq_ref[...], N_HEADS, HEAD_DIM), axis=0)
    kc = jnp.concatenate(_head_norm_t(h_scr[_C_K:_C_K + GROUP, :], gck_ref[...], N_HEADS, HEAD_DIM), axis=0)
    for ref, val in ((cq_ref, qc * (scale * LOG2E)), (ck_ref, kc), (cv_ref, h_scr[_C_V:_C_V + GROUP, :])):
        for c in range(N_HALF):
            ref[0, c] = val[c * LANES:(c + 1) * LANES, :].T

    qd = _head_norm_t(h_scr[_D_Q:_D_Q + GROUP, :], gdq_ref[...], N_HEADS, HEAD_DIM)
    qd = jnp.concatenate([qd[h] for h in SWA_HEAD_ORDER], axis=0)
    dq_ref[0] = (qd * (scale * LOG2E)).T.astype(BF16)
    kd = jnp.concatenate(_head_norm_t(h_scr[_D_K:_D_K + SWA_KV_WIDTH, :], gdk_ref[...], SWA_KV_HEADS, HEAD_DIM),
                         axis=0)
    dk_ref[0] = kd.T.astype(BF16)
    dv_ref[0] = h_scr[_D_V:_D_V + SWA_KV_WIDTH, :].T.astype(BF16)


def _inproj(x, gx, w1t, gaq, gak, gql, gkvl, wuqt, wukvt, gbq, gbk, gcq, gck, gdq, gdk, cos_t, sin_t, kpos_t,
            aslope):
    B, S, _ = x.shape
    tm = INPROJ_TM
    nb = S // ATT_TK
    cpt = tm // ATT_TK
    H = N_HEADS
    full = lambda a: pl.BlockSpec(a.shape, lambda b, t: (0,) * a.ndim)
    in_specs = [pl.BlockSpec((1, tm, D_MODEL), lambda b, t: (b, t, 0)), full(gx), full(w1t), full(gaq), full(gak),
                full(gql), full(gkvl), full(wuqt), full(wukvt), full(gbq), full(gbk), full(gcq), full(gck),
                full(gdq), full(gdk),
                pl.BlockSpec((MLA_ROPE // 2, tm), lambda b, t: (0, t)),
                pl.BlockSpec((MLA_ROPE // 2, tm), lambda b, t: (0, t)),
                pl.BlockSpec((AUG_ROWS, tm), lambda b, t: (0, t)), full(aslope)]
    head_t = lambda w: pl.BlockSpec((1, H, w, tm), lambda b, t: (b, 0, 0, t))
    head_n = pl.BlockSpec((1, H, tm, ATT_DK), lambda b, t: (b, 0, t, 0))
    vt_spec = pl.BlockSpec((1, H, cpt, ATT_DV, ATT_TK), lambda b, t: (b, 0, t, 0, 0))
    nat = lambda w: pl.BlockSpec((1, tm, w), lambda b, t: (b, t, 0))
    halves = pl.BlockSpec((1, N_HALF, tm, LANES), lambda b, t: (b, 0, t, 0))
    out_shape = [
        jax.ShapeDtypeStruct((B, H, ATT_DK, S), BF16),
        jax.ShapeDtypeStruct((B, H, HEAD_DIM, S), F32),
        jax.ShapeDtypeStruct((B, H, S, ATT_DK), BF16),
        jax.ShapeDtypeStruct((B, nb, 1, GROUP), F32),
        jax.ShapeDtypeStruct((B, H, nb, ATT_DV, ATT_TK), BF16),
        jax.ShapeDtypeStruct((B, H, ATT_DK, S), BF16),
        jax.ShapeDtypeStruct((B, H, S, ATT_DK), BF16),
        jax.ShapeDtypeStruct((B, H, nb, ATT_DV, ATT_TK), BF16),
        jax.ShapeDtypeStruct((B, N_HALF, S, LANES), F32),
        jax.ShapeDtypeStruct((B, N_HALF, S, LANES), F32),
        jax.ShapeDtypeStruct((B, N_HALF, S, LANES), F32),
        jax.ShapeDtypeStruct((B, S, GROUP), BF16),
        jax.ShapeDtypeStruct((B, S, SWA_KV_WIDTH), BF16),
        jax.ShapeDtypeStruct((B, S, SWA_KV_WIDTH), BF16),
    ]
    out_specs = [head_t(ATT_DK), head_t(HEAD_DIM), head_n,
                 pl.BlockSpec((1, tm // MOBA_BLOCK, 1, GROUP), lambda b, t: (b, t, 0, 0)), vt_spec,
                 head_t(ATT_DK), head_n, vt_spec,
                 halves, halves, halves, nat(GROUP), nat(SWA_KV_WIDTH), nat(SWA_KV_WIDTH)]
    return pl.pallas_call(
        _inproj_kernel,
        grid=(B, S // tm),
        in_specs=in_specs,
        out_specs=out_specs,
        out_shape=out_shape,
        scratch_shapes=[pltpu.VMEM((IN_COLS, tm), F32)],
        compiler_params=_params(2),
        name="inproj",
    )(x, gx, w1t, gaq, gak, gql, gkvl, wuqt, wukvt, gbq, gbk, gcq, gck, gdq, gdk, cos_t, sin_t, kpos_t, aslope)


def _flash_scratch(tq):
    s_buf, p_buf = pltpu.VMEM((N_HEADS, ATT_TK, tq), F32), pltpu.VMEM((N_HEADS, ATT_TK, tq), BF16)
    return [s_buf, s_buf, p_buf, p_buf, pltpu.VMEM((N_HEADS, ATT_DV, tq), F32)]


def _flash_heads(qt_ref, k_ref, vt_ref, o_ref, s_bufs, p_bufs, acc_scr, rowb_scr, i):
    tq = o_ref.shape[3]
    n_tail = tq // ATT_TK
    assert n_tail == 2
    n_past = i * n_tail
    heads = range(N_HEADS)

    def block_of(pos):
        return jnp.where(pos < n_tail, n_past + pos, pos - n_tail)

    def head_scores(h, slot, blk):
        kb = k_ref[0, h, pl.ds(pl.multiple_of(blk * ATT_TK, ATT_TK), ATT_TK), :]
        s_bufs[slot][h] = jnp.dot(kb, qt_ref[0, h], preferred_element_type=F32)

    def stage_scores(slot, blk):
        for h in heads:
            head_scores(h, slot, blk)

    def stage_softmax(slot, blk, ms, causal=None):
        new_ms, alphas = [], []
        for h in heads:
            m_parts, a_parts = [], []
            rv_row = None if rowb_scr is None else rowb_scr[h, pl.ds(blk, 1), :]
            for c in range(tq // LANES):
                cols = slice(c * LANES, (c + 1) * LANES)
                st = s_bufs[slot][h, :, cols]
                if causal is not None:
                    st = jnp.where(causal[:, cols], st, 2 * NEG)
                cm = jnp.max(st, axis=0, keepdims=True)
                m_old = ms[h][:, cols]
                if rowb_scr is None:
                    m_new = jnp.maximum(m_old, cm)
                    shift = m_new
                else:
                    rv = rv_row[:, cols]
                    m_new = jnp.maximum(m_old, cm + rv)
                    shift = m_new - rv
                a_parts.append(jnp.exp2(m_old - m_new))
                m_parts.append(m_new)
                p_bufs[slot][h, :, cols] = jnp.exp2(st - shift).astype(BF16)
            new_ms.append(jnp.concatenate(m_parts, axis=1))
            alphas.append(jnp.concatenate(a_parts, axis=1))
        return tuple(new_ms), tuple(alphas)

    def head_values(h, slot, blk, alphas):
        acc_scr[h] = alphas[h] * acc_scr[h] + jnp.dot(vt_ref[0, h, blk], p_bufs[slot][h],
                                                      preferred_element_type=F32)

    def stage_values(slot, blk, alphas):
        for h in heads:
            head_values(h, slot, blk, alphas)

    krow = lax.broadcasted_iota(jnp.int32, (ATT_TK, tq), 0)
    ti = lax.broadcasted_iota(jnp.int32, (ATT_TK, tq), 1)
    for h in heads:
        acc_scr[h] = jnp.zeros((ATT_DV, tq), F32)
    stage_scores(0, n_past)
    stage_scores(1, n_past + 1)
    ms = tuple(jnp.full((1, tq), NEG, F32) for _ in heads)
    ms, alphas = stage_softmax(0, n_past, ms, causal=ti >= krow)
    stage_values(0, n_past, alphas)
    ms, alphas = stage_softmax(1, n_past + 1, ms, causal=ti >= krow + ATT_TK)
    stage_scores(0, 0)

    def step(pos, slot, ms, alphas):
        ms, new_alphas = stage_softmax(1 - slot, pos + 1 - n_tail, ms)
        blk = block_of(pos)
        for h in heads:
            head_values(h, slot, blk, alphas)
            head_scores(h, slot, pos)
        return ms, new_alphas

    def pair(first, carry):
        ms, alphas = step(first, 1, *carry)
        return step(first + 1, 0, ms, alphas)

    def trip(t, carry):
        for u in range(FLASH_PAIRS):
            carry = pair(2 * (FLASH_PAIRS * t + u) + 1, carry)
        return carry

    carry = lax.fori_loop(0, i // FLASH_PAIRS, trip, (ms, alphas))
    ms, alphas = lax.fori_loop(i - i % FLASH_PAIRS, i, lambda u, c: pair(2 * u + 1, c), carry)
    last = n_past + 1
    stage_values(1, block_of(last), alphas)
    for h in heads:
        acc = acc_scr[h]
        o_ref[0, h] = acc[:HEAD_DIM, :] / acc[HEAD_DIM:HEAD_DIM + 1, :]


def _moba_kernel(qt_ref, q32t_ref, k_ref, vt_ref, kmean_ref, o_ref, rowb_scr, s0, s1, p0, p1, acc_scr):
    i = pl.program_id(1)
    nb = kmean_ref.shape[2]
    tq = qt_ref.shape[3]
    blk = lax.broadcasted_iota(jnp.int32, (nb, tq), 0)
    col = lax.broadcasted_iota(jnp.int32, (nb, tq), 1)
    qblk = i * (tq // MOBA_BLOCK) + col // MOBA_BLOCK
    past = blk < qblk
    dist0 = (i * tq + col - blk * MOBA_BLOCK).astype(F32)
    for h in range(N_HEADS):
        gate = jnp.dot(kmean_ref[0, h], q32t_ref[0, h], preferred_element_type=F32,
                       precision=lax.Precision.HIGHEST)
        gate = jnp.where(past, gate, NEG)
        sel = blk == qblk
        for _ in range(MOBA_TOPK):
            best = jnp.max(gate, axis=0, keepdims=True)
            first = jnp.min(jnp.where(gate == best, blk, nb), axis=0, keepdims=True)
            pick = blk == first
            sel = jnp.logical_or(sel, jnp.logical_and(pick, past))
            gate = jnp.where(pick, -jnp.inf, gate)
        rowb_scr[h] = jnp.where(sel, (-float(SLOPE_A[h]) * LOG2E) * dist0, 2 * NEG)
    _flash_heads(qt_ref, k_ref, vt_ref, o_ref, (s0, s1), (p0, p1), acc_scr, rowb_scr, i)


def _flash_specs(S, tq):
    H, nb = N_HEADS, S // ATT_TK
    return ([pl.BlockSpec((1, H, ATT_DK, tq), lambda b, i: (b, 0, 0, i)),
             pl.BlockSpec((1, H, S, ATT_DK), lambda b, i: (b, 0, 0, 0)),
             pl.BlockSpec((1, H, nb, ATT_DV, ATT_TK), lambda b, i: (b, 0, 0, 0, 0))],
            pl.BlockSpec((1, H, HEAD_DIM, tq), lambda b, i: (b, 0, 0, i)))


def _moba(aqt, aq32t, ak, avt, kmean_h):
    B, H, _, S = aqt.shape
    tq = ATT_TQ
    nb = S // ATT_TK
    (q_spec, k_spec, vt_spec), o_spec = _flash_specs(S, tq)
    return pl.pallas_call(
        _moba_kernel,
        grid=(B, S // tq),
        in_specs=[q_spec, pl.BlockSpec((1, H, HEAD_DIM, tq), lambda b, i: (b, 0, 0, i)), k_spec, vt_spec,
                  pl.BlockSpec((1, H, nb, HEAD_DIM), lambda b, i: (b, 0, 0, 0))],
        out_specs=o_spec,
        out_shape=jax.ShapeDtypeStruct((B, H, HEAD_DIM, S), F32),
        scratch_shapes=[pltpu.VMEM((H, nb, tq), F32)] + _flash_scratch(tq),
        compiler_params=_params(2),
        name="moba",
    )(aqt, aq32t, ak, avt, kmean_h)


def _mla_kernel(qt_ref, k_ref, vt_ref, o_ref, s0, s1, p0, p1, acc_scr):
    _flash_heads(qt_ref, k_ref, vt_ref, o_ref, (s0, s1), (p0, p1), acc_scr, None, pl.program_id(1))


def _mla(bqt, bk, bvt):
    B, H, _, S = bqt.shape
    tq = ATT_TQ
    in_specs, o_spec = _flash_specs(S, tq)
    return pl.pallas_call(
        _mla_kernel,
        grid=(B, S // tq),
        in_specs=in_specs,
        out_specs=o_spec,
        out_shape=jax.ShapeDtypeStruct((B, H, HEAD_DIM, S), F32),
        scratch_shapes=_flash_scratch(tq),
        compiler_params=_params(2),
        name="mla",
    )(bqt, bk, bvt)


DIL_SPAN = max(d for _, d in DILATED_BRANCHES) * Q_BLOCK
DIL_UNITS = DIL_SPAN // Q_BLOCK
DIL_GROUP = 2


def _dilated_bias():
    qi = np.arange(Q_BLOCK)[:, None]
    kidx = np.arange(2 * Q_BLOCK)[None, :]
    rel = qi + Q_BLOCK - kidx
    out = np.empty((2, len(DILATED_BRANCHES), N_HEADS, Q_BLOCK, 2 * Q_BLOCK), np.float32)
    for bi, (window, d) in enumerate(DILATED_BRANCHES):
        valid = (rel >= 0) & (rel <= window // d)
        for h in range(N_HEADS):
            bias = -SLOPE_C[h] * np.float32(LOG2E) * (d * rel).astype(np.float32)
            out[0, bi, h] = np.where(valid, bias, NEG)
            out[1, bi, h] = np.where(valid & (kidx >= Q_BLOCK), bias, NEG)
    return out


def _rows_load(ref, lead, start, size, stride):
    return jnp.concatenate([ref[lead + (c, pl.ds(start, size, stride=stride), slice(None))]
                            for c in range(N_HALF)], axis=-1)


def _rows_store(ref, lead, start, size, stride, val):
    for c in range(N_HALF):
        ref[lead + (c, pl.ds(start, size, stride=stride), slice(None))] = val[:, c * LANES:(c + 1) * LANES]


def _dilated_kernel(q_ref, kp_ref, kc_ref, vp_ref, vc_ref, bias_ref, o_ref, kbuf, vbuf, m_scr, den_scr, num_scr):
    span = pl.program_id(1)
    kbuf[:, 0:DIL_SPAN, :] = kp_ref[0]
    kbuf[:, DIL_SPAN:, :] = kc_ref[0]
    vbuf[:, 0:DIL_SPAN, :] = vp_ref[0]
    vbuf[:, DIL_SPAN:, :] = vc_ref[0]
    lane_head = lax.broadcasted_iota(jnp.int32, (1, GROUP), 1) // HEAD_DIM
    hmask = [lane_head == h for h in range(N_HEADS)]
    hmask_f = [m.astype(F32) for m in hmask]
    first_span = jnp.where(span == 0, 1, 0)

    def per_head(cols):
        out = cols[N_HEADS - 1]
        for h in range(N_HEADS - 2, -1, -1):
            out = jnp.where(hmask[h], cols[h], out)
        return out

    order = sorted(range(len(DILATED_BRANCHES)), key=lambda b: -DILATED_BRANCHES[b][1])
    for bi in order:
        d = DILATED_BRANCHES[bi][1]
        first, last = bi == order[0], bi == order[-1]

        def group(g, _, bi=bi, d=d, first=first, last=last):
            fronts = []
            for uu in range(DIL_GROUP):
                u = g * DIL_GROUP + uu
                r, n = u % d, u // d
                qstart = n * (Q_BLOCK * d) + r
                kstart = DIL_SPAN + qstart - Q_BLOCK * d
                q = _rows_load(q_ref, (0,), qstart, Q_BLOCK, d)
                k2 = _rows_load(kbuf, (), kstart, 2 * Q_BLOCK, d).astype(BF16)
                v2 = _rows_load(vbuf, (), kstart, 2 * Q_BLOCK, d).astype(BF16)
                q4 = jnp.concatenate([(q * hmask_f[h]).astype(BF16) for h in range(N_HEADS)], axis=0)
                s4 = lax.dot_general(q4, k2, _NT, preferred_element_type=F32)
                variant = jnp.where(n == 0, first_span, 0)
                fronts.append((qstart, s4, v2, variant))
            for qstart, s4, v2, variant in fronts:
                es, ms, ls = [], [], []
                for h in range(N_HEADS):
                    s = s4[h * Q_BLOCK:(h + 1) * Q_BLOCK, :] + bias_ref[variant, bi, h]
                    m = jnp.max(s, axis=-1, keepdims=True)
                    e = jnp.exp2(s - m)
                    ls.append(jnp.sum(e, axis=-1, keepdims=True))
                    ms.append(m)
                    es.append(e.astype(BF16))
                o4 = jnp.dot(jnp.concatenate(es, axis=0), v2, preferred_element_type=F32)
                o = o4[(N_HEADS - 1) * Q_BLOCK:, :]
                for h in range(N_HEADS - 2, -1, -1):
                    o = jnp.where(hmask[h], o4[h * Q_BLOCK:(h + 1) * Q_BLOCK, :], o)
                m_b, l_b = per_head(ms), per_head(ls)
                at = ((), qstart, Q_BLOCK, d)
                if first:
                    _rows_store(m_scr, *at, m_b)
                    _rows_store(num_scr, *at, o)
                    _rows_store(den_scr, *at, l_b)
                else:
                    m_old = _rows_load(m_scr, *at)
                    m_new = jnp.maximum(m_old, m_b)
                    a, b = jnp.exp2(m_old - m_new), jnp.exp2(m_b - m_new)
                    num = a * _rows_load(num_scr, *at) + b * o
                    den = a * _rows_load(den_scr, *at) + b * l_b
                    if last:
                        _rows_store(o_ref, (0,), qstart, Q_BLOCK, d, num / den)
                    else:
                        _rows_store(m_scr, *at, m_new)
                        _rows_store(num_scr, *at, num)
                        _rows_store(den_scr, *at, den)
            return 0

        lax.fori_loop(0, DIL_UNITS // DIL_GROUP, group, 0)


def _dilated(cq, ck, cv):
    B, _, S, _ = cq.shape
    cur = pl.BlockSpec((1, N_HALF, DIL_SPAN, LANES), lambda b, s: (b, 0, s, 0))
    prev = pl.BlockSpec((1, N_HALF, DIL_SPAN, LANES), lambda b, s: (b, 0, jnp.maximum(s - 1, 0), 0))
    bias = jnp.asarray(_dilated_bias())
    return pl.pallas_call(
        _dilated_kernel,
        grid=(B, S // DIL_SPAN),
        in_specs=[cur, prev, cur, prev, cur, pl.BlockSpec(bias.shape, lambda b, s: (0,) * bias.ndim)],
        out_specs=cur,
        out_shape=jax.ShapeDtypeStruct((B, N_HALF, S, LANES), F32),
        scratch_shapes=[pltpu.VMEM((N_HALF, 2 * DIL_SPAN, LANES), F32),
                        pltpu.VMEM((N_HALF, 2 * DIL_SPAN, LANES), F32),
                        pltpu.VMEM((N_HALF, DIL_SPAN, LANES), F32), pltpu.VMEM((N_HALF, DIL_SPAN, LANES), F32),
                        pltpu.VMEM((N_HALF, DIL_SPAN, LANES), F32)],
        compiler_params=_params(2),
        name="dilated",
    )(cq, ck, ck, cv, cv, bias)


SWA_SPAN = 1024
SWA_GROUP = 2
SWA_HEAD_ORDER = (0, 2, 1, 3)


def _swa_bias():
    qi = np.arange(Q_BLOCK)[:, None]
    kidx = np.arange(2 * Q_BLOCK)[None, :]
    rel = qi + Q_BLOCK - kidx
    valid = (rel >= 0) & (rel < SWA_WINDOW)
    out = np.empty((2, N_HEADS, Q_BLOCK, 2 * Q_BLOCK), np.float32)
    for h in range(N_HEADS):
        bias = -SLOPE_D[h] * np.float32(LOG2E) * rel.astype(np.float32)
        out[0, h] = np.where(valid, bias, NEG)
        out[1, h] = np.where(valid & (kidx >= Q_BLOCK), bias, NEG)
    return out


def _swa_kernel(sink_ref, q_ref, k_ref, v_ref, bias_ref, o_ref):
    span = pl.program_id(1)
    units = SWA_SPAN // Q_BLOCK
    half = lax.broadcasted_iota(jnp.int32, (1, LANES), 1) // HEAD_DIM
    lo = half == 0
    kv_mask = [jnp.where(half == g, 1.0, 0.0).astype(BF16) for g in range(SWA_KV_HEADS)]

    def group(g, _):
        fronts = []
        for uu in range(SWA_GROUP):
            u = g * SWA_GROUP + uu
            n = span * units + u
            lo_start = pl.multiple_of(jnp.maximum(n - 1, 0) * Q_BLOCK, Q_BLOCK)
            hi_start = pl.multiple_of(n * Q_BLOCK, Q_BLOCK)
            qstart = pl.multiple_of(u * Q_BLOCK, Q_BLOCK)
            q = q_ref[0, pl.ds(qstart, Q_BLOCK), :]
            k2 = jnp.concatenate([k_ref[0, pl.ds(lo_start, Q_BLOCK), :], k_ref[0, pl.ds(hi_start, Q_BLOCK), :]],
                                 axis=0)
            v2 = jnp.concatenate([v_ref[0, pl.ds(lo_start, Q_BLOCK), :], v_ref[0, pl.ds(hi_start, Q_BLOCK), :]],
                                 axis=0)
            q4 = jnp.concatenate([q[:, (h % 2) * LANES:(h % 2 + 1) * LANES] * kv_mask[h // 2]
                                  for h in range(N_HEADS)], axis=0)
            s4 = lax.dot_general(q4, k2, _NT, preferred_element_type=F32)
            fronts.append((qstart, s4, v2, jnp.where(n == 0, 1, 0)))
        for qstart, s4, v2, variant in fronts:
            es, ls = [], []
            for h in range(N_HEADS):
                s = s4[h * Q_BLOCK:(h + 1) * Q_BLOCK, :] + bias_ref[variant, h]
                sink = sink_ref[h] * LOG2E
                m = jnp.maximum(jnp.max(s, axis=-1, keepdims=True), sink)
                e = jnp.exp2(s - m)
                ls.append(jnp.sum(e, axis=-1, keepdims=True) + jnp.exp2(sink - m))
                es.append(e.astype(BF16))
            o4 = jnp.dot(jnp.concatenate(es, axis=0), v2, preferred_element_type=F32)
            tiles = []
            for t in range(N_HALF):
                a, b = t, t + 2
                tiles.append(jnp.where(lo, o4[a * Q_BLOCK:(a + 1) * Q_BLOCK, :] / ls[a],
                                       o4[b * Q_BLOCK:(b + 1) * Q_BLOCK, :] / ls[b]))
            o_ref[0, pl.ds(qstart, Q_BLOCK), :] = jnp.concatenate(tiles, axis=-1)
        return 0

    lax.fori_loop(0, units // SWA_GROUP, group, 0)


def _swa(dq, dk, dv, sinks):
    B, S, _ = dq.shape
    bias = jnp.asarray(_swa_bias())
    return pl.pallas_call(
        _swa_kernel,
        grid=(B, S // SWA_SPAN),
        in_specs=[pl.BlockSpec(memory_space=pltpu.SMEM),
                  pl.BlockSpec((1, SWA_SPAN, GROUP), lambda b, i: (b, i, 0)),
                  pl.BlockSpec((1, S, SWA_KV_WIDTH), lambda b, i: (b, 0, 0)),
                  pl.BlockSpec((1, S, SWA_KV_WIDTH), lambda b, i: (b, 0, 0)),
                  pl.BlockSpec(bias.shape, lambda b, i: (0,) * bias.ndim)],
        out_specs=pl.BlockSpec((1, SWA_SPAN, GROUP), lambda b, i: (b, i, 0)),
        out_shape=jax.ShapeDtypeStruct((B, S, GROUP), F32),
        compiler_params=_params(2),
        name="swa",
    )(sinks, dq, dk, dv, bias)


def _row_norm(y, g_row):
    return y * lax.rsqrt(jnp.mean(y * y, axis=-1, keepdims=True) + EPS) * g_row


def _tail_kernel(x_ref, oat_ref, obt_ref, oc_ref, od_ref, gg_ref, wo_ref, gm_ref, wup_ref, wdn_ref, out_ref):
    gg = gg_ref[...]

    def col_norm_t(yt, g_row):
        y = (yt * lax.rsqrt(jnp.mean(yt * yt, axis=0, keepdims=True) + EPS)).T
        return y * g_row

    ga = col_norm_t(oat_ref[0], gg[0:1, :])
    gb = col_norm_t(obt_ref[0], gg[1:2, :])
    gc = _row_norm(jnp.concatenate([oc_ref[0, c] for c in range(N_HALF)], axis=-1), gg[2:3, :])
    gd = _row_norm(od_ref[0], gg[3:4, :])
    mixed = jnp.concatenate([ga, gb, gc, gd], axis=-1).astype(BF16)
    x1 = x_ref[0] + jnp.dot(mixed, wo_ref[...], preferred_element_type=F32)
    xn = _row_norm(x1, gm_ref[...]).astype(BF16)
    u = jnp.maximum(jnp.dot(xn, wup_ref[...], preferred_element_type=F32), 0.0)
    out_ref[0] = x1 + jnp.dot((u * u).astype(BF16), wdn_ref[...], preferred_element_type=F32)


def _tail(x, oat, obt, oc, od, gg, wo, gm, wup, wdn):
    B, S, _ = x.shape
    tm = TAIL_TM
    nat = lambda w: pl.BlockSpec((1, tm, w), lambda b, t: (b, t, 0))
    ft = pl.BlockSpec((1, GROUP, tm), lambda b, t: (b, 0, t))
    const = lambda a: pl.BlockSpec(a.shape, lambda b, t: (0,) * a.ndim, pipeline_mode=pl.Buffered(1))
    return pl.pallas_call(
        _tail_kernel,
        grid=(B, S // tm),
        in_specs=[nat(D_MODEL), ft, ft, pl.BlockSpec((1, N_HALF, tm, LANES), lambda b, t: (b, 0, t, 0)),
                  nat(GROUP), const(gg), const(wo), const(gm),
                  const(wup), const(wdn)],
        out_specs=nat(D_MODEL),
        out_shape=jax.ShapeDtypeStruct((B, S, D_MODEL), F32),
        compiler_params=_params(2),
        name="tail",
    )(x, oat, obt, oc, od, gg, wo, gm, wup, wdn)


def _rope_tables_t(S):
    inv = 1.0 / (ROPE_THETA ** (jnp.arange(0, MLA_ROPE, 2, dtype=F32) / MLA_ROPE))
    ang = inv[:, None] * jnp.arange(S, dtype=F32)[None, :]
    return jnp.cos(ang), jnp.sin(ang)


def _moba_aug_tables(S):
    pos = np.arange(S, dtype=np.float32) % MOBA_BLOCK
    kpos = np.zeros((AUG_ROWS, S), np.float32)
    kpos[:3] = pos
    aslope = np.zeros((N_HEADS, AUG_ROWS, 1), np.float32)
    rest = (SLOPE_A * np.float32(LOG2E)).astype(np.float32)
    for r in range(3):
        piece = rest.astype(BF16).astype(np.float32)
        aslope[:, r, 0] = piece
        rest = rest - piece
    return jnp.asarray(kpos), jnp.asarray(aslope)


def _layer(x, cos_t, sin_t, kpos_t, aslope, attn_norm_g, w_in, moba_q_g, moba_k_g, mla_qlat_g, mla_kvlat_g, mla_w_uq, mla_w_ukv,
           mla_q_g, mla_k_g, dil_q_g, dil_k_g, swa_q_g, swa_k_g, swa_sinks, group_out_g, w_o, mlp_norm_g,
           w_up, w_down):
    B, S, _ = x.shape
    col = lambda g: g.reshape(-1, 1)
    (aqt, aq32t, ak, akm, avt, bqt, bk, bvt, cq, ck, cv, dq, dk, dv) = _inproj(
        x, attn_norm_g.reshape(1, -1), w_in.T.astype(BF16), col(moba_q_g), col(moba_k_g), col(mla_qlat_g),
        col(mla_kvlat_g), mla_w_uq.T.astype(BF16), mla_w_ukv.T.astype(BF16), col(mla_q_g), col(mla_k_g),
        col(dil_q_g), col(dil_k_g), col(swa_q_g), col(swa_k_g), cos_t, sin_t, kpos_t, aslope)
    nb = S // MOBA_BLOCK
    kmean_h = akm.reshape(B, nb, N_HEADS, HEAD_DIM).transpose(0, 2, 1, 3)
    oat = _moba(aqt, aq32t, ak, avt, kmean_h).reshape(B, GROUP, S)
    obt = _mla(bqt, bk, bvt).reshape(B, GROUP, S)
    oc = _dilated(cq, ck, cv)
    od = _swa(dq, dk, dv, swa_sinks)
    perm = np.concatenate([np.arange(HEAD_DIM) + HEAD_DIM * h for h in SWA_HEAD_ORDER])
    gg = group_out_g.at[3].set(group_out_g[3][perm])
    wo = jnp.concatenate([w_o[:3 * GROUP], w_o[3 * GROUP + perm]], axis=0)
    return _tail(x, oat, obt, oc, od, gg, wo.astype(BF16), mlp_norm_g.reshape(1, -1),
                 w_up.astype(BF16), w_down.astype(BF16))


def kernel(x, attn_norm_g, w_in, moba_q_g, moba_k_g, mla_qlat_g, mla_kvlat_g, mla_w_uq, mla_w_ukv, mla_q_g,
           mla_k_g, dil_q_g, dil_k_g, swa_q_g, swa_k_g, swa_sinks, group_out_g, w_o, mlp_norm_g, w_up, w_down):
    S = x.shape[1]
    assert S % max(d * Q_BLOCK for _, d in DILATED_BRANCHES) == 0 and S % INPROJ_TM == 0
    cos_t, sin_t = _rope_tables_t(S)
    kpos_t, aslope = _moba_aug_tables(S)
    params = (attn_norm_g, w_in, moba_q_g, moba_k_g, mla_qlat_g, mla_kvlat_g, mla_w_uq, mla_w_ukv, mla_q_g,
              mla_k_g, dil_q_g, dil_k_g, swa_q_g, swa_k_g, swa_sinks, group_out_g, w_o, mlp_norm_g, w_up, w_down)
    for l in range(attn_norm_g.shape[0]):
        x = _layer(x, cos_t, sin_t, kpos_t, aslope, *[p[l] for p in params])
    return x
```

```python
import numpy as np
import jax
import jax.numpy as jnp
from jax import lax
from jax.experimental import pallas as pl
from jax.experimental.pallas import tpu as pltpu

F32 = jnp.float32
BF16 = jnp.bfloat16

D_MODEL = 1024
HEAD_DIM = 64
N_HEADS = 4
GROUP = N_HEADS * HEAD_DIM
LANES = 128
N_HALF = GROUP // LANES
MOBA_BLOCK = 256
MOBA_TOPK = 3
MLA_Q_RANK = 256
MLA_KV_RANK = 128
MLA_NOPE = 64
MLA_ROPE = 32
MLA_QK = MLA_NOPE + MLA_ROPE
ROPE_THETA = 10000.0
DILATED_BRANCHES = ((128, 1), (512, 4), (2048, 16))
Q_BLOCK = 128
SWA_WINDOW = 128
SWA_KV_HEADS = 2
SWA_KV_WIDTH = SWA_KV_HEADS * HEAD_DIM
D_FF = 4 * D_MODEL
EPS = 1e-6
NEG = -1e30

_WIDTHS = (GROUP, GROUP, GROUP, MLA_Q_RANK, MLA_KV_RANK, MLA_ROPE,
           GROUP, GROUP, GROUP, GROUP, SWA_KV_WIDTH, SWA_KV_WIDTH)
_OFFS = tuple(int(v) for v in np.cumsum((0,) + _WIDTHS))
IN_COLS = _OFFS[-1]
(_A_Q, _A_K, _A_V, _B_QL, _B_KVL, _B_KR, _C_Q, _C_K, _C_V, _D_Q, _D_K, _D_V) = _OFFS[:-1]

VMEM_LIMIT = 56 * 1024 * 1024

INPROJ_TM = 512
TAIL_TM = 512
ATT_TK = 256
ATT_TQ = 2 * ATT_TK
ATT_DK = 128
ATT_DV = HEAD_DIM + 16
AUG_ROWS = 8
LOG2E = 1.4426950408889634
FLASH_PAIRS = 2

_NT = (((1,), (1,)), ((), ()))


def _alibi_slopes():
    n = 3 * N_HEADS
    idx = np.arange(1, n + 1, dtype=np.float32).reshape(N_HEADS, 3)
    s = np.exp2(-8.0 * idx / n).astype(np.float32)
    return s[:, 0], s[:, 1], s[:, 2]


SLOPE_A, SLOPE_C, SLOPE_D = _alibi_slopes()


def _params(n_axes):
    return pltpu.CompilerParams(dimension_semantics=("arbitrary",) * n_axes,
                                vmem_limit_bytes=VMEM_LIMIT)


def _head_norm_t(sec, g_col, n_heads, width):
    outs = []
    for h in range(n_heads):
        s = sec[h * width:(h + 1) * width, :]
        ms = jnp.sum(s * s, axis=0, keepdims=True) * (1.0 / width)
        outs.append(s * lax.rsqrt(ms + EPS) * g_col)
    return outs


def _inproj_kernel(x_ref, gx_ref, w1t_ref, gaq_ref, gak_ref, gql_ref, gkvl_ref, wuqt_ref, wukvt_ref,
                   gbq_ref, gbk_ref, gcq_ref, gck_ref, gdq_ref, gdk_ref, cos_ref, sin_ref, kpos_ref, aslope_ref,
                   aqt_ref, aq32t_ref, ak_ref, akm_ref, avt_ref,
                   bqt_ref, bk_ref, bvt_ref,
                   cq_ref, ck_ref, cv_ref, dq_ref, dk_ref, dv_ref,
                   h_scr):
    tm = x_ref.shape[1]
    x = x_ref[0]
    ms = jnp.mean(x * x, axis=-1, keepdims=True)
    xn = (x * lax.rsqrt(ms + EPS) * gx_ref[...]).astype(BF16)
    h_scr[...] = lax.dot_general(w1t_ref[...], xn, _NT, preferred_element_type=F32)

    scale = HEAD_DIM ** -0.5

    ones_rows = jnp.ones((ATT_DV - HEAD_DIM, ATT_TK), F32)

    def store_vt(ref, h, vh):
        for c in range(tm // ATT_TK):
            ref[0, h, c] = jnp.concatenate([vh[:, c * ATT_TK:(c + 1) * ATT_TK], ones_rows], axis=0).astype(BF16)

    pad_rows = jnp.zeros((ATT_DK - HEAD_DIM - AUG_ROWS, tm), F32)
    qa = _head_norm_t(h_scr[_A_Q:_A_Q + GROUP, :], gaq_ref[...], N_HEADS, HEAD_DIM)
    ka = _head_norm_t(h_scr[_A_K:_A_K + GROUP, :], gak_ref[...], N_HEADS, HEAD_DIM)
    ka_nat = jnp.concatenate(ka, axis=0).T
    for c in range(tm // MOBA_BLOCK):
        akm_ref[0, c] = jnp.sum(ka_nat[c * MOBA_BLOCK:(c + 1) * MOBA_BLOCK, :], axis=0,
                                keepdims=True) * (1.0 / MOBA_BLOCK)
    kpos = kpos_ref[...]
    for h in range(N_HEADS):
        aq32t_ref[0, h] = qa[h]
        slope_rows = jnp.broadcast_to(aslope_ref[h], (AUG_ROWS, tm))
        aqt_ref[0, h] = jnp.concatenate([qa[h] * (scale * LOG2E), slope_rows, pad_rows], axis=0).astype(BF16)
        ak_ref[0, h] = jnp.concatenate([ka[h], kpos, pad_rows], axis=0).T.astype(BF16)
        store_vt(avt_ref, h, h_scr[_A_V + h * HEAD_DIM:_A_V + (h + 1) * HEAD_DIM, :])

    cos = cos_ref[...]
    sin = sin_ref[...]
    half = MLA_ROPE // 2

    def rope_pad(t, sc):
        x1 = t[MLA_NOPE:MLA_NOPE + half, :]
        x2 = t[MLA_NOPE + half:MLA_QK, :]
        return jnp.concatenate([t[:MLA_NOPE, :] * sc, (x1 * cos - x2 * sin) * sc, (x1 * sin + x2 * cos) * sc,
                                jnp.zeros((ATT_DK - MLA_QK, tm), F32)], axis=0)

    ql = h_scr[_B_QL:_B_QL + MLA_Q_RANK, :]
    ql = ql * lax.rsqrt(jnp.sum(ql * ql, axis=0, keepdims=True) * (1.0 / MLA_Q_RANK) + EPS) * gql_ref[...]
    qb = jnp.dot(wuqt_ref[...], ql.astype(BF16), preferred_element_type=F32)
    qb = _head_norm_t(qb, gbq_ref[...], N_HEADS, MLA_QK)
    kvl = h_scr[_B_KVL:_B_KVL + MLA_KV_RANK, :]
    kvl = kvl * lax.rsqrt(jnp.sum(kvl * kvl, axis=0, keepdims=True) * (1.0 / MLA_KV_RANK) + EPS) * gkvl_ref[...]
    kvb = jnp.dot(wukvt_ref[...], kvl.astype(BF16), preferred_element_type=F32)
    kr = h_scr[_B_KR:_B_KR + MLA_ROPE, :]
    gbk = gbk_ref[...]
    for h in range(N_HEADS):
        bqt_ref[0, h] = rope_pad(qb[h], MLA_QK ** -0.5 * LOG2E).astype(BF16)
        kh = jnp.concatenate([kvb[h * 2 * HEAD_DIM:h * 2 * HEAD_DIM + MLA_NOPE, :], kr], axis=0)
        kh = kh * lax.rsqrt(jnp.sum(kh * kh, axis=0, keepdims=True) * (1.0 / MLA_QK) + EPS) * gbk
        bk_ref[0, h] = rope_pad(kh, 1.0).T.astype(BF16)
        store_vt(bvt_ref, h, kvb[h * 2 * HEAD_DIM + MLA_NOPE:(h + 1) * 2 * HEAD_DIM, :])

    qc = jnp.concatenate(_head_norm_t(h_scr[_C_Q:_C_Q + GROUP, :], gcq_ref[...], N_HEADS, HEAD_DIM), axis=0)
    kc = jnp.concatenate(_head_norm_t(h_scr[_C_K:_C_K + GROUP, :], gck_ref[...], N_HEADS, HEAD_DIM), axis=0)
    for ref, val in ((cq_ref, qc * (scale * LOG2E)), (ck_ref, kc), (cv_ref, h_scr[_C_V:_C_V + GROUP, :])):
        for c in range(N_HALF):
            ref[0, c] = val[c * LANES:(c + 1) * LANES, :].T

    qd = _head_norm_t(h_scr[_D_Q:_D_Q + GROUP, :], gdq_ref[...], N_HEADS, HEAD_DIM)
    qd = jnp.concatenate([qd[h] for h in SWA_HEAD_ORDER], axis=0)
    dq_ref[0] = (qd * (scale * LOG2E)).T.astype(BF16)
    kd = jnp.concatenate(_head_norm_t(h_scr[_D_K:_D_K + SWA_KV_WIDTH, :], gdk_ref[...], SWA_KV_HEADS, HEAD_DIM),
                         axis=0)
    dk_ref[0] = kd.T.astype(BF16)
    dv_ref[0] = h_scr[_D_V:_D_V + SWA_KV_WIDTH, :].T.astype(BF16)


def _inproj(x, gx, w1t, gaq, gak, gql, gkvl, wuqt, wukvt, gbq, gbk, gcq, gck, gdq, gdk, cos_t, sin_t, kpos_t,
            aslope):
    B, S, _ = x.shape
    tm = INPROJ_TM
    nb = S // ATT_TK
    cpt = tm // ATT_TK
    H = N_HEADS
    full = lambda a: pl.BlockSpec(a.shape, lambda b, t: (0,) * a.ndim)
    in_specs = [pl.BlockSpec((1, tm, D_MODEL), lambda b, t: (b, t, 0)), full(gx), full(w1t), full(gaq), full(gak),
                full(gql), full(gkvl), full(wuqt), full(wukvt), full(gbq), full(gbk), full(gcq), full(gck),
                full(gdq), full(gdk),
                pl.BlockSpec((MLA_ROPE // 2, tm), lambda b, t: (0, t)),
                pl.BlockSpec((MLA_ROPE // 2, tm), lambda b, t: (0, t)),
                pl.BlockSpec((AUG_ROWS, tm), lambda b, t: (0, t)), full(aslope)]
    head_t = lambda w: pl.BlockSpec((1, H, w, tm), lambda b, t: (b, 0, 0, t))
    head_n = pl.BlockSpec((1, H, tm, ATT_DK), lambda b, t: (b, 0, t, 0))
    vt_spec = pl.BlockSpec((1, H, cpt, ATT_DV, ATT_TK), lambda b, t: (b, 0, t, 0, 0))
    nat = lambda w: pl.BlockSpec((1, tm, w), lambda b, t: (b, t, 0))
    halves = pl.BlockSpec((1, N_HALF, tm, LANES), lambda b, t: (b, 0, t, 0))
    out_shape = [
        jax.ShapeDtypeStruct((B, H, ATT_DK, S), BF16),
        jax.ShapeDtypeStruct((B, H, HEAD_DIM, S), F32),
        jax.ShapeDtypeStruct((B, H, S, ATT_DK), BF16),
        jax.ShapeDtypeStruct((B, nb, 1, GROUP), F32),
        jax.ShapeDtypeStruct((B, H, nb, ATT_DV, ATT_TK), BF16),
        jax.ShapeDtypeStruct((B, H, ATT_DK, S), BF16),
        jax.ShapeDtypeStruct((B, H, S, ATT_DK), BF16),
        jax.ShapeDtypeStruct((B, H, nb, ATT_DV, ATT_TK), BF16),
        jax.ShapeDtypeStruct((B, N_HALF, S, LANES), F32),
        jax.ShapeDtypeStruct((B, N_HALF, S, LANES), F32),
        jax.ShapeDtypeStruct((B, N_HALF, S, LANES), F32),
        jax.ShapeDtypeStruct((B, S, GROUP), BF16),
        jax.ShapeDtypeStruct((B, S, SWA_KV_WIDTH), BF16),
        jax.ShapeDtypeStruct((B, S, SWA_KV_WIDTH), BF16),
    ]
    out_specs = [head_t(ATT_DK), head_t(HEAD_DIM), head_n,
                 pl.BlockSpec((1, tm // MOBA_BLOCK, 1, GROUP), lambda b, t: (b, t, 0, 0)), vt_spec,
                 head_t(ATT_DK), head_n, vt_spec,
                 halves, halves, halves, nat(GROUP), nat(SWA_KV_WIDTH), nat(SWA_KV_WIDTH)]
    return pl.pallas_call(
        _inproj_kernel,
        grid=(B, S // tm),
        in_specs=in_specs,
        out_specs=out_specs,
        out_shape=out_shape,
        scratch_shapes=[pltpu.VMEM((IN_COLS, tm), F32)],
        compiler_params=_params(2),
        name="inproj",
    )(x, gx, w1t, gaq, gak, gql, gkvl, wuqt, wukvt, gbq, gbk, gcq, gck, gdq, gdk, cos_t, sin_t, kpos_t, aslope)


def _flash_scratch(tq):
    s_buf, p_buf = pltpu.VMEM((N_HEADS, ATT_TK, tq), F32), pltpu.VMEM((N_HEADS, ATT_TK, tq), BF16)
    return [s_buf, s_buf, p_buf, p_buf, pltpu.VMEM((N_HEADS, ATT_DV, tq), F32)]


def _flash_heads(qt_ref, k_ref, vt_ref, o_ref, s_bufs, p_bufs, acc_scr, rowb_scr, i):
    tq = o_ref.shape[3]
    n_tail = tq // ATT_TK
    assert n_tail == 2
    n_past = i * n_tail
    heads = range(N_HEADS)

    def block_of(pos):
        return jnp.where(pos < n_tail, n_past + pos, pos - n_tail)

    def head_scores(h, slot, blk):
        kb = k_ref[0, h, pl.ds(pl.multiple_of(blk * ATT_TK, ATT_TK), ATT_TK), :]
        s_bufs[slot][h] = jnp.dot(kb, qt_ref[0, h], preferred_element_type=F32)

    def stage_scores(slot, blk):
        for h in heads:
            head_scores(h, slot, blk)

    def stage_softmax(slot, blk, ms, causal=None):
        new_ms, alphas = [], []
        for h in heads:
            m_parts, a_parts = [], []
            rv_row = None if rowb_scr is None else rowb_scr[h, pl.ds(blk, 1), :]
            for c in range(tq // LANES):
                cols = slice(c * LANES, (c + 1) * LANES)
                st = s_bufs[slot][h, :, cols]
                if causal is not None:
                    st = jnp.where(causal[:, cols], st, 2 * NEG)
                cm = jnp.max(st, axis=0, keepdims=True)
                m_old = ms[h][:, cols]
                if rowb_scr is None:
                    m_new = jnp.maximum(m_old, cm)
                    shift = m_new
                else:
                    rv = rv_row[:, cols]
                    m_new = jnp.maximum(m_old, cm + rv)
                    shift = m_new - rv
                a_parts.append(jnp.exp2(m_old - m_new))
                m_parts.append(m_new)
                p_bufs[slot][h, :, cols] = jnp.exp2(st - shift).astype(BF16)
            new_ms.append(jnp.concatenate(m_parts, axis=1))
            alphas.append(jnp.concatenate(a_parts, axis=1))
        return tuple(new_ms), tuple(alphas)

    def head_values(h, slot, blk, alphas):
        acc_scr[h] = alphas[h] * acc_scr[h] + jnp.dot(vt_ref[0, h, blk], p_bufs[slot][h],
                                                      preferred_element_type=F32)

    def stage_values(slot, blk, alphas):
        for h in heads:
            head_values(h, slot, blk, alphas)

    krow = lax.broadcasted_iota(jnp.int32, (ATT_TK, tq), 0)
    ti = lax.broadcasted_iota(jnp.int32, (ATT_TK, tq), 1)
    for h in heads:
        acc_scr[h] = jnp.zeros((ATT_DV, tq), F32)
    stage_scores(0, n_past)
    stage_scores(1, n_past + 1)
    ms = tuple(jnp.full((1, tq), NEG, F32) for _ in heads)
    ms, alphas = stage_softmax(0, n_past, ms, causal=ti >= krow)
    stage_values(0, n_past, alphas)
    ms, alphas = stage_softmax(1, n_past + 1, ms, causal=ti >= krow + ATT_TK)
    stage_scores(0, 0)

    def step(pos, slot, ms, alphas):
        ms, new_alphas = stage_softmax(1 - slot, pos + 1 - n_tail, ms)
        blk = block_of(pos)
        for h in heads:
            head_values(h, slot, blk, alphas)
            head_scores(h, slot, pos)
        return ms, new_alphas

    def pair(first, carry):
        ms, alphas = step(first, 1, *carry)
        return step(first + 1, 0, ms, alphas)

    def trip(t, carry):
        for u in range(FLASH_PAIRS):
            carry = pair(2 * (FLASH_PAIRS * t + u) + 1, carry)
        return carry

    carry = lax.fori_loop(0, i // FLASH_PAIRS, trip, (ms, alphas))
    ms, alphas = lax.fori_loop(i - i % FLASH_PAIRS, i, lambda u, c: pair(2 * u + 1, c), carry)
    last = n_past + 1
    stage_values(1, block_of(last), alphas)
    for h in heads:
        acc = acc_scr[h]
        o_ref[0, h] = acc[:HEAD_DIM, :] / acc[HEAD_DIM:HEAD_DIM + 1, :]


def _moba_kernel(qt_ref, q32t_ref, k_ref, vt_ref, kmean_ref, o_ref, rowb_scr, s0, s1, p0, p1, acc_scr):
    i = pl.program_id(1)
    nb = kmean_ref.shape[2]
    tq = qt_ref.shape[3]
    blk = lax.broadcasted_iota(jnp.int32, (nb, tq), 0)
    col = lax.broadcasted_iota(jnp.int32, (nb, tq), 1)
    qblk = i * (tq // MOBA_BLOCK) + col // MOBA_BLOCK
    past = blk < qblk
    dist0 = (i * tq + col - blk * MOBA_BLOCK).astype(F32)
    for h in range(N_HEADS):
        gate = jnp.dot(kmean_ref[0, h], q32t_ref[0, h], preferred_element_type=F32,
                       precision=lax.Precision.HIGHEST)
        gate = jnp.where(past, gate, NEG)
        sel = blk == qblk
        for _ in range(MOBA_TOPK):
            best = jnp.max(gate, axis=0, keepdims=True)
            first = jnp.min(jnp.where(gate == best, blk, nb), axis=0, keepdims=True)
            pick = blk == first
            sel = jnp.logical_or(sel, jnp.logical_and(pick, past))
            gate = jnp.where(pick, -jnp.inf, gate)
        rowb_scr[h] = jnp.where(sel, (-float(SLOPE_A[h]) * LOG2E) * dist0, 2 * NEG)
    _flash_heads(qt_ref, k_ref, vt_ref, o_ref, (s0, s1), (p0, p1), acc_scr, rowb_scr, i)


def _flash_specs(S, tq):
    H, nb = N_HEADS, S // ATT_TK
    return ([pl.BlockSpec((1, H, ATT_DK, tq), lambda b, i: (b, 0, 0, i)),
             pl.BlockSpec((1, H, S, ATT_DK), lambda b, i: (b, 0, 0, 0)),
             pl.BlockSpec((1, H, nb, ATT_DV, ATT_TK), lambda b, i: (b, 0, 0, 0, 0))],
            pl.BlockSpec((1, H, HEAD_DIM, tq), lambda b, i: (b, 0, 0, i)))


def _moba(aqt, aq32t, ak, avt, kmean_h):
    B, H, _, S = aqt.shape
    tq = ATT_TQ
    nb = S // ATT_TK
    (q_spec, k_spec, vt_spec), o_spec = _flash_specs(S, tq)
    return pl.pallas_call(
        _moba_kernel,
        grid=(B, S // tq),
        in_specs=[q_spec, pl.BlockSpec((1, H, HEAD_DIM, tq), lambda b, i: (b, 0, 0, i)), k_spec, vt_spec,
                  pl.BlockSpec((1, H, nb, HEAD_DIM), lambda b, i: (b, 0, 0, 0))],
        out_specs=o_spec,
        out_shape=jax.ShapeDtypeStruct((B, H, HEAD_DIM, S), F32),
        scratch_shapes=[pltpu.VMEM((H, nb, tq), F32)] + _flash_scratch(tq),
        compiler_params=_params(2),
        name="moba",
    )(aqt, aq32t, ak, avt, kmean_h)


def _mla_kernel(qt_ref, k_ref, vt_ref, o_ref, s0, s1, p0, p1, acc_scr):
    _flash_heads(qt_ref, k_ref, vt_ref, o_ref, (s0, s1), (p0, p1), acc_scr, None, pl.program_id(1))


def _mla(bqt, bk, bvt):
    B, H, _, S = bqt.shape
    tq = ATT_TQ
    in_specs, o_spec = _flash_specs(S, tq)
    return pl.pallas_call(
        _mla_kernel,
        grid=(B, S // tq),
        in_specs=in_specs,
        out_specs=o_spec,
        out_shape=jax.ShapeDtypeStruct((B, H, HEAD_DIM, S), F32),
        scratch_shapes=_flash_scratch(tq),
        compiler_params=_params(2),
        name="mla",
    )(bqt, bk, bvt)


DIL_SPAN = max(d for _, d in DILATED_BRANCHES) * Q_BLOCK
DIL_UNITS = DIL_SPAN // Q_BLOCK
DIL_GROUP = (4, 2, 2)


def _dilated_bias():
    qi = np.arange(Q_BLOCK)[:, None]
    kidx = np.arange(2 * Q_BLOCK)[None, :]
    rel = qi + Q_BLOCK - kidx
    out = np.empty((2, len(DILATED_BRANCHES), N_HEADS, Q_BLOCK, 2 * Q_BLOCK), np.float32)
    for bi, (window, d) in enumerate(DILATED_BRANCHES):
        valid = (rel >= 0) & (rel <= window // d)
        for h in range(N_HEADS):
            bias = -SLOPE_C[h] * np.float32(LOG2E) * (d * rel).astype(np.float32)
            out[0, bi, h] = np.where(valid, bias, NEG)
            out[1, bi, h] = np.where(valid & (kidx >= Q_BLOCK), bias, NEG)
    return out


def _rows_load(ref, lead, start, size, stride):
    return jnp.concatenate([ref[lead + (c, pl.ds(start, size, stride=stride), slice(None))]
                            for c in range(N_HALF)], axis=-1)


def _rows_store(ref, lead, start, size, stride, val):
    for c in range(N_HALF):
        ref[lead + (c, pl.ds(start, size, stride=stride), slice(None))] = val[:, c * LANES:(c + 1) * LANES]


def _dilated_kernel(q_ref, kp_ref, kc_ref, vp_ref, vc_ref, bias_ref, o_ref, kbuf, vbuf, m_scr, den_scr, num_scr):
    span = pl.program_id(1)
    kbuf[:, 0:DIL_SPAN, :] = kp_ref[0]
    kbuf[:, DIL_SPAN:, :] = kc_ref[0]
    vbuf[:, 0:DIL_SPAN, :] = vp_ref[0]
    vbuf[:, DIL_SPAN:, :] = vc_ref[0]
    lane_head = lax.broadcasted_iota(jnp.int32, (1, GROUP), 1) // HEAD_DIM
    hmask = [lane_head == h for h in range(N_HEADS)]
    hmask_f = [m.astype(F32) for m in hmask]
    first_span = jnp.where(span == 0, 1, 0)

    def per_head(cols):
        out = cols[N_HEADS - 1]
        for h in range(N_HEADS - 2, -1, -1):
            out = jnp.where(hmask[h], cols[h], out)
        return out

    order = sorted(range(len(DILATED_BRANCHES)), key=lambda b: -DILATED_BRANCHES[b][1])
    for bi in order:
        d, n_group = DILATED_BRANCHES[bi][1], DIL_GROUP[bi]
        first, last = bi == order[0], bi == order[-1]

        def group(g, _, bi=bi, d=d, n_group=n_group, first=first, last=last):
            fronts = []
            for uu in range(n_group):
                u = g * n_group + uu
                r, n = u % d, u // d
                qstart = n * (Q_BLOCK * d) + r
                kstart = DIL_SPAN + qstart - Q_BLOCK * d
                q = _rows_load(q_ref, (0,), qstart, Q_BLOCK, d)
                k2 = _rows_load(kbuf, (), kstart, 2 * Q_BLOCK, d).astype(BF16)
                v2 = _rows_load(vbuf, (), kstart, 2 * Q_BLOCK, d).astype(BF16)
                q4 = jnp.concatenate([(q * hmask_f[h]).astype(BF16) for h in range(N_HEADS)], axis=0)
                s4 = lax.dot_general(q4, k2, _NT, preferred_element_type=F32)
                variant = jnp.where(n == 0, first_span, 0)
                fronts.append((qstart, s4, v2, variant))
            for qstart, s4, v2, variant in fronts:
                es, ms, ls = [], [], []
                for h in range(N_HEADS):
                    s = s4[h * Q_BLOCK:(h + 1) * Q_BLOCK, :] + bias_ref[variant, bi, h]
                    m = jnp.max(s, axis=-1, keepdims=True)
                    e = jnp.exp2(s - m)
                    ls.append(jnp.sum(e, axis=-1, keepdims=True))
                    ms.append(m)
                    es.append(e.astype(BF16))
                o4 = jnp.dot(jnp.concatenate(es, axis=0), v2, preferred_element_type=F32)
                o = o4[(N_HEADS - 1) * Q_BLOCK:, :]
                for h in range(N_HEADS - 2, -1, -1):
                    o = jnp.where(hmask[h], o4[h * Q_BLOCK:(h + 1) * Q_BLOCK, :], o)
                m_b, l_b = per_head(ms), per_head(ls)
                at = ((), qstart, Q_BLOCK, d)
                if first:
                    _rows_store(m_scr, *at, m_b)
                    _rows_store(num_scr, *at, o)
                    _rows_store(den_scr, *at, l_b)
                else:
                    m_old = _rows_load(m_scr, *at)
                    m_new = jnp.maximum(m_old, m_b)
                    a, b = jnp.exp2(m_old - m_new), jnp.exp2(m_b - m_new)
                    num = a * _rows_load(num_scr, *at) + b * o
                    den = a * _rows_load(den_scr, *at) + b * l_b
                    if last:
                        _rows_store(o_ref, (0,), qstart, Q_BLOCK, d, num / den)
                    else:
                        _rows_store(m_scr, *at, m_new)
                        _rows_store(num_scr, *at, num)
                        _rows_store(den_scr, *at, den)
            return 0

        lax.fori_loop(0, DIL_UNITS // n_group, group, 0)


def _dilated(cq, ck, cv):
    B, _, S, _ = cq.shape
    cur = pl.BlockSpec((1, N_HALF, DIL_SPAN, LANES), lambda b, s: (b, 0, s, 0))
    prev = pl.BlockSpec((1, N_HALF, DIL_SPAN, LANES), lambda b, s: (b, 0, jnp.maximum(s - 1, 0), 0))
    bias = jnp.asarray(_dilated_bias())
    return pl.pallas_call(
        _dilated_kernel,
        grid=(B, S // DIL_SPAN),
        in_specs=[cur, prev, cur, prev, cur, pl.BlockSpec(bias.shape, lambda b, s: (0,) * bias.ndim)],
        out_specs=cur,
        out_shape=jax.ShapeDtypeStruct((B, N_HALF, S, LANES), F32),
        scratch_shapes=[pltpu.VMEM((N_HALF, 2 * DIL_SPAN, LANES), F32),
                        pltpu.VMEM((N_HALF, 2 * DIL_SPAN, LANES), F32),
                        pltpu.VMEM((N_HALF, DIL_SPAN, LANES), F32), pltpu.VMEM((N_HALF, DIL_SPAN, LANES), F32),
                        pltpu.VMEM((N_HALF, DIL_SPAN, LANES), F32)],
        compiler_params=_params(2),
        name="dilated",
    )(cq, ck, ck, cv, cv, bias)


SWA_SPAN = 1024
SWA_GROUP = 4
SWA_HEAD_ORDER = (0, 2, 1, 3)


def _swa_bias():
    qi = np.arange(Q_BLOCK)[:, None]
    kidx = np.arange(2 * Q_BLOCK)[None, :]
    rel = qi + Q_BLOCK - kidx
    valid = (rel >= 0) & (rel < SWA_WINDOW)
    out = np.empty((2, N_HEADS, Q_BLOCK, 2 * Q_BLOCK), np.float32)
    for h in range(N_HEADS):
        bias = -SLOPE_D[h] * np.float32(LOG2E) * rel.astype(np.float32)
        out[0, h] = np.where(valid, bias, NEG)
        out[1, h] = np.where(valid & (kidx >= Q_BLOCK), bias, NEG)
    return out


def _swa_kernel(sink_ref, q_ref, k_ref, v_ref, bias_ref, o_ref):
    span = pl.program_id(1)
    units = SWA_SPAN // Q_BLOCK
    half = lax.broadcasted_iota(jnp.int32, (1, LANES), 1) // HEAD_DIM
    lo = half == 0
    kv_mask = [jnp.where(half == g, 1.0, 0.0).astype(BF16) for g in range(SWA_KV_HEADS)]

    def group(g, _):
        fronts = []
        for uu in range(SWA_GROUP):
            u = g * SWA_GROUP + uu
            n = span * units + u
            lo_start = pl.multiple_of(jnp.maximum(n - 1, 0) * Q_BLOCK, Q_BLOCK)
            hi_start = pl.multiple_of(n * Q_BLOCK, Q_BLOCK)
            qstart = pl.multiple_of(u * Q_BLOCK, Q_BLOCK)
            q = q_ref[0, pl.ds(qstart, Q_BLOCK), :]
            k2 = jnp.concatenate([k_ref[0, pl.ds(lo_start, Q_BLOCK), :], k_ref[0, pl.ds(hi_start, Q_BLOCK), :]],
                                 axis=0)
            v2 = jnp.concatenate([v_ref[0, pl.ds(lo_start, Q_BLOCK), :], v_ref[0, pl.ds(hi_start, Q_BLOCK), :]],
                                 axis=0)
            q4 = jnp.concatenate([q[:, (h % 2) * LANES:(h % 2 + 1) * LANES] * kv_mask[h // 2]
                                  for h in range(N_HEADS)], axis=0)
            s4 = lax.dot_general(q4, k2, _NT, preferred_element_type=F32)
            fronts.append((qstart, s4, v2, jnp.where(n == 0, 1, 0)))
        for qstart, s4, v2, variant in fronts:
            es, ls = [], []
            for h in range(N_HEADS):
                s = s4[h * Q_BLOCK:(h + 1) * Q_BLOCK, :] + bias_ref[variant, h]
                sink = sink_ref[h] * LOG2E
                m = jnp.maximum(jnp.max(s, axis=-1, keepdims=True), sink)
                e = jnp.exp2(s - m)
                ls.append(jnp.sum(e, axis=-1, keepdims=True) + jnp.exp2(sink - m))
                es.append(e.astype(BF16))
            o4 = jnp.dot(jnp.concatenate(es, axis=0), v2, preferred_element_type=F32)
            tiles = []
            for t in range(N_HALF):
                a, b = t, t + 2
                tiles.append(jnp.where(lo, o4[a * Q_BLOCK:(a + 1) * Q_BLOCK, :] / ls[a],
                                       o4[b * Q_BLOCK:(b + 1) * Q_BLOCK, :] / ls[b]))
            o_ref[0, pl.ds(qstart, Q_BLOCK), :] = jnp.concatenate(tiles, axis=-1)
        return 0

    lax.fori_loop(0, units // SWA_GROUP, group, 0)


def _swa(dq, dk, dv, sinks):
    B, S, _ = dq.shape
    bias = jnp.asarray(_swa_bias())
    return pl.pallas_call(
        _swa_kernel,
        grid=(B, S // SWA_SPAN),
        in_specs=[pl.BlockSpec(memory_space=pltpu.SMEM),
                  pl.BlockSpec((1, SWA_SPAN, GROUP), lambda b, i: (b, i, 0)),
                  pl.BlockSpec((1, S, SWA_KV_WIDTH), lambda b, i: (b, 0, 0)),
                  pl.BlockSpec((1, S, SWA_KV_WIDTH), lambda b, i: (b, 0, 0)),
                  pl.BlockSpec(bias.shape, lambda b, i: (0,) * bias.ndim)],
        out_specs=pl.BlockSpec((1, SWA_SPAN, GROUP), lambda b, i: (b, i, 0)),
        out_shape=jax.ShapeDtypeStruct((B, S, GROUP), F32),
        compiler_params=_params(2),
        name="swa",
    )(sinks, dq, dk, dv, bias)


def _row_norm(y, g_row):
    return y * lax.rsqrt(jnp.mean(y * y, axis=-1, keepdims=True) + EPS) * g_row


def _tail_kernel(x_ref, oat_ref, obt_ref, oc_ref, od_ref, gg_ref, wo_ref, gm_ref, wup_ref, wdn_ref, out_ref):
    gg = gg_ref[...]

    def col_norm_t(yt, g_row):
        y = (yt * lax.rsqrt(jnp.mean(yt * yt, axis=0, keepdims=True) + EPS)).T
        return y * g_row

    ga = col_norm_t(oat_ref[0], gg[0:1, :])
    gb = col_norm_t(obt_ref[0], gg[1:2, :])
    gc = _row_norm(jnp.concatenate([oc_ref[0, c] for c in range(N_HALF)], axis=-1), gg[2:3, :])
    gd = _row_norm(od_ref[0], gg[3:4, :])
    mixed = jnp.concatenate([ga, gb, gc, gd], axis=-1).astype(BF16)
    x1 = x_ref[0] + jnp.dot(mixed, wo_ref[...], preferred_element_type=F32)
    xn = _row_norm(x1, gm_ref[...]).astype(BF16)
    u = jnp.maximum(jnp.dot(xn, wup_ref[...], preferred_element_type=F32), 0.0)
    out_ref[0] = x1 + jnp.dot((u * u).astype(BF16), wdn_ref[...], preferred_element_type=F32)


def _tail(x, oat, obt, oc, od, gg, wo, gm, wup, wdn):
    B, S, _ = x.shape
    tm = TAIL_TM
    nat = lambda w: pl.BlockSpec((1, tm, w), lambda b, t: (b, t, 0))
    ft = pl.BlockSpec((1, GROUP, tm), lambda b, t: (b, 0, t))
    const = lambda a: pl.BlockSpec(a.shape, lambda b, t: (0,) * a.ndim, pipeline_mode=pl.Buffered(1))
    return pl.pallas_call(
        _tail_kernel,
        grid=(B, S // tm),
        in_specs=[nat(D_MODEL), ft, ft, pl.BlockSpec((1, N_HALF, tm, LANES), lambda b, t: (b, 0, t, 0)),
                  nat(GROUP), const(gg), const(wo), const(gm),
                  const(wup), const(wdn)],
        out_specs=nat(D_MODEL),
        out_shape=jax.ShapeDtypeStruct((B, S, D_MODEL), F32),
        compiler_params=_params(2),
        name="tail",
    )(x, oat, obt, oc, od, gg, wo, gm, wup, wdn)


def _rope_tables_t(S):
    inv = 1.0 / (ROPE_THETA ** (jnp.arange(0, MLA_ROPE, 2, dtype=F32) / MLA_ROPE))
    ang = inv[:, None] * jnp.arange(S, dtype=F32)[None, :]
    return jnp.cos(ang), jnp.sin(ang)


def _moba_aug_tables(S):
    pos = np.arange(S, dtype=np.float32) % MOBA_BLOCK
    kpos = np.zeros((AUG_ROWS, S), np.float32)
    kpos[:3] = pos
    aslope = np.zeros((N_HEADS, AUG_ROWS, 1), np.float32)
    rest = (SLOPE_A * np.float32(LOG2E)).astype(np.float32)
    for r in range(3):
        piece = rest.astype(BF16).astype(np.float32)
        aslope[:, r, 0] = piece
        rest = rest - piece
    return jnp.asarray(kpos), jnp.asarray(aslope)


def _layer(x, cos_t, sin_t, kpos_t, aslope, attn_norm_g, w_in, moba_q_g, moba_k_g, mla_qlat_g, mla_kvlat_g, mla_w_uq, mla_w_ukv,
           mla_q_g, mla_k_g, dil_q_g, dil_k_g, swa_q_g, swa_k_g, swa_sinks, group_out_g, w_o, mlp_norm_g,
           w_up, w_down):
    B, S, _ = x.shape
    col = lambda g: g.reshape(-1, 1)
    (aqt, aq32t, ak, akm, avt, bqt, bk, bvt, cq, ck, cv, dq, dk, dv) = _inproj(
        x, attn_norm_g.reshape(1, -1), w_in.T.astype(BF16), col(moba_q_g), col(moba_k_g), col(mla_qlat_g),
        col(mla_kvlat_g), mla_w_uq.T.astype(BF16), mla_w_ukv.T.astype(BF16), col(mla_q_g), col(mla_k_g),
        col(dil_q_g), col(dil_k_g), col(swa_q_g), col(swa_k_g), cos_t, sin_t, kpos_t, aslope)
    nb = S // MOBA_BLOCK
    kmean_h = akm.reshape(B, nb, N_HEADS, HEAD_DIM).transpose(0, 2, 1, 3)
    oat = _moba(aqt, aq32t, ak, avt, kmean_h).reshape(B, GROUP, S)
    obt = _mla(bqt, bk, bvt).reshape(B, GROUP, S)
    oc = _dilated(cq, ck, cv)
    od = _swa(dq, dk, dv, swa_sinks)
    perm = np.concatenate([np.arange(HEAD_DIM) + HEAD_DIM * h for h in SWA_HEAD_ORDER])
    gg = group_out_g.at[3].set(group_out_g[3][perm])
    wo = jnp.concatenate([w_o[:3 * GROUP], w_o[3 * GROUP + perm]], axis=0)
    return _tail(x, oat, obt, oc, od, gg, wo.astype(BF16), mlp_norm_g.reshape(1, -1),
                 w_up.astype(BF16), w_down.astype(BF16))


def kernel(x, attn_norm_g, w_in, moba_q_g, moba_k_g, mla_qlat_g, mla_kvlat_g, mla_w_uq, mla_w_ukv, mla_q_g,
           mla_k_g, dil_q_g, dil_k_g, swa_q_g, swa_k_g, swa_sinks, group_out_g, w_o, mlp_norm_g, w_up, w_down):
    S = x.shape[1]
    assert S % max(d * Q_BLOCK for _, d in DILATED_BRANCHES) == 0 and S % INPROJ_TM == 0
    cos_t, sin_t = _rope_tables_t(S)
    kpos_t, aslope = _moba_aug_tables(S)
    params = (attn_norm_g, w_in, moba_q_g, moba_k_g, mla_qlat_g, mla_kvlat_g, mla_w_uq, mla_w_ukv, mla_q_g,
              mla_k_g, dil_q_g, dil_k_g, swa_q_g, swa_k_g, swa_sinks, group_out_g, w_o, mlp_norm_g, w_up, w_down)
    for l in range(attn_norm_g.shape[0]):
        x = _layer(x, cos_t, sin_t, kpos_t, aslope, *[p[l] for p in params])
    return x
```

```python
import numpy as np
import jax
import jax.numpy as jnp
from jax import lax
from jax.experimental import pallas as pl
from jax.experimental.pallas import tpu as pltpu

F32 = jnp.float32
BF16 = jnp.bfloat16

D_MODEL = 1024
HEAD_DIM = 64
N_HEADS = 4
GROUP = N_HEADS * HEAD_DIM
LANES = 128
N_HALF = GROUP // LANES
MOBA_BLOCK = 256
MOBA_TOPK = 3
MLA_Q_RANK = 256
MLA_KV_RANK = 128
MLA_NOPE = 64
MLA_ROPE = 32
MLA_QK = MLA_NOPE + MLA_ROPE
ROPE_THETA = 10000.0
DILATED_BRANCHES = ((128, 1), (512, 4), (2048, 16))
Q_BLOCK = 128
SWA_WINDOW = 128
SWA_KV_HEADS = 2
SWA_KV_WIDTH = SWA_KV_HEADS * HEAD_DIM
D_FF = 4 * D_MODEL
EPS = 1e-6
NEG = -1e30

_WIDTHS = (GROUP, GROUP, GROUP, MLA_Q_RANK, MLA_KV_RANK, MLA_ROPE,
           GROUP, GROUP, GROUP, GROUP, SWA_KV_WIDTH, SWA_KV_WIDTH)
_OFFS = tuple(int(v) for v in np.cumsum((0,) + _WIDTHS))
IN_COLS = _OFFS[-1]
(_A_Q, _A_K, _A_V, _B_QL, _B_KVL, _B_KR, _C_Q, _C_K, _C_V, _D_Q, _D_K, _D_V) = _OFFS[:-1]

VMEM_LIMIT = 56 * 1024 * 1024

INPROJ_TM = 512
TAIL_TM = 512
ATT_TK = 256
ATT_TQ = 2 * ATT_TK
ATT_DK = 128
ATT_DV = HEAD_DIM + 16
AUG_ROWS = 8
LOG2E = 1.4426950408889634
FLASH_PAIRS = 2

_NT = (((1,), (1,)), ((), ()))


def _alibi_slopes():
    n = 3 * N_HEADS
    idx = np.arange(1, n + 1, dtype=np.float32).reshape(N_HEADS, 3)
    s = np.exp2(-8.0 * idx / n).astype(np.float32)
    return s[:, 0], s[:, 1], s[:, 2]


SLOPE_A, SLOPE_C, SLOPE_D = _alibi_slopes()


def _params(n_axes):
    return pltpu.CompilerParams(dimension_semantics=("arbitrary",) * n_axes,
                                vmem_limit_bytes=VMEM_LIMIT)


def _head_norm_t(sec, g_col, n_heads, width):
    outs = []
    for h in range(n_heads):
        s = sec[h * width:(h + 1) * width, :]
        ms = jnp.sum(s * s, axis=0, keepdims=True) * (1.0 / width)
        outs.append(s * lax.rsqrt(ms + EPS) * g_col)
    return outs


def _inproj_kernel(x_ref, gx_ref, w1t_ref, gaq_ref, gak_ref, gql_ref, gkvl_ref, wuqt_ref, wukvt_ref,
                   gbq_ref, gbk_ref, gcq_ref, gck_ref, gdq_ref, gdk_ref, cos_ref, sin_ref, kpos_ref, aslope_ref,
                   aqt_ref, aq32t_ref, ak_ref, akm_ref, avt_ref,
                   bqt_ref, bk_ref, bvt_ref,
                   cq_ref, ck_ref, cv_ref, dq_ref, dk_ref, dv_ref,
                   h_scr):
    tm = x_ref.shape[1]
    x = x_ref[0]
    ms = jnp.mean(x * x, axis=-1, keepdims=True)
    xn = (x * lax.rsqrt(ms + EPS) * gx_ref[...]).astype(BF16)
    h_scr[...] = lax.dot_general(w1t_ref[...], xn, _NT, preferred_element_type=F32)

    scale = HEAD_DIM ** -0.5

    ones_rows = jnp.ones((ATT_DV - HEAD_DIM, ATT_TK), F32)

    def store_vt(ref, h, vh):
        for c in range(tm // ATT_TK):
            ref[0, h, c] = jnp.concatenate([vh[:, c * ATT_TK:(c + 1) * ATT_TK], ones_rows], axis=0).astype(BF16)

    pad_rows = jnp.zeros((ATT_DK - HEAD_DIM - AUG_ROWS, tm), F32)
    qa = _head_norm_t(h_scr[_A_Q:_A_Q + GROUP, :], gaq_ref[...], N_HEADS, HEAD_DIM)
    ka = _head_norm_t(h_scr[_A_K:_A_K + GROUP, :], gak_ref[...], N_HEADS, HEAD_DIM)
    ka_nat = jnp.concatenate(ka, axis=0).T
    for c in range(tm // MOBA_BLOCK):
        akm_ref[0, c] = jnp.sum(ka_nat[c * MOBA_BLOCK:(c + 1) * MOBA_BLOCK, :], axis=0,
                                keepdims=True) * (1.0 / MOBA_BLOCK)
    kpos = kpos_ref[...]
    for h in range(N_HEADS):
        aq32t_ref[0, h] = qa[h]
        slope_rows = jnp.broadcast_to(aslope_ref[h], (AUG_ROWS, tm))
        aqt_ref[0, h] = jnp.concatenate([qa[h] * (scale * LOG2E), slope_rows, pad_rows], axis=0).astype(BF16)
        ak_ref[0, h] = jnp.concatenate([ka[h], kpos, pad_rows], axis=0).T.astype(BF16)
        store_vt(avt_ref, h, h_scr[_A_V + h * HEAD_DIM:_A_V + (h + 1) * HEAD_DIM, :])

    cos = cos_ref[...]
    sin = sin_ref[...]
    half = MLA_ROPE // 2

    def rope_pad(t, sc):
        x1 = t[MLA_NOPE:MLA_NOPE + half, :]
        x2 = t[MLA_NOPE + half:MLA_QK, :]
        return jnp.concatenate([t[:MLA_NOPE, :] * sc, (x1 * cos - x2 * sin) * sc, (x1 * sin + x2 * cos) * sc,
                                jnp.zeros((ATT_DK - MLA_QK, tm), F32)], axis=0)

    ql = h_scr[_B_QL:_B_QL + MLA_Q_RANK, :]
    ql = ql * lax.rsqrt(jnp.sum(ql * ql, axis=0, keepdims=True) * (1.0 / MLA_Q_RANK) + EPS) * gql_ref[...]
    qb = jnp.dot(wuqt_ref[...], ql.astype(BF16), preferred_element_type=F32)
    qb = _head_norm_t(qb, gbq_ref[...], N_HEADS, MLA_QK)
    kvl = h_scr[_B_KVL:_B_KVL + MLA_KV_RANK, :]
    kvl = kvl * lax.rsqrt(jnp.sum(kvl * kvl, axis=0, keepdims=True) * (1.0 / MLA_KV_RANK) + EPS) * gkvl_ref[...]
    kvb = jnp.dot(wukvt_ref[...], kvl.astype(BF16), preferred_element_type=F32)
    kr = h_scr[_B_KR:_B_KR + MLA_ROPE, :]
    gbk = gbk_ref[...]
    for h in range(N_HEADS):
        bqt_ref[0, h] = rope_pad(qb[h], MLA_QK ** -0.5 * LOG2E).astype(BF16)
        kh = jnp.concatenate([kvb[h * 2 * HEAD_DIM:h * 2 * HEAD_DIM + MLA_NOPE, :], kr], axis=0)
        kh = kh * lax.rsqrt(jnp.sum(kh * kh, axis=0, keepdims=True) * (1.0 / MLA_QK) + EPS) * gbk
        bk_ref[0, h] = rope_pad(kh, 1.0).T.astype(BF16)
        store_vt(bvt_ref, h, kvb[h * 2 * HEAD_DIM + MLA_NOPE:(h + 1) * 2 * HEAD_DIM, :])

    qc = jnp.concatenate(_head_norm_t(h_scr[_C_Q:_C_Q + GROUP, :], gcq_ref[...], N_HEADS, HEAD_DIM), axis=0)
    kc = jnp.concatenate(_head_norm_t(h_scr[_C_K:_C_K + GROUP, :], gck_ref[...], N_HEADS, HEAD_DIM), axis=0)
    for ref, val in ((cq_ref, qc * (scale * LOG2E)), (ck_ref, kc), (cv_ref, h_scr[_C_V:_C_V + GROUP, :])):
        for c in range(N_HALF):
            ref[0, c] = val[c * LANES:(c + 1) * LANES, :].T

    qd = _head_norm_t(h_scr[_D_Q:_D_Q + GROUP, :], gdq_ref[...], N_HEADS, HEAD_DIM)
    qd = jnp.concatenate([qd[h] for h in SWA_HEAD_ORDER], axis=0)
    dq_ref[0] = (qd * (scale * LOG2E)).T.astype(BF16)
    kd = jnp.concatenate(_head_norm_t(h_scr[_D_K:_D_K + SWA_KV_WIDTH, :], gdk_ref[...], SWA_KV_HEADS, HEAD_DIM),
                         axis=0)
    dk_ref[0] = kd.T.astype(BF16)
    dv_ref[0] = h_scr[_D_V:_D_V + SWA_KV_WIDTH, :].T.astype(BF16)


def _inproj(x, gx, w1t, gaq, gak, gql, gkvl, wuqt, wukvt, gbq, gbk, gcq, gck, gdq, gdk, cos_t, sin_t, kpos_t,
            aslope):
    B, S, _ = x.shape
    tm = INPROJ_TM
    nb = S // ATT_TK
    cpt = tm // ATT_TK
    H = N_HEADS
    full = lambda a: pl.BlockSpec(a.shape, lambda b, t: (0,) * a.ndim)
    in_specs = [pl.BlockSpec((1, tm, D_MODEL), lambda b, t: (b, t, 0)), full(gx), full(w1t), full(gaq), full(gak),
                full(gql), full(gkvl), full(wuqt), full(wukvt), full(gbq), full(gbk), full(gcq), full(gck),
                full(gdq), full(gdk),
                pl.BlockSpec((MLA_ROPE // 2, tm), lambda b, t: (0, t)),
                pl.BlockSpec((MLA_ROPE // 2, tm), lambda b, t: (0, t)),
                pl.BlockSpec((AUG_ROWS, tm), lambda b, t: (0, t)), full(aslope)]
    head_t = lambda w: pl.BlockSpec((1, H, w, tm), lambda b, t: (b, 0, 0, t))
    head_n = pl.BlockSpec((1, H, tm, ATT_DK), lambda b, t: (b, 0, t, 0))
    vt_spec = pl.BlockSpec((1, H, cpt, ATT_DV, ATT_TK), lambda b, t: (b, 0, t, 0, 0))
    nat = lambda w: pl.BlockSpec((1, tm, w), lambda b, t: (b, t, 0))
    halves = pl.BlockSpec((1, N_HALF, tm, LANES), lambda b, t: (b, 0, t, 0))
    out_shape = [
        jax.ShapeDtypeStruct((B, H, ATT_DK, S), BF16),
        jax.ShapeDtypeStruct((B, H, HEAD_DIM, S), F32),
        jax.ShapeDtypeStruct((B, H, S, ATT_DK), BF16),
        jax.ShapeDtypeStruct((B, nb, 1, GROUP), F32),
        jax.ShapeDtypeStruct((B, H, nb, ATT_DV, ATT_TK), BF16),
        jax.ShapeDtypeStruct((B, H, ATT_DK, S), BF16),
        jax.ShapeDtypeStruct((B, H, S, ATT_DK), BF16),
        jax.ShapeDtypeStruct((B, H, nb, ATT_DV, ATT_TK), BF16),
        jax.ShapeDtypeStruct((B, N_HALF, S, LANES), F32),
        jax.ShapeDtypeStruct((B, N_HALF, S, LANES), F32),
        jax.ShapeDtypeStruct((B, N_HALF, S, LANES), F32),
        jax.ShapeDtypeStruct((B, S, GROUP), BF16),
        jax.ShapeDtypeStruct((B, S, SWA_KV_WIDTH), BF16),
        jax.ShapeDtypeStruct((B, S, SWA_KV_WIDTH), BF16),
    ]
    out_specs = [head_t(ATT_DK), head_t(HEAD_DIM), head_n,
                 pl.BlockSpec((1, tm // MOBA_BLOCK, 1, GROUP), lambda b, t: (b, t, 0, 0)), vt_spec,
                 head_t(ATT_DK), head_n, vt_spec,
                 halves, halves, halves, nat(GROUP), nat(SWA_KV_WIDTH), nat(SWA_KV_WIDTH)]
    return pl.pallas_call(
        _inproj_kernel,
        grid=(B, S // tm),
        in_specs=in_specs,
        out_specs=out_specs,
        out_shape=out_shape,
        scratch_shapes=[pltpu.VMEM((IN_COLS, tm), F32)],
        compiler_params=_params(2),
        name="inproj",
    )(x, gx, w1t, gaq, gak, gql, gkvl, wuqt, wukvt, gbq, gbk, gcq, gck, gdq, gdk, cos_t, sin_t, kpos_t, aslope)


def _flash_scratch(tq):
    strips = (N_HEADS, tq // LANES, ATT_TK, LANES)
    s_buf, p_buf = pltpu.VMEM(strips, F32), pltpu.VMEM(strips, BF16)
    return [s_buf, s_buf, p_buf, p_buf, pltpu.VMEM((N_HEADS, ATT_DV, tq), F32)]


def _flash_heads(qt_ref, k_ref, vt_ref, o_ref, s_bufs, p_bufs, acc_scr, rowb_scr, i):
    tq = o_ref.shape[3]
    n_tail = tq // ATT_TK
    assert n_tail == 2
    n_past = i * n_tail
    heads = range(N_HEADS)

    def block_of(pos):
        return jnp.where(pos < n_tail, n_past + pos, pos - n_tail)

    def head_scores(h, slot, blk):
        kb = k_ref[0, h, pl.ds(pl.multiple_of(blk * ATT_TK, ATT_TK), ATT_TK), :]
        s = jnp.dot(kb, qt_ref[0, h], preferred_element_type=F32)
        for c in range(tq // LANES):
            s_bufs[slot][h, c] = s[:, c * LANES:(c + 1) * LANES]

    def stage_scores(slot, blk):
        for h in heads:
            head_scores(h, slot, blk)

    def stage_softmax(slot, blk, ms, causal=None):
        new_ms, alphas = [], []
        for h in heads:
            m_parts, a_parts = [], []
            rv_row = None if rowb_scr is None else rowb_scr[h, pl.ds(blk, 1), :]
            for c in range(tq // LANES):
                cols = slice(c * LANES, (c + 1) * LANES)
                st = s_bufs[slot][h, c]
                if causal is not None:
                    st = jnp.where(causal[:, cols], st, 2 * NEG)
                cm = jnp.max(st, axis=0, keepdims=True)
                m_old = ms[h][:, cols]
                if rowb_scr is None:
                    m_new = jnp.maximum(m_old, cm)
                    shift = m_new
                else:
                    rv = rv_row[:, cols]
                    m_new = jnp.maximum(m_old, cm + rv)
                    shift = m_new - rv
                a_parts.append(jnp.exp2(m_old - m_new))
                m_parts.append(m_new)
                p_bufs[slot][h, c] = jnp.exp2(st - shift).astype(BF16)
            new_ms.append(jnp.concatenate(m_parts, axis=1))
            alphas.append(jnp.concatenate(a_parts, axis=1))
        return tuple(new_ms), tuple(alphas)

    def head_values(h, slot, blk, alphas):
        p = jnp.concatenate([p_bufs[slot][h, c] for c in range(tq // LANES)], axis=1)
        acc_scr[h] = alphas[h] * acc_scr[h] + jnp.dot(vt_ref[0, h, blk], p, preferred_element_type=F32)

    def stage_values(slot, blk, alphas):
        for h in heads:
            head_values(h, slot, blk, alphas)

    krow = lax.broadcasted_iota(jnp.int32, (ATT_TK, tq), 0)
    ti = lax.broadcasted_iota(jnp.int32, (ATT_TK, tq), 1)
    for h in heads:
        acc_scr[h] = jnp.zeros((ATT_DV, tq), F32)
    stage_scores(0, n_past)
    stage_scores(1, n_past + 1)
    ms = tuple(jnp.full((1, tq), NEG, F32) for _ in heads)
    ms, alphas = stage_softmax(0, n_past, ms, causal=ti >= krow)
    stage_values(0, n_past, alphas)
    ms, alphas = stage_softmax(1, n_past + 1, ms, causal=ti >= krow + ATT_TK)
    stage_scores(0, 0)

    def step(pos, slot, ms, alphas):
        ms, new_alphas = stage_softmax(1 - slot, pos + 1 - n_tail, ms)
        blk = block_of(pos)
        for h in heads:
            head_values(h, slot, blk, alphas)
            head_scores(h, slot, pos)
        return ms, new_alphas

    def pair(first, carry):
        ms, alphas = step(first, 1, *carry)
        return step(first + 1, 0, ms, alphas)

    def trip(t, carry):
        for u in range(FLASH_PAIRS):
            carry = pair(2 * (FLASH_PAIRS * t + u) + 1, carry)
        return carry

    carry = lax.fori_loop(0, i // FLASH_PAIRS, trip, (ms, alphas))
    ms, alphas = lax.fori_loop(i - i % FLASH_PAIRS, i, lambda u, c: pair(2 * u + 1, c), carry)
    last = n_past + 1
    stage_values(1, block_of(last), alphas)
    for h in heads:
        acc = acc_scr[h]
        o_ref[0, h] = acc[:HEAD_DIM, :] / acc[HEAD_DIM:HEAD_DIM + 1, :]


def _moba_kernel(qt_ref, q32t_ref, k_ref, vt_ref, kmean_ref, o_ref, rowb_scr, s0, s1, p0, p1, acc_scr):
    i = pl.program_id(1)
    nb = kmean_ref.shape[2]
    tq = qt_ref.shape[3]
    blk = lax.broadcasted_iota(jnp.int32, (nb, tq), 0)
    col = lax.broadcasted_iota(jnp.int32, (nb, tq), 1)
    qblk = i * (tq // MOBA_BLOCK) + col // MOBA_BLOCK
    past = blk < qblk
    dist0 = (i * tq + col - blk * MOBA_BLOCK).astype(F32)
    for h in range(N_HEADS):
        gate = jnp.dot(kmean_ref[0, h], q32t_ref[0, h], preferred_element_type=F32,
                       precision=lax.Precision.HIGHEST)
        gate = jnp.where(past, gate, NEG)
        sel = blk == qblk
        for _ in range(MOBA_TOPK):
            best = jnp.max(gate, axis=0, keepdims=True)
            first = jnp.min(jnp.where(gate == best, blk, nb), axis=0, keepdims=True)
            pick = blk == first
            sel = jnp.logical_or(sel, jnp.logical_and(pick, past))
            gate = jnp.where(pick, -jnp.inf, gate)
        rowb_scr[h] = jnp.where(sel, (-float(SLOPE_A[h]) * LOG2E) * dist0, 2 * NEG)
    _flash_heads(qt_ref, k_ref, vt_ref, o_ref, (s0, s1), (p0, p1), acc_scr, rowb_scr, i)


def _flash_specs(S, tq):
    H, nb = N_HEADS, S // ATT_TK
    return ([pl.BlockSpec((1, H, ATT_DK, tq), lambda b, i: (b, 0, 0, i)),
             pl.BlockSpec((1, H, S, ATT_DK), lambda b, i: (b, 0, 0, 0)),
             pl.BlockSpec((1, H, nb, ATT_DV, ATT_TK), lambda b, i: (b, 0, 0, 0, 0))],
            pl.BlockSpec((1, H, HEAD_DIM, tq), lambda b, i: (b, 0, 0, i)))


def _moba(aqt, aq32t, ak, avt, kmean_h):
    B, H, _, S = aqt.shape
    tq = ATT_TQ
    nb = S // ATT_TK
    (q_spec, k_spec, vt_spec), o_spec = _flash_specs(S, tq)
    return pl.pallas_call(
        _moba_kernel,
        grid=(B, S // tq),
        in_specs=[q_spec, pl.BlockSpec((1, H, HEAD_DIM, tq), lambda b, i: (b, 0, 0, i)), k_spec, vt_spec,
                  pl.BlockSpec((1, H, nb, HEAD_DIM), lambda b, i: (b, 0, 0, 0))],
        out_specs=o_spec,
        out_shape=jax.ShapeDtypeStruct((B, H, HEAD_DIM, S), F32),
        scratch_shapes=[pltpu.VMEM((H, nb, tq), F32)] + _flash_scratch(tq),
        compiler_params=_params(2),
        name="moba",
    )(aqt, aq32t, ak, avt, kmean_h)


def _mla_kernel(qt_ref, k_ref, vt_ref, o_ref, s0, s1, p0, p1, acc_scr):
    _flash_heads(qt_ref, k_ref, vt_ref, o_ref, (s0, s1), (p0, p1), acc_scr, None, pl.program_id(1))


def _mla(bqt, bk, bvt):
    B, H, _, S = bqt.shape
    tq = ATT_TQ
    in_specs, o_spec = _flash_specs(S, tq)
    return pl.pallas_call(
        _mla_kernel,
        grid=(B, S // tq),
        in_specs=in_specs,
        out_specs=o_spec,
        out_shape=jax.ShapeDtypeStruct((B, H, HEAD_DIM, S), F32),
        scratch_shapes=_flash_scratch(tq),
        compiler_params=_params(2),
        name="mla",
    )(bqt, bk, bvt)


DIL_SPAN = max(d for _, d in DILATED_BRANCHES) * Q_BLOCK
DIL_UNITS = DIL_SPAN // Q_BLOCK
DIL_GROUP = (4, 2, 2)


def _dilated_bias():
    qi = np.arange(Q_BLOCK)[:, None]
    kidx = np.arange(2 * Q_BLOCK)[None, :]
    rel = qi + Q_BLOCK - kidx
    out = np.empty((2, len(DILATED_BRANCHES), N_HEADS, Q_BLOCK, 2 * Q_BLOCK), np.float32)
    for bi, (window, d) in enumerate(DILATED_BRANCHES):
        valid = (rel >= 0) & (rel <= window // d)
        for h in range(N_HEADS):
            bias = -SLOPE_C[h] * np.float32(LOG2E) * (d * rel).astype(np.float32)
            out[0, bi, h] = np.where(valid, bias, NEG)
            out[1, bi, h] = np.where(valid & (kidx >= Q_BLOCK), bias, NEG)
    return out


def _rows_load(ref, lead, start, size, stride):
    return jnp.concatenate([ref[lead + (c, pl.ds(start, size, stride=stride), slice(None))]
                            for c in range(N_HALF)], axis=-1)


def _rows_store(ref, lead, start, size, stride, val):
    for c in range(N_HALF):
        ref[lead + (c, pl.ds(start, size, stride=stride), slice(None))] = val[:, c * LANES:(c + 1) * LANES]


def _dilated_kernel(q_ref, kp_ref, kc_ref, vp_ref, vc_ref, bias_ref, o_ref, kbuf, vbuf, m_scr, den_scr, num_scr):
    span = pl.program_id(1)
    kbuf[:, 0:DIL_SPAN, :] = kp_ref[0]
    kbuf[:, DIL_SPAN:, :] = kc_ref[0]
    vbuf[:, 0:DIL_SPAN, :] = vp_ref[0]
    vbuf[:, DIL_SPAN:, :] = vc_ref[0]
    lane_head = lax.broadcasted_iota(jnp.int32, (1, GROUP), 1) // HEAD_DIM
    hmask = [lane_head == h for h in range(N_HEADS)]
    hmask_f = [m.astype(F32) for m in hmask]
    first_span = jnp.where(span == 0, 1, 0)

    def per_head(cols):
        out = cols[N_HEADS - 1]
        for h in range(N_HEADS - 2, -1, -1):
            out = jnp.where(hmask[h], cols[h], out)
        return out

    order = sorted(range(len(DILATED_BRANCHES)), key=lambda b: -DILATED_BRANCHES[b][1])
    for bi in order:
        d, n_group = DILATED_BRANCHES[bi][1], DIL_GROUP[bi]
        first, last = bi == order[0], bi == order[-1]

        def group(g, _, bi=bi, d=d, n_group=n_group, first=first, last=last):
            fronts = []
            for uu in range(n_group):
                u = g * n_group + uu
                r, n = u % d, u // d
                qstart = n * (Q_BLOCK * d) + r
                kstart = DIL_SPAN + qstart - Q_BLOCK * d
                q = _rows_load(q_ref, (0,), qstart, Q_BLOCK, d)
                k2 = _rows_load(kbuf, (), kstart, 2 * Q_BLOCK, d).astype(BF16)
                v2 = _rows_load(vbuf, (), kstart, 2 * Q_BLOCK, d).astype(BF16)
                q4 = jnp.concatenate([(q * hmask_f[h]).astype(BF16) for h in range(N_HEADS)], axis=0)
                s4 = lax.dot_general(q4, k2, _NT, preferred_element_type=F32)
                variant = jnp.where(n == 0, first_span, 0)
                fronts.append((qstart, s4, v2, variant))
            for qstart, s4, v2, variant in fronts:
                es, ms, ls = [], [], []
                for h in range(N_HEADS):
                    s = s4[h * Q_BLOCK:(h + 1) * Q_BLOCK, :] + bias_ref[variant, bi, h]
                    m = jnp.max(s, axis=-1, keepdims=True)
                    e = jnp.exp2(s - m)
                    ls.append(jnp.sum(e, axis=-1, keepdims=True))
                    ms.append(m)
                    es.append(e.astype(BF16))
                o4 = jnp.dot(jnp.concatenate(es, axis=0), v2, preferred_element_type=F32)
                o = o4[(N_HEADS - 1) * Q_BLOCK:, :]
                for h in range(N_HEADS - 2, -1, -1):
                    o = jnp.where(hmask[h], o4[h * Q_BLOCK:(h + 1) * Q_BLOCK, :], o)
                m_b, l_b = per_head(ms), per_head(ls)
                at = ((), qstart, Q_BLOCK, d)
                if first:
                    _rows_store(m_scr, *at, m_b)
                    _rows_store(num_scr, *at, o)
                    _rows_store(den_scr, *at, l_b)
                else:
                    m_old = _rows_load(m_scr, *at)
                    m_new = jnp.maximum(m_old, m_b)
                    a, b = jnp.exp2(m_old - m_new), jnp.exp2(m_b - m_new)
                    num = a * _rows_load(num_scr, *at) + b * o
                    den = a * _rows_load(den_scr, *at) + b * l_b
                    if last:
                        _rows_store(o_ref, (0,), qstart, Q_BLOCK, d, num / den)
                    else:
                        _rows_store(m_scr, *at, m_new)
                        _rows_store(num_scr, *at, num)
                        _rows_store(den_scr, *at, den)
            return 0

        lax.fori_loop(0, DIL_UNITS // n_group, group, 0)


def _dilated(cq, ck, cv):
    B, _, S, _ = cq.shape
    cur = pl.BlockSpec((1, N_HALF, DIL_SPAN, LANES), lambda b, s: (b, 0, s, 0))
    prev = pl.BlockSpec((1, N_HALF, DIL_SPAN, LANES), lambda b, s: (b, 0, jnp.maximum(s - 1, 0), 0))
    bias = jnp.asarray(_dilated_bias())
    return pl.pallas_call(
        _dilated_kernel,
        grid=(B, S // DIL_SPAN),
        in_specs=[cur, prev, cur, prev, cur, pl.BlockSpec(bias.shape, lambda b, s: (0,) * bias.ndim)],
        out_specs=cur,
        out_shape=jax.ShapeDtypeStruct((B, N_HALF, S, LANES), F32),
        scratch_shapes=[pltpu.VMEM((N_HALF, 2 * DIL_SPAN, LANES), F32),
                        pltpu.VMEM((N_HALF, 2 * DIL_SPAN, LANES), F32),
                        pltpu.VMEM((N_HALF, DIL_SPAN, LANES), F32), pltpu.VMEM((N_HALF, DIL_SPAN, LANES), F32),
                        pltpu.VMEM((N_HALF, DIL_SPAN, LANES), F32)],
        compiler_params=_params(2),
        name="dilated",
    )(cq, ck, ck, cv, cv, bias)


SWA_SPAN = 1024
SWA_GROUP = 4
SWA_HEAD_ORDER = (0, 2, 1, 3)


def _swa_bias():
    qi = np.arange(Q_BLOCK)[:, None]
    kidx = np.arange(2 * Q_BLOCK)[None, :]
    rel = qi + Q_BLOCK - kidx
    valid = (rel >= 0) & (rel < SWA_WINDOW)
    out = np.empty((2, N_HEADS, Q_BLOCK, 2 * Q_BLOCK), np.float32)
    for h in range(N_HEADS):
        bias = -SLOPE_D[h] * np.float32(LOG2E) * rel.astype(np.float32)
        out[0, h] = np.where(valid, bias, NEG)
        out[1, h] = np.where(valid & (kidx >= Q_BLOCK), bias, NEG)
    return out


def _swa_kernel(sink_ref, q_ref, k_ref, v_ref, bias_ref, o_ref):
    span = pl.program_id(1)
    units = SWA_SPAN // Q_BLOCK
    half = lax.broadcasted_iota(jnp.int32, (1, LANES), 1) // HEAD_DIM
    lo = half == 0
    kv_mask = [jnp.where(half == g, 1.0, 0.0).astype(BF16) for g in range(SWA_KV_HEADS)]

    def group(g, _):
        fronts = []
        for uu in range(SWA_GROUP):
            u = g * SWA_GROUP + uu
            n = span * units + u
            lo_start = pl.multiple_of(jnp.maximum(n - 1, 0) * Q_BLOCK, Q_BLOCK)
            hi_start = pl.multiple_of(n * Q_BLOCK, Q_BLOCK)
            qstart = pl.multiple_of(u * Q_BLOCK, Q_BLOCK)
            q = q_ref[0, pl.ds(qstart, Q_BLOCK), :]
            k2 = jnp.concatenate([k_ref[0, pl.ds(lo_start, Q_BLOCK), :], k_ref[0, pl.ds(hi_start, Q_BLOCK), :]],
                                 axis=0)
            v2 = jnp.concatenate([v_ref[0, pl.ds(lo_start, Q_BLOCK), :], v_ref[0, pl.ds(hi_start, Q_BLOCK), :]],
                                 axis=0)
            q4 = jnp.concatenate([q[:, (h % 2) * LANES:(h % 2 + 1) * LANES] * kv_mask[h // 2]
                                  for h in range(N_HEADS)], axis=0)
            s4 = lax.dot_general(q4, k2, _NT, preferred_element_type=F32)
            fronts.append((qstart, s4, v2, jnp.where(n == 0, 1, 0)))
        for qstart, s4, v2, variant in fronts:
            es, ls = [], []
            for h in range(N_HEADS):
                s = s4[h * Q_BLOCK:(h + 1) * Q_BLOCK, :] + bias_ref[variant, h]
                sink = sink_ref[h] * LOG2E
                m = jnp.maximum(jnp.max(s, axis=-1, keepdims=True), sink)
                e = jnp.exp2(s - m)
                ls.append(jnp.sum(e, axis=-1, keepdims=True) + jnp.exp2(sink - m))
                es.append(e.astype(BF16))
            o4 = jnp.dot(jnp.concatenate(es, axis=0), v2, preferred_element_type=F32)
            tiles = []
            for t in range(N_HALF):
                a, b = t, t + 2
                tiles.append(jnp.where(lo, o4[a * Q_BLOCK:(a + 1) * Q_BLOCK, :] / ls[a],
                                       o4[b * Q_BLOCK:(b + 1) * Q_BLOCK, :] / ls[b]))
            o_ref[0, pl.ds(qstart, Q_BLOCK), :] = jnp.concatenate(tiles, axis=-1)
        return 0

    lax.fori_loop(0, units // SWA_GROUP, group, 0)


def _swa(dq, dk, dv, sinks):
    B, S, _ = dq.shape
    bias = jnp.asarray(_swa_bias())
    return pl.pallas_call(
        _swa_kernel,
        grid=(B, S // SWA_SPAN),
        in_specs=[pl.BlockSpec(memory_space=pltpu.SMEM),
                  pl.BlockSpec((1, SWA_SPAN, GROUP), lambda b, i: (b, i, 0)),
                  pl.BlockSpec((1, S, SWA_KV_WIDTH), lambda b, i: (b, 0, 0)),
                  pl.BlockSpec((1, S, SWA_KV_WIDTH), lambda b, i: (b, 0, 0)),
                  pl.BlockSpec(bias.shape, lambda b, i: (0,) * bias.ndim)],
        out_specs=pl.BlockSpec((1, SWA_SPAN, GROUP), lambda b, i: (b, i, 0)),
        out_shape=jax.ShapeDtypeStruct((B, S, GROUP), F32),
        compiler_params=_params(2),
        name="swa",
    )(sinks, dq, dk, dv, bias)


def _row_norm(y, g_row):
    return y * lax.rsqrt(jnp.mean(y * y, axis=-1, keepdims=True) + EPS) * g_row


def _tail_kernel(x_ref, oat_ref, obt_ref, oc_ref, od_ref, gg_ref, wo_ref, gm_ref, wup_ref, wdn_ref, out_ref):
    gg = gg_ref[...]

    def col_norm_t(yt, g_row):
        y = (yt * lax.rsqrt(jnp.mean(yt * yt, axis=0, keepdims=True) + EPS)).T
        return y * g_row

    ga = col_norm_t(oat_ref[0], gg[0:1, :])
    gb = col_norm_t(obt_ref[0], gg[1:2, :])
    gc = _row_norm(jnp.concatenate([oc_ref[0, c] for c in range(N_HALF)], axis=-1), gg[2:3, :])
    gd = _row_norm(od_ref[0], gg[3:4, :])
    mixed = jnp.concatenate([ga, gb, gc, gd], axis=-1).astype(BF16)
    x1 = x_ref[0] + jnp.dot(mixed, wo_ref[...], preferred_element_type=F32)
    xn = _row_norm(x1, gm_ref[...]).astype(BF16)
    u = jnp.maximum(jnp.dot(xn, wup_ref[...], preferred_element_type=F32), 0.0)
    out_ref[0] = x1 + jnp.dot((u * u).astype(BF16), wdn_ref[...], preferred_element_type=F32)


def _tail(x, oat, obt, oc, od, gg, wo, gm, wup, wdn):
    B, S, _ = x.shape
    tm = TAIL_TM
    nat = lambda w: pl.BlockSpec((1, tm, w), lambda b, t: (b, t, 0))
    ft = pl.BlockSpec((1, GROUP, tm), lambda b, t: (b, 0, t))
    const = lambda a: pl.BlockSpec(a.shape, lambda b, t: (0,) * a.ndim, pipeline_mode=pl.Buffered(1))
    return pl.pallas_call(
        _tail_kernel,
        grid=(B, S // tm),
        in_specs=[nat(D_MODEL), ft, ft, pl.BlockSpec((1, N_HALF, tm, LANES), lambda b, t: (b, 0, t, 0)),
                  nat(GROUP), const(gg), const(wo), const(gm),
                  const(wup), const(wdn)],
        out_specs=nat(D_MODEL),
        out_shape=jax.ShapeDtypeStruct((B, S, D_MODEL), F32),
        compiler_params=_params(2),
        name="tail",
    )(x, oat, obt, oc, od, gg, wo, gm, wup, wdn)


def _rope_tables_t(S):
    inv = 1.0 / (ROPE_THETA ** (jnp.arange(0, MLA_ROPE, 2, dtype=F32) / MLA_ROPE))
    ang = inv[:, None] * jnp.arange(S, dtype=F32)[None, :]
    return jnp.cos(ang), jnp.sin(ang)


def _moba_aug_tables(S):
    pos = np.arange(S, dtype=np.float32) % MOBA_BLOCK
    kpos = np.zeros((AUG_ROWS, S), np.float32)
    kpos[:3] = pos
    aslope = np.zeros((N_HEADS, AUG_ROWS, 1), np.float32)
    rest = (SLOPE_A * np.float32(LOG2E)).astype(np.float32)
    for r in range(3):
        piece = rest.astype(BF16).astype(np.float32)
        aslope[:, r, 0] = piece
        rest = rest - piece
    return jnp.asarray(kpos), jnp.asarray(aslope)


def _layer(x, cos_t, sin_t, kpos_t, aslope, attn_norm_g, w_in, moba_q_g, moba_k_g, mla_qlat_g, mla_kvlat_g, mla_w_uq, mla_w_ukv,
           mla_q_g, mla_k_g, dil_q_g, dil_k_g, swa_q_g, swa_k_g, swa_sinks, group_out_g, w_o, mlp_norm_g,
           w_up, w_down):
    B, S, _ = x.shape
    col = lambda g: g.reshape(-1, 1)
    (aqt, aq32t, ak, akm, avt, bqt, bk, bvt, cq, ck, cv, dq, dk, dv) = _inproj(
        x, attn_norm_g.reshape(1, -1), w_in.T.astype(BF16), col(moba_q_g), col(moba_k_g), col(mla_qlat_g),
        col(mla_kvlat_g), mla_w_uq.T.astype(BF16), mla_w_ukv.T.astype(BF16), col(mla_q_g), col(mla_k_g),
        col(dil_q_g), col(dil_k_g), col(swa_q_g), col(swa_k_g), cos_t, sin_t, kpos_t, aslope)
    nb = S // MOBA_BLOCK
    kmean_h = akm.reshape(B, nb, N_HEADS, HEAD_DIM).transpose(0, 2, 1, 3)
    oat = _moba(aqt, aq32t, ak, avt, kmean_h).reshape(B, GROUP, S)
    obt = _mla(bqt, bk, bvt).reshape(B, GROUP, S)
    oc = _dilated(cq, ck, cv)
    od = _swa(dq, dk, dv, swa_sinks)
    perm = np.concatenate([np.arange(HEAD_DIM) + HEAD_DIM * h for h in SWA_HEAD_ORDER])
    gg = group_out_g.at[3].set(group_out_g[3][perm])
    wo = jnp.concatenate([w_o[:3 * GROUP], w_o[3 * GROUP + perm]], axis=0)
    return _tail(x, oat, obt, oc, od, gg, wo.astype(BF16), mlp_norm_g.reshape(1, -1),
                 w_up.astype(BF16), w_down.astype(BF16))


def kernel(x, attn_norm_g, w_in, moba_q_g, moba_k_g, mla_qlat_g, mla_kvlat_g, mla_w_uq, mla_w_ukv, mla_q_g,
           mla_k_g, dil_q_g, dil_k_g, swa_q_g, swa_k_g, swa_sinks, group_out_g, w_o, mlp_norm_g, w_up, w_down):
    S = x.shape[1]
    assert S % max(d * Q_BLOCK for _, d in DILATED_BRANCHES) == 0 and S % INPROJ_TM == 0
    cos_t, sin_t = _rope_tables_t(S)
    kpos_t, aslope = _moba_aug_tables(S)
    params = (attn_norm_g, w_in, moba_q_g, moba_k_g, mla_qlat_g, mla_kvlat_g, mla_w_uq, mla_w_ukv, mla_q_g,
              mla_k_g, dil_q_g, dil_k_g, swa_q_g, swa_k_g, swa_sinks, group_out_g, w_o, mlp_norm_g, w_up, w_down)
    for l in range(attn_norm_g.shape[0]):
        x = _layer(x, cos_t, sin_t, kpos_t, aslope, *[p[l] for p in params])
    return x
```

```python
import numpy as np
import jax
import jax.numpy as jnp
from jax import lax
from jax.experimental import pallas as pl
from jax.experimental.pallas import tpu as pltpu

F32 = jnp.float32
BF16 = jnp.bfloat16

D_MODEL = 1024
HEAD_DIM = 64
N_HEADS = 4
GROUP = N_HEADS * HEAD_DIM
LANES = 128
N_HALF = GROUP // LANES
MOBA_BLOCK = 256
MOBA_TOPK = 3
MLA_Q_RANK = 256
MLA_KV_RANK = 128
MLA_NOPE = 64
MLA_ROPE = 32
MLA_QK = MLA_NOPE + MLA_ROPE
ROPE_THETA = 10000.0
DILATED_BRANCHES = ((128, 1), (512, 4), (2048, 16))
Q_BLOCK = 128
SWA_WINDOW = 128
SWA_KV_HEADS = 2
SWA_KV_WIDTH = SWA_KV_HEADS * HEAD_DIM
D_FF = 4 * D_MODEL
EPS = 1e-6
NEG = -1e30

_SECTIONS = (("a_q", GROUP), ("a_k", GROUP), ("a_v", GROUP), ("b_ql", MLA_Q_RANK), ("b_kvl", MLA_KV_RANK),
             ("b_kr", MLA_ROPE), ("c_q", GROUP), ("c_k", GROUP), ("c_v", GROUP), ("d_q", GROUP),
             ("d_k", SWA_KV_WIDTH), ("d_v", SWA_KV_WIDTH))
_CHUNKS = (("a_q", "a_k", "a_v", "b_ql", "b_kvl", "b_kr"), ("c_q", "c_k", "d_q", "d_k"), ("c_v", "d_v"))


def _projection_layout():
    width = dict(_SECTIONS)
    col, start = 0, {}
    for name, w in _SECTIONS:
        start[name] = col
        col += w
    rows, where, chunk_rows = [], {}, []
    for ci, chunk in enumerate(_CHUNKS):
        off = 0
        for name in chunk:
            where[name] = (ci, off, width[name])
            rows.extend(range(start[name], start[name] + width[name]))
            off += width[name]
        chunk_rows.append(off)
    return np.asarray(rows), where, tuple(chunk_rows)


_W1T_ROWS, _SECTION_AT, _CHUNK_ROWS = _projection_layout()
IN_COLS = len(_W1T_ROWS)

VMEM_LIMIT = 56 * 1024 * 1024

INPROJ_TM = 512
TAIL_TM = 512
ATT_TK = 256
ATT_TQ = 2 * ATT_TK
ATT_DK = 128
ATT_DV = HEAD_DIM + 16
AUG_ROWS = 8
LOG2E = 1.4426950408889634
FLASH_PAIRS = 2

_NT = (((1,), (1,)), ((), ()))


def _alibi_slopes():
    n = 3 * N_HEADS
    idx = np.arange(1, n + 1, dtype=np.float32).reshape(N_HEADS, 3)
    s = np.exp2(-8.0 * idx / n).astype(np.float32)
    return s[:, 0], s[:, 1], s[:, 2]


SLOPE_A, SLOPE_C, SLOPE_D = _alibi_slopes()


def _params(n_axes):
    return pltpu.CompilerParams(dimension_semantics=("arbitrary",) * n_axes,
                                vmem_limit_bytes=VMEM_LIMIT)


def _head_norm_t(sec, g_col, n_heads, width):
    outs = []
    for h in range(n_heads):
        s = sec[h * width:(h + 1) * width, :]
        ms = jnp.sum(s * s, axis=0, keepdims=True) * (1.0 / width)
        outs.append(s * lax.rsqrt(ms + EPS) * g_col)
    return outs


def _inproj_kernel(x_ref, gx_ref, w1t_ref, gaq_ref, gak_ref, gql_ref, gkvl_ref, wuqt_ref, wukvt_ref,
                   gbq_ref, gbk_ref, gcq_ref, gck_ref, gdq_ref, gdk_ref, cos_ref, sin_ref, kpos_ref, aslope_ref,
                   aqt_ref, aq32t_ref, ak_ref, akm_ref, avt_ref,
                   bqt_ref, bk_ref, bvt_ref,
                   cq_ref, ck_ref, cv_ref, dq_ref, dk_ref, dv_ref,
                   *h_scrs):
    tm = x_ref.shape[1]
    x = x_ref[0]
    ms = jnp.mean(x * x, axis=-1, keepdims=True)
    xn = (x * lax.rsqrt(ms + EPS) * gx_ref[...]).astype(BF16)
    row0 = 0
    for h_scr, n_rows in zip(h_scrs, _CHUNK_ROWS):
        h_scr[...] = lax.dot_general(w1t_ref[row0:row0 + n_rows, :], xn, _NT, preferred_element_type=F32)
        row0 += n_rows

    def sec(name, lo=0, hi=None):
        chunk, off, width = _SECTION_AT[name]
        return h_scrs[chunk][off + lo:off + (width if hi is None else hi), :]

    scale = HEAD_DIM ** -0.5

    ones_rows = jnp.ones((ATT_DV - HEAD_DIM, ATT_TK), F32)

    def store_vt(ref, h, vh):
        for c in range(tm // ATT_TK):
            ref[0, h, c] = jnp.concatenate([vh[:, c * ATT_TK:(c + 1) * ATT_TK], ones_rows], axis=0).astype(BF16)

    pad_rows = jnp.zeros((ATT_DK - HEAD_DIM - AUG_ROWS, tm), F32)
    qa = _head_norm_t(sec("a_q"), gaq_ref[...], N_HEADS, HEAD_DIM)
    ka = _head_norm_t(sec("a_k"), gak_ref[...], N_HEADS, HEAD_DIM)
    ka_nat = jnp.concatenate(ka, axis=0).T
    for c in range(tm // MOBA_BLOCK):
        akm_ref[0, c] = jnp.sum(ka_nat[c * MOBA_BLOCK:(c + 1) * MOBA_BLOCK, :], axis=0,
                                keepdims=True) * (1.0 / MOBA_BLOCK)
    kpos = kpos_ref[...]
    for h in range(N_HEADS):
        aq32t_ref[0, h] = qa[h]
        slope_rows = jnp.broadcast_to(aslope_ref[h], (AUG_ROWS, tm))
        aqt_ref[0, h] = jnp.concatenate([qa[h] * (scale * LOG2E), slope_rows, pad_rows], axis=0).astype(BF16)
        ak_ref[0, h] = jnp.concatenate([ka[h], kpos, pad_rows], axis=0).T.astype(BF16)
        store_vt(avt_ref, h, sec("a_v", h * HEAD_DIM, (h + 1) * HEAD_DIM))

    cos = cos_ref[...]
    sin = sin_ref[...]
    half = MLA_ROPE // 2

    def rope_pad(t, sc):
        x1 = t[MLA_NOPE:MLA_NOPE + half, :]
        x2 = t[MLA_NOPE + half:MLA_QK, :]
        return jnp.concatenate([t[:MLA_NOPE, :] * sc, (x1 * cos - x2 * sin) * sc, (x1 * sin + x2 * cos) * sc,
                                jnp.zeros((ATT_DK - MLA_QK, tm), F32)], axis=0)

    ql = sec("b_ql")
    ql = ql * lax.rsqrt(jnp.sum(ql * ql, axis=0, keepdims=True) * (1.0 / MLA_Q_RANK) + EPS) * gql_ref[...]
    qb = jnp.dot(wuqt_ref[...], ql.astype(BF16), preferred_element_type=F32)
    qb = _head_norm_t(qb, gbq_ref[...], N_HEADS, MLA_QK)
    kvl = sec("b_kvl")
    kvl = kvl * lax.rsqrt(jnp.sum(kvl * kvl, axis=0, keepdims=True) * (1.0 / MLA_KV_RANK) + EPS) * gkvl_ref[...]
    kvb = jnp.dot(wukvt_ref[...], kvl.astype(BF16), preferred_element_type=F32)
    kr = sec("b_kr")
    gbk = gbk_ref[...]
    for h in range(N_HEADS):
        bqt_ref[0, h] = rope_pad(qb[h], MLA_QK ** -0.5 * LOG2E).astype(BF16)
        kh = jnp.concatenate([kvb[h * 2 * HEAD_DIM:h * 2 * HEAD_DIM + MLA_NOPE, :], kr], axis=0)
        kh = kh * lax.rsqrt(jnp.sum(kh * kh, axis=0, keepdims=True) * (1.0 / MLA_QK) + EPS) * gbk
        bk_ref[0, h] = rope_pad(kh, 1.0).T.astype(BF16)
        store_vt(bvt_ref, h, kvb[h * 2 * HEAD_DIM + MLA_NOPE:(h + 1) * 2 * HEAD_DIM, :])

    qc = jnp.concatenate(_head_norm_t(sec("c_q"), gcq_ref[...], N_HEADS, HEAD_DIM), axis=0)
    kc = jnp.concatenate(_head_norm_t(sec("c_k"), gck_ref[...], N_HEADS, HEAD_DIM), axis=0)
    for ref, val in ((cq_ref, qc * (scale * LOG2E)), (ck_ref, kc), (cv_ref, sec("c_v"))):
        for c in range(N_HALF):
            ref[0, c] = val[c * LANES:(c + 1) * LANES, :].T

    qd = _head_norm_t(sec("d_q"), gdq_ref[...], N_HEADS, HEAD_DIM)
    qd = jnp.concatenate([qd[h] for h in SWA_HEAD_ORDER], axis=0)
    dq_ref[0] = (qd * (scale * LOG2E)).T.astype(BF16)
    kd = jnp.concatenate(_head_norm_t(sec("d_k"), gdk_ref[...], SWA_KV_HEADS, HEAD_DIM),
                         axis=0)
    dk_ref[0] = kd.T.astype(BF16)
    dv_ref[0] = sec("d_v").T.astype(BF16)


def _inproj(x, gx, w1t, gaq, gak, gql, gkvl, wuqt, wukvt, gbq, gbk, gcq, gck, gdq, gdk, cos_t, sin_t, kpos_t,
            aslope):
    B, S, _ = x.shape
    tm = INPROJ_TM
    nb = S // ATT_TK
    cpt = tm // ATT_TK
    H = N_HEADS
    full = lambda a: pl.BlockSpec(a.shape, lambda b, t: (0,) * a.ndim)
    in_specs = [pl.BlockSpec((1, tm, D_MODEL), lambda b, t: (b, t, 0)), full(gx), full(w1t), full(gaq), full(gak),
                full(gql), full(gkvl), full(wuqt), full(wukvt), full(gbq), full(gbk), full(gcq), full(gck),
                full(gdq), full(gdk),
                pl.BlockSpec((MLA_ROPE // 2, tm), lambda b, t: (0, t)),
                pl.BlockSpec((MLA_ROPE // 2, tm), lambda b, t: (0, t)),
                pl.BlockSpec((AUG_ROWS, tm), lambda b, t: (0, t)), full(aslope)]
    head_t = lambda w: pl.BlockSpec((1, H, w, tm), lambda b, t: (b, 0, 0, t))
    head_n = pl.BlockSpec((1, H, tm, ATT_DK), lambda b, t: (b, 0, t, 0))
    vt_spec = pl.BlockSpec((1, H, cpt, ATT_DV, ATT_TK), lambda b, t: (b, 0, t, 0, 0))
    nat = lambda w: pl.BlockSpec((1, tm, w), lambda b, t: (b, t, 0))
    halves = pl.BlockSpec((1, N_HALF, tm, LANES), lambda b, t: (b, 0, t, 0))
    out_shape = [
        jax.ShapeDtypeStruct((B, H, ATT_DK, S), BF16),
        jax.ShapeDtypeStruct((B, H, HEAD_DIM, S), F32),
        jax.ShapeDtypeStruct((B, H, S, ATT_DK), BF16),
        jax.ShapeDtypeStruct((B, nb, 1, GROUP), F32),
        jax.ShapeDtypeStruct((B, H, nb, ATT_DV, ATT_TK), BF16),
        jax.ShapeDtypeStruct((B, H, ATT_DK, S), BF16),
        jax.ShapeDtypeStruct((B, H, S, ATT_DK), BF16),
        jax.ShapeDtypeStruct((B, H, nb, ATT_DV, ATT_TK), BF16),
        jax.ShapeDtypeStruct((B, N_HALF, S, LANES), F32),
        jax.ShapeDtypeStruct((B, N_HALF, S, LANES), F32),
        jax.ShapeDtypeStruct((B, N_HALF, S, LANES), F32),
        jax.ShapeDtypeStruct((B, S, GROUP), BF16),
        jax.ShapeDtypeStruct((B, S, SWA_KV_WIDTH), BF16),
        jax.ShapeDtypeStruct((B, S, SWA_KV_WIDTH), BF16),
    ]
    out_specs = [head_t(ATT_DK), head_t(HEAD_DIM), head_n,
                 pl.BlockSpec((1, tm // MOBA_BLOCK, 1, GROUP), lambda b, t: (b, t, 0, 0)), vt_spec,
                 head_t(ATT_DK), head_n, vt_spec,
                 halves, halves, halves, nat(GROUP), nat(SWA_KV_WIDTH), nat(SWA_KV_WIDTH)]
    return pl.pallas_call(
        _inproj_kernel,
        grid=(B, S // tm),
        in_specs=in_specs,
        out_specs=out_specs,
        out_shape=out_shape,
        scratch_shapes=[pltpu.VMEM((n_rows, tm), F32) for n_rows in _CHUNK_ROWS],
        compiler_params=_params(2),
        name="inproj",
    )(x, gx, w1t, gaq, gak, gql, gkvl, wuqt, wukvt, gbq, gbk, gcq, gck, gdq, gdk, cos_t, sin_t, kpos_t, aslope)


def _flash_scratch(tq):
    s_buf, p_buf = pltpu.VMEM((N_HEADS, ATT_TK, tq), F32), pltpu.VMEM((N_HEADS, ATT_TK, tq), BF16)
    return [s_buf, s_buf, p_buf, p_buf, pltpu.VMEM((N_HEADS, ATT_DV, tq), F32)]


def _flash_heads(qt_ref, k_ref, vt_ref, o_ref, s_bufs, p_bufs, acc_scr, rowb_scr, i):
    tq = o_ref.shape[3]
    n_tail = tq // ATT_TK
    assert n_tail == 2
    n_past = i * n_tail
    heads = range(N_HEADS)

    def block_of(pos):
        return jnp.where(pos < n_tail, n_past + pos, pos - n_tail)

    def head_scores(h, slot, blk):
        kb = k_ref[0, h, pl.ds(pl.multiple_of(blk * ATT_TK, ATT_TK), ATT_TK), :]
        s_bufs[slot][h] = jnp.dot(kb, qt_ref[0, h], preferred_element_type=F32)

    def stage_scores(slot, blk):
        for h in heads:
            head_scores(h, slot, blk)

    def stage_softmax(slot, blk, ms, causal=None):
        new_ms, alphas = [], []
        for h in heads:
            m_parts, a_parts = [], []
            rv_row = None if rowb_scr is None else rowb_scr[h, pl.ds(blk, 1), :]
            for c in range(tq // LANES):
                cols = slice(c * LANES, (c + 1) * LANES)
                st = s_bufs[slot][h, :, cols]
                if causal is not None:
                    st = jnp.where(causal[:, cols], st, 2 * NEG)
                cm = jnp.max(st, axis=0, keepdims=True)
                m_old = ms[h][:, cols]
                if rowb_scr is None:
                    m_new = jnp.maximum(m_old, cm)
                    shift = m_new
                else:
                    rv = rv_row[:, cols]
                    m_new = jnp.maximum(m_old, cm + rv)
                    shift = m_new - rv
                a_parts.append(jnp.exp2(m_old - m_new))
                m_parts.append(m_new)
                p_bufs[slot][h, :, cols] = jnp.exp2(st - shift).astype(BF16)
            new_ms.append(jnp.concatenate(m_parts, axis=1))
            alphas.append(jnp.concatenate(a_parts, axis=1))
        return tuple(new_ms), tuple(alphas)

    def head_values(h, slot, blk, alphas):
        acc_scr[h] = alphas[h] * acc_scr[h] + jnp.dot(vt_ref[0, h, blk], p_bufs[slot][h],
                                                      preferred_element_type=F32)

    def stage_values(slot, blk, alphas):
        for h in heads:
            head_values(h, slot, blk, alphas)

    krow = lax.broadcasted_iota(jnp.int32, (ATT_TK, tq), 0)
    ti = lax.broadcasted_iota(jnp.int32, (ATT_TK, tq), 1)
    for h in heads:
        acc_scr[h] = jnp.zeros((ATT_DV, tq), F32)
    stage_scores(0, n_past)
    stage_scores(1, n_past + 1)
    ms = tuple(jnp.full((1, tq), NEG, F32) for _ in heads)
    ms, alphas = stage_softmax(0, n_past, ms, causal=ti >= krow)
    stage_values(0, n_past, alphas)
    ms, alphas = stage_softmax(1, n_past + 1, ms, causal=ti >= krow + ATT_TK)
    stage_scores(0, 0)

    def step(pos, slot, ms, alphas):
        ms, new_alphas = stage_softmax(1 - slot, pos + 1 - n_tail, ms)
        blk = block_of(pos)
        for h in heads:
            head_values(h, slot, blk, alphas)
            head_scores(h, slot, pos)
        return ms, new_alphas

    def pair(first, carry):
        ms, alphas = step(first, 1, *carry)
        return step(first + 1, 0, ms, alphas)

    def trip(t, carry):
        for u in range(FLASH_PAIRS):
            carry = pair(2 * (FLASH_PAIRS * t + u) + 1, carry)
        return carry

    carry = lax.fori_loop(0, i // FLASH_PAIRS, trip, (ms, alphas))
    ms, alphas = lax.fori_loop(i - i % FLASH_PAIRS, i, lambda u, c: pair(2 * u + 1, c), carry)
    last = n_past + 1
    stage_values(1, block_of(last), alphas)
    for h in heads:
        acc = acc_scr[h]
        o_ref[0, h] = acc[:HEAD_DIM, :] / acc[HEAD_DIM:HEAD_DIM + 1, :]


def _moba_kernel(qt_ref, q32t_ref, k_ref, vt_ref, kmean_ref, o_ref, rowb_scr, s0, s1, p0, p1, acc_scr):
    i = pl.program_id(1)
    nb = kmean_ref.shape[2]
    tq = qt_ref.shape[3]
    blk = lax.broadcasted_iota(jnp.int32, (nb, tq), 0)
    col = lax.broadcasted_iota(jnp.int32, (nb, tq), 1)
    qblk = i * (tq // MOBA_BLOCK) + col // MOBA_BLOCK
    past = blk < qblk
    dist0 = (i * tq + col - blk * MOBA_BLOCK).astype(F32)
    for h in range(N_HEADS):
        gate = jnp.dot(kmean_ref[0, h], q32t_ref[0, h], preferred_element_type=F32,
                       precision=lax.Precision.HIGHEST)
        gate = jnp.where(past, gate, NEG)
        sel = blk == qblk
        for _ in range(MOBA_TOPK):
            best = jnp.max(gate, axis=0, keepdims=True)
            first = jnp.min(jnp.where(gate == best, blk, nb), axis=0, keepdims=True)
            pick = blk == first
            sel = jnp.logical_or(sel, jnp.logical_and(pick, past))
            gate = jnp.where(pick, -jnp.inf, gate)
        rowb_scr[h] = jnp.where(sel, (-float(SLOPE_A[h]) * LOG2E) * dist0, 2 * NEG)
    _flash_heads(qt_ref, k_ref, vt_ref, o_ref, (s0, s1), (p0, p1), acc_scr, rowb_scr, i)


def _flash_specs(S, tq):
    H, nb = N_HEADS, S // ATT_TK
    return ([pl.BlockSpec((1, H, ATT_DK, tq), lambda b, i: (b, 0, 0, i)),
             pl.BlockSpec((1, H, S, ATT_DK), lambda b, i: (b, 0, 0, 0)),
             pl.BlockSpec((1, H, nb, ATT_DV, ATT_TK), lambda b, i: (b, 0, 0, 0, 0))],
            pl.BlockSpec((1, H, HEAD_DIM, tq), lambda b, i: (b, 0, 0, i)))


def _moba(aqt, aq32t, ak, avt, kmean_h):
    B, H, _, S = aqt.shape
    tq = ATT_TQ
    nb = S // ATT_TK
    (q_spec, k_spec, vt_spec), o_spec = _flash_specs(S, tq)
    return pl.pallas_call(
        _moba_kernel,
        grid=(B, S // tq),
        in_specs=[q_spec, pl.BlockSpec((1, H, HEAD_DIM, tq), lambda b, i: (b, 0, 0, i)), k_spec, vt_spec,
                  pl.BlockSpec((1, H, nb, HEAD_DIM), lambda b, i: (b, 0, 0, 0))],
        out_specs=o_spec,
        out_shape=jax.ShapeDtypeStruct((B, H, HEAD_DIM, S), F32),
        scratch_shapes=[pltpu.VMEM((H, nb, tq), F32)] + _flash_scratch(tq),
        compiler_params=_params(2),
        name="moba",
    )(aqt, aq32t, ak, avt, kmean_h)


def _mla_kernel(qt_ref, k_ref, vt_ref, o_ref, s0, s1, p0, p1, acc_scr):
    _flash_heads(qt_ref, k_ref, vt_ref, o_ref, (s0, s1), (p0, p1), acc_scr, None, pl.program_id(1))


def _mla(bqt, bk, bvt):
    B, H, _, S = bqt.shape
    tq = ATT_TQ
    in_specs, o_spec = _flash_specs(S, tq)
    return pl.pallas_call(
        _mla_kernel,
        grid=(B, S // tq),
        in_specs=in_specs,
        out_specs=o_spec,
        out_shape=jax.ShapeDtypeStruct((B, H, HEAD_DIM, S), F32),
        scratch_shapes=_flash_scratch(tq),
        compiler_params=_params(2),
        name="mla",
    )(bqt, bk, bvt)


DIL_SPAN = max(d for _, d in DILATED_BRANCHES) * Q_BLOCK
DIL_UNITS = DIL_SPAN // Q_BLOCK
DIL_GROUP = (4, 2, 2)


def _dilated_bias():
    qi = np.arange(Q_BLOCK)[:, None]
    kidx = np.arange(2 * Q_BLOCK)[None, :]
    rel = qi + Q_BLOCK - kidx
    out = np.empty((2, len(DILATED_BRANCHES), N_HEADS, Q_BLOCK, 2 * Q_BLOCK), np.float32)
    for bi, (window, d) in enumerate(DILATED_BRANCHES):
        valid = (rel >= 0) & (rel <= window // d)
        for h in range(N_HEADS):
            bias = -SLOPE_C[h] * np.float32(LOG2E) * (d * rel).astype(np.float32)
            out[0, bi, h] = np.where(valid, bias, NEG)
            out[1, bi, h] = np.where(valid & (kidx >= Q_BLOCK), bias, NEG)
    return out


def _rows_load(ref, lead, start, size, stride):
    return jnp.concatenate([ref[lead + (c, pl.ds(start, size, stride=stride), slice(None))]
                            for c in range(N_HALF)], axis=-1)


def _rows_store(ref, lead, start, size, stride, val):
    for c in range(N_HALF):
        ref[lead + (c, pl.ds(start, size, stride=stride), slice(None))] = val[:, c * LANES:(c + 1) * LANES]


def _dilated_kernel(q_ref, kp_ref, kc_ref, vp_ref, vc_ref, bias_ref, o_ref, kbuf, vbuf, m_scr, den_scr, num_scr):
    span = pl.program_id(1)
    kbuf[:, 0:DIL_SPAN, :] = kp_ref[0]
    kbuf[:, DIL_SPAN:, :] = kc_ref[0]
    vbuf[:, 0:DIL_SPAN, :] = vp_ref[0]
    vbuf[:, DIL_SPAN:, :] = vc_ref[0]
    lane_head = lax.broadcasted_iota(jnp.int32, (1, GROUP), 1) // HEAD_DIM
    hmask = [lane_head == h for h in range(N_HEADS)]
    hmask_f = [m.astype(F32) for m in hmask]
    first_span = jnp.where(span == 0, 1, 0)

    def per_head(cols):
        out = cols[N_HEADS - 1]
        for h in range(N_HEADS - 2, -1, -1):
            out = jnp.where(hmask[h], cols[h], out)
        return out

    order = sorted(range(len(DILATED_BRANCHES)), key=lambda b: -DILATED_BRANCHES[b][1])
    for bi in order:
        d, n_group = DILATED_BRANCHES[bi][1], DIL_GROUP[bi]
        first, last = bi == order[0], bi == order[-1]

        def group(g, _, bi=bi, d=d, n_group=n_group, first=first, last=last):
            fronts = []
            for uu in range(n_group):
                u = g * n_group + uu
                r, n = u % d, u // d
                qstart = n * (Q_BLOCK * d) + r
                kstart = DIL_SPAN + qstart - Q_BLOCK * d
                q = _rows_load(q_ref, (0,), qstart, Q_BLOCK, d)
                k2 = _rows_load(kbuf, (), kstart, 2 * Q_BLOCK, d).astype(BF16)
                v2 = _rows_load(vbuf, (), kstart, 2 * Q_BLOCK, d).astype(BF16)
                q4 = jnp.concatenate([(q * hmask_f[h]).astype(BF16) for h in range(N_HEADS)], axis=0)
                s4 = lax.dot_general(q4, k2, _NT, preferred_element_type=F32)
                variant = jnp.where(n == 0, first_span, 0)
                fronts.append((qstart, s4, v2, variant))
            for qstart, s4, v2, variant in fronts:
                es, ms, ls = [], [], []
                for h in range(N_HEADS):
                    s = s4[h * Q_BLOCK:(h + 1) * Q_BLOCK, :] + bias_ref[variant, bi, h]
                    m = jnp.max(s, axis=-1, keepdims=True)
                    e = jnp.exp2(s - m)
                    ls.append(jnp.sum(e, axis=-1, keepdims=True))
                    ms.append(m)
                    es.append(e.astype(BF16))
                o4 = jnp.dot(jnp.concatenate(es, axis=0), v2, preferred_element_type=F32)
                o = o4[(N_HEADS - 1) * Q_BLOCK:, :]
                for h in range(N_HEADS - 2, -1, -1):
                    o = jnp.where(hmask[h], o4[h * Q_BLOCK:(h + 1) * Q_BLOCK, :], o)
                m_b, l_b = per_head(ms), per_head(ls)
                at = ((), qstart, Q_BLOCK, d)
                if first:
                    _rows_store(m_scr, *at, m_b)
                    _rows_store(num_scr, *at, o)
                    _rows_store(den_scr, *at, l_b)
                else:
                    m_old = _rows_load(m_scr, *at)
                    m_new = jnp.maximum(m_old, m_b)
                    a, b = jnp.exp2(m_old - m_new), jnp.exp2(m_b - m_new)
                    num = a * _rows_load(num_scr, *at) + b * o
                    den = a * _rows_load(den_scr, *at) + b * l_b
                    if last:
                        _rows_store(o_ref, (0,), qstart, Q_BLOCK, d, num / den)
                    else:
                        _rows_store(m_scr, *at, m_new)
                        _rows_store(num_scr, *at, num)
                        _rows_store(den_scr, *at, den)
            return 0

        lax.fori_loop(0, DIL_UNITS // n_group, group, 0)


def _dilated(cq, ck, cv):
    B, _, S, _ = cq.shape
    cur = pl.BlockSpec((1, N_HALF, DIL_SPAN, LANES), lambda b, s: (b, 0, s, 0))
    prev = pl.BlockSpec((1, N_HALF, DIL_SPAN, LANES), lambda b, s: (b, 0, jnp.maximum(s - 1, 0), 0))
    bias = jnp.asarray(_dilated_bias())
    return pl.pallas_call(
        _dilated_kernel,
        grid=(B, S // DIL_SPAN),
        in_specs=[cur, prev, cur, prev, cur, pl.BlockSpec(bias.shape, lambda b, s: (0,) * bias.ndim)],
        out_specs=cur,
        out_shape=jax.ShapeDtypeStruct((B, N_HALF, S, LANES), F32),
        scratch_shapes=[pltpu.VMEM((N_HALF, 2 * DIL_SPAN, LANES), F32),
                        pltpu.VMEM((N_HALF, 2 * DIL_SPAN, LANES), F32),
                        pltpu.VMEM((N_HALF, DIL_SPAN, LANES), F32), pltpu.VMEM((N_HALF, DIL_SPAN, LANES), F32),
                        pltpu.VMEM((N_HALF, DIL_SPAN, LANES), F32)],
        compiler_params=_params(2),
        name="dilated",
    )(cq, ck, ck, cv, cv, bias)


SWA_SPAN = 1024
SWA_GROUP = 4
SWA_HEAD_ORDER = (0, 2, 1, 3)


def _swa_bias():
    qi = np.arange(Q_BLOCK)[:, None]
    kidx = np.arange(2 * Q_BLOCK)[None, :]
    rel = qi + Q_BLOCK - kidx
    valid = (rel >= 0) & (rel < SWA_WINDOW)
    out = np.empty((2, N_HEADS, Q_BLOCK, 2 * Q_BLOCK), np.float32)
    for h in range(N_HEADS):
        bias = -SLOPE_D[h] * np.float32(LOG2E) * rel.astype(np.float32)
        out[0, h] = np.where(valid, bias, NEG)
        out[1, h] = np.where(valid & (kidx >= Q_BLOCK), bias, NEG)
    return out


def _swa_kernel(sink_ref, q_ref, k_ref, v_ref, bias_ref, o_ref):
    span = pl.program_id(1)
    units = SWA_SPAN // Q_BLOCK
    half = lax.broadcasted_iota(jnp.int32, (1, LANES), 1) // HEAD_DIM
    lo = half == 0
    kv_mask = [jnp.where(half == g, 1.0, 0.0).astype(BF16) for g in range(SWA_KV_HEADS)]

    def group(g, _):
        fronts = []
        for uu in range(SWA_GROUP):
            u = g * SWA_GROUP + uu
            n = span * units + u
            lo_start = pl.multiple_of(jnp.maximum(n - 1, 0) * Q_BLOCK, Q_BLOCK)
            hi_start = pl.multiple_of(n * Q_BLOCK, Q_BLOCK)
            qstart = pl.multiple_of(u * Q_BLOCK, Q_BLOCK)
            q = q_ref[0, pl.ds(qstart, Q_BLOCK), :]
            k2 = jnp.concatenate([k_ref[0, pl.ds(lo_start, Q_BLOCK), :], k_ref[0, pl.ds(hi_start, Q_BLOCK), :]],
                                 axis=0)
            v2 = jnp.concatenate([v_ref[0, pl.ds(lo_start, Q_BLOCK), :], v_ref[0, pl.ds(hi_start, Q_BLOCK), :]],
                                 axis=0)
            q4 = jnp.concatenate([q[:, (h % 2) * LANES:(h % 2 + 1) * LANES] * kv_mask[h // 2]
                                  for h in range(N_HEADS)], axis=0)
            s4 = lax.dot_general(q4, k2, _NT, preferred_element_type=F32)
            fronts.append((qstart, s4, v2, jnp.where(n == 0, 1, 0)))
        for qstart, s4, v2, variant in fronts:
            es, ls = [], []
            for h in range(N_HEADS):
                s = s4[h * Q_BLOCK:(h + 1) * Q_BLOCK, :] + bias_ref[variant, h]
                sink = sink_ref[h] * LOG2E
                m = jnp.maximum(jnp.max(s, axis=-1, keepdims=True), sink)
                e = jnp.exp2(s - m)
                ls.append(jnp.sum(e, axis=-1, keepdims=True) + jnp.exp2(sink - m))
                es.append(e.astype(BF16))
            o4 = jnp.dot(jnp.concatenate(es, axis=0), v2, preferred_element_type=F32)
            tiles = []
            for t in range(N_HALF):
                a, b = t, t + 2
                tiles.append(jnp.where(lo, o4[a * Q_BLOCK:(a + 1) * Q_BLOCK, :] / ls[a],
                                       o4[b * Q_BLOCK:(b + 1) * Q_BLOCK, :] / ls[b]))
            o_ref[0, pl.ds(qstart, Q_BLOCK), :] = jnp.concatenate(tiles, axis=-1)
        return 0

    lax.fori_loop(0, units // SWA_GROUP, group, 0)


def _swa(dq, dk, dv, sinks):
    B, S, _ = dq.shape
    bias = jnp.asarray(_swa_bias())
    return pl.pallas_call(
        _swa_kernel,
        grid=(B, S // SWA_SPAN),
        in_specs=[pl.BlockSpec(memory_space=pltpu.SMEM),
                  pl.BlockSpec((1, SWA_SPAN, GROUP), lambda b, i: (b, i, 0)),
                  pl.BlockSpec((1, S, SWA_KV_WIDTH), lambda b, i: (b, 0, 0)),
                  pl.BlockSpec((1, S, SWA_KV_WIDTH), lambda b, i: (b, 0, 0)),
                  pl.BlockSpec(bias.shape, lambda b, i: (0,) * bias.ndim)],
        out_specs=pl.BlockSpec((1, SWA_SPAN, GROUP), lambda b, i: (b, i, 0)),
        out_shape=jax.ShapeDtypeStruct((B, S, GROUP), F32),
        compiler_params=_params(2),
        name="swa",
    )(sinks, dq, dk, dv, bias)


def _row_norm(y, g_row):
    return y * lax.rsqrt(jnp.mean(y * y, axis=-1, keepdims=True) + EPS) * g_row


def _tail_kernel(x_ref, oat_ref, obt_ref, oc_ref, od_ref, gg_ref, wo_ref, gm_ref, wup_ref, wdn_ref, out_ref):
    gg = gg_ref[...]

    def col_norm_t(yt, g_row):
        y = (yt * lax.rsqrt(jnp.mean(yt * yt, axis=0, keepdims=True) + EPS)).T
        return y * g_row

    ga = col_norm_t(oat_ref[0], gg[0:1, :])
    gb = col_norm_t(obt_ref[0], gg[1:2, :])
    gc = _row_norm(jnp.concatenate([oc_ref[0, c] for c in range(N_HALF)], axis=-1), gg[2:3, :])
    gd = _row_norm(od_ref[0], gg[3:4, :])
    mixed = jnp.concatenate([ga, gb, gc, gd], axis=-1).astype(BF16)
    x1 = x_ref[0] + jnp.dot(mixed, wo_ref[...], preferred_element_type=F32)
    xn = _row_norm(x1, gm_ref[...]).astype(BF16)
    u = jnp.maximum(jnp.dot(xn, wup_ref[...], preferred_element_type=F32), 0.0)
    out_ref[0] = x1 + jnp.dot((u * u).astype(BF16), wdn_ref[...], preferred_element_type=F32)


def _tail(x, oat, obt, oc, od, gg, wo, gm, wup, wdn):
    B, S, _ = x.shape
    tm = TAIL_TM
    nat = lambda w: pl.BlockSpec((1, tm, w), lambda b, t: (b, t, 0))
    ft = pl.BlockSpec((1, GROUP, tm), lambda b, t: (b, 0, t))
    const = lambda a: pl.BlockSpec(a.shape, lambda b, t: (0,) * a.ndim, pipeline_mode=pl.Buffered(1))
    return pl.pallas_call(
        _tail_kernel,
        grid=(B, S // tm),
        in_specs=[nat(D_MODEL), ft, ft, pl.BlockSpec((1, N_HALF, tm, LANES), lambda b, t: (b, 0, t, 0)),
                  nat(GROUP), const(gg), const(wo), const(gm),
                  const(wup), const(wdn)],
        out_specs=nat(D_MODEL),
        out_shape=jax.ShapeDtypeStruct((B, S, D_MODEL), F32),
        compiler_params=_params(2),
        name="tail",
    )(x, oat, obt, oc, od, gg, wo, gm, wup, wdn)


def _rope_tables_t(S):
    inv = 1.0 / (ROPE_THETA ** (jnp.arange(0, MLA_ROPE, 2, dtype=F32) / MLA_ROPE))
    ang = inv[:, None] * jnp.arange(S, dtype=F32)[None, :]
    return jnp.cos(ang), jnp.sin(ang)


def _moba_aug_tables(S):
    pos = np.arange(S, dtype=np.float32) % MOBA_BLOCK
    kpos = np.zeros((AUG_ROWS, S), np.float32)
    kpos[:3] = pos
    aslope = np.zeros((N_HEADS, AUG_ROWS, 1), np.float32)
    rest = (SLOPE_A * np.float32(LOG2E)).astype(np.float32)
    for r in range(3):
        piece = rest.astype(BF16).astype(np.float32)
        aslope[:, r, 0] = piece
        rest = rest - piece
    return jnp.asarray(kpos), jnp.asarray(aslope)


def _layer(x, cos_t, sin_t, kpos_t, aslope, attn_norm_g, w_in, moba_q_g, moba_k_g, mla_qlat_g, mla_kvlat_g, mla_w_uq, mla_w_ukv,
           mla_q_g, mla_k_g, dil_q_g, dil_k_g, swa_q_g, swa_k_g, swa_sinks, group_out_g, w_o, mlp_norm_g,
           w_up, w_down):
    B, S, _ = x.shape
    col = lambda g: g.reshape(-1, 1)
    (aqt, aq32t, ak, akm, avt, bqt, bk, bvt, cq, ck, cv, dq, dk, dv) = _inproj(
        x, attn_norm_g.reshape(1, -1), w_in[:, _W1T_ROWS].T.astype(BF16), col(moba_q_g), col(moba_k_g), col(mla_qlat_g),
        col(mla_kvlat_g), mla_w_uq.T.astype(BF16), mla_w_ukv.T.astype(BF16), col(mla_q_g), col(mla_k_g),
        col(dil_q_g), col(dil_k_g), col(swa_q_g), col(swa_k_g), cos_t, sin_t, kpos_t, aslope)
    nb = S // MOBA_BLOCK
    kmean_h = akm.reshape(B, nb, N_HEADS, HEAD_DIM).transpose(0, 2, 1, 3)
    oat = _moba(aqt, aq32t, ak, avt, kmean_h).reshape(B, GROUP, S)
    obt = _mla(bqt, bk, bvt).reshape(B, GROUP, S)
    oc = _dilated(cq, ck, cv)
    od = _swa(dq, dk, dv, swa_sinks)
    perm = np.concatenate([np.arange(HEAD_DIM) + HEAD_DIM * h for h in SWA_HEAD_ORDER])
    gg = group_out_g.at[3].set(group_out_g[3][perm])
    wo = jnp.concatenate([w_o[:3 * GROUP], w_o[3 * GROUP + perm]], axis=0)
    return _tail(x, oat, obt, oc, od, gg, wo.astype(BF16), mlp_norm_g.reshape(1, -1),
                 w_up.astype(BF16), w_down.astype(BF16))


def kernel(x, attn_norm_g, w_in, moba_q_g, moba_k_g, mla_qlat_g, mla_kvlat_g, mla_w_uq, mla_w_ukv, mla_q_g,
           mla_k_g, dil_q_g, dil_k_g, swa_q_g, swa_k_g, swa_sinks, group_out_g, w_o, mlp_norm_g, w_up, w_down):
    S = x.shape[1]
    assert S % max(d * Q_BLOCK for _, d in DILATED_BRANCHES) == 0 and S % INPROJ_TM == 0
    cos_t, sin_t = _rope_tables_t(S)
    kpos_t, aslope = _moba_aug_tables(S)
    params = (attn_norm_g, w_in, moba_q_g, moba_k_g, mla_qlat_g, mla_kvlat_g, mla_w_uq, mla_w_ukv, mla_q_g,
              mla_k_g, dil_q_g, dil_k_g, swa_q_g, swa_k_g, swa_sinks, group_out_g, w_o, mlp_norm_g, w_up, w_down)
    for l in range(attn_norm_g.shape[0]):
        x = _layer(x, cos_t, sin_t, kpos_t, aslope, *[p[l] for p in params])
    return x
```

```python
import numpy as np
import jax
import jax.numpy as jnp
from jax import lax
from jax.experimental import pallas as pl
from jax.experimental.pallas import tpu as pltpu

F32 = jnp.float32
BF16 = jnp.bfloat16

D_MODEL = 1024
HEAD_DIM = 64
N_HEADS = 4
GROUP = N_HEADS * HEAD_DIM
LANES = 128
N_HALF = GROUP // LANES
MOBA_BLOCK = 256
MOBA_TOPK = 3
MLA_Q_RANK = 256
MLA_KV_RANK = 128
MLA_NOPE = 64
MLA_ROPE = 32
MLA_QK = MLA_NOPE + MLA_ROPE
ROPE_THETA = 10000.0
DILATED_BRANCHES = ((128, 1), (512, 4), (2048, 16))
Q_BLOCK = 128
SWA_WINDOW = 128
SWA_KV_HEADS = 2
SWA_KV_WIDTH = SWA_KV_HEADS * HEAD_DIM
D_FF = 4 * D_MODEL
EPS = 1e-6
NEG = -1e30

_SECTIONS = (("a_q", GROUP), ("a_k", GROUP), ("a_v", GROUP), ("b_ql", MLA_Q_RANK), ("b_kvl", MLA_KV_RANK),
             ("b_kr", MLA_ROPE), ("c_q", GROUP), ("c_k", GROUP), ("c_v", GROUP), ("d_q", GROUP),
             ("d_k", SWA_KV_WIDTH), ("d_v", SWA_KV_WIDTH))
_CHUNKS = (("a_q", "a_k", "a_v", "b_ql", "b_kvl", "b_kr"), ("c_q", "c_k", "d_q", "d_k"), ("c_v", "d_v"))


def _projection_layout():
    width = dict(_SECTIONS)
    col, start = 0, {}
    for name, w in _SECTIONS:
        start[name] = col
        col += w
    rows, where, chunk_rows = [], {}, []
    for ci, chunk in enumerate(_CHUNKS):
        off = 0
        for name in chunk:
            where[name] = (ci, off, width[name])
            rows.extend(range(start[name], start[name] + width[name]))
            off += width[name]
        chunk_rows.append(off)
    return np.asarray(rows), where, tuple(chunk_rows)


_W1T_ROWS, _SECTION_AT, _CHUNK_ROWS = _projection_layout()
IN_COLS = len(_W1T_ROWS)

VMEM_LIMIT = 56 * 1024 * 1024

INPROJ_TM = 512
TAIL_TM = 512
ATT_TK = 256
ATT_TQ = 2 * ATT_TK
ATT_DK = 128
ATT_DV = HEAD_DIM + 16
AUG_ROWS = 8
LOG2E = 1.4426950408889634
FLASH_PAIRS = 2
FLASH_TILES = 2

_NT = (((1,), (1,)), ((), ()))


def _alibi_slopes():
    n = 3 * N_HEADS
    idx = np.arange(1, n + 1, dtype=np.float32).reshape(N_HEADS, 3)
    s = np.exp2(-8.0 * idx / n).astype(np.float32)
    return s[:, 0], s[:, 1], s[:, 2]


SLOPE_A, SLOPE_C, SLOPE_D = _alibi_slopes()


def _params(n_axes):
    return pltpu.CompilerParams(dimension_semantics=("arbitrary",) * n_axes,
                                vmem_limit_bytes=VMEM_LIMIT)


def _head_norm_t(sec, g_col, n_heads, width):
    outs = []
    for h in range(n_heads):
        s = sec[h * width:(h + 1) * width, :]
        ms = jnp.sum(s * s, axis=0, keepdims=True) * (1.0 / width)
        outs.append(s * lax.rsqrt(ms + EPS) * g_col)
    return outs


def _inproj_kernel(x_ref, gx_ref, w1t_ref, gaq_ref, gak_ref, gql_ref, gkvl_ref, wuqt_ref, wukvt_ref,
                   gbq_ref, gbk_ref, gcq_ref, gck_ref, gdq_ref, gdk_ref, cos_ref, sin_ref, kpos_ref, aslope_ref,
                   aqt_ref, aq32t_ref, ak_ref, akm_ref, avt_ref,
                   bqt_ref, bk_ref, bvt_ref,
                   cq_ref, ck_ref, cv_ref, dq_ref, dk_ref, dv_ref,
                   *h_scrs):
    tm = x_ref.shape[1]
    x = x_ref[0]
    ms = jnp.mean(x * x, axis=-1, keepdims=True)
    xn = (x * lax.rsqrt(ms + EPS) * gx_ref[...]).astype(BF16)
    row0 = 0
    for h_scr, n_rows in zip(h_scrs, _CHUNK_ROWS):
        h_scr[...] = lax.dot_general(w1t_ref[row0:row0 + n_rows, :], xn, _NT, preferred_element_type=F32)
        row0 += n_rows

    def sec(name, lo=0, hi=None):
        chunk, off, width = _SECTION_AT[name]
        return h_scrs[chunk][off + lo:off + (width if hi is None else hi), :]

    scale = HEAD_DIM ** -0.5

    ones_rows = jnp.ones((ATT_DV - HEAD_DIM, ATT_TK), F32)

    def store_vt(ref, h, vh):
        for c in range(tm // ATT_TK):
            ref[0, h, c] = jnp.concatenate([vh[:, c * ATT_TK:(c + 1) * ATT_TK], ones_rows], axis=0).astype(BF16)

    pad_rows = jnp.zeros((ATT_DK - HEAD_DIM - AUG_ROWS, tm), F32)
    qa = _head_norm_t(sec("a_q"), gaq_ref[...], N_HEADS, HEAD_DIM)
    ka = _head_norm_t(sec("a_k"), gak_ref[...], N_HEADS, HEAD_DIM)
    ka_nat = jnp.concatenate(ka, axis=0).T
    for c in range(tm // MOBA_BLOCK):
        akm_ref[0, c] = jnp.sum(ka_nat[c * MOBA_BLOCK:(c + 1) * MOBA_BLOCK, :], axis=0,
                                keepdims=True) * (1.0 / MOBA_BLOCK)
    kpos = kpos_ref[...]
    for h in range(N_HEADS):
        aq32t_ref[0, h] = qa[h]
        slope_rows = jnp.broadcast_to(aslope_ref[h], (AUG_ROWS, tm))
        aqt_ref[0, h] = jnp.concatenate([qa[h] * (scale * LOG2E), slope_rows, pad_rows], axis=0).astype(BF16)
        ak_ref[0, h] = jnp.concatenate([ka[h], kpos, pad_rows], axis=0).T.astype(BF16)
        store_vt(avt_ref, h, sec("a_v", h * HEAD_DIM, (h + 1) * HEAD_DIM))

    cos = cos_ref[...]
    sin = sin_ref[...]
    half = MLA_ROPE // 2

    def rope_pad(t, sc):
        x1 = t[MLA_NOPE:MLA_NOPE + half, :]
        x2 = t[MLA_NOPE + half:MLA_QK, :]
        return jnp.concatenate([t[:MLA_NOPE, :] * sc, (x1 * cos - x2 * sin) * sc, (x1 * sin + x2 * cos) * sc,
                                jnp.zeros((ATT_DK - MLA_QK, tm), F32)], axis=0)

    ql = sec("b_ql")
    ql = ql * lax.rsqrt(jnp.sum(ql * ql, axis=0, keepdims=True) * (1.0 / MLA_Q_RANK) + EPS) * gql_ref[...]
    qb = jnp.dot(wuqt_ref[...], ql.astype(BF16), preferred_element_type=F32)
    qb = _head_norm_t(qb, gbq_ref[...], N_HEADS, MLA_QK)
    kvl = sec("b_kvl")
    kvl = kvl * lax.rsqrt(jnp.sum(kvl * kvl, axis=0, keepdims=True) * (1.0 / MLA_KV_RANK) + EPS) * gkvl_ref[...]
    kvb = jnp.dot(wukvt_ref[...], kvl.astype(BF16), preferred_element_type=F32)
    kr = sec("b_kr")
    gbk = gbk_ref[...]
    for h in range(N_HEADS):
        bqt_ref[0, h] = rope_pad(qb[h], MLA_QK ** -0.5 * LOG2E).astype(BF16)
        kh = jnp.concatenate([kvb[h * 2 * HEAD_DIM:h * 2 * HEAD_DIM + MLA_NOPE, :], kr], axis=0)
        kh = kh * lax.rsqrt(jnp.sum(kh * kh, axis=0, keepdims=True) * (1.0 / MLA_QK) + EPS) * gbk
        bk_ref[0, h] = rope_pad(kh, 1.0).T.astype(BF16)
        store_vt(bvt_ref, h, kvb[h * 2 * HEAD_DIM + MLA_NOPE:(h + 1) * 2 * HEAD_DIM, :])

    qc = jnp.concatenate(_head_norm_t(sec("c_q"), gcq_ref[...], N_HEADS, HEAD_DIM), axis=0)
    kc = jnp.concatenate(_head_norm_t(sec("c_k"), gck_ref[...], N_HEADS, HEAD_DIM), axis=0)
    for ref, val in ((cq_ref, qc * (scale * LOG2E)), (ck_ref, kc), (cv_ref, sec("c_v"))):
        for c in range(N_HALF):
            ref[0, c] = val[c * LANES:(c + 1) * LANES, :].T

    qd = _head_norm_t(sec("d_q"), gdq_ref[...], N_HEADS, HEAD_DIM)
    qd = jnp.concatenate([qd[h] for h in SWA_HEAD_ORDER], axis=0)
    dq_ref[0] = (qd * (scale * LOG2E)).T.astype(BF16)
    kd = jnp.concatenate(_head_norm_t(sec("d_k"), gdk_ref[...], SWA_KV_HEADS, HEAD_DIM),
                         axis=0)
    dk_ref[0] = kd.T.astype(BF16)
    dv_ref[0] = sec("d_v").T.astype(BF16)


def _inproj(x, gx, w1t, gaq, gak, gql, gkvl, wuqt, wukvt, gbq, gbk, gcq, gck, gdq, gdk, cos_t, sin_t, kpos_t,
            aslope):
    B, S, _ = x.shape
    tm = INPROJ_TM
    nb = S // ATT_TK
    cpt = tm // ATT_TK
    H = N_HEADS
    full = lambda a: pl.BlockSpec(a.shape, lambda b, t: (0,) * a.ndim)
    in_specs = [pl.BlockSpec((1, tm, D_MODEL), lambda b, t: (b, t, 0)), full(gx), full(w1t), full(gaq), full(gak),
                full(gql), full(gkvl), full(wuqt), full(wukvt), full(gbq), full(gbk), full(gcq), full(gck),
                full(gdq), full(gdk),
                pl.BlockSpec((MLA_ROPE // 2, tm), lambda b, t: (0, t)),
                pl.BlockSpec((MLA_ROPE // 2, tm), lambda b, t: (0, t)),
                pl.BlockSpec((AUG_ROWS, tm), lambda b, t: (0, t)), full(aslope)]
    head_t = lambda w: pl.BlockSpec((1, H, w, tm), lambda b, t: (b, 0, 0, t))
    head_n = pl.BlockSpec((1, H, tm, ATT_DK), lambda b, t: (b, 0, t, 0))
    vt_spec = pl.BlockSpec((1, H, cpt, ATT_DV, ATT_TK), lambda b, t: (b, 0, t, 0, 0))
    nat = lambda w: pl.BlockSpec((1, tm, w), lambda b, t: (b, t, 0))
    halves = pl.BlockSpec((1, N_HALF, tm, LANES), lambda b, t: (b, 0, t, 0))
    out_shape = [
        jax.ShapeDtypeStruct((B, H, ATT_DK, S), BF16),
        jax.ShapeDtypeStruct((B, H, HEAD_DIM, S), F32),
        jax.ShapeDtypeStruct((B, H, S, ATT_DK), BF16),
        jax.ShapeDtypeStruct((B, nb, 1, GROUP), F32),
        jax.ShapeDtypeStruct((B, H, nb, ATT_DV, ATT_TK), BF16),
        jax.ShapeDtypeStruct((B, H, ATT_DK, S), BF16),
        jax.ShapeDtypeStruct((B, H, S, ATT_DK), BF16),
        jax.ShapeDtypeStruct((B, H, nb, ATT_DV, ATT_TK), BF16),
        jax.ShapeDtypeStruct((B, N_HALF, S, LANES), F32),
        jax.ShapeDtypeStruct((B, N_HALF, S, LANES), F32),
        jax.ShapeDtypeStruct((B, N_HALF, S, LANES), F32),
        jax.ShapeDtypeStruct((B, S, GROUP), BF16),
        jax.ShapeDtypeStruct((B, S, SWA_KV_WIDTH), BF16),
        jax.ShapeDtypeStruct((B, S, SWA_KV_WIDTH), BF16),
    ]
    out_specs = [head_t(ATT_DK), head_t(HEAD_DIM), head_n,
                 pl.BlockSpec((1, tm // MOBA_BLOCK, 1, GROUP), lambda b, t: (b, t, 0, 0)), vt_spec,
                 head_t(ATT_DK), head_n, vt_spec,
                 halves, halves, halves, nat(GROUP), nat(SWA_KV_WIDTH), nat(SWA_KV_WIDTH)]
    return pl.pallas_call(
        _inproj_kernel,
        grid=(B, S // tm),
        in_specs=in_specs,
        out_specs=out_specs,
        out_shape=out_shape,
        scratch_shapes=[pltpu.VMEM((n_rows, tm), F32) for n_rows in _CHUNK_ROWS],
        compiler_params=_params(2),
        name="inproj",
    )(x, gx, w1t, gaq, gak, gql, gkvl, wuqt, wukvt, gbq, gbk, gcq, gck, gdq, gdk, cos_t, sin_t, kpos_t, aslope)


def _flash_heads(qt_ref, k_ref, vt_ref, o_ref, s_bufs, p_bufs, acc_scr, rowb_scr, i, cols):
    tq = ATT_TQ
    n_tail = tq // ATT_TK
    assert n_tail == 2
    n_past = i * n_tail
    heads = range(N_HEADS)

    def block_of(pos):
        return jnp.where(pos < n_tail, n_past + pos, pos - n_tail)

    def head_scores(h, slot, blk):
        kb = k_ref[0, h, pl.ds(pl.multiple_of(blk * ATT_TK, ATT_TK), ATT_TK), :]
        s_bufs[slot][h] = jnp.dot(kb, qt_ref[0, h, :, cols], preferred_element_type=F32)

    def stage_scores(slot, blk):
        for h in heads:
            head_scores(h, slot, blk)

    def stage_softmax(slot, blk, ms, causal=None):
        new_ms, alphas = [], []
        for h in heads:
            m_parts, a_parts = [], []
            rv_row = None if rowb_scr is None else rowb_scr[h, pl.ds(blk, 1), :]
            for c in range(tq // LANES):
                cols = slice(c * LANES, (c + 1) * LANES)
                st = s_bufs[slot][h, :, cols]
                if causal is not None:
                    st = jnp.where(causal[:, cols], st, 2 * NEG)
                cm = jnp.max(st, axis=0, keepdims=True)
                m_old = ms[h][:, cols]
                if rowb_scr is None:
                    m_new = jnp.maximum(m_old, cm)
                    shift = m_new
                else:
                    rv = rv_row[:, cols]
                    m_new = jnp.maximum(m_old, cm + rv)
                    shift = m_new - rv
                a_parts.append(jnp.exp2(m_old - m_new))
                m_parts.append(m_new)
                p_bufs[slot][h, :, cols] = jnp.exp2(st - shift).astype(BF16)
            new_ms.append(jnp.concatenate(m_parts, axis=1))
            alphas.append(jnp.concatenate(a_parts, axis=1))
        return tuple(new_ms), tuple(alphas)

    def head_values(h, slot, blk, alphas):
        acc_scr[h] = alphas[h] * acc_scr[h] + jnp.dot(vt_ref[0, h, blk], p_bufs[slot][h],
                                                      preferred_element_type=F32)

    def stage_values(slot, blk, alphas):
        for h in heads:
            head_values(h, slot, blk, alphas)

    krow = lax.broadcasted_iota(jnp.int32, (ATT_TK, tq), 0)
    ti = lax.broadcasted_iota(jnp.int32, (ATT_TK, tq), 1)
    for h in heads:
        acc_scr[h] = jnp.zeros((ATT_DV, tq), F32)
    stage_scores(0, n_past)
    stage_scores(1, n_past + 1)
    ms = tuple(jnp.full((1, tq), NEG, F32) for _ in heads)
    ms, alphas = stage_softmax(0, n_past, ms, causal=ti >= krow)
    stage_values(0, n_past, alphas)
    ms, alphas = stage_softmax(1, n_past + 1, ms, causal=ti >= krow + ATT_TK)
    stage_scores(0, 0)

    def step(pos, slot, ms, alphas):
        ms, new_alphas = stage_softmax(1 - slot, pos + 1 - n_tail, ms)
        blk = block_of(pos)
        for h in heads:
            head_values(h, slot, blk, alphas)
            head_scores(h, slot, pos)
        return ms, new_alphas

    def pair(first, carry):
        ms, alphas = step(first, 1, *carry)
        return step(first + 1, 0, ms, alphas)

    def trip(t, carry):
        for u in range(FLASH_PAIRS):
            carry = pair(2 * (FLASH_PAIRS * t + u) + 1, carry)
        return carry

    carry = lax.fori_loop(0, i // FLASH_PAIRS, trip, (ms, alphas))
    ms, alphas = lax.fori_loop(i - i % FLASH_PAIRS, i, lambda u, c: pair(2 * u + 1, c), carry)
    last = n_past + 1
    stage_values(1, block_of(last), alphas)
    for h in heads:
        acc = acc_scr[h]
        o_ref[0, h, :, cols] = acc[:HEAD_DIM, :] / acc[HEAD_DIM:HEAD_DIM + 1, :]


def _tile_cols(t):
    return slice(t * ATT_TQ, (t + 1) * ATT_TQ)


def _moba_kernel(qt_ref, q32t_ref, k_ref, vt_ref, kmean_ref, o_ref, rowb_scr, s0, s1, p0, p1, acc_scr):
    nb = kmean_ref.shape[2]
    tq = ATT_TQ
    blk = lax.broadcasted_iota(jnp.int32, (nb, tq), 0)
    col = lax.broadcasted_iota(jnp.int32, (nb, tq), 1)
    for t in range(FLASH_TILES):
        i = pl.program_id(1) * FLASH_TILES + t
        qblk = i * (tq // MOBA_BLOCK) + col // MOBA_BLOCK
        past = blk < qblk
        dist0 = (i * tq + col - blk * MOBA_BLOCK).astype(F32)
        for h in range(N_HEADS):
            gate = jnp.dot(kmean_ref[0, h], q32t_ref[0, h, :, _tile_cols(t)], preferred_element_type=F32,
                           precision=lax.Precision.HIGHEST)
            gate = jnp.where(past, gate, NEG)
            sel = blk == qblk
            for _ in range(MOBA_TOPK):
                best = jnp.max(gate, axis=0, keepdims=True)
                first = jnp.min(jnp.where(gate == best, blk, nb), axis=0, keepdims=True)
                pick = blk == first
                sel = jnp.logical_or(sel, jnp.logical_and(pick, past))
                gate = jnp.where(pick, -jnp.inf, gate)
            rowb_scr[h] = jnp.where(sel, (-float(SLOPE_A[h]) * LOG2E) * dist0, 2 * NEG)
        _flash_heads(qt_ref, k_ref, vt_ref, o_ref, (s0, s1), (p0, p1), acc_scr, rowb_scr, i, _tile_cols(t))


def _flash_specs(S):
    H, nb, tqs = N_HEADS, S // ATT_TK, FLASH_TILES * ATT_TQ
    return ([pl.BlockSpec((1, H, ATT_DK, tqs), lambda b, i: (b, 0, 0, i)),
             pl.BlockSpec((1, H, S, ATT_DK), lambda b, i: (b, 0, 0, 0)),
             pl.BlockSpec((1, H, nb, ATT_DV, ATT_TK), lambda b, i: (b, 0, 0, 0, 0))],
            pl.BlockSpec((1, H, HEAD_DIM, tqs), lambda b, i: (b, 0, 0, i)))


def _flash_scratch():
    s_buf = pltpu.VMEM((N_HEADS, ATT_TK, ATT_TQ), F32)
    p_buf = pltpu.VMEM((N_HEADS, ATT_TK, ATT_TQ), BF16)
    return [s_buf, s_buf, p_buf, p_buf, pltpu.VMEM((N_HEADS, ATT_DV, ATT_TQ), F32)]


def _moba(aqt, aq32t, ak, avt, kmean_h):
    B, H, _, S = aqt.shape
    nb = S // ATT_TK
    tqs = FLASH_TILES * ATT_TQ
    (q_spec, k_spec, vt_spec), o_spec = _flash_specs(S)
    return pl.pallas_call(
        _moba_kernel,
        grid=(B, S // tqs),
        in_specs=[q_spec, pl.BlockSpec((1, H, HEAD_DIM, tqs), lambda b, i: (b, 0, 0, i)), k_spec, vt_spec,
                  pl.BlockSpec((1, H, nb, HEAD_DIM), lambda b, i: (b, 0, 0, 0))],
        out_specs=o_spec,
        out_shape=jax.ShapeDtypeStruct((B, H, HEAD_DIM, S), F32),
        scratch_shapes=[pltpu.VMEM((H, nb, ATT_TQ), F32)] + _flash_scratch(),
        compiler_params=_params(2),
        name="moba",
    )(aqt, aq32t, ak, avt, kmean_h)


def _mla_kernel(qt_ref, k_ref, vt_ref, o_ref, s0, s1, p0, p1, acc_scr):
    for t in range(FLASH_TILES):
        _flash_heads(qt_ref, k_ref, vt_ref, o_ref, (s0, s1), (p0, p1), acc_scr, None,
                     pl.program_id(1) * FLASH_TILES + t, _tile_cols(t))


def _mla(bqt, bk, bvt):
    B, H, _, S = bqt.shape
    in_specs, o_spec = _flash_specs(S)
    return pl.pallas_call(
        _mla_kernel,
        grid=(B, S // (FLASH_TILES * ATT_TQ)),
        in_specs=in_specs,
        out_specs=o_spec,
        out_shape=jax.ShapeDtypeStruct((B, H, HEAD_DIM, S), F32),
        scratch_shapes=_flash_scratch(),
        compiler_params=_params(2),
        name="mla",
    )(bqt, bk, bvt)


DIL_SPAN = max(d for _, d in DILATED_BRANCHES) * Q_BLOCK
DIL_UNITS = DIL_SPAN // Q_BLOCK
DIL_GROUP = (4, 2, 2)


def _dilated_bias():
    qi = np.arange(Q_BLOCK)[:, None]
    kidx = np.arange(2 * Q_BLOCK)[None, :]
    rel = qi + Q_BLOCK - kidx
    out = np.empty((2, len(DILATED_BRANCHES), N_HEADS, Q_BLOCK, 2 * Q_BLOCK), np.float32)
    for bi, (window, d) in enumerate(DILATED_BRANCHES):
        valid = (rel >= 0) & (rel <= window // d)
        for h in range(N_HEADS):
            bias = -SLOPE_C[h] * np.float32(LOG2E) * (d * rel).astype(np.float32)
            out[0, bi, h] = np.where(valid, bias, NEG)
            out[1, bi, h] = np.where(valid & (kidx >= Q_BLOCK), bias, NEG)
    return out


def _rows_load(ref, lead, start, size, stride):
    return jnp.concatenate([ref[lead + (c, pl.ds(start, size, stride=stride), slice(None))]
                            for c in range(N_HALF)], axis=-1)


def _rows_store(ref, lead, start, size, stride, val):
    for c in range(N_HALF):
        ref[lead + (c, pl.ds(start, size, stride=stride), slice(None))] = val[:, c * LANES:(c + 1) * LANES]


def _dilated_kernel(q_ref, kp_ref, kc_ref, vp_ref, vc_ref, bias_ref, o_ref, kbuf, vbuf, m_scr, den_scr, num_scr):
    span = pl.program_id(1)
    kbuf[:, 0:DIL_SPAN, :] = kp_ref[0]
    kbuf[:, DIL_SPAN:, :] = kc_ref[0]
    vbuf[:, 0:DIL_SPAN, :] = vp_ref[0]
    vbuf[:, DIL_SPAN:, :] = vc_ref[0]
    lane_head = lax.broadcasted_iota(jnp.int32, (1, GROUP), 1) // HEAD_DIM
    hmask = [lane_head == h for h in range(N_HEADS)]
    hmask_f = [m.astype(F32) for m in hmask]
    first_span = jnp.where(span == 0, 1, 0)

    def per_head(cols):
        out = cols[N_HEADS - 1]
        for h in range(N_HEADS - 2, -1, -1):
            out = jnp.where(hmask[h], cols[h], out)
        return out

    order = sorted(range(len(DILATED_BRANCHES)), key=lambda b: -DILATED_BRANCHES[b][1])
    for bi in order:
        d, n_group = DILATED_BRANCHES[bi][1], DIL_GROUP[bi]
        first, last = bi == order[0], bi == order[-1]

        def group(g, _, bi=bi, d=d, n_group=n_group, first=first, last=last):
            fronts = []
            for uu in range(n_group):
                u = g * n_group + uu
                r, n = u % d, u // d
                qstart = n * (Q_BLOCK * d) + r
                kstart = DIL_SPAN + qstart - Q_BLOCK * d
                q = _rows_load(q_ref, (0,), qstart, Q_BLOCK, d)
                k2 = _rows_load(kbuf, (), kstart, 2 * Q_BLOCK, d).astype(BF16)
                v2 = _rows_load(vbuf, (), kstart, 2 * Q_BLOCK, d).astype(BF16)
                q4 = jnp.concatenate([(q * hmask_f[h]).astype(BF16) for h in range(N_HEADS)], axis=0)
                s4 = lax.dot_general(q4, k2, _NT, preferred_element_type=F32)
                variant = jnp.where(n == 0, first_span, 0)
                fronts.append((qstart, s4, v2, variant))
            for qstart, s4, v2, variant in fronts:
                es, ms, ls = [], [], []
                for h in range(N_HEADS):
                    s = s4[h * Q_BLOCK:(h + 1) * Q_BLOCK, :] + bias_ref[variant, bi, h]
                    m = jnp.max(s, axis=-1, keepdims=True)
                    e = jnp.exp2(s - m)
                    ls.append(jnp.sum(e, axis=-1, keepdims=True))
                    ms.append(m)
                    es.append(e.astype(BF16))
                o4 = jnp.dot(jnp.concatenate(es, axis=0), v2, preferred_element_type=F32)
                o = o4[(N_HEADS - 1) * Q_BLOCK:, :]
                for h in range(N_HEADS - 2, -1, -1):
                    o = jnp.where(hmask[h], o4[h * Q_BLOCK:(h + 1) * Q_BLOCK, :], o)
                m_b, l_b = per_head(ms), per_head(ls)
                at = ((), qstart, Q_BLOCK, d)
                if first:
                    _rows_store(m_scr, *at, m_b)
                    _rows_store(num_scr, *at, o)
                    _rows_store(den_scr, *at, l_b)
                else:
                    m_old = _rows_load(m_scr, *at)
                    m_new = jnp.maximum(m_old, m_b)
                    a, b = jnp.exp2(m_old - m_new), jnp.exp2(m_b - m_new)
                    num = a * _rows_load(num_scr, *at) + b * o
                    den = a * _rows_load(den_scr, *at) + b * l_b
                    if last:
                        _rows_store(o_ref, (0,), qstart, Q_BLOCK, d, num / den)
                    else:
                        _rows_store(m_scr, *at, m_new)
                        _rows_store(num_scr, *at, num)
                        _rows_store(den_scr, *at, den)
            return 0

        lax.fori_loop(0, DIL_UNITS // n_group, group, 0)


def _dilated(cq, ck, cv):
    B, _, S, _ = cq.shape
    cur = pl.BlockSpec((1, N_HALF, DIL_SPAN, LANES), lambda b, s: (b, 0, s, 0))
    prev = pl.BlockSpec((1, N_HALF, DIL_SPAN, LANES), lambda b, s: (b, 0, jnp.maximum(s - 1, 0), 0))
    bias = jnp.asarray(_dilated_bias())
    return pl.pallas_call(
        _dilated_kernel,
        grid=(B, S // DIL_SPAN),
        in_specs=[cur, prev, cur, prev, cur, pl.BlockSpec(bias.shape, lambda b, s: (0,) * bias.ndim)],
        out_specs=cur,
        out_shape=jax.ShapeDtypeStruct((B, N_HALF, S, LANES), F32),
        scratch_shapes=[pltpu.VMEM((N_HALF, 2 * DIL_SPAN, LANES), F32),
                        pltpu.VMEM((N_HALF, 2 * DIL_SPAN, LANES), F32),
                        pltpu.VMEM((N_HALF, DIL_SPAN, LANES), F32), pltpu.VMEM((N_HALF, DIL_SPAN, LANES), F32),
                        pltpu.VMEM((N_HALF, DIL_SPAN, LANES), F32)],
        compiler_params=_params(2),
        name="dilated",
    )(cq, ck, ck, cv, cv, bias)


SWA_SPAN = 2048
SWA_GROUP = 4
SWA_HEAD_ORDER = (0, 2, 1, 3)


def _swa_bias():
    qi = np.arange(Q_BLOCK)[:, None]
    kidx = np.arange(2 * Q_BLOCK)[None, :]
    rel = qi + Q_BLOCK - kidx
    valid = (rel >= 0) & (rel < SWA_WINDOW)
    out = np.empty((2, N_HEADS, Q_BLOCK, 2 * Q_BLOCK), np.float32)
    for h in range(N_HEADS):
        bias = -SLOPE_D[h] * np.float32(LOG2E) * rel.astype(np.float32)
        out[0, h] = np.where(valid, bias, NEG)
        out[1, h] = np.where(valid & (kidx >= Q_BLOCK), bias, NEG)
    return out


def _swa_kernel(sink_ref, q_ref, k_ref, v_ref, bias_ref, o_ref):
    span = pl.program_id(1)
    units = SWA_SPAN // Q_BLOCK
    half = lax.broadcasted_iota(jnp.int32, (1, LANES), 1) // HEAD_DIM
    lo = half == 0
    kv_mask = [jnp.where(half == g, 1.0, 0.0).astype(BF16) for g in range(SWA_KV_HEADS)]

    def group(g, _):
        fronts = []
        for uu in range(SWA_GROUP):
            u = g * SWA_GROUP + uu
            n = span * units + u
            lo_start = pl.multiple_of(jnp.maximum(n - 1, 0) * Q_BLOCK, Q_BLOCK)
            hi_start = pl.multiple_of(n * Q_BLOCK, Q_BLOCK)
            qstart = pl.multiple_of(u * Q_BLOCK, Q_BLOCK)
            q = q_ref[0, pl.ds(qstart, Q_BLOCK), :]
            k2 = jnp.concatenate([k_ref[0, pl.ds(lo_start, Q_BLOCK), :], k_ref[0, pl.ds(hi_start, Q_BLOCK), :]],
                                 axis=0)
            v2 = jnp.concatenate([v_ref[0, pl.ds(lo_start, Q_BLOCK), :], v_ref[0, pl.ds(hi_start, Q_BLOCK), :]],
                                 axis=0)
            q4 = jnp.concatenate([q[:, (h % 2) * LANES:(h % 2 + 1) * LANES] * kv_mask[h // 2]
                                  for h in range(N_HEADS)], axis=0)
            s4 = lax.dot_general(q4, k2, _NT, preferred_element_type=F32)
            fronts.append((qstart, s4, v2, jnp.where(n == 0, 1, 0)))
        for qstart, s4, v2, variant in fronts:
            es, ls = [], []
            for h in range(N_HEADS):
                s = s4[h * Q_BLOCK:(h + 1) * Q_BLOCK, :] + bias_ref[variant, h]
                sink = sink_ref[h] * LOG2E
                m = jnp.maximum(jnp.max(s, axis=-1, keepdims=True), sink)
                e = jnp.exp2(s - m)
                ls.append(jnp.sum(e, axis=-1, keepdims=True) + jnp.exp2(sink - m))
                es.append(e.astype(BF16))
            o4 = jnp.dot(jnp.concatenate(es, axis=0), v2, preferred_element_type=F32)
            tiles = []
            for t in range(N_HALF):
                a, b = t, t + 2
                tiles.append(jnp.where(lo, o4[a * Q_BLOCK:(a + 1) * Q_BLOCK, :] / ls[a],
                                       o4[b * Q_BLOCK:(b + 1) * Q_BLOCK, :] / ls[b]))
            o_ref[0, pl.ds(qstart, Q_BLOCK), :] = jnp.concatenate(tiles, axis=-1)
        return 0

    lax.fori_loop(0, units // SWA_GROUP, group, 0)


def _swa(dq, dk, dv, sinks):
    B, S, _ = dq.shape
    bias = jnp.asarray(_swa_bias())
    return pl.pallas_call(
        _swa_kernel,
        grid=(B, S // SWA_SPAN),
        in_specs=[pl.BlockSpec(memory_space=pltpu.SMEM),
                  pl.BlockSpec((1, SWA_SPAN, GROUP), lambda b, i: (b, i, 0)),
                  pl.BlockSpec((1, S, SWA_KV_WIDTH), lambda b, i: (b, 0, 0)),
                  pl.BlockSpec((1, S, SWA_KV_WIDTH), lambda b, i: (b, 0, 0)),
                  pl.BlockSpec(bias.shape, lambda b, i: (0,) * bias.ndim)],
        out_specs=pl.BlockSpec((1, SWA_SPAN, GROUP), lambda b, i: (b, i, 0)),
        out_shape=jax.ShapeDtypeStruct((B, S, GROUP), F32),
        compiler_params=_params(2),
        name="swa",
    )(sinks, dq, dk, dv, bias)


def _row_norm(y, g_row):
    return y * lax.rsqrt(jnp.mean(y * y, axis=-1, keepdims=True) + EPS) * g_row


def _tail_kernel(x_ref, oat_ref, obt_ref, oc_ref, od_ref, gg_ref, wo_ref, gm_ref, wup_ref, wdn_ref, out_ref):
    gg = gg_ref[...]

    def col_norm_t(yt, g_row):
        y = (yt * lax.rsqrt(jnp.mean(yt * yt, axis=0, keepdims=True) + EPS)).T
        return y * g_row

    ga = col_norm_t(oat_ref[0], gg[0:1, :])
    gb = col_norm_t(obt_ref[0], gg[1:2, :])
    gc = _row_norm(jnp.concatenate([oc_ref[0, c] for c in range(N_HALF)], axis=-1), gg[2:3, :])
    gd = _row_norm(od_ref[0], gg[3:4, :])
    mixed = jnp.concatenate([ga, gb, gc, gd], axis=-1).astype(BF16)
    x1 = x_ref[0] + jnp.dot(mixed, wo_ref[...], preferred_element_type=F32)
    xn = _row_norm(x1, gm_ref[...]).astype(BF16)
    u = jnp.maximum(jnp.dot(xn, wup_ref[...], preferred_element_type=F32), 0.0)
    out_ref[0] = x1 + jnp.dot((u * u).astype(BF16), wdn_ref[...], preferred_element_type=F32)


def _tail(x, oat, obt, oc, od, gg, wo, gm, wup, wdn):
    B, S, _ = x.shape
    tm = TAIL_TM
    nat = lambda w: pl.BlockSpec((1, tm, w), lambda b, t: (b, t, 0))
    ft = pl.BlockSpec((1, GROUP, tm), lambda b, t: (b, 0, t))
    const = lambda a: pl.BlockSpec(a.shape, lambda b, t: (0,) * a.ndim, pipeline_mode=pl.Buffered(1))
    return pl.pallas_call(
        _tail_kernel,
        grid=(B, S // tm),
        in_specs=[nat(D_MODEL), ft, ft, pl.BlockSpec((1, N_HALF, tm, LANES), lambda b, t: (b, 0, t, 0)),
                  nat(GROUP), const(gg), const(wo), const(gm),
                  const(wup), const(wdn)],
        out_specs=nat(D_MODEL),
        out_shape=jax.ShapeDtypeStruct((B, S, D_MODEL), F32),
        compiler_params=_params(2),
        name="tail",
    )(x, oat, obt, oc, od, gg, wo, gm, wup, wdn)


def _rope_tables_t(S):
    inv = 1.0 / (ROPE_THETA ** (jnp.arange(0, MLA_ROPE, 2, dtype=F32) / MLA_ROPE))
    ang = inv[:, None] * jnp.arange(S, dtype=F32)[None, :]
    return jnp.cos(ang), jnp.sin(ang)


def _moba_aug_tables(S):
    pos = np.arange(S, dtype=np.float32) % MOBA_BLOCK
    kpos = np.zeros((AUG_ROWS, S), np.float32)
    kpos[:3] = pos
    aslope = np.zeros((N_HEADS, AUG_ROWS, 1), np.float32)
    rest = (SLOPE_A * np.float32(LOG2E)).astype(np.float32)
    for r in range(3):
        piece = rest.astype(BF16).astype(np.float32)
        aslope[:, r, 0] = piece
        rest = rest - piece
    return jnp.asarray(kpos), jnp.asarray(aslope)


def _layer(x, cos_t, sin_t, kpos_t, aslope, attn_norm_g, w_in, moba_q_g, moba_k_g, mla_qlat_g, mla_kvlat_g, mla_w_uq, mla_w_ukv,
           mla_q_g, mla_k_g, dil_q_g, dil_k_g, swa_q_g, swa_k_g, swa_sinks, group_out_g, w_o, mlp_norm_g,
           w_up, w_down):
    B, S, _ = x.shape
    col = lambda g: g.reshape(-1, 1)
    (aqt, aq32t, ak, akm, avt, bqt, bk, bvt, cq, ck, cv, dq, dk, dv) = _inproj(
        x, attn_norm_g.reshape(1, -1), w_in[:, _W1T_ROWS].T.astype(BF16), col(moba_q_g), col(moba_k_g), col(mla_qlat_g),
        col(mla_kvlat_g), mla_w_uq.T.astype(BF16), mla_w_ukv.T.astype(BF16), col(mla_q_g), col(mla_k_g),
        col(dil_q_g), col(dil_k_g), col(swa_q_g), col(swa_k_g), cos_t, sin_t, kpos_t, aslope)
    nb = S // MOBA_BLOCK
    kmean_h = akm.reshape(B, nb, N_HEADS, HEAD_DIM).transpose(0, 2, 1, 3)
    oat = _moba(aqt, aq32t, ak, avt, kmean_h).reshape(B, GROUP, S)
    obt = _mla(bqt, bk, bvt).reshape(B, GROUP, S)
    oc = _dilated(cq, ck, cv)
    od = _swa(dq, dk, dv, swa_sinks)
    perm = np.concatenate([np.arange(HEAD_DIM) + HEAD_DIM * h for h in SWA_HEAD_ORDER])
    gg = group_out_g.at[3].set(group_out_g[3][perm])
    wo = jnp.concatenate([w_o[:3 * GROUP], w_o[3 * GROUP + perm]], axis=0)
    return _tail(x, oat, obt, oc, od, gg, wo.astype(BF16), mlp_norm_g.reshape(1, -1),
                 w_up.astype(BF16), w_down.astype(BF16))


def kernel(x, attn_norm_g, w_in, moba_q_g, moba_k_g, mla_qlat_g, mla_kvlat_g, mla_w_uq, mla_w_ukv, mla_q_g,
           mla_k_g, dil_q_g, dil_k_g, swa_q_g, swa_k_g, swa_sinks, group_out_g, w_o, mlp_norm_g, w_up, w_down):
    S = x.shape[1]
    assert S % max(d * Q_BLOCK for _, d in DILATED_BRANCHES) == 0 and S % INPROJ_TM == 0
    cos_t, sin_t = _rope_tables_t(S)
    kpos_t, aslope = _moba_aug_tables(S)
    params = (attn_norm_g, w_in, moba_q_g, moba_k_g, mla_qlat_g, mla_kvlat_g, mla_w_uq, mla_w_ukv, mla_q_g,
              mla_k_g, dil_q_g, dil_k_g, swa_q_g, swa_k_g, swa_sinks, group_out_g, w_o, mlp_norm_g, w_up, w_down)
    for l in range(attn_norm_g.shape[0]):
        x = _layer(x, cos_t, sin_t, kpos_t, aslope, *[p[l] for p in params])
    return x
```

```python
import numpy as np
import jax
import jax.numpy as jnp
from jax import lax
from jax.experimental import pallas as pl
from jax.experimental.pallas import tpu as pltpu

F32 = jnp.float32
BF16 = jnp.bfloat16

D_MODEL = 1024
HEAD_DIM = 64
N_HEADS = 4
GROUP = N_HEADS * HEAD_DIM
LANES = 128
N_HALF = GROUP // LANES
MOBA_BLOCK = 256
MOBA_TOPK = 3
MLA_Q_RANK = 256
MLA_KV_RANK = 128
MLA_NOPE = 64
MLA_ROPE = 32
MLA_QK = MLA_NOPE + MLA_ROPE
ROPE_THETA = 10000.0
DILATED_BRANCHES = ((128, 1), (512, 4), (2048, 16))
Q_BLOCK = 128
SWA_WINDOW = 128
SWA_KV_HEADS = 2
SWA_KV_WIDTH = SWA_KV_HEADS * HEAD_DIM
D_FF = 4 * D_MODEL
EPS = 1e-6
NEG = -1e30

_SECTIONS = (("a_q", GROUP), ("a_k", GROUP), ("a_v", GROUP), ("b_ql", MLA_Q_RANK), ("b_kvl", MLA_KV_RANK),
             ("b_kr", MLA_ROPE), ("c_q", GROUP), ("c_k", GROUP), ("c_v", GROUP), ("d_q", GROUP),
             ("d_k", SWA_KV_WIDTH), ("d_v", SWA_KV_WIDTH))
_CHUNKS = (("a_q", "a_k", "a_v", "b_ql", "b_kvl", "b_kr"), ("c_q", "c_k", "d_q", "d_k"), ("c_v", "d_v"))


def _projection_layout():
    width = dict(_SECTIONS)
    col, start = 0, {}
    for name, w in _SECTIONS:
        start[name] = col
        col += w
    rows, where, chunk_rows = [], {}, []
    for ci, chunk in enumerate(_CHUNKS):
        off = 0
        for name in chunk:
            where[name] = (ci, off, width[name])
            rows.extend(range(start[name], start[name] + width[name]))
            off += width[name]
        chunk_rows.append(off)
    return np.asarray(rows), where, tuple(chunk_rows)


_W1T_ROWS, _SECTION_AT, _CHUNK_ROWS = _projection_layout()
IN_COLS = len(_W1T_ROWS)

VMEM_LIMIT = 56 * 1024 * 1024

INPROJ_TM = 512
TAIL_TM = 512
ATT_TK = 256
ATT_TQ = 2 * ATT_TK
ATT_DK = 128
ATT_DV = HEAD_DIM + 16
AUG_ROWS = 8
LOG2E = 1.4426950408889634
FLASH_PAIRS = 2

_NT = (((1,), (1,)), ((), ()))


def _alibi_slopes():
    n = 3 * N_HEADS
    idx = np.arange(1, n + 1, dtype=np.float32).reshape(N_HEADS, 3)
    s = np.exp2(-8.0 * idx / n).astype(np.float32)
    return s[:, 0], s[:, 1], s[:, 2]


SLOPE_A, SLOPE_C, SLOPE_D = _alibi_slopes()


def _params(n_axes):
    return pltpu.CompilerParams(dimension_semantics=("arbitrary",) * n_axes,
                                vmem_limit_bytes=VMEM_LIMIT)


def _head_norm_t(sec, g_col, n_heads, width, rs=None):
    outs = []
    for h in range(n_heads):
        s = sec[h * width:(h + 1) * width, :]
        ms = jnp.sum(s * s, axis=0, keepdims=True) * (1.0 / width)
        factor = lax.rsqrt(ms + EPS) if rs is None else rs * lax.rsqrt(rs * rs * ms + EPS)
        outs.append(s * factor * g_col)
    return outs


def _inproj_kernel(x_ref, gx_ref, w1t_ref, gaq_ref, gak_ref, gql_ref, gkvl_ref, wuqt_ref, wukvt_ref,
                   gbq_ref, gbk_ref, gcq_ref, gck_ref, gdq_ref, gdk_ref, cos_ref, sin_ref, kpos_ref, aslope_ref,
                   aqt_ref, aq32t_ref, ak_ref, akm_ref, avt_ref,
                   bqt_ref, bk_ref, bvt_ref,
                   cq_ref, ck_ref, cv_ref, dq_ref, dk_ref, dv_ref,
                   *h_scrs):
    tm = x_ref.shape[1]
    x = x_ref[0]
    xn = (x * gx_ref[...]).astype(BF16)
    rs_col = lax.rsqrt(jnp.mean(x * x, axis=-1, keepdims=True) + EPS)
    rs = jnp.broadcast_to(rs_col, (tm, LANES)).T[0:1, :]
    row0 = 0
    for h_scr, n_rows in zip(h_scrs, _CHUNK_ROWS):
        h_scr[...] = lax.dot_general(w1t_ref[row0:row0 + n_rows, :], xn, _NT, preferred_element_type=F32)
        row0 += n_rows

    def sec(name, lo=0, hi=None):
        chunk, off, width = _SECTION_AT[name]
        return h_scrs[chunk][off + lo:off + (width if hi is None else hi), :]

    scale = HEAD_DIM ** -0.5

    ones_rows = jnp.ones((ATT_DV - HEAD_DIM, ATT_TK), F32)

    def store_vt(ref, h, vh):
        for c in range(tm // ATT_TK):
            ref[0, h, c] = jnp.concatenate([vh[:, c * ATT_TK:(c + 1) * ATT_TK], ones_rows], axis=0).astype(BF16)

    pad_rows = jnp.zeros((ATT_DK - HEAD_DIM - AUG_ROWS, tm), F32)
    qa = _head_norm_t(sec("a_q"), gaq_ref[...], N_HEADS, HEAD_DIM, rs)
    ka = _head_norm_t(sec("a_k"), gak_ref[...], N_HEADS, HEAD_DIM, rs)
    ka_nat = jnp.concatenate(ka, axis=0).T
    for c in range(tm // MOBA_BLOCK):
        akm_ref[0, c] = jnp.sum(ka_nat[c * MOBA_BLOCK:(c + 1) * MOBA_BLOCK, :], axis=0,
                                keepdims=True) * (1.0 / MOBA_BLOCK)
    kpos = kpos_ref[...]
    for h in range(N_HEADS):
        aq32t_ref[0, h] = qa[h]
        slope_rows = jnp.broadcast_to(aslope_ref[h], (AUG_ROWS, tm))
        aqt_ref[0, h] = jnp.concatenate([qa[h] * (scale * LOG2E), slope_rows, pad_rows], axis=0).astype(BF16)
        ak_ref[0, h] = jnp.concatenate([ka[h], kpos, pad_rows], axis=0).T.astype(BF16)
        store_vt(avt_ref, h, sec("a_v", h * HEAD_DIM, (h + 1) * HEAD_DIM) * rs)

    cos = cos_ref[...]
    sin = sin_ref[...]
    half = MLA_ROPE // 2

    def rope_pad(t, sc):
        x1 = t[MLA_NOPE:MLA_NOPE + half, :]
        x2 = t[MLA_NOPE + half:MLA_QK, :]
        return jnp.concatenate([t[:MLA_NOPE, :] * sc, (x1 * cos - x2 * sin) * sc, (x1 * sin + x2 * cos) * sc,
                                jnp.zeros((ATT_DK - MLA_QK, tm), F32)], axis=0)

    ql = _head_norm_t(sec("b_ql"), gql_ref[...], 1, MLA_Q_RANK, rs)[0]
    qb = jnp.dot(wuqt_ref[...], ql.astype(BF16), preferred_element_type=F32)
    qb = _head_norm_t(qb, gbq_ref[...], N_HEADS, MLA_QK)
    kvl = _head_norm_t(sec("b_kvl"), gkvl_ref[...], 1, MLA_KV_RANK, rs)[0]
    kvb = jnp.dot(wukvt_ref[...], kvl.astype(BF16), preferred_element_type=F32)
    kr = sec("b_kr") * rs
    gbk = gbk_ref[...]
    for h in range(N_HEADS):
        bqt_ref[0, h] = rope_pad(qb[h], MLA_QK ** -0.5 * LOG2E).astype(BF16)
        kh = jnp.concatenate([kvb[h * 2 * HEAD_DIM:h * 2 * HEAD_DIM + MLA_NOPE, :], kr], axis=0)
        kh = kh * lax.rsqrt(jnp.sum(kh * kh, axis=0, keepdims=True) * (1.0 / MLA_QK) + EPS) * gbk
        bk_ref[0, h] = rope_pad(kh, 1.0).T.astype(BF16)
        store_vt(bvt_ref, h, kvb[h * 2 * HEAD_DIM + MLA_NOPE:(h + 1) * 2 * HEAD_DIM, :])

    qc = jnp.concatenate(_head_norm_t(sec("c_q"), gcq_ref[...], N_HEADS, HEAD_DIM, rs), axis=0)
    kc = jnp.concatenate(_head_norm_t(sec("c_k"), gck_ref[...], N_HEADS, HEAD_DIM, rs), axis=0)
    for ref, val in ((cq_ref, qc * (scale * LOG2E)), (ck_ref, kc), (cv_ref, sec("c_v") * rs)):
        for c in range(N_HALF):
            ref[0, c] = val[c * LANES:(c + 1) * LANES, :].T

    qd = _head_norm_t(sec("d_q"), gdq_ref[...], N_HEADS, HEAD_DIM, rs)
    qd = jnp.concatenate([qd[h] for h in SWA_HEAD_ORDER], axis=0)
    dq_ref[0] = (qd * (scale * LOG2E)).T.astype(BF16)
    kd = jnp.concatenate(_head_norm_t(sec("d_k"), gdk_ref[...], SWA_KV_HEADS, HEAD_DIM, rs),
                         axis=0)
    dk_ref[0] = kd.T.astype(BF16)
    dv_ref[0] = (sec("d_v") * rs).T.astype(BF16)


def _inproj(x, gx, w1t, gaq, gak, gql, gkvl, wuqt, wukvt, gbq, gbk, gcq, gck, gdq, gdk, cos_t, sin_t, kpos_t,
            aslope):
    B, S, _ = x.shape
    tm = INPROJ_TM
    nb = S // ATT_TK
    cpt = tm // ATT_TK
    H = N_HEADS
    full = lambda a: pl.BlockSpec(a.shape, lambda b, t: (0,) * a.ndim)
    in_specs = [pl.BlockSpec((1, tm, D_MODEL), lambda b, t: (b, t, 0)), full(gx), full(w1t), full(gaq), full(gak),
                full(gql), full(gkvl), full(wuqt), full(wukvt), full(gbq), full(gbk), full(gcq), full(gck),
                full(gdq), full(gdk),
                pl.BlockSpec((MLA_ROPE // 2, tm), lambda b, t: (0, t)),
                pl.BlockSpec((MLA_ROPE // 2, tm), lambda b, t: (0, t)),
                pl.BlockSpec((AUG_ROWS, tm), lambda b, t: (0, t)), full(aslope)]
    head_t = lambda w: pl.BlockSpec((1, H, w, tm), lambda b, t: (b, 0, 0, t))
    head_n = pl.BlockSpec((1, H, tm, ATT_DK), lambda b, t: (b, 0, t, 0))
    vt_spec = pl.BlockSpec((1, H, cpt, ATT_DV, ATT_TK), lambda b, t: (b, 0, t, 0, 0))
    nat = lambda w: pl.BlockSpec((1, tm, w), lambda b, t: (b, t, 0))
    halves = pl.BlockSpec((1, N_HALF, tm, LANES), lambda b, t: (b, 0, t, 0))
    out_shape = [
        jax.ShapeDtypeStruct((B, H, ATT_DK, S), BF16),
        jax.ShapeDtypeStruct((B, H, HEAD_DIM, S), F32),
        jax.ShapeDtypeStruct((B, H, S, ATT_DK), BF16),
        jax.ShapeDtypeStruct((B, nb, 1, GROUP), F32),
        jax.ShapeDtypeStruct((B, H, nb, ATT_DV, ATT_TK), BF16),
        jax.ShapeDtypeStruct((B, H, ATT_DK, S), BF16),
        jax.ShapeDtypeStruct((B, H, S, ATT_DK), BF16),
        jax.ShapeDtypeStruct((B, H, nb, ATT_DV, ATT_TK), BF16),
        jax.ShapeDtypeStruct((B, N_HALF, S, LANES), F32),
        jax.ShapeDtypeStruct((B, N_HALF, S, LANES), F32),
        jax.ShapeDtypeStruct((B, N_HALF, S, LANES), F32),
        jax.ShapeDtypeStruct((B, S, GROUP), BF16),
        jax.ShapeDtypeStruct((B, S, SWA_KV_WIDTH), BF16),
        jax.ShapeDtypeStruct((B, S, SWA_KV_WIDTH), BF16),
    ]
    out_specs = [head_t(ATT_DK), head_t(HEAD_DIM), head_n,
                 pl.BlockSpec((1, tm // MOBA_BLOCK, 1, GROUP), lambda b, t: (b, t, 0, 0)), vt_spec,
                 head_t(ATT_DK), head_n, vt_spec,
                 halves, halves, halves, nat(GROUP), nat(SWA_KV_WIDTH), nat(SWA_KV_WIDTH)]
    return pl.pallas_call(
        _inproj_kernel,
        grid=(B, S // tm),
        in_specs=in_specs,
        out_specs=out_specs,
        out_shape=out_shape,
        scratch_shapes=[pltpu.VMEM((n_rows, tm), F32) for n_rows in _CHUNK_ROWS],
        compiler_params=_params(2),
        name="inproj",
    )(x, gx, w1t, gaq, gak, gql, gkvl, wuqt, wukvt, gbq, gbk, gcq, gck, gdq, gdk, cos_t, sin_t, kpos_t, aslope)


def _flash_scratch(tq):
    s_buf, p_buf = pltpu.VMEM((N_HEADS, ATT_TK, tq), F32), pltpu.VMEM((N_HEADS, ATT_TK, tq), BF16)
    return [s_buf, s_buf, p_buf, p_buf, pltpu.VMEM((N_HEADS, ATT_DV, tq), F32)]


def _flash_heads(qt_ref, k_ref, vt_ref, o_ref, s_bufs, p_bufs, acc_scr, rowb_scr, i):
    tq = o_ref.shape[3]
    n_tail = tq // ATT_TK
    assert n_tail == 2
    n_past = i * n_tail
    heads = range(N_HEADS)

    def block_of(pos):
        return jnp.where(pos < n_tail, n_past + pos, pos - n_tail)

    def head_scores(h, slot, blk):
        kb = k_ref[0, h, pl.ds(pl.multiple_of(blk * ATT_TK, ATT_TK), ATT_TK), :]
        s_bufs[slot][h] = jnp.dot(kb, qt_ref[0, h], preferred_element_type=F32)

    def stage_scores(slot, blk):
        for h in heads:
            head_scores(h, slot, blk)

    def stage_softmax(slot, blk, ms, causal=None):
        new_ms, alphas = [], []
        for h in heads:
            m_parts, a_parts = [], []
            rv_row = None if rowb_scr is None else rowb_scr[h, pl.ds(blk, 1), :]
            for c in range(tq // LANES):
                cols = slice(c * LANES, (c + 1) * LANES)
                st = s_bufs[slot][h, :, cols]
                if causal is not None:
                    st = jnp.where(causal[:, cols], st, 2 * NEG)
                cm = jnp.max(st, axis=0, keepdims=True)
                m_old = ms[h][:, cols]
                if rowb_scr is None:
                    m_new = jnp.maximum(m_old, cm)
                    shift = m_new
                else:
                    rv = rv_row[:, cols]
                    m_new = jnp.maximum(m_old, cm + rv)
                    shift = m_new - rv
                a_parts.append(jnp.exp2(m_old - m_new))
                m_parts.append(m_new)
                p_bufs[slot][h, :, cols] = jnp.exp2(st - shift).astype(BF16)
            new_ms.append(jnp.concatenate(m_parts, axis=1))
            alphas.append(jnp.concatenate(a_parts, axis=1))
        return tuple(new_ms), tuple(alphas)

    def head_values(h, slot, blk, alphas):
        acc_scr[h] = alphas[h] * acc_scr[h] + jnp.dot(vt_ref[0, h, blk], p_bufs[slot][h],
                                                      preferred_element_type=F32)

    def stage_values(slot, blk, alphas):
        for h in heads:
            head_values(h, slot, blk, alphas)

    krow = lax.broadcasted_iota(jnp.int32, (ATT_TK, tq), 0)
    ti = lax.broadcasted_iota(jnp.int32, (ATT_TK, tq), 1)
    for h in heads:
        acc_scr[h] = jnp.zeros((ATT_DV, tq), F32)
    stage_scores(0, n_past)
    stage_scores(1, n_past + 1)
    ms = tuple(jnp.full((1, tq), NEG, F32) for _ in heads)
    ms, alphas = stage_softmax(0, n_past, ms, causal=ti >= krow)
    stage_values(0, n_past, alphas)
    ms, alphas = stage_softmax(1, n_past + 1, ms, causal=ti >= krow + ATT_TK)
    stage_scores(0, 0)

    def step(pos, slot, ms, alphas):
        ms, new_alphas = stage_softmax(1 - slot, pos + 1 - n_tail, ms)
        blk = block_of(pos)
        for h in heads:
            head_values(h, slot, blk, alphas)
            head_scores(h, slot, pos)
        return ms, new_alphas

    def pair(first, carry):
        ms, alphas = step(first, 1, *carry)
        return step(first + 1, 0, ms, alphas)

    def trip(t, carry):
        for u in range(FLASH_PAIRS):
            carry = pair(2 * (FLASH_PAIRS * t + u) + 1, carry)
        return carry

    carry = lax.fori_loop(0, i // FLASH_PAIRS, trip, (ms, alphas))
    ms, alphas = lax.fori_loop(i - i % FLASH_PAIRS, i, lambda u, c: pair(2 * u + 1, c), carry)
    last = n_past + 1
    stage_values(1, block_of(last), alphas)
    for h in heads:
        acc = acc_scr[h]
        o_ref[0, h] = acc[:HEAD_DIM, :] / acc[HEAD_DIM:HEAD_DIM + 1, :]


def _moba_kernel(qt_ref, q32t_ref, k_ref, vt_ref, kmean_ref, o_ref, rowb_scr, s0, s1, p0, p1, acc_scr):
    i = pl.program_id(1)
    nb = kmean_ref.shape[2]
    tq = qt_ref.shape[3]
    blk = lax.broadcasted_iota(jnp.int32, (nb, tq), 0)
    col = lax.broadcasted_iota(jnp.int32, (nb, tq), 1)
    qblk = i * (tq // MOBA_BLOCK) + col // MOBA_BLOCK
    past = blk < qblk
    dist0 = (i * tq + col - blk * MOBA_BLOCK).astype(F32)
    for h in range(N_HEADS):
        gate = jnp.dot(kmean_ref[0, h], q32t_ref[0, h], preferred_element_type=F32,
                       precision=lax.Precision.HIGHEST)
        gate = jnp.where(past, gate, NEG)
        sel = blk == qblk
        for _ in range(MOBA_TOPK):
            best = jnp.max(gate, axis=0, keepdims=True)
            first = jnp.min(jnp.where(gate == best, blk, nb), axis=0, keepdims=True)
            pick = blk == first
            sel = jnp.logical_or(sel, jnp.logical_and(pick, past))
            gate = jnp.where(pick, -jnp.inf, gate)
        rowb_scr[h] = jnp.where(sel, (-float(SLOPE_A[h]) * LOG2E) * dist0, 2 * NEG)
    _flash_heads(qt_ref, k_ref, vt_ref, o_ref, (s0, s1), (p0, p1), acc_scr, rowb_scr, i)


def _flash_specs(S, tq):
    H, nb = N_HEADS, S // ATT_TK
    return ([pl.BlockSpec((1, H, ATT_DK, tq), lambda b, i: (b, 0, 0, i)),
             pl.BlockSpec((1, H, S, ATT_DK), lambda b, i: (b, 0, 0, 0)),
             pl.BlockSpec((1, H, nb, ATT_DV, ATT_TK), lambda b, i: (b, 0, 0, 0, 0))],
            pl.BlockSpec((1, H, HEAD_DIM, tq), lambda b, i: (b, 0, 0, i)))


def _moba(aqt, aq32t, ak, avt, kmean_h):
    B, H, _, S = aqt.shape
    tq = ATT_TQ
    nb = S // ATT_TK
    (q_spec, k_spec, vt_spec), o_spec = _flash_specs(S, tq)
    return pl.pallas_call(
        _moba_kernel,
        grid=(B, S // tq),
        in_specs=[q_spec, pl.BlockSpec((1, H, HEAD_DIM, tq), lambda b, i: (b, 0, 0, i)), k_spec, vt_spec,
                  pl.BlockSpec((1, H, nb, HEAD_DIM), lambda b, i: (b, 0, 0, 0))],
        out_specs=o_spec,
        out_shape=jax.ShapeDtypeStruct((B, H, HEAD_DIM, S), F32),
        scratch_shapes=[pltpu.VMEM((H, nb, tq), F32)] + _flash_scratch(tq),
        compiler_params=_params(2),
        name="moba",
    )(aqt, aq32t, ak, avt, kmean_h)


def _mla_kernel(qt_ref, k_ref, vt_ref, o_ref, s0, s1, p0, p1, acc_scr):
    _flash_heads(qt_ref, k_ref, vt_ref, o_ref, (s0, s1), (p0, p1), acc_scr, None, pl.program_id(1))


def _mla(bqt, bk, bvt):
    B, H, _, S = bqt.shape
    tq = ATT_TQ
    in_specs, o_spec = _flash_specs(S, tq)
    return pl.pallas_call(
        _mla_kernel,
        grid=(B, S // tq),
        in_specs=in_specs,
        out_specs=o_spec,
        out_shape=jax.ShapeDtypeStruct((B, H, HEAD_DIM, S), F32),
        scratch_shapes=_flash_scratch(tq),
        compiler_params=_params(2),
        name="mla",
    )(bqt, bk, bvt)


DIL_SPAN = max(d for _, d in DILATED_BRANCHES) * Q_BLOCK
DIL_UNITS = DIL_SPAN // Q_BLOCK
DIL_GROUP = (4, 2, 2)


def _dilated_bias():
    qi = np.arange(Q_BLOCK)[:, None]
    kidx = np.arange(2 * Q_BLOCK)[None, :]
    rel = qi + Q_BLOCK - kidx
    out = np.empty((2, len(DILATED_BRANCHES), N_HEADS, Q_BLOCK, 2 * Q_BLOCK), np.float32)
    for bi, (window, d) in enumerate(DILATED_BRANCHES):
        valid = (rel >= 0) & (rel <= window // d)
        for h in range(N_HEADS):
            bias = -SLOPE_C[h] * np.float32(LOG2E) * (d * rel).astype(np.float32)
            out[0, bi, h] = np.where(valid, bias, NEG)
            out[1, bi, h] = np.where(valid & (kidx >= Q_BLOCK), bias, NEG)
    return out


def _rows_load(ref, lead, start, size, stride):
    return jnp.concatenate([ref[lead + (c, pl.ds(start, size, stride=stride), slice(None))]
                            for c in range(N_HALF)], axis=-1)


def _rows_store(ref, lead, start, size, stride, val):
    for c in range(N_HALF):
        ref[lead + (c, pl.ds(start, size, stride=stride), slice(None))] = val[:, c * LANES:(c + 1) * LANES]


def _dilated_kernel(q_ref, kp_ref, kc_ref, vp_ref, vc_ref, bias_ref, o_ref, kbuf, vbuf, m_scr, den_scr, num_scr):
    span = pl.program_id(1)
    kbuf[:, 0:DIL_SPAN, :] = kp_ref[0]
    kbuf[:, DIL_SPAN:, :] = kc_ref[0]
    vbuf[:, 0:DIL_SPAN, :] = vp_ref[0]
    vbuf[:, DIL_SPAN:, :] = vc_ref[0]
    lane_head = lax.broadcasted_iota(jnp.int32, (1, GROUP), 1) // HEAD_DIM
    hmask = [lane_head == h for h in range(N_HEADS)]
    hmask_f = [m.astype(F32) for m in hmask]
    first_span = jnp.where(span == 0, 1, 0)

    def per_head(cols):
        out = cols[N_HEADS - 1]
        for h in range(N_HEADS - 2, -1, -1):
            out = jnp.where(hmask[h], cols[h], out)
        return out

    order = sorted(range(len(DILATED_BRANCHES)), key=lambda b: -DILATED_BRANCHES[b][1])
    for bi in order:
        d, n_group = DILATED_BRANCHES[bi][1], DIL_GROUP[bi]
        first, last = bi == order[0], bi == order[-1]

        def group(g, _, bi=bi, d=d, n_group=n_group, first=first, last=last):
            fronts = []
            for uu in range(n_group):
                u = g * n_group + uu
                r, n = u % d, u // d
                qstart = n * (Q_BLOCK * d) + r
                kstart = DIL_SPAN + qstart - Q_BLOCK * d
                q = _rows_load(q_ref, (0,), qstart, Q_BLOCK, d)
                k2 = _rows_load(kbuf, (), kstart, 2 * Q_BLOCK, d).astype(BF16)
                v2 = _rows_load(vbuf, (), kstart, 2 * Q_BLOCK, d).astype(BF16)
                q4 = jnp.concatenate([(q * hmask_f[h]).astype(BF16) for h in range(N_HEADS)], axis=0)
                s4 = lax.dot_general(q4, k2, _NT, preferred_element_type=F32)
                variant = jnp.where(n == 0, first_span, 0)
                fronts.append((qstart, s4, v2, variant))
            for qstart, s4, v2, variant in fronts:
                es, ms, ls = [], [], []
                for h in range(N_HEADS):
                    s = s4[h * Q_BLOCK:(h + 1) * Q_BLOCK, :] + bias_ref[variant, bi, h]
                    m = jnp.max(s, axis=-1, keepdims=True)
                    e = jnp.exp2(s - m)
                    ls.append(jnp.sum(e, axis=-1, keepdims=True))
                    ms.append(m)
                    es.append(e.astype(BF16))
                o4 = jnp.dot(jnp.concatenate(es, axis=0), v2, preferred_element_type=F32)
                o = o4[(N_HEADS - 1) * Q_BLOCK:, :]
                for h in range(N_HEADS - 2, -1, -1):
                    o = jnp.where(hmask[h], o4[h * Q_BLOCK:(h + 1) * Q_BLOCK, :], o)
                m_b, l_b = per_head(ms), per_head(ls)
                at = ((), qstart, Q_BLOCK, d)
                if first:
                    _rows_store(m_scr, *at, m_b)
                    _rows_store(num_scr, *at, o)
                    _rows_store(den_scr, *at, l_b)
                else:
                    m_old = _rows_load(m_scr, *at)
                    m_new = jnp.maximum(m_old, m_b)
                    a, b = jnp.exp2(m_old - m_new), jnp.exp2(m_b - m_new)
                    num = a * _rows_load(num_scr, *at) + b * o
                    den = a * _rows_load(den_scr, *at) + b * l_b
                    if last:
                        _rows_store(o_ref, (0,), qstart, Q_BLOCK, d, num / den)
                    else:
                        _rows_store(m_scr, *at, m_new)
                        _rows_store(num_scr, *at, num)
                        _rows_store(den_scr, *at, den)
            return 0

        lax.fori_loop(0, DIL_UNITS // n_group, group, 0)


def _dilated(cq, ck, cv):
    B, _, S, _ = cq.shape
    cur = pl.BlockSpec((1, N_HALF, DIL_SPAN, LANES), lambda b, s: (b, 0, s, 0))
    prev = pl.BlockSpec((1, N_HALF, DIL_SPAN, LANES), lambda b, s: (b, 0, jnp.maximum(s - 1, 0), 0))
    bias = jnp.asarray(_dilated_bias())
    return pl.pallas_call(
        _dilated_kernel,
        grid=(B, S // DIL_SPAN),
        in_specs=[cur, prev, cur, prev, cur, pl.BlockSpec(bias.shape, lambda b, s: (0,) * bias.ndim)],
        out_specs=cur,
        out_shape=jax.ShapeDtypeStruct((B, N_HALF, S, LANES), F32),
        scratch_shapes=[pltpu.VMEM((N_HALF, 2 * DIL_SPAN, LANES), F32),
                        pltpu.VMEM((N_HALF, 2 * DIL_SPAN, LANES), F32),
                        pltpu.VMEM((N_HALF, DIL_SPAN, LANES), F32), pltpu.VMEM((N_HALF, DIL_SPAN, LANES), F32),
                        pltpu.VMEM((N_HALF, DIL_SPAN, LANES), F32)],
        compiler_params=_params(2),
        name="dilated",
    )(cq, ck, ck, cv, cv, bias)


SWA_SPAN = 1024
SWA_GROUP = 4
SWA_HEAD_ORDER = (0, 2, 1, 3)


def _swa_bias():
    qi = np.arange(Q_BLOCK)[:, None]
    kidx = np.arange(2 * Q_BLOCK)[None, :]
    rel = qi + Q_BLOCK - kidx
    valid = (rel >= 0) & (rel < SWA_WINDOW)
    out = np.empty((2, N_HEADS, Q_BLOCK, 2 * Q_BLOCK), np.float32)
    for h in range(N_HEADS):
        bias = -SLOPE_D[h] * np.float32(LOG2E) * rel.astype(np.float32)
        out[0, h] = np.where(valid, bias, NEG)
        out[1, h] = np.where(valid & (kidx >= Q_BLOCK), bias, NEG)
    return out


def _swa_kernel(sink_ref, q_ref, k_ref, v_ref, bias_ref, o_ref):
    span = pl.program_id(1)
    units = SWA_SPAN // Q_BLOCK
    half = lax.broadcasted_iota(jnp.int32, (1, LANES), 1) // HEAD_DIM
    lo = half == 0
    kv_mask = [jnp.where(half == g, 1.0, 0.0).astype(BF16) for g in range(SWA_KV_HEADS)]

    def group(g, _):
        fronts = []
        for uu in range(SWA_GROUP):
            u = g * SWA_GROUP + uu
            n = span * units + u
            lo_start = pl.multiple_of(jnp.maximum(n - 1, 0) * Q_BLOCK, Q_BLOCK)
            hi_start = pl.multiple_of(n * Q_BLOCK, Q_BLOCK)
            qstart = pl.multiple_of(u * Q_BLOCK, Q_BLOCK)
            q = q_ref[0, pl.ds(qstart, Q_BLOCK), :]
            k2 = jnp.concatenate([k_ref[0, pl.ds(lo_start, Q_BLOCK), :], k_ref[0, pl.ds(hi_start, Q_BLOCK), :]],
                                 axis=0)
            v2 = jnp.concatenate([v_ref[0, pl.ds(lo_start, Q_BLOCK), :], v_ref[0, pl.ds(hi_start, Q_BLOCK), :]],
                                 axis=0)
            q4 = jnp.concatenate([q[:, (h % 2) * LANES:(h % 2 + 1) * LANES] * kv_mask[h // 2]
                                  for h in range(N_HEADS)], axis=0)
            s4 = lax.dot_general(q4, k2, _NT, preferred_element_type=F32)
            fronts.append((qstart, s4, v2, jnp.where(n == 0, 1, 0)))
        for qstart, s4, v2, variant in fronts:
            es, ls = [], []
            for h in range(N_HEADS):
                s = s4[h * Q_BLOCK:(h + 1) * Q_BLOCK, :] + bias_ref[variant, h]
                sink = sink_ref[h] * LOG2E
                m = jnp.maximum(jnp.max(s, axis=-1, keepdims=True), sink)
                e = jnp.exp2(s - m)
                ls.append(jnp.sum(e, axis=-1, keepdims=True) + jnp.exp2(sink - m))
                es.append(e.astype(BF16))
            o4 = jnp.dot(jnp.concatenate(es, axis=0), v2, preferred_element_type=F32)
            tiles = []
            for t in range(N_HALF):
                a, b = t, t + 2
                tiles.append(jnp.where(lo, o4[a * Q_BLOCK:(a + 1) * Q_BLOCK, :] / ls[a],
                                       o4[b * Q_BLOCK:(b + 1) * Q_BLOCK, :] / ls[b]))
            o_ref[0, pl.ds(qstart, Q_BLOCK), :] = jnp.concatenate(tiles, axis=-1)
        return 0

    lax.fori_loop(0, units // SWA_GROUP, group, 0)


def _swa(dq, dk, dv, sinks):
    B, S, _ = dq.shape
    bias = jnp.asarray(_swa_bias())
    return pl.pallas_call(
        _swa_kernel,
        grid=(B, S // SWA_SPAN),
        in_specs=[pl.BlockSpec(memory_space=pltpu.SMEM),
                  pl.BlockSpec((1, SWA_SPAN, GROUP), lambda b, i: (b, i, 0)),
                  pl.BlockSpec((1, S, SWA_KV_WIDTH), lambda b, i: (b, 0, 0)),
                  pl.BlockSpec((1, S, SWA_KV_WIDTH), lambda b, i: (b, 0, 0)),
                  pl.BlockSpec(bias.shape, lambda b, i: (0,) * bias.ndim)],
        out_specs=pl.BlockSpec((1, SWA_SPAN, GROUP), lambda b, i: (b, i, 0)),
        out_shape=jax.ShapeDtypeStruct((B, S, GROUP), F32),
        compiler_params=_params(2),
        name="swa",
    )(sinks, dq, dk, dv, bias)


def _row_norm(y, g_row):
    return y * lax.rsqrt(jnp.mean(y * y, axis=-1, keepdims=True) + EPS) * g_row


def _tail_kernel(x_ref, oat_ref, obt_ref, oc_ref, od_ref, gg_ref, wo_ref, gm_ref, wup_ref, wdn_ref, out_ref):
    gg = gg_ref[...]

    def col_norm_t(yt, g_row):
        y = (yt * lax.rsqrt(jnp.mean(yt * yt, axis=0, keepdims=True) + EPS)).T
        return y * g_row

    ga = col_norm_t(oat_ref[0], gg[0:1, :])
    gb = col_norm_t(obt_ref[0], gg[1:2, :])
    gc = _row_norm(jnp.concatenate([oc_ref[0, c] for c in range(N_HALF)], axis=-1), gg[2:3, :])
    gd = _row_norm(od_ref[0], gg[3:4, :])
    mixed = jnp.concatenate([ga, gb, gc, gd], axis=-1).astype(BF16)
    x1 = x_ref[0] + jnp.dot(mixed, wo_ref[...], preferred_element_type=F32)
    xn = _row_norm(x1, gm_ref[...]).astype(BF16)
    u = jnp.maximum(jnp.dot(xn, wup_ref[...], preferred_element_type=F32), 0.0)
    out_ref[0] = x1 + jnp.dot((u * u).astype(BF16), wdn_ref[...], preferred_element_type=F32)


def _tail(x, oat, obt, oc, od, gg, wo, gm, wup, wdn):
    B, S, _ = x.shape
    tm = TAIL_TM
    nat = lambda w: pl.BlockSpec((1, tm, w), lambda b, t: (b, t, 0))
    ft = pl.BlockSpec((1, GROUP, tm), lambda b, t: (b, 0, t))
    const = lambda a: pl.BlockSpec(a.shape, lambda b, t: (0,) * a.ndim, pipeline_mode=pl.Buffered(1))
    return pl.pallas_call(
        _tail_kernel,
        grid=(B, S // tm),
        in_specs=[nat(D_MODEL), ft, ft, pl.BlockSpec((1, N_HALF, tm, LANES), lambda b, t: (b, 0, t, 0)),
                  nat(GROUP), const(gg), const(wo), const(gm),
                  const(wup), const(wdn)],
        out_specs=nat(D_MODEL),
        out_shape=jax.ShapeDtypeStruct((B, S, D_MODEL), F32),
        compiler_params=_params(2),
        name="tail",
    )(x, oat, obt, oc, od, gg, wo, gm, wup, wdn)


def _rope_tables_t(S):
    inv = 1.0 / (ROPE_THETA ** (jnp.arange(0, MLA_ROPE, 2, dtype=F32) / MLA_ROPE))
    ang = inv[:, None] * jnp.arange(S, dtype=F32)[None, :]
    return jnp.cos(ang), jnp.sin(ang)


def _moba_aug_tables(S):
    pos = np.arange(S, dtype=np.float32) % MOBA_BLOCK
    kpos = np.zeros((AUG_ROWS, S), np.float32)
    kpos[:3] = pos
    aslope = np.zeros((N_HEADS, AUG_ROWS, 1), np.float32)
    rest = (SLOPE_A * np.float32(LOG2E)).astype(np.float32)
    for r in range(3):
        piece = rest.astype(BF16).astype(np.float32)
        aslope[:, r, 0] = piece
        rest = rest - piece
    return jnp.asarray(kpos), jnp.asarray(aslope)


def _layer(x, cos_t, sin_t, kpos_t, aslope, attn_norm_g, w_in, moba_q_g, moba_k_g, mla_qlat_g, mla_kvlat_g, mla_w_uq, mla_w_ukv,
           mla_q_g, mla_k_g, dil_q_g, dil_k_g, swa_q_g, swa_k_g, swa_sinks, group_out_g, w_o, mlp_norm_g,
           w_up, w_down):
    B, S, _ = x.shape
    col = lambda g: g.reshape(-1, 1)
    (aqt, aq32t, ak, akm, avt, bqt, bk, bvt, cq, ck, cv, dq, dk, dv) = _inproj(
        x, attn_norm_g.reshape(1, -1), w_in[:, _W1T_ROWS].T.astype(BF16), col(moba_q_g), col(moba_k_g), col(mla_qlat_g),
        col(mla_kvlat_g), mla_w_uq.T.astype(BF16), mla_w_ukv.T.astype(BF16), col(mla_q_g), col(mla_k_g),
        col(dil_q_g), col(dil_k_g), col(swa_q_g), col(swa_k_g), cos_t, sin_t, kpos_t, aslope)
    nb = S // MOBA_BLOCK
    kmean_h = akm.reshape(B, nb, N_HEADS, HEAD_DIM).transpose(0, 2, 1, 3)
    oat = _moba(aqt, aq32t, ak, avt, kmean_h).reshape(B, GROUP, S)
    obt = _mla(bqt, bk, bvt).reshape(B, GROUP, S)
    oc = _dilated(cq, ck, cv)
    od = _swa(dq, dk, dv, swa_sinks)
    perm = np.concatenate([np.arange(HEAD_DIM) + HEAD_DIM * h for h in SWA_HEAD_ORDER])
    gg = group_out_g.at[3].set(group_out_g[3][perm])
    wo = jnp.concatenate([w_o[:3 * GROUP], w_o[3 * GROUP + perm]], axis=0)
    return _tail(x, oat, obt, oc, od, gg, wo.astype(BF16), mlp_norm_g.reshape(1, -1),
                 w_up.astype(BF16), w_down.astype(BF16))


def kernel(x, attn_norm_g, w_in, moba_q_g, moba_k_g, mla_qlat_g, mla_kvlat_g, mla_w_uq, mla_w_ukv, mla_q_g,
           mla_k_g, dil_q_g, dil_k_g, swa_q_g, swa_k_g, swa_sinks, group_out_g, w_o, mlp_norm_g, w_up, w_down):
    S = x.shape[1]
    assert S % max(d * Q_BLOCK for _, d in DILATED_BRANCHES) == 0 and S % INPROJ_TM == 0
    cos_t, sin_t = _rope_tables_t(S)
    kpos_t, aslope = _moba_aug_tables(S)
    params = (attn_norm_g, w_in, moba_q_g, moba_k_g, mla_qlat_g, mla_kvlat_g, mla_w_uq, mla_w_ukv, mla_q_g,
              mla_k_g, dil_q_g, dil_k_g, swa_q_g, swa_k_g, swa_sinks, group_out_g, w_o, mlp_norm_g, w_up, w_down)
    for l in range(attn_norm_g.shape[0]):
        x = _layer(x, cos_t, sin_t, kpos_t, aslope, *[p[l] for p in params])
    return x
```

```python
import numpy as np
import jax
import jax.numpy as jnp
from jax import lax
from jax.experimental import pallas as pl
from jax.experimental.pallas import tpu as pltpu

F32 = jnp.float32
BF16 = jnp.bfloat16

D_MODEL = 1024
HEAD_DIM = 64
N_HEADS = 4
GROUP = N_HEADS * HEAD_DIM
LANES = 128
N_HALF = GROUP // LANES
MOBA_BLOCK = 256
MOBA_TOPK = 3
MLA_Q_RANK = 256
MLA_KV_RANK = 128
MLA_NOPE = 64
MLA_ROPE = 32
MLA_QK = MLA_NOPE + MLA_ROPE
ROPE_THETA = 10000.0
DILATED_BRANCHES = ((128, 1), (512, 4), (2048, 16))
Q_BLOCK = 128
SWA_WINDOW = 128
SWA_KV_HEADS = 2
SWA_KV_WIDTH = SWA_KV_HEADS * HEAD_DIM
D_FF = 4 * D_MODEL
EPS = 1e-6
NEG = -1e30

_SECTIONS = (("a_q", GROUP), ("a_k", GROUP), ("a_v", GROUP), ("b_ql", MLA_Q_RANK), ("b_kvl", MLA_KV_RANK),
             ("b_kr", MLA_ROPE), ("c_q", GROUP), ("c_k", GROUP), ("c_v", GROUP), ("d_q", GROUP),
             ("d_k", SWA_KV_WIDTH), ("d_v", SWA_KV_WIDTH))
_CHUNKS = (("a_q", "a_k", "a_v", "b_ql", "b_kvl", "b_kr"), ("c_q", "c_k", "d_q", "d_k"), ("c_v", "d_v"))


def _projection_layout():
    width = dict(_SECTIONS)
    col, start = 0, {}
    for name, w in _SECTIONS:
        start[name] = col
        col += w
    rows, where, chunk_rows = [], {}, []
    for ci, chunk in enumerate(_CHUNKS):
        off = 0
        for name in chunk:
            where[name] = (ci, off, width[name])
            rows.extend(range(start[name], start[name] + width[name]))
            off += width[name]
        chunk_rows.append(off)
    return np.asarray(rows), where, tuple(chunk_rows)


_W1T_ROWS, _SECTION_AT, _CHUNK_ROWS = _projection_layout()
IN_COLS = len(_W1T_ROWS)

VMEM_LIMIT = 56 * 1024 * 1024

INPROJ_TM = 512
TAIL_TM = 512
ATT_TK = 256
ATT_TQ = 2 * ATT_TK
ATT_DK = 128
ATT_DV = HEAD_DIM + 16
AUG_ROWS = 8
LOG2E = 1.4426950408889634
FLASH_PAIRS = 2

_NT = (((1,), (1,)), ((), ()))


def _alibi_slopes():
    n = 3 * N_HEADS
    idx = np.arange(1, n + 1, dtype=np.float32).reshape(N_HEADS, 3)
    s = np.exp2(-8.0 * idx / n).astype(np.float32)
    return s[:, 0], s[:, 1], s[:, 2]


SLOPE_A, SLOPE_C, SLOPE_D = _alibi_slopes()


def _params(n_axes):
    return pltpu.CompilerParams(dimension_semantics=("arbitrary",) * n_axes,
                                vmem_limit_bytes=VMEM_LIMIT)


def _head_norm_t(sec, g_col, n_heads, width):
    outs = []
    for h in range(n_heads):
        s = sec[h * width:(h + 1) * width, :]
        ms = jnp.sum(s * s, axis=0, keepdims=True) * (1.0 / width)
        outs.append(s * lax.rsqrt(ms + EPS) * g_col)
    return outs


def _inproj_kernel(x_ref, gx_ref, w1t_ref, gaq_ref, gak_ref, gql_ref, gkvl_ref, wuqt_ref, wukvt_ref,
                   gbq_ref, gbk_ref, gcq_ref, gck_ref, gdq_ref, gdk_ref, cos_ref, sin_ref, kpos_ref, aslope_ref,
                   aqt_ref, aq32t_ref, ak_ref, akm_ref, avt_ref,
                   bqt_ref, bk_ref, bvt_ref,
                   cq_ref, ck_ref, cv_ref, dq_ref, dk_ref, dv_ref,
                   *h_scrs):
    tm = x_ref.shape[1]
    x = x_ref[0]
    ms = jnp.mean(x * x, axis=-1, keepdims=True)
    xn = (x * lax.rsqrt(ms + EPS) * gx_ref[...]).astype(BF16)
    row0 = 0
    for h_scr, n_rows in zip(h_scrs, _CHUNK_ROWS):
        h_scr[...] = lax.dot_general(w1t_ref[row0:row0 + n_rows, :], xn, _NT, preferred_element_type=F32)
        row0 += n_rows

    def sec(name, lo=0, hi=None):
        chunk, off, width = _SECTION_AT[name]
        return h_scrs[chunk][off + lo:off + (width if hi is None else hi), :]

    scale = HEAD_DIM ** -0.5

    ones_rows = jnp.ones((ATT_DV - HEAD_DIM, ATT_TK), F32)

    def store_vt(ref, h, vh):
        for c in range(tm // ATT_TK):
            ref[0, h, c] = jnp.concatenate([vh[:, c * ATT_TK:(c + 1) * ATT_TK], ones_rows], axis=0).astype(BF16)

    pad_rows = jnp.zeros((ATT_DK - HEAD_DIM - AUG_ROWS, tm), F32)
    qa = _head_norm_t(sec("a_q"), gaq_ref[...], N_HEADS, HEAD_DIM)
    ka = _head_norm_t(sec("a_k"), gak_ref[...], N_HEADS, HEAD_DIM)
    ka_nat = jnp.concatenate(ka, axis=0).T
    for c in range(tm // MOBA_BLOCK):
        akm_ref[0, c] = jnp.sum(ka_nat[c * MOBA_BLOCK:(c + 1) * MOBA_BLOCK, :], axis=0,
                                keepdims=True) * (1.0 / MOBA_BLOCK)
    kpos = kpos_ref[...]
    for h in range(N_HEADS):
        aq32t_ref[0, h] = qa[h]
        slope_rows = jnp.broadcast_to(aslope_ref[h], (AUG_ROWS, tm))
        aqt_ref[0, h] = jnp.concatenate([qa[h] * (scale * LOG2E), slope_rows, pad_rows], axis=0).astype(BF16)
        ak_ref[0, h] = jnp.concatenate([ka[h], kpos, pad_rows], axis=0).T.astype(BF16)
        store_vt(avt_ref, h, sec("a_v", h * HEAD_DIM, (h + 1) * HEAD_DIM))

    cos = cos_ref[...]
    sin = sin_ref[...]
    half = MLA_ROPE // 2

    def rope_pad(t, sc):
        x1 = t[MLA_NOPE:MLA_NOPE + half, :]
        x2 = t[MLA_NOPE + half:MLA_QK, :]
        return jnp.concatenate([t[:MLA_NOPE, :] * sc, (x1 * cos - x2 * sin) * sc, (x1 * sin + x2 * cos) * sc,
                                jnp.zeros((ATT_DK - MLA_QK, tm), F32)], axis=0)

    ql = sec("b_ql")
    ql = ql * lax.rsqrt(jnp.sum(ql * ql, axis=0, keepdims=True) * (1.0 / MLA_Q_RANK) + EPS) * gql_ref[...]
    qb = jnp.dot(wuqt_ref[...], ql.astype(BF16), preferred_element_type=F32)
    qb = _head_norm_t(qb, gbq_ref[...], N_HEADS, MLA_QK)
    kvl = sec("b_kvl")
    kvl = kvl * lax.rsqrt(jnp.sum(kvl * kvl, axis=0, keepdims=True) * (1.0 / MLA_KV_RANK) + EPS) * gkvl_ref[...]
    kvb = jnp.dot(wukvt_ref[...], kvl.astype(BF16), preferred_element_type=F32)
    kr = sec("b_kr")
    gbk = gbk_ref[...]
    for h in range(N_HEADS):
        bqt_ref[0, h] = rope_pad(qb[h], MLA_QK ** -0.5 * LOG2E).astype(BF16)
        kh = jnp.concatenate([kvb[h * 2 * HEAD_DIM:h * 2 * HEAD_DIM + MLA_NOPE, :], kr], axis=0)
        kh = kh * lax.rsqrt(jnp.sum(kh * kh, axis=0, keepdims=True) * (1.0 / MLA_QK) + EPS) * gbk
        bk_ref[0, h] = rope_pad(kh, 1.0).T.astype(BF16)
        store_vt(bvt_ref, h, kvb[h * 2 * HEAD_DIM + MLA_NOPE:(h + 1) * 2 * HEAD_DIM, :])

    qc = jnp.concatenate(_head_norm_t(sec("c_q"), gcq_ref[...], N_HEADS, HEAD_DIM), axis=0)
    kc = jnp.concatenate(_head_norm_t(sec("c_k"), gck_ref[...], N_HEADS, HEAD_DIM), axis=0)
    for ref, val in ((cq_ref, qc * (scale * LOG2E)), (ck_ref, kc), (cv_ref, sec("c_v"))):
        for c in range(N_HALF):
            ref[0, c] = val[c * LANES:(c + 1) * LANES, :].T

    qd = _head_norm_t(sec("d_q"), gdq_ref[...], N_HEADS, HEAD_DIM)
    qd = jnp.concatenate([qd[h] for h in SWA_HEAD_ORDER], axis=0)
    dq_ref[0] = (qd * (scale * LOG2E)).T.astype(BF16)
    kd = jnp.concatenate(_head_norm_t(sec("d_k"), gdk_ref[...], SWA_KV_HEADS, HEAD_DIM),
                         axis=0)
    dk_ref[0] = kd.T.astype(BF16)
    dv_ref[0] = sec("d_v").T.astype(BF16)


def _inproj(x, gx, w1t, gaq, gak, gql, gkvl, wuqt, wukvt, gbq, gbk, gcq, gck, gdq, gdk, cos_t, sin_t, kpos_t,
            aslope):
    B, S, _ = x.shape
    tm = INPROJ_TM
    nb = S // ATT_TK
    cpt = tm // ATT_TK
    H = N_HEADS
    full = lambda a: pl.BlockSpec(a.shape, lambda b, t: (0,) * a.ndim)
    in_specs = [pl.BlockSpec((1, tm, D_MODEL), lambda b, t: (b, t, 0)), full(gx), full(w1t), full(gaq), full(gak),
                full(gql), full(gkvl), full(wuqt), full(wukvt), full(gbq), full(gbk), full(gcq), full(gck),
                full(gdq), full(gdk),
                pl.BlockSpec((MLA_ROPE // 2, tm), lambda b, t: (0, t)),
                pl.BlockSpec((MLA_ROPE // 2, tm), lambda b, t: (0, t)),
                pl.BlockSpec((AUG_ROWS, tm), lambda b, t: (0, t)), full(aslope)]
    head_t = lambda w: pl.BlockSpec((1, H, w, tm), lambda b, t: (b, 0, 0, t))
    head_n = pl.BlockSpec((1, H, tm, ATT_DK), lambda b, t: (b, 0, t, 0))
    vt_spec = pl.BlockSpec((1, H, cpt, ATT_DV, ATT_TK), lambda b, t: (b, 0, t, 0, 0))
    nat = lambda w: pl.BlockSpec((1, tm, w), lambda b, t: (b, t, 0))
    halves = pl.BlockSpec((1, N_HALF, tm, LANES), lambda b, t: (b, 0, t, 0))
    out_shape = [
        jax.ShapeDtypeStruct((B, H, ATT_DK, S), BF16),
        jax.ShapeDtypeStruct((B, H, HEAD_DIM, S), F32),
        jax.ShapeDtypeStruct((B, H, S, ATT_DK), BF16),
        jax.ShapeDtypeStruct((B, nb, 1, GROUP), F32),
        jax.ShapeDtypeStruct((B, H, nb, ATT_DV, ATT_TK), BF16),
        jax.ShapeDtypeStruct((B, H, ATT_DK, S), BF16),
        jax.ShapeDtypeStruct((B, H, S, ATT_DK), BF16),
        jax.ShapeDtypeStruct((B, H, nb, ATT_DV, ATT_TK), BF16),
        jax.ShapeDtypeStruct((B, N_HALF, S, LANES), F32),
        jax.ShapeDtypeStruct((B, N_HALF, S, LANES), F32),
        jax.ShapeDtypeStruct((B, N_HALF, S, LANES), F32),
        jax.ShapeDtypeStruct((B, S, GROUP), BF16),
        jax.ShapeDtypeStruct((B, S, SWA_KV_WIDTH), BF16),
        jax.ShapeDtypeStruct((B, S, SWA_KV_WIDTH), BF16),
    ]
    out_specs = [head_t(ATT_DK), head_t(HEAD_DIM), head_n,
                 pl.BlockSpec((1, tm // MOBA_BLOCK, 1, GROUP), lambda b, t: (b, t, 0, 0)), vt_spec,
                 head_t(ATT_DK), head_n, vt_spec,
                 halves, halves, halves, nat(GROUP), nat(SWA_KV_WIDTH), nat(SWA_KV_WIDTH)]
    return pl.pallas_call(
        _inproj_kernel,
        grid=(B, S // tm),
        in_specs=in_specs,
        out_specs=out_specs,
        out_shape=out_shape,
        scratch_shapes=[pltpu.VMEM((n_rows, tm), F32) for n_rows in _CHUNK_ROWS],
        compiler_params=_params(2),
        name="inproj",
    )(x, gx, w1t, gaq, gak, gql, gkvl, wuqt, wukvt, gbq, gbk, gcq, gck, gdq, gdk, cos_t, sin_t, kpos_t, aslope)


def _flash_heads(qt_ref, k_ref, vt_ref, o_ref, s_bufs, p_bufs, acc_scr, rowb_scr, i):
    tq = o_ref.shape[3]
    n_rowb = 0 if rowb_scr is None else rowb_scr.shape[0]
    n_tail = tq // ATT_TK
    assert n_tail == 2
    n_past = i * n_tail
    heads = range(acc_scr.shape[0])

    def block_of(pos):
        return jnp.where(pos < n_tail, n_past + pos, pos - n_tail)

    def head_scores(h, slot, blk):
        kb = k_ref[0, h, pl.ds(pl.multiple_of(blk * ATT_TK, ATT_TK), ATT_TK), :]
        s_bufs[slot][h] = jnp.dot(kb, qt_ref[0, h], preferred_element_type=F32)

    def stage_scores(slot, blk):
        for h in heads:
            head_scores(h, slot, blk)

    def stage_softmax(slot, blk, ms, causal=None):
        new_ms, alphas = [], []
        for h in heads:
            m_parts, a_parts = [], []
            rv_row = rowb_scr[h, pl.ds(blk, 1), :] if h < n_rowb else None
            for c in range(tq // LANES):
                cols = slice(c * LANES, (c + 1) * LANES)
                st = s_bufs[slot][h, :, cols]
                if causal is not None:
                    st = jnp.where(causal[:, cols], st, 2 * NEG)
                cm = jnp.max(st, axis=0, keepdims=True)
                m_old = ms[h][:, cols]
                if rv_row is None:
                    m_new = jnp.maximum(m_old, cm)
                    shift = m_new
                else:
                    rv = rv_row[:, cols]
                    m_new = jnp.maximum(m_old, cm + rv)
                    shift = m_new - rv
                a_parts.append(jnp.exp2(m_old - m_new))
                m_parts.append(m_new)
                p_bufs[slot][h, :, cols] = jnp.exp2(st - shift).astype(BF16)
            new_ms.append(jnp.concatenate(m_parts, axis=1))
            alphas.append(jnp.concatenate(a_parts, axis=1))
        return tuple(new_ms), tuple(alphas)

    def head_values(h, slot, blk, alphas):
        acc_scr[h] = alphas[h] * acc_scr[h] + jnp.dot(vt_ref[0, h, blk], p_bufs[slot][h],
                                                      preferred_element_type=F32)

    def stage_values(slot, blk, alphas):
        for h in heads:
            head_values(h, slot, blk, alphas)

    krow = lax.broadcasted_iota(jnp.int32, (ATT_TK, tq), 0)
    ti = lax.broadcasted_iota(jnp.int32, (ATT_TK, tq), 1)
    for h in heads:
        acc_scr[h] = jnp.zeros((ATT_DV, tq), F32)
    stage_scores(0, n_past)
    stage_scores(1, n_past + 1)
    ms = tuple(jnp.full((1, tq), NEG, F32) for _ in heads)
    ms, alphas = stage_softmax(0, n_past, ms, causal=ti >= krow)
    stage_values(0, n_past, alphas)
    ms, alphas = stage_softmax(1, n_past + 1, ms, causal=ti >= krow + ATT_TK)
    stage_scores(0, 0)

    def step(pos, slot, ms, alphas):
        ms, new_alphas = stage_softmax(1 - slot, pos + 1 - n_tail, ms)
        blk = block_of(pos)
        for h in heads:
            head_values(h, slot, blk, alphas)
            head_scores(h, slot, pos)
        return ms, new_alphas

    def pair(first, carry):
        ms, alphas = step(first, 1, *carry)
        return step(first + 1, 0, ms, alphas)

    def trip(t, carry):
        for u in range(FLASH_PAIRS):
            carry = pair(2 * (FLASH_PAIRS * t + u) + 1, carry)
        return carry

    carry = lax.fori_loop(0, i // FLASH_PAIRS, trip, (ms, alphas))
    ms, alphas = lax.fori_loop(i - i % FLASH_PAIRS, i, lambda u, c: pair(2 * u + 1, c), carry)
    last = n_past + 1
    stage_values(1, block_of(last), alphas)
    for h in heads:
        acc = acc_scr[h]
        o_ref[0, h] = acc[:HEAD_DIM, :] / acc[HEAD_DIM:HEAD_DIM + 1, :]


def _moba_kernel(qt_ref, q32t_ref, k_ref, vt_ref, kmean_ref, bqt_ref, bk_ref, bvt_ref, o_ref, ob_ref,
                 rowb_scr, s0, s1, p0, p1, acc_scr):
    i = pl.program_id(1)
    nb = kmean_ref.shape[2]
    tq = qt_ref.shape[3]
    blk = lax.broadcasted_iota(jnp.int32, (nb, tq), 0)
    col = lax.broadcasted_iota(jnp.int32, (nb, tq), 1)
    qblk = i * (tq // MOBA_BLOCK) + col // MOBA_BLOCK
    past = blk < qblk
    dist0 = (i * tq + col - blk * MOBA_BLOCK).astype(F32)
    for h in range(N_HEADS):
        gate = jnp.dot(kmean_ref[0, h], q32t_ref[0, h], preferred_element_type=F32,
                       precision=lax.Precision.HIGHEST)
        gate = jnp.where(past, gate, NEG)
        sel = blk == qblk
        for _ in range(MOBA_TOPK):
            best = jnp.max(gate, axis=0, keepdims=True)
            first = jnp.min(jnp.where(gate == best, blk, nb), axis=0, keepdims=True)
            pick = blk == first
            sel = jnp.logical_or(sel, jnp.logical_and(pick, past))
            gate = jnp.where(pick, -jnp.inf, gate)
        rowb_scr[h] = jnp.where(sel, (-float(SLOPE_A[h]) * LOG2E) * dist0, 2 * NEG)
    _flash_heads(_HeadGroups(qt_ref, bqt_ref), _HeadGroups(k_ref, bk_ref), _HeadGroups(vt_ref, bvt_ref),
                 _HeadGroups(o_ref, ob_ref), (s0, s1), (p0, p1), acc_scr, rowb_scr, i)


class _HeadGroups:
    def __init__(self, *refs):
        self.refs = refs
        self.shape = refs[0].shape

    def _at(self, idx):
        return self.refs[idx[1] // N_HEADS], (idx[0], idx[1] % N_HEADS) + tuple(idx[2:])

    def __getitem__(self, idx):
        ref, at = self._at(idx)
        return ref[at]

    def __setitem__(self, idx, val):
        ref, at = self._at(idx)
        ref[at] = val


def _dense_attention(aqt, aq32t, ak, avt, kmean_h, bqt, bk, bvt):
    B, H, _, S = aqt.shape
    tq = ATT_TQ
    nb = S // ATT_TK
    q_spec = pl.BlockSpec((1, H, ATT_DK, tq), lambda b, i: (b, 0, 0, i))
    k_spec = pl.BlockSpec((1, H, S, ATT_DK), lambda b, i: (b, 0, 0, 0), pipeline_mode=pl.Buffered(1))
    vt_spec = pl.BlockSpec((1, H, nb, ATT_DV, ATT_TK), lambda b, i: (b, 0, 0, 0, 0), pipeline_mode=pl.Buffered(1))
    o_spec = pl.BlockSpec((1, H, HEAD_DIM, tq), lambda b, i: (b, 0, 0, i))
    o_shape = jax.ShapeDtypeStruct((B, H, HEAD_DIM, S), F32)
    heads = 2 * H
    s_buf, p_buf = pltpu.VMEM((heads, ATT_TK, tq), F32), pltpu.VMEM((heads, ATT_TK, tq), BF16)
    return pl.pallas_call(
        _moba_kernel,
        grid=(B, S // tq),
        in_specs=[q_spec, pl.BlockSpec((1, H, HEAD_DIM, tq), lambda b, i: (b, 0, 0, i)), k_spec, vt_spec,
                  pl.BlockSpec((1, H, nb, HEAD_DIM), lambda b, i: (b, 0, 0, 0)), q_spec, k_spec, vt_spec],
        out_specs=[o_spec, o_spec],
        out_shape=[o_shape, o_shape],
        scratch_shapes=[pltpu.VMEM((H, nb, tq), F32), s_buf, s_buf, p_buf, p_buf,
                        pltpu.VMEM((heads, ATT_DV, tq), F32)],
        compiler_params=_params(2),
        name="dense_attention",
    )(aqt, aq32t, ak, avt, kmean_h, bqt, bk, bvt)


DIL_SPAN = max(d for _, d in DILATED_BRANCHES) * Q_BLOCK
DIL_UNITS = DIL_SPAN // Q_BLOCK
DIL_GROUP = (4, 2, 2)


def _dilated_bias():
    qi = np.arange(Q_BLOCK)[:, None]
    kidx = np.arange(2 * Q_BLOCK)[None, :]
    rel = qi + Q_BLOCK - kidx
    out = np.empty((2, len(DILATED_BRANCHES), N_HEADS, Q_BLOCK, 2 * Q_BLOCK), np.float32)
    for bi, (window, d) in enumerate(DILATED_BRANCHES):
        valid = (rel >= 0) & (rel <= window // d)
        for h in range(N_HEADS):
            bias = -SLOPE_C[h] * np.float32(LOG2E) * (d * rel).astype(np.float32)
            out[0, bi, h] = np.where(valid, bias, NEG)
            out[1, bi, h] = np.where(valid & (kidx >= Q_BLOCK), bias, NEG)
    return out


def _rows_load(ref, lead, start, size, stride):
    return jnp.concatenate([ref[lead + (c, pl.ds(start, size, stride=stride), slice(None))]
                            for c in range(N_HALF)], axis=-1)


def _rows_store(ref, lead, start, size, stride, val):
    for c in range(N_HALF):
        ref[lead + (c, pl.ds(start, size, stride=stride), slice(None))] = val[:, c * LANES:(c + 1) * LANES]


def _dilated_kernel(q_ref, kp_ref, kc_ref, vp_ref, vc_ref, bias_ref, o_ref, kbuf, vbuf, m_scr, den_scr, num_scr):
    span = pl.program_id(1)
    kbuf[:, 0:DIL_SPAN, :] = kp_ref[0]
    kbuf[:, DIL_SPAN:, :] = kc_ref[0]
    vbuf[:, 0:DIL_SPAN, :] = vp_ref[0]
    vbuf[:, DIL_SPAN:, :] = vc_ref[0]
    lane_head = lax.broadcasted_iota(jnp.int32, (1, GROUP), 1) // HEAD_DIM
    hmask = [lane_head == h for h in range(N_HEADS)]
    hmask_f = [m.astype(F32) for m in hmask]
    first_span = jnp.where(span == 0, 1, 0)

    def per_head(cols):
        out = cols[N_HEADS - 1]
        for h in range(N_HEADS - 2, -1, -1):
            out = jnp.where(hmask[h], cols[h], out)
        return out

    order = sorted(range(len(DILATED_BRANCHES)), key=lambda b: -DILATED_BRANCHES[b][1])
    for bi in order:
        d, n_group = DILATED_BRANCHES[bi][1], DIL_GROUP[bi]
        first, last = bi == order[0], bi == order[-1]

        def group(g, _, bi=bi, d=d, n_group=n_group, first=first, last=last):
            fronts = []
            for uu in range(n_group):
                u = g * n_group + uu
                r, n = u % d, u // d
                qstart = n * (Q_BLOCK * d) + r
                kstart = DIL_SPAN + qstart - Q_BLOCK * d
                q = _rows_load(q_ref, (0,), qstart, Q_BLOCK, d)
                k2 = _rows_load(kbuf, (), kstart, 2 * Q_BLOCK, d).astype(BF16)
                v2 = _rows_load(vbuf, (), kstart, 2 * Q_BLOCK, d).astype(BF16)
                q4 = jnp.concatenate([(q * hmask_f[h]).astype(BF16) for h in range(N_HEADS)], axis=0)
                s4 = lax.dot_general(q4, k2, _NT, preferred_element_type=F32)
                variant = jnp.where(n == 0, first_span, 0)
                fronts.append((qstart, s4, v2, variant))
            for qstart, s4, v2, variant in fronts:
                es, ms, ls = [], [], []
                for h in range(N_HEADS):
                    s = s4[h * Q_BLOCK:(h + 1) * Q_BLOCK, :] + bias_ref[variant, bi, h]
                    m = jnp.max(s, axis=-1, keepdims=True)
                    e = jnp.exp2(s - m)
                    ls.append(jnp.sum(e, axis=-1, keepdims=True))
                    ms.append(m)
                    es.append(e.astype(BF16))
                o4 = jnp.dot(jnp.concatenate(es, axis=0), v2, preferred_element_type=F32)
                o = o4[(N_HEADS - 1) * Q_BLOCK:, :]
                for h in range(N_HEADS - 2, -1, -1):
                    o = jnp.where(hmask[h], o4[h * Q_BLOCK:(h + 1) * Q_BLOCK, :], o)
                m_b, l_b = per_head(ms), per_head(ls)
                at = ((), qstart, Q_BLOCK, d)
                if first:
                    _rows_store(m_scr, *at, m_b)
                    _rows_store(num_scr, *at, o)
                    _rows_store(den_scr, *at, l_b)
                else:
                    m_old = _rows_load(m_scr, *at)
                    m_new = jnp.maximum(m_old, m_b)
                    a, b = jnp.exp2(m_old - m_new), jnp.exp2(m_b - m_new)
                    num = a * _rows_load(num_scr, *at) + b * o
                    den = a * _rows_load(den_scr, *at) + b * l_b
                    if last:
                        _rows_store(o_ref, (0,), qstart, Q_BLOCK, d, num / den)
                    else:
                        _rows_store(m_scr, *at, m_new)
                        _rows_store(num_scr, *at, num)
                        _rows_store(den_scr, *at, den)
            return 0

        lax.fori_loop(0, DIL_UNITS // n_group, group, 0)


def _dilated(cq, ck, cv):
    B, _, S, _ = cq.shape
    cur = pl.BlockSpec((1, N_HALF, DIL_SPAN, LANES), lambda b, s: (b, 0, s, 0))
    prev = pl.BlockSpec((1, N_HALF, DIL_SPAN, LANES), lambda b, s: (b, 0, jnp.maximum(s - 1, 0), 0))
    bias = jnp.asarray(_dilated_bias())
    return pl.pallas_call(
        _dilated_kernel,
        grid=(B, S // DIL_SPAN),
        in_specs=[cur, prev, cur, prev, cur, pl.BlockSpec(bias.shape, lambda b, s: (0,) * bias.ndim)],
        out_specs=cur,
        out_shape=jax.ShapeDtypeStruct((B, N_HALF, S, LANES), F32),
        scratch_shapes=[pltpu.VMEM((N_HALF, 2 * DIL_SPAN, LANES), F32),
                        pltpu.VMEM((N_HALF, 2 * DIL_SPAN, LANES), F32),
                        pltpu.VMEM((N_HALF, DIL_SPAN, LANES), F32), pltpu.VMEM((N_HALF, DIL_SPAN, LANES), F32),
                        pltpu.VMEM((N_HALF, DIL_SPAN, LANES), F32)],
        compiler_params=_params(2),
        name="dilated",
    )(cq, ck, ck, cv, cv, bias)


SWA_SPAN = 1024
SWA_GROUP = 4
SWA_HEAD_ORDER = (0, 2, 1, 3)


def _swa_bias():
    qi = np.arange(Q_BLOCK)[:, None]
    kidx = np.arange(2 * Q_BLOCK)[None, :]
    rel = qi + Q_BLOCK - kidx
    valid = (rel >= 0) & (rel < SWA_WINDOW)
    out = np.empty((2, N_HEADS, Q_BLOCK, 2 * Q_BLOCK), np.float32)
    for h in range(N_HEADS):
        bias = -SLOPE_D[h] * np.float32(LOG2E) * rel.astype(np.float32)
        out[0, h] = np.where(valid, bias, NEG)
        out[1, h] = np.where(valid & (kidx >= Q_BLOCK), bias, NEG)
    return out


def _swa_kernel(sink_ref, q_ref, k_ref, v_ref, bias_ref, o_ref):
    span = pl.program_id(1)
    units = SWA_SPAN // Q_BLOCK
    half = lax.broadcasted_iota(jnp.int32, (1, LANES), 1) // HEAD_DIM
    lo = half == 0
    kv_mask = [jnp.where(half == g, 1.0, 0.0).astype(BF16) for g in range(SWA_KV_HEADS)]

    def group(g, _):
        fronts = []
        for uu in range(SWA_GROUP):
            u = g * SWA_GROUP + uu
            n = span * units + u
            lo_start = pl.multiple_of(jnp.maximum(n - 1, 0) * Q_BLOCK, Q_BLOCK)
            hi_start = pl.multiple_of(n * Q_BLOCK, Q_BLOCK)
            qstart = pl.multiple_of(u * Q_BLOCK, Q_BLOCK)
            q = q_ref[0, pl.ds(qstart, Q_BLOCK), :]
            k2 = jnp.concatenate([k_ref[0, pl.ds(lo_start, Q_BLOCK), :], k_ref[0, pl.ds(hi_start, Q_BLOCK), :]],
                                 axis=0)
            v2 = jnp.concatenate([v_ref[0, pl.ds(lo_start, Q_BLOCK), :], v_ref[0, pl.ds(hi_start, Q_BLOCK), :]],
                                 axis=0)
            q4 = jnp.concatenate([q[:, (h % 2) * LANES:(h % 2 + 1) * LANES] * kv_mask[h // 2]
                                  for h in range(N_HEADS)], axis=0)
            s4 = lax.dot_general(q4, k2, _NT, preferred_element_type=F32)
            fronts.append((qstart, s4, v2, jnp.where(n == 0, 1, 0)))
        for qstart, s4, v2, variant in fronts:
            es, ls = [], []
            for h in range(N_HEADS):
                s = s4[h * Q_BLOCK:(h + 1) * Q_BLOCK, :] + bias_ref[variant, h]
                sink = sink_ref[h] * LOG2E
                m = jnp.maximum(jnp.max(s, axis=-1, keepdims=True), sink)
                e = jnp.exp2(s - m)
                ls.append(jnp.sum(e, axis=-1, keepdims=True) + jnp.exp2(sink - m))
                es.append(e.astype(BF16))
            o4 = jnp.dot(jnp.concatenate(es, axis=0), v2, preferred_element_type=F32)
            tiles = []
            for t in range(N_HALF):
                a, b = t, t + 2
                tiles.append(jnp.where(lo, o4[a * Q_BLOCK:(a + 1) * Q_BLOCK, :] / ls[a],
                                       o4[b * Q_BLOCK:(b + 1) * Q_BLOCK, :] / ls[b]))
            o_ref[0, pl.ds(qstart, Q_BLOCK), :] = jnp.concatenate(tiles, axis=-1)
        return 0

    lax.fori_loop(0, units // SWA_GROUP, group, 0)


def _swa(dq, dk, dv, sinks):
    B, S, _ = dq.shape
    bias = jnp.asarray(_swa_bias())
    return pl.pallas_call(
        _swa_kernel,
        grid=(B, S // SWA_SPAN),
        in_specs=[pl.BlockSpec(memory_space=pltpu.SMEM),
                  pl.BlockSpec((1, SWA_SPAN, GROUP), lambda b, i: (b, i, 0)),
                  pl.BlockSpec((1, S, SWA_KV_WIDTH), lambda b, i: (b, 0, 0)),
                  pl.BlockSpec((1, S, SWA_KV_WIDTH), lambda b, i: (b, 0, 0)),
                  pl.BlockSpec(bias.shape, lambda b, i: (0,) * bias.ndim)],
        out_specs=pl.BlockSpec((1, SWA_SPAN, GROUP), lambda b, i: (b, i, 0)),
        out_shape=jax.ShapeDtypeStruct((B, S, GROUP), F32),
        compiler_params=_params(2),
        name="swa",
    )(sinks, dq, dk, dv, bias)


def _row_norm(y, g_row):
    return y * lax.rsqrt(jnp.mean(y * y, axis=-1, keepdims=True) + EPS) * g_row


def _tail_kernel(x_ref, oat_ref, obt_ref, oc_ref, od_ref, gg_ref, wo_ref, gm_ref, wup_ref, wdn_ref, out_ref):
    gg = gg_ref[...]

    def col_norm_t(yt, g_row):
        y = (yt * lax.rsqrt(jnp.mean(yt * yt, axis=0, keepdims=True) + EPS)).T
        return y * g_row

    ga = col_norm_t(oat_ref[0], gg[0:1, :])
    gb = col_norm_t(obt_ref[0], gg[1:2, :])
    gc = _row_norm(jnp.concatenate([oc_ref[0, c] for c in range(N_HALF)], axis=-1), gg[2:3, :])
    gd = _row_norm(od_ref[0], gg[3:4, :])
    mixed = jnp.concatenate([ga, gb, gc, gd], axis=-1).astype(BF16)
    x1 = x_ref[0] + jnp.dot(mixed, wo_ref[...], preferred_element_type=F32)
    xn = _row_norm(x1, gm_ref[...]).astype(BF16)
    u = jnp.maximum(jnp.dot(xn, wup_ref[...], preferred_element_type=F32), 0.0)
    out_ref[0] = x1 + jnp.dot((u * u).astype(BF16), wdn_ref[...], preferred_element_type=F32)


def _tail(x, oat, obt, oc, od, gg, wo, gm, wup, wdn):
    B, S, _ = x.shape
    tm = TAIL_TM
    nat = lambda w: pl.BlockSpec((1, tm, w), lambda b, t: (b, t, 0))
    ft = pl.BlockSpec((1, GROUP, tm), lambda b, t: (b, 0, t))
    const = lambda a: pl.BlockSpec(a.shape, lambda b, t: (0,) * a.ndim, pipeline_mode=pl.Buffered(1))
    return pl.pallas_call(
        _tail_kernel,
        grid=(B, S // tm),
        in_specs=[nat(D_MODEL), ft, ft, pl.BlockSpec((1, N_HALF, tm, LANES), lambda b, t: (b, 0, t, 0)),
                  nat(GROUP), const(gg), const(wo), const(gm),
                  const(wup), const(wdn)],
        out_specs=nat(D_MODEL),
        out_shape=jax.ShapeDtypeStruct((B, S, D_MODEL), F32),
        compiler_params=_params(2),
        name="tail",
    )(x, oat, obt, oc, od, gg, wo, gm, wup, wdn)


def _rope_tables_t(S):
    inv = 1.0 / (ROPE_THETA ** (jnp.arange(0, MLA_ROPE, 2, dtype=F32) / MLA_ROPE))
    ang = inv[:, None] * jnp.arange(S, dtype=F32)[None, :]
    return jnp.cos(ang), jnp.sin(ang)


def _moba_aug_tables(S):
    pos = np.arange(S, dtype=np.float32) % MOBA_BLOCK
    kpos = np.zeros((AUG_ROWS, S), np.float32)
    kpos[:3] = pos
    aslope = np.zeros((N_HEADS, AUG_ROWS, 1), np.float32)
    rest = (SLOPE_A * np.float32(LOG2E)).astype(np.float32)
    for r in range(3):
        piece = rest.astype(BF16).astype(np.float32)
        aslope[:, r, 0] = piece
        rest = rest - piece
    return jnp.asarray(kpos), jnp.asarray(aslope)


def _layer(x, cos_t, sin_t, kpos_t, aslope, attn_norm_g, w_in, moba_q_g, moba_k_g, mla_qlat_g, mla_kvlat_g, mla_w_uq, mla_w_ukv,
           mla_q_g, mla_k_g, dil_q_g, dil_k_g, swa_q_g, swa_k_g, swa_sinks, group_out_g, w_o, mlp_norm_g,
           w_up, w_down):
    B, S, _ = x.shape
    col = lambda g: g.reshape(-1, 1)
    (aqt, aq32t, ak, akm, avt, bqt, bk, bvt, cq, ck, cv, dq, dk, dv) = _inproj(
        x, attn_norm_g.reshape(1, -1), w_in[:, _W1T_ROWS].T.astype(BF16), col(moba_q_g), col(moba_k_g), col(mla_qlat_g),
        col(mla_kvlat_g), mla_w_uq.T.astype(BF16), mla_w_ukv.T.astype(BF16), col(mla_q_g), col(mla_k_g),
        col(dil_q_g), col(dil_k_g), col(swa_q_g), col(swa_k_g), cos_t, sin_t, kpos_t, aslope)
    nb = S // MOBA_BLOCK
    kmean_h = akm.reshape(B, nb, N_HEADS, HEAD_DIM).transpose(0, 2, 1, 3)
    oat, obt = _dense_attention(aqt, aq32t, ak, avt, kmean_h, bqt, bk, bvt)
    oat, obt = oat.reshape(B, GROUP, S), obt.reshape(B, GROUP, S)
    oc = _dilated(cq, ck, cv)
    od = _swa(dq, dk, dv, swa_sinks)
    perm = np.concatenate([np.arange(HEAD_DIM) + HEAD_DIM * h for h in SWA_HEAD_ORDER])
    gg = group_out_g.at[3].set(group_out_g[3][perm])
    wo = jnp.concatenate([w_o[:3 * GROUP], w_o[3 * GROUP + perm]], axis=0)
    return _tail(x, oat, obt, oc, od, gg, wo.astype(BF16), mlp_norm_g.reshape(1, -1),
                 w_up.astype(BF16), w_down.astype(BF16))


def kernel(x, attn_norm_g, w_in, moba_q_g, moba_k_g, mla_qlat_g, mla_kvlat_g, mla_w_uq, mla_w_ukv, mla_q_g,
           mla_k_g, dil_q_g, dil_k_g, swa_q_g, swa_k_g, swa_sinks, group_out_g, w_o, mlp_norm_g, w_up, w_down):
    S = x.shape[1]
    assert S % max(d * Q_BLOCK for _, d in DILATED_BRANCHES) == 0 and S % INPROJ_TM == 0
    cos_t, sin_t = _rope_tables_t(S)
    kpos_t, aslope = _moba_aug_tables(S)
    params = (attn_norm_g, w_in, moba_q_g, moba_k_g, mla_qlat_g, mla_kvlat_g, mla_w_uq, mla_w_ukv, mla_q_g,
              mla_k_g, dil_q_g, dil_k_g, swa_q_g, swa_k_g, swa_sinks, group_out_g, w_o, mlp_norm_g, w_up, w_down)
    for l in range(attn_norm_g.shape[0]):
        x = _layer(x, cos_t, sin_t, kpos_t, aslope, *[p[l] for p in params])
    return x
```

```python
import numpy as np
import jax
import jax.numpy as jnp
from jax import lax
from jax.experimental import pallas as pl
from jax.experimental.pallas import tpu as pltpu

F32 = jnp.float32
BF16 = jnp.bfloat16

D_MODEL = 1024
HEAD_DIM = 64
N_HEADS = 4
GROUP = N_HEADS * HEAD_DIM
LANES = 128
N_HALF = GROUP // LANES
MOBA_BLOCK = 256
MOBA_TOPK = 3
MLA_Q_RANK = 256
MLA_KV_RANK = 128
MLA_NOPE = 64
MLA_ROPE = 32
MLA_QK = MLA_NOPE + MLA_ROPE
ROPE_THETA = 10000.0
DILATED_BRANCHES = ((128, 1), (512, 4), (2048, 16))
Q_BLOCK = 128
SWA_WINDOW = 128
SWA_KV_HEADS = 2
SWA_KV_WIDTH = SWA_KV_HEADS * HEAD_DIM
D_FF = 4 * D_MODEL
EPS = 1e-6
NEG = -1e30

_SECTIONS = (("a_q", GROUP), ("a_k", GROUP), ("a_v", GROUP), ("b_ql", MLA_Q_RANK), ("b_kvl", MLA_KV_RANK),
             ("b_kr", MLA_ROPE), ("c_q", GROUP), ("c_k", GROUP), ("c_v", GROUP), ("d_q", GROUP),
             ("d_k", SWA_KV_WIDTH), ("d_v", SWA_KV_WIDTH))
_CHUNKS = (("a_q", "a_k", "a_v", "b_ql", "b_kvl", "b_kr"), ("c_q", "c_k", "d_q", "d_k"), ("c_v", "d_v"))


def _projection_layout():
    width = dict(_SECTIONS)
    col, start = 0, {}
    for name, w in _SECTIONS:
        start[name] = col
        col += w
    rows, where, chunk_rows = [], {}, []
    for ci, chunk in enumerate(_CHUNKS):
        off = 0
        for name in chunk:
            where[name] = (ci, off, width[name])
            rows.extend(range(start[name], start[name] + width[name]))
            off += width[name]
        chunk_rows.append(off)
    return np.asarray(rows), where, tuple(chunk_rows)


_W1T_ROWS, _SECTION_AT, _CHUNK_ROWS = _projection_layout()
IN_COLS = len(_W1T_ROWS)

VMEM_LIMIT = 56 * 1024 * 1024

INPROJ_TM = 512
TAIL_TM = 512
ATT_TK = 256
ATT_TQ = 2 * ATT_TK
ATT_DK = 128
ATT_DV = HEAD_DIM + 16
AUG_ROWS = 8
LOG2E = 1.4426950408889634
FLASH_PAIRS = 3

_NT = (((1,), (1,)), ((), ()))


def _alibi_slopes():
    n = 3 * N_HEADS
    idx = np.arange(1, n + 1, dtype=np.float32).reshape(N_HEADS, 3)
    s = np.exp2(-8.0 * idx / n).astype(np.float32)
    return s[:, 0], s[:, 1], s[:, 2]


SLOPE_A, SLOPE_C, SLOPE_D = _alibi_slopes()


def _params(n_axes):
    return pltpu.CompilerParams(dimension_semantics=("arbitrary",) * n_axes,
                                vmem_limit_bytes=VMEM_LIMIT)


def _head_norm_t(sec, g_col, n_heads, width):
    outs = []
    for h in range(n_heads):
        s = sec[h * width:(h + 1) * width, :]
        ms = jnp.sum(s * s, axis=0, keepdims=True) * (1.0 / width)
        outs.append(s * lax.rsqrt(ms + EPS) * g_col)
    return outs


def _inproj_kernel(x_ref, gx_ref, w1t_ref, gaq_ref, gak_ref, gql_ref, gkvl_ref, wuqt_ref, wukvt_ref,
                   gbq_ref, gbk_ref, gcq_ref, gck_ref, gdq_ref, gdk_ref, cos_ref, sin_ref, kpos_ref, aslope_ref,
                   aqt_ref, aq32t_ref, ak_ref, akm_ref, avt_ref,
                   bqt_ref, bk_ref, bvt_ref,
                   cq_ref, ck_ref, cv_ref, dq_ref, dk_ref, dv_ref,
                   *h_scrs):
    tm = x_ref.shape[1]
    x = x_ref[0]
    ms = jnp.mean(x * x, axis=-1, keepdims=True)
    xn = (x * lax.rsqrt(ms + EPS) * gx_ref[...]).astype(BF16)
    row0 = 0
    for h_scr, n_rows in zip(h_scrs, _CHUNK_ROWS):
        h_scr[...] = lax.dot_general(w1t_ref[row0:row0 + n_rows, :], xn, _NT, preferred_element_type=F32)
        row0 += n_rows

    def sec(name, lo=0, hi=None):
        chunk, off, width = _SECTION_AT[name]
        return h_scrs[chunk][off + lo:off + (width if hi is None else hi), :]

    scale = HEAD_DIM ** -0.5

    ones_rows = jnp.ones((ATT_DV - HEAD_DIM, ATT_TK), F32)

    def store_vt(ref, h, vh):
        for c in range(tm // ATT_TK):
            ref[0, h, c] = jnp.concatenate([vh[:, c * ATT_TK:(c + 1) * ATT_TK], ones_rows], axis=0).astype(BF16)

    pad_rows = jnp.zeros((ATT_DK - HEAD_DIM - AUG_ROWS, tm), F32)
    qa = _head_norm_t(sec("a_q"), gaq_ref[...], N_HEADS, HEAD_DIM)
    ka = _head_norm_t(sec("a_k"), gak_ref[...], N_HEADS, HEAD_DIM)
    ka_nat = jnp.concatenate(ka, axis=0).T
    for c in range(tm // MOBA_BLOCK):
        akm_ref[0, c] = jnp.sum(ka_nat[c * MOBA_BLOCK:(c + 1) * MOBA_BLOCK, :], axis=0,
                                keepdims=True) * (1.0 / MOBA_BLOCK)
    kpos = kpos_ref[...]
    for h in range(N_HEADS):
        aq32t_ref[0, h] = qa[h]
        slope_rows = jnp.broadcast_to(aslope_ref[h], (AUG_ROWS, tm))
        aqt_ref[0, h] = jnp.concatenate([qa[h] * (scale * LOG2E), slope_rows, pad_rows], axis=0).astype(BF16)
        ak_ref[0, h] = jnp.concatenate([ka[h], kpos, pad_rows], axis=0).T.astype(BF16)
        store_vt(avt_ref, h, sec("a_v", h * HEAD_DIM, (h + 1) * HEAD_DIM))

    cos = cos_ref[...]
    sin = sin_ref[...]
    half = MLA_ROPE // 2

    def rope_pad(t, sc):
        x1 = t[MLA_NOPE:MLA_NOPE + half, :]
        x2 = t[MLA_NOPE + half:MLA_QK, :]
        return jnp.concatenate([t[:MLA_NOPE, :] * sc, (x1 * cos - x2 * sin) * sc, (x1 * sin + x2 * cos) * sc,
                                jnp.zeros((ATT_DK - MLA_QK, tm), F32)], axis=0)

    ql = sec("b_ql")
    ql = ql * lax.rsqrt(jnp.sum(ql * ql, axis=0, keepdims=True) * (1.0 / MLA_Q_RANK) + EPS) * gql_ref[...]
    qb = jnp.dot(wuqt_ref[...], ql.astype(BF16), preferred_element_type=F32)
    qb = _head_norm_t(qb, gbq_ref[...], N_HEADS, MLA_QK)
    kvl = sec("b_kvl")
    kvl = kvl * lax.rsqrt(jnp.sum(kvl * kvl, axis=0, keepdims=True) * (1.0 / MLA_KV_RANK) + EPS) * gkvl_ref[...]
    kvb = jnp.dot(wukvt_ref[...], kvl.astype(BF16), preferred_element_type=F32)
    kr = sec("b_kr")
    gbk = gbk_ref[...]
    for h in range(N_HEADS):
        bqt_ref[0, h] = rope_pad(qb[h], MLA_QK ** -0.5 * LOG2E).astype(BF16)
        kh = jnp.concatenate([kvb[h * 2 * HEAD_DIM:h * 2 * HEAD_DIM + MLA_NOPE, :], kr], axis=0)
        kh = kh * lax.rsqrt(jnp.sum(kh * kh, axis=0, keepdims=True) * (1.0 / MLA_QK) + EPS) * gbk
        bk_ref[0, h] = rope_pad(kh, 1.0).T.astype(BF16)
        store_vt(bvt_ref, h, kvb[h * 2 * HEAD_DIM + MLA_NOPE:(h + 1) * 2 * HEAD_DIM, :])

    qc = jnp.concatenate(_head_norm_t(sec("c_q"), gcq_ref[...], N_HEADS, HEAD_DIM), axis=0)
    kc = jnp.concatenate(_head_norm_t(sec("c_k"), gck_ref[...], N_HEADS, HEAD_DIM), axis=0)
    for ref, val in ((cq_ref, qc * (scale * LOG2E)), (ck_ref, kc), (cv_ref, sec("c_v"))):
        for c in range(N_HALF):
            ref[0, c] = val[c * LANES:(c + 1) * LANES, :].T

    qd = _head_norm_t(sec("d_q"), gdq_ref[...], N_HEADS, HEAD_DIM)
    qd = jnp.concatenate([qd[h] for h in SWA_HEAD_ORDER], axis=0)
    dq_ref[0] = (qd * (scale * LOG2E)).T.astype(BF16)
    kd = jnp.concatenate(_head_norm_t(sec("d_k"), gdk_ref[...], SWA_KV_HEADS, HEAD_DIM),
                         axis=0)
    dk_ref[0] = kd.T.astype(BF16)
    dv_ref[0] = sec("d_v").T.astype(BF16)


def _inproj(x, gx, w1t, gaq, gak, gql, gkvl, wuqt, wukvt, gbq, gbk, gcq, gck, gdq, gdk, cos_t, sin_t, kpos_t,
            aslope):
    B, S, _ = x.shape
    tm = INPROJ_TM
    nb = S // ATT_TK
    cpt = tm // ATT_TK
    H = N_HEADS
    full = lambda a: pl.BlockSpec(a.shape, lambda b, t: (0,) * a.ndim)
    in_specs = [pl.BlockSpec((1, tm, D_MODEL), lambda b, t: (b, t, 0)), full(gx), full(w1t), full(gaq), full(gak),
                full(gql), full(gkvl), full(wuqt), full(wukvt), full(gbq), full(gbk), full(gcq), full(gck),
                full(gdq), full(gdk),
                pl.BlockSpec((MLA_ROPE // 2, tm), lambda b, t: (0, t)),
                pl.BlockSpec((MLA_ROPE // 2, tm), lambda b, t: (0, t)),
                pl.BlockSpec((AUG_ROWS, tm), lambda b, t: (0, t)), full(aslope)]
    head_t = lambda w: pl.BlockSpec((1, H, w, tm), lambda b, t: (b, 0, 0, t))
    head_n = pl.BlockSpec((1, H, tm, ATT_DK), lambda b, t: (b, 0, t, 0))
    vt_spec = pl.BlockSpec((1, H, cpt, ATT_DV, ATT_TK), lambda b, t: (b, 0, t, 0, 0))
    nat = lambda w: pl.BlockSpec((1, tm, w), lambda b, t: (b, t, 0))
    halves = pl.BlockSpec((1, N_HALF, tm, LANES), lambda b, t: (b, 0, t, 0))
    out_shape = [
        jax.ShapeDtypeStruct((B, H, ATT_DK, S), BF16),
        jax.ShapeDtypeStruct((B, H, HEAD_DIM, S), F32),
        jax.ShapeDtypeStruct((B, H, S, ATT_DK), BF16),
        jax.ShapeDtypeStruct((B, nb, 1, GROUP), F32),
        jax.ShapeDtypeStruct((B, H, nb, ATT_DV, ATT_TK), BF16),
        jax.ShapeDtypeStruct((B, H, ATT_DK, S), BF16),
        jax.ShapeDtypeStruct((B, H, S, ATT_DK), BF16),
        jax.ShapeDtypeStruct((B, H, nb, ATT_DV, ATT_TK), BF16),
        jax.ShapeDtypeStruct((B, N_HALF, S, LANES), F32),
        jax.ShapeDtypeStruct((B, N_HALF, S, LANES), F32),
        jax.ShapeDtypeStruct((B, N_HALF, S, LANES), F32),
        jax.ShapeDtypeStruct((B, S, GROUP), BF16),
        jax.ShapeDtypeStruct((B, S, SWA_KV_WIDTH), BF16),
        jax.ShapeDtypeStruct((B, S, SWA_KV_WIDTH), BF16),
    ]
    out_specs = [head_t(ATT_DK), head_t(HEAD_DIM), head_n,
                 pl.BlockSpec((1, tm // MOBA_BLOCK, 1, GROUP), lambda b, t: (b, t, 0, 0)), vt_spec,
                 head_t(ATT_DK), head_n, vt_spec,
                 halves, halves, halves, nat(GROUP), nat(SWA_KV_WIDTH), nat(SWA_KV_WIDTH)]
    return pl.pallas_call(
        _inproj_kernel,
        grid=(B, S // tm),
        in_specs=in_specs,
        out_specs=out_specs,
        out_shape=out_shape,
        scratch_shapes=[pltpu.VMEM((n_rows, tm), F32) for n_rows in _CHUNK_ROWS],
        compiler_params=_params(2),
        name="inproj",
    )(x, gx, w1t, gaq, gak, gql, gkvl, wuqt, wukvt, gbq, gbk, gcq, gck, gdq, gdk, cos_t, sin_t, kpos_t, aslope)


def _flash_heads(qt_ref, k_ref, vt_ref, o_ref, s_bufs, p_bufs, acc_scr, rowb_scr, i):
    tq = o_ref.shape[3]
    n_rowb = 0 if rowb_scr is None else rowb_scr.shape[0]
    n_tail = tq // ATT_TK
    assert n_tail == 2
    n_past = i * n_tail
    heads = range(acc_scr.shape[0])

    def block_of(pos):
        return jnp.where(pos < n_tail, n_past + pos, pos - n_tail)

    def head_scores(h, slot, blk):
        kb = k_ref[0, h, pl.ds(pl.multiple_of(blk * ATT_TK, ATT_TK), ATT_TK), :]
        s_bufs[slot][h] = jnp.dot(kb, qt_ref[0, h], preferred_element_type=F32)

    def stage_scores(slot, blk):
        for h in heads:
            head_scores(h, slot, blk)

    def stage_softmax(slot, blk, ms, causal=None):
        new_ms, alphas = [], []
        for h in heads:
            m_parts, a_parts = [], []
            rv_row = rowb_scr[h, pl.ds(blk, 1), :] if h < n_rowb else None
            for c in range(tq // LANES):
                cols = slice(c * LANES, (c + 1) * LANES)
                st = s_bufs[slot][h, :, cols]
                if causal is not None:
                    st = jnp.where(causal[:, cols], st, 2 * NEG)
                cm = jnp.max(st, axis=0, keepdims=True)
                m_old = ms[h][:, cols]
                if rv_row is None:
                    m_new = jnp.maximum(m_old, cm)
                    shift = m_new
                else:
                    rv = rv_row[:, cols]
                    m_new = jnp.maximum(m_old, cm + rv)
                    shift = m_new - rv
                a_parts.append(jnp.exp2(m_old - m_new))
                m_parts.append(m_new)
                p_bufs[slot][h, :, cols] = jnp.exp2(st - shift).astype(BF16)
            new_ms.append(jnp.concatenate(m_parts, axis=1))
            alphas.append(jnp.concatenate(a_parts, axis=1))
        return tuple(new_ms), tuple(alphas)

    def head_values(h, slot, blk, alphas):
        acc_scr[h] = alphas[h] * acc_scr[h] + jnp.dot(vt_ref[0, h, blk], p_bufs[slot][h],
                                                      preferred_element_type=F32)

    def stage_values(slot, blk, alphas):
        for h in heads:
            head_values(h, slot, blk, alphas)

    krow = lax.broadcasted_iota(jnp.int32, (ATT_TK, tq), 0)
    ti = lax.broadcasted_iota(jnp.int32, (ATT_TK, tq), 1)
    for h in heads:
        acc_scr[h] = jnp.zeros((ATT_DV, tq), F32)
    stage_scores(0, n_past)
    stage_scores(1, n_past + 1)
    ms = tuple(jnp.full((1, tq), NEG, F32) for _ in heads)
    ms, alphas = stage_softmax(0, n_past, ms, causal=ti >= krow)
    stage_values(0, n_past, alphas)
    ms, alphas = stage_softmax(1, n_past + 1, ms, causal=ti >= krow + ATT_TK)
    stage_scores(0, 0)

    def step(pos, slot, ms, alphas):
        ms, new_alphas = stage_softmax(1 - slot, pos + 1 - n_tail, ms)
        blk = block_of(pos)
        for h in heads:
            head_values(h, slot, blk, alphas)
            head_scores(h, slot, pos)
        return ms, new_alphas

    def pair(first, carry):
        ms, alphas = step(first, 1, *carry)
        return step(first + 1, 0, ms, alphas)

    def trip(t, carry):
        for u in range(FLASH_PAIRS):
            carry = pair(2 * (FLASH_PAIRS * t + u) + 1, carry)
        return carry

    carry = lax.fori_loop(0, i // FLASH_PAIRS, trip, (ms, alphas))
    ms, alphas = lax.fori_loop(i - i % FLASH_PAIRS, i, lambda u, c: pair(2 * u + 1, c), carry)
    last = n_past + 1
    stage_values(1, block_of(last), alphas)
    for h in heads:
        acc = acc_scr[h]
        o_ref[0, h] = acc[:HEAD_DIM, :] / acc[HEAD_DIM:HEAD_DIM + 1, :]


def _moba_kernel(qt_ref, q32t_ref, k_ref, vt_ref, kmean_ref, bqt_ref, bk_ref, bvt_ref, o_ref, ob_ref,
                 rowb_scr, s0, s1, p0, p1, acc_scr):
    i = pl.program_id(1)
    nb = kmean_ref.shape[2]
    tq = qt_ref.shape[3]
    blk = lax.broadcasted_iota(jnp.int32, (nb, tq), 0)
    col = lax.broadcasted_iota(jnp.int32, (nb, tq), 1)
    qblk = i * (tq // MOBA_BLOCK) + col // MOBA_BLOCK
    past = blk < qblk
    dist0 = (i * tq + col - blk * MOBA_BLOCK).astype(F32)
    for h in range(N_HEADS):
        gate = jnp.dot(kmean_ref[0, h], q32t_ref[0, h], preferred_element_type=F32,
                       precision=lax.Precision.HIGHEST)
        gate = jnp.where(past, gate, NEG)
        sel = blk == qblk
        for _ in range(MOBA_TOPK):
            best = jnp.max(gate, axis=0, keepdims=True)
            first = jnp.min(jnp.where(gate == best, blk, nb), axis=0, keepdims=True)
            pick = blk == first
            sel = jnp.logical_or(sel, jnp.logical_and(pick, past))
            gate = jnp.where(pick, -jnp.inf, gate)
        rowb_scr[h] = jnp.where(sel, (-float(SLOPE_A[h]) * LOG2E) * dist0, 2 * NEG)
    _flash_heads(_HeadGroups(qt_ref, bqt_ref), _HeadGroups(k_ref, bk_ref), _HeadGroups(vt_ref, bvt_ref),
                 _HeadGroups(o_ref, ob_ref), (s0, s1), (p0, p1), acc_scr, rowb_scr, i)


class _HeadGroups:
    def __init__(self, *refs):
        self.refs = refs
        self.shape = refs[0].shape

    def _at(self, idx):
        return self.refs[idx[1] // N_HEADS], (idx[0], idx[1] % N_HEADS) + tuple(idx[2:])

    def __getitem__(self, idx):
        ref, at = self._at(idx)
        return ref[at]

    def __setitem__(self, idx, val):
        ref, at = self._at(idx)
        ref[at] = val


def _dense_attention(aqt, aq32t, ak, avt, kmean_h, bqt, bk, bvt):
    B, H, _, S = aqt.shape
    tq = ATT_TQ
    nb = S // ATT_TK
    q_spec = pl.BlockSpec((1, H, ATT_DK, tq), lambda b, i: (b, 0, 0, i))
    k_spec = pl.BlockSpec((1, H, S, ATT_DK), lambda b, i: (b, 0, 0, 0), pipeline_mode=pl.Buffered(1))
    vt_spec = pl.BlockSpec((1, H, nb, ATT_DV, ATT_TK), lambda b, i: (b, 0, 0, 0, 0), pipeline_mode=pl.Buffered(1))
    o_spec = pl.BlockSpec((1, H, HEAD_DIM, tq), lambda b, i: (b, 0, 0, i))
    o_shape = jax.ShapeDtypeStruct((B, H, HEAD_DIM, S), F32)
    heads = 2 * H
    s_buf, p_buf = pltpu.VMEM((heads, ATT_TK, tq), F32), pltpu.VMEM((heads, ATT_TK, tq), BF16)
    return pl.pallas_call(
        _moba_kernel,
        grid=(B, S // tq),
        in_specs=[q_spec, pl.BlockSpec((1, H, HEAD_DIM, tq), lambda b, i: (b, 0, 0, i)), k_spec, vt_spec,
                  pl.BlockSpec((1, H, nb, HEAD_DIM), lambda b, i: (b, 0, 0, 0)), q_spec, k_spec, vt_spec],
        out_specs=[o_spec, o_spec],
        out_shape=[o_shape, o_shape],
        scratch_shapes=[pltpu.VMEM((H, nb, tq), F32), s_buf, s_buf, p_buf, p_buf,
                        pltpu.VMEM((heads, ATT_DV, tq), F32)],
        compiler_params=_params(2),
        name="dense_attention",
    )(aqt, aq32t, ak, avt, kmean_h, bqt, bk, bvt)


DIL_SPAN = max(d for _, d in DILATED_BRANCHES) * Q_BLOCK
DIL_UNITS = DIL_SPAN // Q_BLOCK
DIL_GROUP = (4, 2, 2)


def _dilated_bias():
    qi = np.arange(Q_BLOCK)[:, None]
    kidx = np.arange(2 * Q_BLOCK)[None, :]
    rel = qi + Q_BLOCK - kidx
    out = np.empty((2, len(DILATED_BRANCHES), N_HEADS, Q_BLOCK, 2 * Q_BLOCK), np.float32)
    for bi, (window, d) in enumerate(DILATED_BRANCHES):
        valid = (rel >= 0) & (rel <= window // d)
        for h in range(N_HEADS):
            bias = -SLOPE_C[h] * np.float32(LOG2E) * (d * rel).astype(np.float32)
            out[0, bi, h] = np.where(valid, bias, NEG)
            out[1, bi, h] = np.where(valid & (kidx >= Q_BLOCK), bias, NEG)
    return out


def _rows_load(ref, lead, start, size, stride):
    return jnp.concatenate([ref[lead + (c, pl.ds(start, size, stride=stride), slice(None))]
                            for c in range(N_HALF)], axis=-1)


def _rows_store(ref, lead, start, size, stride, val):
    for c in range(N_HALF):
        ref[lead + (c, pl.ds(start, size, stride=stride), slice(None))] = val[:, c * LANES:(c + 1) * LANES]


def _dilated_kernel(q_ref, kp_ref, kc_ref, vp_ref, vc_ref, bias_ref, o_ref, kbuf, vbuf, m_scr, den_scr, num_scr):
    span = pl.program_id(1)
    kbuf[:, 0:DIL_SPAN, :] = kp_ref[0]
    kbuf[:, DIL_SPAN:, :] = kc_ref[0]
    vbuf[:, 0:DIL_SPAN, :] = vp_ref[0]
    vbuf[:, DIL_SPAN:, :] = vc_ref[0]
    lane_head = lax.broadcasted_iota(jnp.int32, (1, GROUP), 1) // HEAD_DIM
    hmask = [lane_head == h for h in range(N_HEADS)]
    hmask_f = [m.astype(F32) for m in hmask]
    first_span = jnp.where(span == 0, 1, 0)

    def per_head(cols):
        out = cols[N_HEADS - 1]
        for h in range(N_HEADS - 2, -1, -1):
            out = jnp.where(hmask[h], cols[h], out)
        return out

    order = sorted(range(len(DILATED_BRANCHES)), key=lambda b: -DILATED_BRANCHES[b][1])
    for bi in order:
        d, n_group = DILATED_BRANCHES[bi][1], DIL_GROUP[bi]
        first, last = bi == order[0], bi == order[-1]

        def group(g, _, bi=bi, d=d, n_group=n_group, first=first, last=last):
            fronts = []
            for uu in range(n_group):
                u = g * n_group + uu
                r, n = u % d, u // d
                qstart = n * (Q_BLOCK * d) + r
                kstart = DIL_SPAN + qstart - Q_BLOCK * d
                q = _rows_load(q_ref, (0,), qstart, Q_BLOCK, d)
                k2 = _rows_load(kbuf, (), kstart, 2 * Q_BLOCK, d).astype(BF16)
                v2 = _rows_load(vbuf, (), kstart, 2 * Q_BLOCK, d).astype(BF16)
                q4 = jnp.concatenate([(q * hmask_f[h]).astype(BF16) for h in range(N_HEADS)], axis=0)
                s4 = lax.dot_general(q4, k2, _NT, preferred_element_type=F32)
                variant = jnp.where(n == 0, first_span, 0)
                fronts.append((qstart, s4, v2, variant))
            for qstart, s4, v2, variant in fronts:
                es, ms, ls = [], [], []
                for h in range(N_HEADS):
                    s = s4[h * Q_BLOCK:(h + 1) * Q_BLOCK, :] + bias_ref[variant, bi, h]
                    m = jnp.max(s, axis=-1, keepdims=True)
                    e = jnp.exp2(s - m)
                    ls.append(jnp.sum(e, axis=-1, keepdims=True))
                    ms.append(m)
                    es.append(e.astype(BF16))
                o4 = jnp.dot(jnp.concatenate(es, axis=0), v2, preferred_element_type=F32)
                o = o4[(N_HEADS - 1) * Q_BLOCK:, :]
                for h in range(N_HEADS - 2, -1, -1):
                    o = jnp.where(hmask[h], o4[h * Q_BLOCK:(h + 1) * Q_BLOCK, :], o)
                m_b, l_b = per_head(ms), per_head(ls)
                at = ((), qstart, Q_BLOCK, d)
                if first:
                    _rows_store(m_scr, *at, m_b)
                    _rows_store(num_scr, *at, o)
                    _rows_store(den_scr, *at, l_b)
                else:
                    m_old = _rows_load(m_scr, *at)
                    m_new = jnp.maximum(m_old, m_b)
                    a, b = jnp.exp2(m_old - m_new), jnp.exp2(m_b - m_new)
                    num = a * _rows_load(num_scr, *at) + b * o
                    den = a * _rows_load(den_scr, *at) + b * l_b
                    if last:
                        _rows_store(o_ref, (0,), qstart, Q_BLOCK, d, num / den)
                    else:
                        _rows_store(m_scr, *at, m_new)
                        _rows_store(num_scr, *at, num)
                        _rows_store(den_scr, *at, den)
            return 0

        lax.fori_loop(0, DIL_UNITS // n_group, group, 0)


def _dilated(cq, ck, cv):
    B, _, S, _ = cq.shape
    cur = pl.BlockSpec((1, N_HALF, DIL_SPAN, LANES), lambda b, s: (b, 0, s, 0))
    prev = pl.BlockSpec((1, N_HALF, DIL_SPAN, LANES), lambda b, s: (b, 0, jnp.maximum(s - 1, 0), 0))
    bias = jnp.asarray(_dilated_bias())
    return pl.pallas_call(
        _dilated_kernel,
        grid=(B, S // DIL_SPAN),
        in_specs=[cur, prev, cur, prev, cur, pl.BlockSpec(bias.shape, lambda b, s: (0,) * bias.ndim)],
        out_specs=cur,
        out_shape=jax.ShapeDtypeStruct((B, N_HALF, S, LANES), F32),
        scratch_shapes=[pltpu.VMEM((N_HALF, 2 * DIL_SPAN, LANES), F32),
                        pltpu.VMEM((N_HALF, 2 * DIL_SPAN, LANES), F32),
                        pltpu.VMEM((N_HALF, DIL_SPAN, LANES), F32), pltpu.VMEM((N_HALF, DIL_SPAN, LANES), F32),
                        pltpu.VMEM((N_HALF, DIL_SPAN, LANES), F32)],
        compiler_params=_params(2),
        name="dilated",
    )(cq, ck, ck, cv, cv, bias)


SWA_SPAN = 1024
SWA_GROUP = 4
SWA_HEAD_ORDER = (0, 2, 1, 3)


def _swa_bias():
    qi = np.arange(Q_BLOCK)[:, None]
    kidx = np.arange(2 * Q_BLOCK)[None, :]
    rel = qi + Q_BLOCK - kidx
    valid = (rel >= 0) & (rel < SWA_WINDOW)
    out = np.empty((2, N_HEADS, Q_BLOCK, 2 * Q_BLOCK), np.float32)
    for h in range(N_HEADS):
        bias = -SLOPE_D[h] * np.float32(LOG2E) * rel.astype(np.float32)
        out[0, h] = np.where(valid, bias, NEG)
        out[1, h] = np.where(valid & (kidx >= Q_BLOCK), bias, NEG)
    return out


def _swa_kernel(sink_ref, q_ref, k_ref, v_ref, bias_ref, o_ref):
    span = pl.program_id(1)
    units = SWA_SPAN // Q_BLOCK
    half = lax.broadcasted_iota(jnp.int32, (1, LANES), 1) // HEAD_DIM
    lo = half == 0
    kv_mask = [jnp.where(half == g, 1.0, 0.0).astype(BF16) for g in range(SWA_KV_HEADS)]

    def group(g, _):
        fronts = []
        for uu in range(SWA_GROUP):
            u = g * SWA_GROUP + uu
            n = span * units + u
            lo_start = pl.multiple_of(jnp.maximum(n - 1, 0) * Q_BLOCK, Q_BLOCK)
            hi_start = pl.multiple_of(n * Q_BLOCK, Q_BLOCK)
            qstart = pl.multiple_of(u * Q_BLOCK, Q_BLOCK)
            q = q_ref[0, pl.ds(qstart, Q_BLOCK), :]
            k2 = jnp.concatenate([k_ref[0, pl.ds(lo_start, Q_BLOCK), :], k_ref[0, pl.ds(hi_start, Q_BLOCK), :]],
                                 axis=0)
            v2 = jnp.concatenate([v_ref[0, pl.ds(lo_start, Q_BLOCK), :], v_ref[0, pl.ds(hi_start, Q_BLOCK), :]],
                                 axis=0)
            q4 = jnp.concatenate([q[:, (h % 2) * LANES:(h % 2 + 1) * LANES] * kv_mask[h // 2]
                                  for h in range(N_HEADS)], axis=0)
            s4 = lax.dot_general(q4, k2, _NT, preferred_element_type=F32)
            fronts.append((qstart, s4, v2, jnp.where(n == 0, 1, 0)))
        for qstart, s4, v2, variant in fronts:
            es, ls = [], []
            for h in range(N_HEADS):
                s = s4[h * Q_BLOCK:(h + 1) * Q_BLOCK, :] + bias_ref[variant, h]
                sink = sink_ref[h] * LOG2E
                m = jnp.maximum(jnp.max(s, axis=-1, keepdims=True), sink)
                e = jnp.exp2(s - m)
                ls.append(jnp.sum(e, axis=-1, keepdims=True) + jnp.exp2(sink - m))
                es.append(e.astype(BF16))
            o4 = jnp.dot(jnp.concatenate(es, axis=0), v2, preferred_element_type=F32)
            tiles = []
            for t in range(N_HALF):
                a, b = t, t + 2
                tiles.append(jnp.where(lo, o4[a * Q_BLOCK:(a + 1) * Q_BLOCK, :] / ls[a],
                                       o4[b * Q_BLOCK:(b + 1) * Q_BLOCK, :] / ls[b]))
            o_ref[0, pl.ds(qstart, Q_BLOCK), :] = jnp.concatenate(tiles, axis=-1)
        return 0

    lax.fori_loop(0, units // SWA_GROUP, group, 0)


def _swa(dq, dk, dv, sinks):
    B, S, _ = dq.shape
    bias = jnp.asarray(_swa_bias())
    return pl.pallas_call(
        _swa_kernel,
        grid=(B, S // SWA_SPAN),
        in_specs=[pl.BlockSpec(memory_space=pltpu.SMEM),
                  pl.BlockSpec((1, SWA_SPAN, GROUP), lambda b, i: (b, i, 0)),
                  pl.BlockSpec((1, S, SWA_KV_WIDTH), lambda b, i: (b, 0, 0)),
                  pl.BlockSpec((1, S, SWA_KV_WIDTH), lambda b, i: (b, 0, 0)),
                  pl.BlockSpec(bias.shape, lambda b, i: (0,) * bias.ndim)],
        out_specs=pl.BlockSpec((1, SWA_SPAN, GROUP), lambda b, i: (b, i, 0)),
        out_shape=jax.ShapeDtypeStruct((B, S, GROUP), F32),
        compiler_params=_params(2),
        name="swa",
    )(sinks, dq, dk, dv, bias)


def _row_norm(y, g_row):
    return y * lax.rsqrt(jnp.mean(y * y, axis=-1, keepdims=True) + EPS) * g_row


def _tail_kernel(x_ref, oat_ref, obt_ref, oc_ref, od_ref, gg_ref, wo_ref, gm_ref, wup_ref, wdn_ref, out_ref):
    gg = gg_ref[...]

    def col_norm_t(yt, g_row):
        y = (yt * lax.rsqrt(jnp.mean(yt * yt, axis=0, keepdims=True) + EPS)).T
        return y * g_row

    ga = col_norm_t(oat_ref[0], gg[0:1, :])
    gb = col_norm_t(obt_ref[0], gg[1:2, :])
    gc = _row_norm(jnp.concatenate([oc_ref[0, c] for c in range(N_HALF)], axis=-1), gg[2:3, :])
    gd = _row_norm(od_ref[0], gg[3:4, :])
    mixed = jnp.concatenate([ga, gb, gc, gd], axis=-1).astype(BF16)
    x1 = x_ref[0] + jnp.dot(mixed, wo_ref[...], preferred_element_type=F32)
    xn = _row_norm(x1, gm_ref[...]).astype(BF16)
    u = jnp.maximum(jnp.dot(xn, wup_ref[...], preferred_element_type=F32), 0.0)
    out_ref[0] = x1 + jnp.dot((u * u).astype(BF16), wdn_ref[...], preferred_element_type=F32)


def _tail(x, oat, obt, oc, od, gg, wo, gm, wup, wdn):
    B, S, _ = x.shape
    tm = TAIL_TM
    nat = lambda w: pl.BlockSpec((1, tm, w), lambda b, t: (b, t, 0))
    ft = pl.BlockSpec((1, GROUP, tm), lambda b, t: (b, 0, t))
    const = lambda a: pl.BlockSpec(a.shape, lambda b, t: (0,) * a.ndim, pipeline_mode=pl.Buffered(1))
    return pl.pallas_call(
        _tail_kernel,
        grid=(B, S // tm),
        in_specs=[nat(D_MODEL), ft, ft, pl.BlockSpec((1, N_HALF, tm, LANES), lambda b, t: (b, 0, t, 0)),
                  nat(GROUP), const(gg), const(wo), const(gm),
                  const(wup), const(wdn)],
        out_specs=nat(D_MODEL),
        out_shape=jax.ShapeDtypeStruct((B, S, D_MODEL), F32),
        compiler_params=_params(2),
        name="tail",
    )(x, oat, obt, oc, od, gg, wo, gm, wup, wdn)


def _rope_tables_t(S):
    inv = 1.0 / (ROPE_THETA ** (jnp.arange(0, MLA_ROPE, 2, dtype=F32) / MLA_ROPE))
    ang = inv[:, None] * jnp.arange(S, dtype=F32)[None, :]
    return jnp.cos(ang), jnp.sin(ang)


def _moba_aug_tables(S):
    pos = np.arange(S, dtype=np.float32) % MOBA_BLOCK
    kpos = np.zeros((AUG_ROWS, S), np.float32)
    kpos[:3] = pos
    aslope = np.zeros((N_HEADS, AUG_ROWS, 1), np.float32)
    rest = (SLOPE_A * np.float32(LOG2E)).astype(np.float32)
    for r in range(3):
        piece = rest.astype(BF16).astype(np.float32)
        aslope[:, r, 0] = piece
        rest = rest - piece
    return jnp.asarray(kpos), jnp.asarray(aslope)


def _layer(x, cos_t, sin_t, kpos_t, aslope, attn_norm_g, w_in, moba_q_g, moba_k_g, mla_qlat_g, mla_kvlat_g, mla_w_uq, mla_w_ukv,
           mla_q_g, mla_k_g, dil_q_g, dil_k_g, swa_q_g, swa_k_g, swa_sinks, group_out_g, w_o, mlp_norm_g,
           w_up, w_down):
    B, S, _ = x.shape
    col = lambda g: g.reshape(-1, 1)
    (aqt, aq32t, ak, akm, avt, bqt, bk, bvt, cq, ck, cv, dq, dk, dv) = _inproj(
        x, attn_norm_g.reshape(1, -1), w_in[:, _W1T_ROWS].T.astype(BF16), col(moba_q_g), col(moba_k_g), col(mla_qlat_g),
        col(mla_kvlat_g), mla_w_uq.T.astype(BF16), mla_w_ukv.T.astype(BF16), col(mla_q_g), col(mla_k_g),
        col(dil_q_g), col(dil_k_g), col(swa_q_g), col(swa_k_g), cos_t, sin_t, kpos_t, aslope)
    nb = S // MOBA_BLOCK
    kmean_h = akm.reshape(B, nb, N_HEADS, HEAD_DIM).transpose(0, 2, 1, 3)
    oat, obt = _dense_attention(aqt, aq32t, ak, avt, kmean_h, bqt, bk, bvt)
    oat, obt = oat.reshape(B, GROUP, S), obt.reshape(B, GROUP, S)
    oc = _dilated(cq, ck, cv)
    od = _swa(dq, dk, dv, swa_sinks)
    perm = np.concatenate([np.arange(HEAD_DIM) + HEAD_DIM * h for h in SWA_HEAD_ORDER])
    gg = group_out_g.at[3].set(group_out_g[3][perm])
    wo = jnp.concatenate([w_o[:3 * GROUP], w_o[3 * GROUP + perm]], axis=0)
    return _tail(x, oat, obt, oc, od, gg, wo.astype(BF16), mlp_norm_g.reshape(1, -1),
                 w_up.astype(BF16), w_down.astype(BF16))


def kernel(x, attn_norm_g, w_in, moba_q_g, moba_k_g, mla_qlat_g, mla_kvlat_g, mla_w_uq, mla_w_ukv, mla_q_g,
           mla_k_g, dil_q_g, dil_k_g, swa_q_g, swa_k_g, swa_sinks, group_out_g, w_o, mlp_norm_g, w_up, w_down):
    S = x.shape[1]
    assert S % max(d * Q_BLOCK for _, d in DILATED_BRANCHES) == 0 and S % INPROJ_TM == 0
    cos_t, sin_t = _rope_tables_t(S)
    kpos_t, aslope = _moba_aug_tables(S)
    params = (attn_norm_g, w_in, moba_q_g, moba_k_g, mla_qlat_g, mla_kvlat_g, mla_w_uq, mla_w_ukv, mla_q_g,
              mla_k_g, dil_q_g, dil_k_g, swa_q_g, swa_k_g, swa_sinks, group_out_g, w_o, mlp_norm_g, w_up, w_down)
    for l in range(attn_norm_g.shape[0]):
        x = _layer(x, cos_t, sin_t, kpos_t, aslope, *[p[l] for p in params])
    return x
```

```python
import numpy as np
import jax
import jax.numpy as jnp
from jax import lax
from jax.experimental import pallas as pl
from jax.experimental.pallas import tpu as pltpu

F32 = jnp.float32
BF16 = jnp.bfloat16

D_MODEL = 1024
HEAD_DIM = 64
N_HEADS = 4
GROUP = N_HEADS * HEAD_DIM
LANES = 128
N_HALF = GROUP // LANES
MOBA_BLOCK = 256
MOBA_TOPK = 3
MLA_Q_RANK = 256
MLA_KV_RANK = 128
MLA_NOPE = 64
MLA_ROPE = 32
MLA_QK = MLA_NOPE + MLA_ROPE
ROPE_THETA = 10000.0
DILATED_BRANCHES = ((128, 1), (512, 4), (2048, 16))
Q_BLOCK = 128
SWA_WINDOW = 128
SWA_KV_HEADS = 2
SWA_KV_WIDTH = SWA_KV_HEADS * HEAD_DIM
D_FF = 4 * D_MODEL
EPS = 1e-6
NEG = -1e30

_SECTIONS = (("a_q", GROUP), ("a_k", GROUP), ("a_v", GROUP), ("b_ql", MLA_Q_RANK), ("b_kvl", MLA_KV_RANK),
             ("b_kr", MLA_ROPE), ("c_q", GROUP), ("c_k", GROUP), ("c_v", GROUP), ("d_q", GROUP),
             ("d_k", SWA_KV_WIDTH), ("d_v", SWA_KV_WIDTH))
_CHUNKS = (("a_q", "a_k", "a_v", "b_ql", "b_kvl", "b_kr"), ("c_q", "c_k", "d_q", "d_k"), ("c_v", "d_v"))


def _projection_layout():
    width = dict(_SECTIONS)
    col, start = 0, {}
    for name, w in _SECTIONS:
        start[name] = col
        col += w
    rows, where, chunk_rows = [], {}, []
    for ci, chunk in enumerate(_CHUNKS):
        off = 0
        for name in chunk:
            where[name] = (ci, off, width[name])
            rows.extend(range(start[name], start[name] + width[name]))
            off += width[name]
        chunk_rows.append(off)
    return np.asarray(rows), where, tuple(chunk_rows)


_W1T_ROWS, _SECTION_AT, _CHUNK_ROWS = _projection_layout()
IN_COLS = len(_W1T_ROWS)

VMEM_LIMIT = 56 * 1024 * 1024

INPROJ_TM = 512
TAIL_TM = 512
ATT_TK = 256
ATT_TQ = 2 * ATT_TK
ATT_DK = 128
ATT_DV = HEAD_DIM + 16
AUG_ROWS = 8
LOG2E = 1.4426950408889634
FLASH_PAIRS = 3

_NT = (((1,), (1,)), ((), ()))


def _alibi_slopes():
    n = 3 * N_HEADS
    idx = np.arange(1, n + 1, dtype=np.float32).reshape(N_HEADS, 3)
    s = np.exp2(-8.0 * idx / n).astype(np.float32)
    return s[:, 0], s[:, 1], s[:, 2]


SLOPE_A, SLOPE_C, SLOPE_D = _alibi_slopes()


def _params(n_axes):
    return pltpu.CompilerParams(dimension_semantics=("arbitrary",) * n_axes,
                                vmem_limit_bytes=VMEM_LIMIT)


def _head_norm_t(sec, g_col, n_heads, width):
    outs = []
    for h in range(n_heads):
        s = sec[h * width:(h + 1) * width, :]
        ms = jnp.sum(s * s, axis=0, keepdims=True) * (1.0 / width)
        outs.append(s * lax.rsqrt(ms + EPS) * g_col)
    return outs


def _inproj_kernel(x_ref, gx_ref, w1t_ref, gaq_ref, gak_ref, gql_ref, gkvl_ref, wuqt_ref, wukvt_ref,
                   gbq_ref, gbk_ref, gcq_ref, gck_ref, gdq_ref, gdk_ref, cos_ref, sin_ref, kpos_ref, aslope_ref,
                   aqt_ref, aq32t_ref, ak_ref, akm_ref, avt_ref,
                   bqt_ref, bk_ref, bvt_ref,
                   cq_ref, ck_ref, cv_ref, dq_ref, dk_ref, dv_ref,
                   *h_scrs):
    tm = x_ref.shape[1]
    x = x_ref[0]
    ms = jnp.mean(x * x, axis=-1, keepdims=True)
    xn = (x * lax.rsqrt(ms + EPS) * gx_ref[...]).astype(BF16)
    row0 = 0
    for h_scr, n_rows in zip(h_scrs, _CHUNK_ROWS):
        h_scr[...] = lax.dot_general(w1t_ref[row0:row0 + n_rows, :], xn, _NT, preferred_element_type=F32)
        row0 += n_rows

    def sec(name, lo=0, hi=None):
        chunk, off, width = _SECTION_AT[name]
        return h_scrs[chunk][off + lo:off + (width if hi is None else hi), :]

    scale = HEAD_DIM ** -0.5

    ones_rows = jnp.ones((ATT_DV - HEAD_DIM, ATT_TK), F32)

    def store_vt(ref, h, vh):
        for c in range(tm // ATT_TK):
            ref[0, h, c] = jnp.concatenate([vh[:, c * ATT_TK:(c + 1) * ATT_TK], ones_rows], axis=0).astype(BF16)

    pad_rows = jnp.zeros((ATT_DK - HEAD_DIM - AUG_ROWS, tm), F32)
    qa = _head_norm_t(sec("a_q"), gaq_ref[...], N_HEADS, HEAD_DIM)
    ka = _head_norm_t(sec("a_k"), gak_ref[...], N_HEADS, HEAD_DIM)
    ka_nat = jnp.concatenate(ka, axis=0).T
    for c in range(tm // MOBA_BLOCK):
        akm_ref[0, c] = jnp.sum(ka_nat[c * MOBA_BLOCK:(c + 1) * MOBA_BLOCK, :], axis=0,
                                keepdims=True) * (1.0 / MOBA_BLOCK)
    kpos = kpos_ref[...]
    for h in range(N_HEADS):
        aq32t_ref[0, h] = qa[h]
        slope_rows = jnp.broadcast_to(aslope_ref[h], (AUG_ROWS, tm))
        aqt_ref[0, h] = jnp.concatenate([qa[h] * (scale * LOG2E), slope_rows, pad_rows], axis=0).astype(BF16)
        ak_ref[0, h] = jnp.concatenate([ka[h], kpos, pad_rows], axis=0).T.astype(BF16)
        store_vt(avt_ref, h, sec("a_v", h * HEAD_DIM, (h + 1) * HEAD_DIM))

    cos = cos_ref[...]
    sin = sin_ref[...]
    half = MLA_ROPE // 2

    def rope_pad(t, sc):
        x1 = t[MLA_NOPE:MLA_NOPE + half, :]
        x2 = t[MLA_NOPE + half:MLA_QK, :]
        return jnp.concatenate([t[:MLA_NOPE, :] * sc, (x1 * cos - x2 * sin) * sc, (x1 * sin + x2 * cos) * sc,
                                jnp.zeros((ATT_DK - MLA_QK, tm), F32)], axis=0)

    ql = sec("b_ql")
    ql = ql * lax.rsqrt(jnp.sum(ql * ql, axis=0, keepdims=True) * (1.0 / MLA_Q_RANK) + EPS) * gql_ref[...]
    qb = jnp.dot(wuqt_ref[...], ql.astype(BF16), preferred_element_type=F32)
    qb = _head_norm_t(qb, gbq_ref[...], N_HEADS, MLA_QK)
    kvl = sec("b_kvl")
    kvl = kvl * lax.rsqrt(jnp.sum(kvl * kvl, axis=0, keepdims=True) * (1.0 / MLA_KV_RANK) + EPS) * gkvl_ref[...]
    kvb = jnp.dot(wukvt_ref[...], kvl.astype(BF16), preferred_element_type=F32)
    kr = sec("b_kr")
    gbk = gbk_ref[...]
    for h in range(N_HEADS):
        bqt_ref[0, h] = rope_pad(qb[h], MLA_QK ** -0.5 * LOG2E).astype(BF16)
        kh = jnp.concatenate([kvb[h * 2 * HEAD_DIM:h * 2 * HEAD_DIM + MLA_NOPE, :], kr], axis=0)
        kh = kh * lax.rsqrt(jnp.sum(kh * kh, axis=0, keepdims=True) * (1.0 / MLA_QK) + EPS) * gbk
        bk_ref[0, h] = rope_pad(kh, 1.0).T.astype(BF16)
        store_vt(bvt_ref, h, kvb[h * 2 * HEAD_DIM + MLA_NOPE:(h + 1) * 2 * HEAD_DIM, :])

    qc = jnp.concatenate(_head_norm_t(sec("c_q"), gcq_ref[...], N_HEADS, HEAD_DIM), axis=0)
    kc = jnp.concatenate(_head_norm_t(sec("c_k"), gck_ref[...], N_HEADS, HEAD_DIM), axis=0)
    for ref, val in ((cq_ref, qc * (scale * LOG2E)), (ck_ref, kc), (cv_ref, sec("c_v"))):
        for c in range(N_HALF):
            ref[0, c] = val[c * LANES:(c + 1) * LANES, :].T

    qd = _head_norm_t(sec("d_q"), gdq_ref[...], N_HEADS, HEAD_DIM)
    qd = jnp.concatenate([qd[h] for h in SWA_HEAD_ORDER], axis=0)
    dq_ref[0] = (qd * (scale * LOG2E)).T.astype(BF16)
    kd = jnp.concatenate(_head_norm_t(sec("d_k"), gdk_ref[...], SWA_KV_HEADS, HEAD_DIM),
                         axis=0)
    dk_ref[0] = kd.T.astype(BF16)
    dv_ref[0] = sec("d_v").T.astype(BF16)


def _inproj(x, gx, w1t, gaq, gak, gql, gkvl, wuqt, wukvt, gbq, gbk, gcq, gck, gdq, gdk, cos_t, sin_t, kpos_t,
            aslope):
    B, S, _ = x.shape
    tm = INPROJ_TM
    nb = S // ATT_TK
    cpt = tm // ATT_TK
    H = N_HEADS
    full = lambda a: pl.BlockSpec(a.shape, lambda b, t: (0,) * a.ndim)
    in_specs = [pl.BlockSpec((1, tm, D_MODEL), lambda b, t: (b, t, 0)), full(gx), full(w1t), full(gaq), full(gak),
                full(gql), full(gkvl), full(wuqt), full(wukvt), full(gbq), full(gbk), full(gcq), full(gck),
                full(gdq), full(gdk),
                pl.BlockSpec((MLA_ROPE // 2, tm), lambda b, t: (0, t)),
                pl.BlockSpec((MLA_ROPE // 2, tm), lambda b, t: (0, t)),
                pl.BlockSpec((AUG_ROWS, tm), lambda b, t: (0, t)), full(aslope)]
    head_t = lambda w: pl.BlockSpec((1, H, w, tm), lambda b, t: (b, 0, 0, t))
    head_n = pl.BlockSpec((1, H, tm, ATT_DK), lambda b, t: (b, 0, t, 0))
    vt_spec = pl.BlockSpec((1, H, cpt, ATT_DV, ATT_TK), lambda b, t: (b, 0, t, 0, 0))
    nat = lambda w: pl.BlockSpec((1, tm, w), lambda b, t: (b, t, 0))
    halves = pl.BlockSpec((1, N_HALF, tm, LANES), lambda b, t: (b, 0, t, 0))
    out_shape = [
        jax.ShapeDtypeStruct((B, H, ATT_DK, S), BF16),
        jax.ShapeDtypeStruct((B, H, HEAD_DIM, S), F32),
        jax.ShapeDtypeStruct((B, H, S, ATT_DK), BF16),
        jax.ShapeDtypeStruct((B, nb, 1, GROUP), F32),
        jax.ShapeDtypeStruct((B, H, nb, ATT_DV, ATT_TK), BF16),
        jax.ShapeDtypeStruct((B, H, ATT_DK, S), BF16),
        jax.ShapeDtypeStruct((B, H, S, ATT_DK), BF16),
        jax.ShapeDtypeStruct((B, H, nb, ATT_DV, ATT_TK), BF16),
        jax.ShapeDtypeStruct((B, N_HALF, S, LANES), F32),
        jax.ShapeDtypeStruct((B, N_HALF, S, LANES), F32),
        jax.ShapeDtypeStruct((B, N_HALF, S, LANES), F32),
        jax.ShapeDtypeStruct((B, S, GROUP), BF16),
        jax.ShapeDtypeStruct((B, S, SWA_KV_WIDTH), BF16),
        jax.ShapeDtypeStruct((B, S, SWA_KV_WIDTH), BF16),
    ]
    out_specs = [head_t(ATT_DK), head_t(HEAD_DIM), head_n,
                 pl.BlockSpec((1, tm // MOBA_BLOCK, 1, GROUP), lambda b, t: (b, t, 0, 0)), vt_spec,
                 head_t(ATT_DK), head_n, vt_spec,
                 halves, halves, halves, nat(GROUP), nat(SWA_KV_WIDTH), nat(SWA_KV_WIDTH)]
    return pl.pallas_call(
        _inproj_kernel,
        grid=(B, S // tm),
        in_specs=in_specs,
        out_specs=out_specs,
        out_shape=out_shape,
        scratch_shapes=[pltpu.VMEM((n_rows, tm), F32) for n_rows in _CHUNK_ROWS],
        compiler_params=_params(2),
        name="inproj",
    )(x, gx, w1t, gaq, gak, gql, gkvl, wuqt, wukvt, gbq, gbk, gcq, gck, gdq, gdk, cos_t, sin_t, kpos_t, aslope)


def _flash_heads(qt_ref, k_ref, vt_ref, o_ref, s_bufs, p_bufs, acc_scr, rowb_scr, i):
    tq = o_ref.shape[3]
    n_rowb = 0 if rowb_scr is None else rowb_scr.shape[0]
    n_tail = tq // ATT_TK
    assert n_tail == 2
    n_past = i * n_tail
    heads = range(acc_scr.shape[0])

    def block_of(pos):
        return jnp.where(pos < n_tail, n_past + pos, pos - n_tail)

    def head_scores(h, slot, blk):
        kb = k_ref[0, h, pl.ds(pl.multiple_of(blk * ATT_TK, ATT_TK), ATT_TK), :]
        s_bufs[slot][h] = jnp.dot(kb, qt_ref[0, h], preferred_element_type=F32)

    def stage_scores(slot, blk):
        for h in heads:
            head_scores(h, slot, blk)

    def stage_softmax(slot, blk, ms, causal=None):
        new_ms, alphas = [], []
        for h in heads:
            m_parts, a_parts = [], []
            rv_row = rowb_scr[h, pl.ds(blk, 1), :] if h < n_rowb else None
            for c in range(tq // LANES):
                cols = slice(c * LANES, (c + 1) * LANES)
                st = s_bufs[slot][h, :, cols]
                if causal is not None:
                    st = jnp.where(causal[:, cols], st, 2 * NEG)
                cm = jnp.max(st, axis=0, keepdims=True)
                m_old = ms[h][:, cols]
                if rv_row is None:
                    m_new = jnp.maximum(m_old, cm)
                    shift = m_new
                else:
                    rv = rv_row[:, cols]
                    m_new = jnp.maximum(m_old, cm + rv)
                    shift = m_new - rv
                a_parts.append(jnp.exp2(m_old - m_new))
                m_parts.append(m_new)
                p_bufs[slot][h, :, cols] = jnp.exp2(st - shift).astype(BF16)
            new_ms.append(jnp.concatenate(m_parts, axis=1))
            alphas.append(jnp.concatenate(a_parts, axis=1))
        return tuple(new_ms), tuple(alphas)

    def head_values(h, slot, blk, alphas):
        acc_scr[h] = alphas[h] * acc_scr[h] + jnp.dot(vt_ref[0, h, blk], p_bufs[slot][h],
                                                      preferred_element_type=F32)

    def stage_values(slot, blk, alphas):
        for h in heads:
            head_values(h, slot, blk, alphas)

    krow = lax.broadcasted_iota(jnp.int32, (ATT_TK, tq), 0)
    ti = lax.broadcasted_iota(jnp.int32, (ATT_TK, tq), 1)
    for h in heads:
        acc_scr[h] = jnp.zeros((ATT_DV, tq), F32)
    stage_scores(0, n_past)
    stage_scores(1, n_past + 1)
    ms = tuple(jnp.full((1, tq), NEG, F32) for _ in heads)
    ms, alphas = stage_softmax(0, n_past, ms, causal=ti >= krow)
    stage_values(0, n_past, alphas)
    ms, alphas = stage_softmax(1, n_past + 1, ms, causal=ti >= krow + ATT_TK)
    stage_scores(0, 0)

    def step(pos, slot, ms, alphas):
        ms, new_alphas = stage_softmax(1 - slot, pos + 1 - n_tail, ms)
        blk = block_of(pos)
        for h in heads:
            head_values(h, slot, blk, alphas)
            head_scores(h, slot, pos)
        return ms, new_alphas

    def pair(first, carry):
        ms, alphas = step(first, 1, *carry)
        return step(first + 1, 0, ms, alphas)

    def trip(t, carry):
        for u in range(FLASH_PAIRS):
            carry = pair(2 * (FLASH_PAIRS * t + u) + 1, carry)
        return carry

    carry = lax.fori_loop(0, i // FLASH_PAIRS, trip, (ms, alphas))

    def leftover(n_pairs):
        def run(c):
            for u in range(n_pairs):
                c = pair(2 * (i - n_pairs + u) + 1, c)
            return c
        return run

    ms, alphas = lax.switch(i % FLASH_PAIRS, [leftover(n) for n in range(FLASH_PAIRS)], carry)
    last = n_past + 1
    stage_values(1, block_of(last), alphas)
    for h in heads:
        acc = acc_scr[h]
        o_ref[0, h] = acc[:HEAD_DIM, :] / acc[HEAD_DIM:HEAD_DIM + 1, :]


def _moba_kernel(qt_ref, q32t_ref, k_ref, vt_ref, kmean_ref, bqt_ref, bk_ref, bvt_ref, o_ref, ob_ref,
                 rowb_scr, s0, s1, p0, p1, acc_scr):
    i = pl.program_id(1)
    nb = kmean_ref.shape[2]
    tq = qt_ref.shape[3]
    blk = lax.broadcasted_iota(jnp.int32, (nb, tq), 0)
    col = lax.broadcasted_iota(jnp.int32, (nb, tq), 1)
    qblk = i * (tq // MOBA_BLOCK) + col // MOBA_BLOCK
    past = blk < qblk
    dist0 = (i * tq + col - blk * MOBA_BLOCK).astype(F32)
    for h in range(N_HEADS):
        gate = jnp.dot(kmean_ref[0, h], q32t_ref[0, h], preferred_element_type=F32,
                       precision=lax.Precision.HIGHEST)
        gate = jnp.where(past, gate, NEG)
        sel = blk == qblk
        for _ in range(MOBA_TOPK):
            best = jnp.max(gate, axis=0, keepdims=True)
            first = jnp.min(jnp.where(gate == best, blk, nb), axis=0, keepdims=True)
            pick = blk == first
            sel = jnp.logical_or(sel, jnp.logical_and(pick, past))
            gate = jnp.where(pick, -jnp.inf, gate)
        rowb_scr[h] = jnp.where(sel, (-float(SLOPE_A[h]) * LOG2E) * dist0, 2 * NEG)
    _flash_heads(_HeadGroups(qt_ref, bqt_ref), _HeadGroups(k_ref, bk_ref), _HeadGroups(vt_ref, bvt_ref),
                 _HeadGroups(o_ref, ob_ref), (s0, s1), (p0, p1), acc_scr, rowb_scr, i)


class _HeadGroups:
    def __init__(self, *refs):
        self.refs = refs
        self.shape = refs[0].shape

    def _at(self, idx):
        return self.refs[idx[1] // N_HEADS], (idx[0], idx[1] % N_HEADS) + tuple(idx[2:])

    def __getitem__(self, idx):
        ref, at = self._at(idx)
        return ref[at]

    def __setitem__(self, idx, val):
        ref, at = self._at(idx)
        ref[at] = val


def _dense_attention(aqt, aq32t, ak, avt, kmean_h, bqt, bk, bvt):
    B, H, _, S = aqt.shape
    tq = ATT_TQ
    nb = S // ATT_TK
    q_spec = pl.BlockSpec((1, H, ATT_DK, tq), lambda b, i: (b, 0, 0, i))
    k_spec = pl.BlockSpec((1, H, S, ATT_DK), lambda b, i: (b, 0, 0, 0), pipeline_mode=pl.Buffered(1))
    vt_spec = pl.BlockSpec((1, H, nb, ATT_DV, ATT_TK), lambda b, i: (b, 0, 0, 0, 0), pipeline_mode=pl.Buffered(1))
    o_spec = pl.BlockSpec((1, H, HEAD_DIM, tq), lambda b, i: (b, 0, 0, i))
    o_shape = jax.ShapeDtypeStruct((B, H, HEAD_DIM, S), F32)
    heads = 2 * H
    s_buf, p_buf = pltpu.VMEM((heads, ATT_TK, tq), F32), pltpu.VMEM((heads, ATT_TK, tq), BF16)
    return pl.pallas_call(
        _moba_kernel,
        grid=(B, S // tq),
        in_specs=[q_spec, pl.BlockSpec((1, H, HEAD_DIM, tq), lambda b, i: (b, 0, 0, i)), k_spec, vt_spec,
                  pl.BlockSpec((1, H, nb, HEAD_DIM), lambda b, i: (b, 0, 0, 0)), q_spec, k_spec, vt_spec],
        out_specs=[o_spec, o_spec],
        out_shape=[o_shape, o_shape],
        scratch_shapes=[pltpu.VMEM((H, nb, tq), F32), s_buf, s_buf, p_buf, p_buf,
                        pltpu.VMEM((heads, ATT_DV, tq), F32)],
        compiler_params=_params(2),
        name="dense_attention",
    )(aqt, aq32t, ak, avt, kmean_h, bqt, bk, bvt)


DIL_SPAN = max(d for _, d in DILATED_BRANCHES) * Q_BLOCK
DIL_UNITS = DIL_SPAN // Q_BLOCK
DIL_GROUP = (4, 2, 2)


def _dilated_bias():
    qi = np.arange(Q_BLOCK)[:, None]
    kidx = np.arange(2 * Q_BLOCK)[None, :]
    rel = qi + Q_BLOCK - kidx
    out = np.empty((2, len(DILATED_BRANCHES), N_HEADS, Q_BLOCK, 2 * Q_BLOCK), np.float32)
    for bi, (window, d) in enumerate(DILATED_BRANCHES):
        valid = (rel >= 0) & (rel <= window // d)
        for h in range(N_HEADS):
            bias = -SLOPE_C[h] * np.float32(LOG2E) * (d * rel).astype(np.float32)
            out[0, bi, h] = np.where(valid, bias, NEG)
            out[1, bi, h] = np.where(valid & (kidx >= Q_BLOCK), bias, NEG)
    return out


def _rows_load(ref, lead, start, size, stride):
    return jnp.concatenate([ref[lead + (c, pl.ds(start, size, stride=stride), slice(None))]
                            for c in range(N_HALF)], axis=-1)


def _rows_store(ref, lead, start, size, stride, val):
    for c in range(N_HALF):
        ref[lead + (c, pl.ds(start, size, stride=stride), slice(None))] = val[:, c * LANES:(c + 1) * LANES]


def _dilated_kernel(q_ref, kp_ref, kc_ref, vp_ref, vc_ref, bias_ref, o_ref, kbuf, vbuf, m_scr, den_scr, num_scr):
    span = pl.program_id(1)
    kbuf[:, 0:DIL_SPAN, :] = kp_ref[0]
    kbuf[:, DIL_SPAN:, :] = kc_ref[0]
    vbuf[:, 0:DIL_SPAN, :] = vp_ref[0]
    vbuf[:, DIL_SPAN:, :] = vc_ref[0]
    lane_head = lax.broadcasted_iota(jnp.int32, (1, GROUP), 1) // HEAD_DIM
    hmask = [lane_head == h for h in range(N_HEADS)]
    hmask_f = [m.astype(F32) for m in hmask]
    first_span = jnp.where(span == 0, 1, 0)

    def per_head(cols):
        out = cols[N_HEADS - 1]
        for h in range(N_HEADS - 2, -1, -1):
            out = jnp.where(hmask[h], cols[h], out)
        return out

    order = sorted(range(len(DILATED_BRANCHES)), key=lambda b: -DILATED_BRANCHES[b][1])
    for bi in order:
        d, n_group = DILATED_BRANCHES[bi][1], DIL_GROUP[bi]
        first, last = bi == order[0], bi == order[-1]

        def group(g, _, bi=bi, d=d, n_group=n_group, first=first, last=last):
            fronts = []
            for uu in range(n_group):
                u = g * n_group + uu
                r, n = u % d, u // d
                qstart = n * (Q_BLOCK * d) + r
                kstart = DIL_SPAN + qstart - Q_BLOCK * d
                q = _rows_load(q_ref, (0,), qstart, Q_BLOCK, d)
                k2 = _rows_load(kbuf, (), kstart, 2 * Q_BLOCK, d).astype(BF16)
                v2 = _rows_load(vbuf, (), kstart, 2 * Q_BLOCK, d).astype(BF16)
                q4 = jnp.concatenate([(q * hmask_f[h]).astype(BF16) for h in range(N_HEADS)], axis=0)
                s4 = lax.dot_general(q4, k2, _NT, preferred_element_type=F32)
                variant = jnp.where(n == 0, first_span, 0)
                fronts.append((qstart, s4, v2, variant))
            for qstart, s4, v2, variant in fronts:
                es, ms, ls = [], [], []
                for h in range(N_HEADS):
                    s = s4[h * Q_BLOCK:(h + 1) * Q_BLOCK, :] + bias_ref[variant, bi, h]
                    m = jnp.max(s, axis=-1, keepdims=True)
                    e = jnp.exp2(s - m)
                    ls.append(jnp.sum(e, axis=-1, keepdims=True))
                    ms.append(m)
                    es.append(e.astype(BF16))
                o4 = jnp.dot(jnp.concatenate(es, axis=0), v2, preferred_element_type=F32)
                o = o4[(N_HEADS - 1) * Q_BLOCK:, :]
                for h in range(N_HEADS - 2, -1, -1):
                    o = jnp.where(hmask[h], o4[h * Q_BLOCK:(h + 1) * Q_BLOCK, :], o)
                m_b, l_b = per_head(ms), per_head(ls)
                at = ((), qstart, Q_BLOCK, d)
                if first:
                    _rows_store(m_scr, *at, m_b)
                    _rows_store(num_scr, *at, o)
                    _rows_store(den_scr, *at, l_b)
                else:
                    m_old = _rows_load(m_scr, *at)
                    m_new = jnp.maximum(m_old, m_b)
                    a, b = jnp.exp2(m_old - m_new), jnp.exp2(m_b - m_new)
                    num = a * _rows_load(num_scr, *at) + b * o
                    den = a * _rows_load(den_scr, *at) + b * l_b
                    if last:
                        _rows_store(o_ref, (0,), qstart, Q_BLOCK, d, num / den)
                    else:
                        _rows_store(m_scr, *at, m_new)
                        _rows_store(num_scr, *at, num)
                        _rows_store(den_scr, *at, den)
            return 0

        lax.fori_loop(0, DIL_UNITS // n_group, group, 0)


def _dilated(cq, ck, cv):
    B, _, S, _ = cq.shape
    cur = pl.BlockSpec((1, N_HALF, DIL_SPAN, LANES), lambda b, s: (b, 0, s, 0))
    prev = pl.BlockSpec((1, N_HALF, DIL_SPAN, LANES), lambda b, s: (b, 0, jnp.maximum(s - 1, 0), 0))
    bias = jnp.asarray(_dilated_bias())
    return pl.pallas_call(
        _dilated_kernel,
        grid=(B, S // DIL_SPAN),
        in_specs=[cur, prev, cur, prev, cur, pl.BlockSpec(bias.shape, lambda b, s: (0,) * bias.ndim)],
        out_specs=cur,
        out_shape=jax.ShapeDtypeStruct((B, N_HALF, S, LANES), F32),
        scratch_shapes=[pltpu.VMEM((N_HALF, 2 * DIL_SPAN, LANES), F32),
                        pltpu.VMEM((N_HALF, 2 * DIL_SPAN, LANES), F32),
                        pltpu.VMEM((N_HALF, DIL_SPAN, LANES), F32), pltpu.VMEM((N_HALF, DIL_SPAN, LANES), F32),
                        pltpu.VMEM((N_HALF, DIL_SPAN, LANES), F32)],
        compiler_params=_params(2),
        name="dilated",
    )(cq, ck, ck, cv, cv, bias)


SWA_SPAN = 1024
SWA_GROUP = 4
SWA_HEAD_ORDER = (0, 2, 1, 3)


def _swa_bias():
    qi = np.arange(Q_BLOCK)[:, None]
    kidx = np.arange(2 * Q_BLOCK)[None, :]
    rel = qi + Q_BLOCK - kidx
    valid = (rel >= 0) & (rel < SWA_WINDOW)
    out = np.empty((2, N_HEADS, Q_BLOCK, 2 * Q_BLOCK), np.float32)
    for h in range(N_HEADS):
        bias = -SLOPE_D[h] * np.float32(LOG2E) * rel.astype(np.float32)
        out[0, h] = np.where(valid, bias, NEG)
        out[1, h] = np.where(valid & (kidx >= Q_BLOCK), bias, NEG)
    return out


def _swa_kernel(sink_ref, q_ref, k_ref, v_ref, bias_ref, o_ref):
    span = pl.program_id(1)
    units = SWA_SPAN // Q_BLOCK
    half = lax.broadcasted_iota(jnp.int32, (1, LANES), 1) // HEAD_DIM
    lo = half == 0
    kv_mask = [jnp.where(half == g, 1.0, 0.0).astype(BF16) for g in range(SWA_KV_HEADS)]

    def group(g, _):
        fronts = []
        for uu in range(SWA_GROUP):
            u = g * SWA_GROUP + uu
            n = span * units + u
            lo_start = pl.multiple_of(jnp.maximum(n - 1, 0) * Q_BLOCK, Q_BLOCK)
            hi_start = pl.multiple_of(n * Q_BLOCK, Q_BLOCK)
            qstart = pl.multiple_of(u * Q_BLOCK, Q_BLOCK)
            q = q_ref[0, pl.ds(qstart, Q_BLOCK), :]
            k2 = jnp.concatenate([k_ref[0, pl.ds(lo_start, Q_BLOCK), :], k_ref[0, pl.ds(hi_start, Q_BLOCK), :]],
                                 axis=0)
            v2 = jnp.concatenate([v_ref[0, pl.ds(lo_start, Q_BLOCK), :], v_ref[0, pl.ds(hi_start, Q_BLOCK), :]],
                                 axis=0)
            q4 = jnp.concatenate([q[:, (h % 2) * LANES:(h % 2 + 1) * LANES] * kv_mask[h // 2]
                                  for h in range(N_HEADS)], axis=0)
            s4 = lax.dot_general(q4, k2, _NT, preferred_element_type=F32)
            fronts.append((qstart, s4, v2, jnp.where(n == 0, 1, 0)))
        for qstart, s4, v2, variant in fronts:
            es, ls = [], []
            for h in range(N_HEADS):
                s = s4[h * Q_BLOCK:(h + 1) * Q_BLOCK, :] + bias_ref[variant, h]
                sink = sink_ref[h] * LOG2E
                m = jnp.maximum(jnp.max(s, axis=-1, keepdims=True), sink)
                e = jnp.exp2(s - m)
                ls.append(jnp.sum(e, axis=-1, keepdims=True) + jnp.exp2(sink - m))
                es.append(e.astype(BF16))
            o4 = jnp.dot(jnp.concatenate(es, axis=0), v2, preferred_element_type=F32)
            tiles = []
            for t in range(N_HALF):
                a, b = t, t + 2
                tiles.append(jnp.where(lo, o4[a * Q_BLOCK:(a + 1) * Q_BLOCK, :] / ls[a],
                                       o4[b * Q_BLOCK:(b + 1) * Q_BLOCK, :] / ls[b]))
            o_ref[0, pl.ds(qstart, Q_BLOCK), :] = jnp.concatenate(tiles, axis=-1)
        return 0

    lax.fori_loop(0, units // SWA_GROUP, group, 0)


def _swa(dq, dk, dv, sinks):
    B, S, _ = dq.shape
    bias = jnp.asarray(_swa_bias())
    return pl.pallas_call(
        _swa_kernel,
        grid=(B, S // SWA_SPAN),
        in_specs=[pl.BlockSpec(memory_space=pltpu.SMEM),
                  pl.BlockSpec((1, SWA_SPAN, GROUP), lambda b, i: (b, i, 0)),
                  pl.BlockSpec((1, S, SWA_KV_WIDTH), lambda b, i: (b, 0, 0)),
                  pl.BlockSpec((1, S, SWA_KV_WIDTH), lambda b, i: (b, 0, 0)),
                  pl.BlockSpec(bias.shape, lambda b, i: (0,) * bias.ndim)],
        out_specs=pl.BlockSpec((1, SWA_SPAN, GROUP), lambda b, i: (b, i, 0)),
        out_shape=jax.ShapeDtypeStruct((B, S, GROUP), F32),
        compiler_params=_params(2),
        name="swa",
    )(sinks, dq, dk, dv, bias)


def _row_norm(y, g_row):
    return y * lax.rsqrt(jnp.mean(y * y, axis=-1, keepdims=True) + EPS) * g_row


def _tail_kernel(x_ref, oat_ref, obt_ref, oc_ref, od_ref, gg_ref, wo_ref, gm_ref, wup_ref, wdn_ref, out_ref):
    gg = gg_ref[...]

    def col_norm_t(yt, g_row):
        y = (yt * lax.rsqrt(jnp.mean(yt * yt, axis=0, keepdims=True) + EPS)).T
        return y * g_row

    ga = col_norm_t(oat_ref[0], gg[0:1, :])
    gb = col_norm_t(obt_ref[0], gg[1:2, :])
    gc = _row_norm(jnp.concatenate([oc_ref[0, c] for c in range(N_HALF)], axis=-1), gg[2:3, :])
    gd = _row_norm(od_ref[0], gg[3:4, :])
    mixed = jnp.concatenate([ga, gb, gc, gd], axis=-1).astype(BF16)
    x1 = x_ref[0] + jnp.dot(mixed, wo_ref[...], preferred_element_type=F32)
    xn = _row_norm(x1, gm_ref[...]).astype(BF16)
    u = jnp.maximum(jnp.dot(xn, wup_ref[...], preferred_element_type=F32), 0.0)
    out_ref[0] = x1 + jnp.dot((u * u).astype(BF16), wdn_ref[...], preferred_element_type=F32)


def _tail(x, oat, obt, oc, od, gg, wo, gm, wup, wdn):
    B, S, _ = x.shape
    tm = TAIL_TM
    nat = lambda w: pl.BlockSpec((1, tm, w), lambda b, t: (b, t, 0))
    ft = pl.BlockSpec((1, GROUP, tm), lambda b, t: (b, 0, t))
    const = lambda a: pl.BlockSpec(a.shape, lambda b, t: (0,) * a.ndim, pipeline_mode=pl.Buffered(1))
    return pl.pallas_call(
        _tail_kernel,
        grid=(B, S // tm),
        in_specs=[nat(D_MODEL), ft, ft, pl.BlockSpec((1, N_HALF, tm, LANES), lambda b, t: (b, 0, t, 0)),
                  nat(GROUP), const(gg), const(wo), const(gm),
                  const(wup), const(wdn)],
        out_specs=nat(D_MODEL),
        out_shape=jax.ShapeDtypeStruct((B, S, D_MODEL), F32),
        compiler_params=_params(2),
        name="tail",
    )(x, oat, obt, oc, od, gg, wo, gm, wup, wdn)


def _rope_tables_t(S):
    inv = 1.0 / (ROPE_THETA ** (jnp.arange(0, MLA_ROPE, 2, dtype=F32) / MLA_ROPE))
    ang = inv[:, None] * jnp.arange(S, dtype=F32)[None, :]
    return jnp.cos(ang), jnp.sin(ang)


def _moba_aug_tables(S):
    pos = np.arange(S, dtype=np.float32) % MOBA_BLOCK
    kpos = np.zeros((AUG_ROWS, S), np.float32)
    kpos[:3] = pos
    aslope = np.zeros((N_HEADS, AUG_ROWS, 1), np.float32)
    rest = (SLOPE_A * np.float32(LOG2E)).astype(np.float32)
    for r in range(3):
        piece = rest.astype(BF16).astype(np.float32)
        aslope[:, r, 0] = piece
        rest = rest - piece
    return jnp.asarray(kpos), jnp.asarray(aslope)


def _layer(x, cos_t, sin_t, kpos_t, aslope, attn_norm_g, w_in, moba_q_g, moba_k_g, mla_qlat_g, mla_kvlat_g, mla_w_uq, mla_w_ukv,
           mla_q_g, mla_k_g, dil_q_g, dil_k_g, swa_q_g, swa_k_g, swa_sinks, group_out_g, w_o, mlp_norm_g,
           w_up, w_down):
    B, S, _ = x.shape
    col = lambda g: g.reshape(-1, 1)
    (aqt, aq32t, ak, akm, avt, bqt, bk, bvt, cq, ck, cv, dq, dk, dv) = _inproj(
        x, attn_norm_g.reshape(1, -1), w_in[:, _W1T_ROWS].T.astype(BF16), col(moba_q_g), col(moba_k_g), col(mla_qlat_g),
        col(mla_kvlat_g), mla_w_uq.T.astype(BF16), mla_w_ukv.T.astype(BF16), col(mla_q_g), col(mla_k_g),
        col(dil_q_g), col(dil_k_g), col(swa_q_g), col(swa_k_g), cos_t, sin_t, kpos_t, aslope)
    nb = S // MOBA_BLOCK
    kmean_h = akm.reshape(B, nb, N_HEADS, HEAD_DIM).transpose(0, 2, 1, 3)
    oat, obt = _dense_attention(aqt, aq32t, ak, avt, kmean_h, bqt, bk, bvt)
    oat, obt = oat.reshape(B, GROUP, S), obt.reshape(B, GROUP, S)
    oc = _dilated(cq, ck, cv)
    od = _swa(dq, dk, dv, swa_sinks)
    perm = np.concatenate([np.arange(HEAD_DIM) + HEAD_DIM * h for h in SWA_HEAD_ORDER])
    gg = group_out_g.at[3].set(group_out_g[3][perm])
    wo = jnp.concatenate([w_o[:3 * GROUP], w_o[3 * GROUP + perm]], axis=0)
    return _tail(x, oat, obt, oc, od, gg, wo.astype(BF16), mlp_norm_g.reshape(1, -1),
                 w_up.astype(BF16), w_down.astype(BF16))


def kernel(x, attn_norm_g, w_in, moba_q_g, moba_k_g, mla_qlat_g, mla_kvlat_g, mla_w_uq, mla_w_ukv, mla_q_g,
           mla_k_g, dil_q_g, dil_k_g, swa_q_g, swa_k_g, swa_sinks, group_out_g, w_o, mlp_norm_g, w_up, w_down):
    S = x.shape[1]
    assert S % max(d * Q_BLOCK for _, d in DILATED_BRANCHES) == 0 and S % INPROJ_TM == 0
    cos_t, sin_t = _rope_tables_t(S)
    kpos_t, aslope = _moba_aug_tables(S)
    params = (attn_norm_g, w_in, moba_q_g, moba_k_g, mla_qlat_g, mla_kvlat_g, mla_w_uq, mla_w_ukv, mla_q_g,
              mla_k_g, dil_q_g, dil_k_g, swa_q_g, swa_k_g, swa_sinks, group_out_g, w_o, mlp_norm_g, w_up, w_down)
    for l in range(attn_norm_g.shape[0]):
        x = _layer(x, cos_t, sin_t, kpos_t, aslope, *[p[l] for p in params])
    return x
```
